```python
import jax
import jax.numpy as jnp
from jax import lax
import numpy as np

D_MODEL = 1024
BATCH = 4
SEQ = 4096
DEPTH = 2

N_HEADS = 16
N_KV_GROUPS = 4
HEAD_DIM = 64
HEADS_PER_GROUP = N_HEADS // N_KV_GROUPS
L_CMP = 32
CMP_STRIDE = 16
CMP_HIDDEN = 256
L_SEL = 64
N_SELECT = 16
WINDOW = 512
Q_BLOCK = 64
NSA_Q = N_HEADS * HEAD_DIM
NSA_KV = N_KV_GROUPS * HEAD_DIM
NSA_IN = NSA_Q + 6 * NSA_KV + 3 * N_HEADS

D_RNN = 1408
LRU_BLOCKS = 8
LRU_BLOCK_W = D_RNN // LRU_BLOCKS
LRU_C = 8.0
CONV_W = 4

D_FF = 2816
EPS = 1e-6

kernel_name = "nsa_rglru_interleaved_hybrid"


def rmsnorm(x, g):
    xf = x.astype(jnp.float32)
    y = xf * lax.rsqrt(jnp.mean(xf * xf, axis=-1, keepdims=True) + EPS)
    return (y * g.astype(jnp.float32)).astype(x.dtype)


def masked_softmax(scores, mask):
    s = jnp.where(mask, scores.astype(jnp.float32), -jnp.inf)
    m = jnp.max(s, axis=-1, keepdims=True)
    m = jnp.where(jnp.isfinite(m), m, 0.0)
    e = jnp.exp(s - m)
    d = jnp.sum(e, axis=-1, keepdims=True)
    return e / jnp.where(d > 0, d, 1.0)


def alibi_slopes():
    return 2.0 ** (-8.0 * jnp.arange(1, N_HEADS + 1, dtype=jnp.float32) / N_HEADS)


def compress_blocks(raw, pos, w1, b1, w2, b2):
    b, s = raw.shape[0], raw.shape[1]
    n_cmp = (s - L_CMP) // CMP_STRIDE + 1
    idx = jnp.arange(n_cmp)[:, None] * CMP_STRIDE + jnp.arange(L_CMP)[None, :]
    blocks = raw[:, idx] + pos[None, None, :, None, :]
    flat = blocks.transpose(0, 1, 3, 2, 4).reshape(b, n_cmp, N_KV_GROUPS, L_CMP * HEAD_DIM)
    hid = jax.nn.gelu(flat @ w1 + b1)
    return hid @ w2 + b2


def nsa_mixer(h, w_in, b_gate, cmp_pos, cmp_w1, cmp_b1, cmp_w2, cmp_b2, w_out):
    b, s, _ = h.shape
    G, R, dh = N_KV_GROUPS, HEADS_PER_GROUP, HEAD_DIM
    proj = h @ w_in
    cuts = [NSA_Q + i * NSA_KV for i in range(7)]
    q, kc_raw, vc_raw, k_sel_all, v_sel_all, k_win, v_win, g_logit = jnp.split(proj, cuts, axis=-1)
    q = q.reshape(b, s, G, R, dh) * (HEAD_DIM ** -0.5)
    kv = lambda z: z.reshape(b, s, G, dh)
    kc_raw, vc_raw, k_sel_all, v_sel_all, k_win, v_win = map(
        kv, (kc_raw, vc_raw, k_sel_all, v_sel_all, k_win, v_win))
    gates = jax.nn.sigmoid((g_logit + b_gate).astype(jnp.float32)).reshape(b, s, G, R, 3)

    kc = compress_blocks(kc_raw, cmp_pos[0], cmp_w1[0], cmp_b1[0], cmp_w2[0], cmp_b2[0])
    vc = compress_blocks(vc_raw, cmp_pos[1], cmp_w1[1], cmp_b1[1], cmp_w2[1], cmp_b2[1])
    n_cmp = kc.shape[1]
    cmp_start = jnp.arange(n_cmp) * CMP_STRIDE
    cmp_end = cmp_start + L_CMP - 1

    n_sb = s // L_SEL
    n_sel = min(N_SELECT, n_sb)
    sb_start = jnp.arange(n_sb) * L_SEL
    overlap = ((cmp_start[:, None] < sb_start[None, :] + L_SEL)
               & (cmp_start[:, None] + L_CMP > sb_start[None, :])).astype(jnp.float32)
    ks_t = k_sel_all.transpose(0, 2, 1, 3)
    vs_t = v_sel_all.transpose(0, 2, 1, 3)
    gather = jax.vmap(jax.vmap(lambda src, ix: src[ix]))
    sel_off = jnp.arange(L_SEL)
    jb = jnp.arange(n_sb)

    kw_pad = jnp.pad(k_win, ((0, 0), (WINDOW, 0), (0, 0), (0, 0)))
    vw_pad = jnp.pad(v_win, ((0, 0), (WINDOW, 0), (0, 0), (0, 0)))
    slopes = alibi_slopes().reshape(G, R)[None, None, :, :, None]

    def block(qb):
        start = qb * Q_BLOCK
        t = start + jnp.arange(Q_BLOCK)
        qq = lax.dynamic_slice_in_dim(q, start, Q_BLOCK, axis=1)
        gg = lax.dynamic_slice_in_dim(gates, start, Q_BLOCK, axis=1)

        dist_c = t[:, None] - cmp_end[None, :]
        s_c = (jnp.einsum('bqgrd,bngd->bqgrn', qq, kc).astype(jnp.float32)
               - slopes * jnp.abs(dist_c).astype(jnp.float32)[None, :, None, None, :])
        p_c = masked_softmax(s_c, (dist_c >= 0)[None, :, None, None, :])
        o_c = jnp.einsum('bqgrn,bngd->bqgrd', p_c.astype(vc.dtype), vc)

        imp = jnp.einsum('bqgrn,nj->bqgj', p_c, overlap)
        valid_b = (sb_start[None, :] <= t[:, None])[None, :, None, :]
        cur = (t // L_SEL)[:, None]
        forced = ((jb[None, :] == 0) | (jb[None, :] == cur) | (jb[None, :] == cur - 1))[None, :, None, :]
        score = jnp.where(valid_b, jnp.where(forced, jnp.inf, imp), -jnp.inf)
        top_val, top_idx = lax.top_k(score, n_sel)
        tok = (top_idx[..., None] * L_SEL + sel_off).reshape(b, Q_BLOCK, G, n_sel * L_SEL)
        blk_ok = jnp.broadcast_to((top_val > -jnp.inf)[..., None],
                                  top_val.shape + (L_SEL,)).reshape(tok.shape)
        tok_t = tok.transpose(0, 2, 1, 3)
        k_sel = gather(ks_t, tok_t)
        v_sel = gather(vs_t, tok_t)
        dist_s = t[None, :, None, None] - tok
        s_s = (jnp.einsum('bqgrd,bgqtd->bqgrt', qq, k_sel).astype(jnp.float32)
               - slopes * jnp.abs(dist_s).astype(jnp.float32)[:, :, :, None, :])
        p_s = masked_softmax(s_s, (blk_ok & (dist_s >= 0))[:, :, :, None, :])
        o_s = jnp.einsum('bqgrt,bgqtd->bqgrd', p_s.astype(v_sel.dtype), v_sel)

        kw_b = lax.dynamic_slice_in_dim(kw_pad, start, WINDOW + Q_BLOCK, axis=1)
        vw_b = lax.dynamic_slice_in_dim(vw_pad, start, WINDOW + Q_BLOCK, axis=1)
        s_pos = start - WINDOW + jnp.arange(WINDOW + Q_BLOCK)
        dist_w = t[:, None] - s_pos[None, :]
        mask_w = (dist_w >= 0) & (dist_w < WINDOW) & (s_pos[None, :] >= 0)
        s_w = (jnp.einsum('bqgrd,bkgd->bqgrk', qq, kw_b).astype(jnp.float32)
               - slopes * jnp.abs(dist_w).astype(jnp.float32)[None, :, None, None, :])
        p_w = masked_softmax(s_w, mask_w[None, :, None, None, :])
        o_w = jnp.einsum('bqgrk,bkgd->bqgrd', p_w.astype(vw_b.dtype), vw_b)

        out = (gg[..., 0:1] * o_c.astype(jnp.float32)
               + gg[..., 1:2] * o_s.astype(jnp.float32)
               + gg[..., 2:3] * o_w.astype(jnp.float32))
        return out.astype(h.dtype).reshape(b, Q_BLOCK, NSA_Q)

    outs = lax.map(block, jnp.arange(s // Q_BLOCK))
    y = outs.transpose(1, 0, 2, 3).reshape(b, s, NSA_Q)
    return y @ w_out


def lru_mixer(h, w_in, conv_w, conv_b, w_a, b_a, w_x, b_x, lam, w_out):
    b, s, _ = h.shape
    proj = h @ w_in
    gate_br, rec = jnp.split(proj, [D_RNN], axis=-1)
    gate = jax.nn.gelu(gate_br, approximate=True)
    xr = lax.conv_general_dilated(rec, conv_w[:, None, :], window_strides=(1,),
                                  padding=[(CONV_W - 1, 0)],
                                  dimension_numbers=('NWC', 'WIO', 'NWC'),
                                  feature_group_count=D_RNN) + conv_b
    xb = xr.reshape(b, s, LRU_BLOCKS, LRU_BLOCK_W)
    r = jax.nn.sigmoid((jnp.einsum('bsnc,ncd->bsnd', xb, w_a).reshape(b, s, D_RNN) + b_a).astype(jnp.float32))
    i = jax.nn.sigmoid((jnp.einsum('bsnc,ncd->bsnd', xb, w_x).reshape(b, s, D_RNN) + b_x).astype(jnp.float32))
    log_a = -LRU_C * r * jax.nn.softplus(-lam.astype(jnp.float32))
    a = jnp.exp(log_a)
    u = jnp.sqrt(jnp.maximum(-jnp.expm1(2.0 * log_a), 0.0)) * (i * xr.astype(jnp.float32))

    def combine(left, right):
        a1, b1 = left
        a2, b2 = right
        return a1 * a2, a2 * b1 + b2

    _, hs = lax.associative_scan(combine, (a, u), axis=1)
    return (hs.astype(h.dtype) * gate) @ w_out


def swiglu(h, w_in, w_out):
    g, u = jnp.split(h @ w_in, [D_FF], axis=-1)
    return (jax.nn.silu(g) * u) @ w_out


def setup_inputs(seed: int = 0) -> dict:
    key = jax.random.key(seed)
    ks = jax.random.split(key, 24)
    n_a = (DEPTH + 1) // 2
    n_b = DEPTH // 2
    f32 = jnp.float32

    def nrm(k, shape, scale):
        return jax.random.normal(k, shape, f32) * scale

    x = nrm(ks[0], (BATCH, SEQ, D_MODEL), 1.0)
    norm_mix = 1.0 + nrm(ks[1], (DEPTH, D_MODEL), 0.02)
    norm_ffn = 1.0 + nrm(ks[2], (DEPTH, D_MODEL), 0.02)
    norm_final = 1.0 + nrm(ks[3], (D_MODEL,), 0.02)
    nsa_w_in = nrm(ks[4], (n_a, D_MODEL, NSA_IN), D_MODEL ** -0.5)
    nsa_b_gate = nrm(ks[5], (n_a, 3 * N_HEADS), 0.02)
    nsa_cmp_pos = nrm(ks[6], (n_a, 2, L_CMP, HEAD_DIM), 0.02)
    nsa_cmp_w1 = nrm(ks[7], (n_a, 2, L_CMP * HEAD_DIM, CMP_HIDDEN), (L_CMP * HEAD_DIM) ** -0.5)
    nsa_cmp_b1 = nrm(ks[8], (n_a, 2, CMP_HIDDEN), 0.02)
    nsa_cmp_w2 = nrm(ks[9], (n_a, 2, CMP_HIDDEN, HEAD_DIM), CMP_HIDDEN ** -0.5)
    nsa_cmp_b2 = nrm(ks[10], (n_a, 2, HEAD_DIM), 0.02)
    nsa_w_out = nrm(ks[11], (n_a, NSA_Q, D_MODEL), NSA_Q ** -0.5)
    lru_w_in = nrm(ks[12], (n_b, D_MODEL, 2 * D_RNN), D_MODEL ** -0.5)
    lru_conv_w = nrm(ks[13], (n_b, CONV_W, D_RNN), CONV_W ** -0.5)
    lru_conv_b = nrm(ks[14], (n_b, D_RNN), 0.02)
    lru_w_a = nrm(ks[15], (n_b, LRU_BLOCKS, LRU_BLOCK_W, LRU_BLOCK_W), LRU_BLOCK_W ** -0.5)
    lru_b_a = nrm(ks[16], (n_b, D_RNN), 0.02)
    lru_w_x = nrm(ks[17], (n_b, LRU_BLOCKS, LRU_BLOCK_W, LRU_BLOCK_W), LRU_BLOCK_W ** -0.5)
    lru_b_x = nrm(ks[18], (n_b, D_RNN), 0.02)
    a_c = jax.random.uniform(ks[19], (n_b, D_RNN), f32, 0.9, 0.999)
    sig = a_c ** (1.0 / LRU_C)
    lru_lambda = jnp.log(sig) - jnp.log1p(-sig)
    lru_w_out = nrm(ks[20], (n_b, D_RNN, D_MODEL), D_RNN ** -0.5)
    ffn_w_in = nrm(ks[21], (DEPTH, D_MODEL, 2 * D_FF), D_MODEL ** -0.5)
    ffn_w_out = nrm(ks[22], (DEPTH, D_FF, D_MODEL), D_FF ** -0.5)
    return {
        "x": x, "norm_mix": norm_mix, "norm_ffn": norm_ffn, "norm_final": norm_final,
        "nsa_w_in": nsa_w_in, "nsa_b_gate": nsa_b_gate, "nsa_cmp_pos": nsa_cmp_pos,
        "nsa_cmp_w1": nsa_cmp_w1, "nsa_cmp_b1": nsa_cmp_b1, "nsa_cmp_w2": nsa_cmp_w2,
        "nsa_cmp_b2": nsa_cmp_b2, "nsa_w_out": nsa_w_out,
        "lru_w_in": lru_w_in, "lru_conv_w": lru_conv_w, "lru_conv_b": lru_conv_b,
        "lru_w_a": lru_w_a, "lru_b_a": lru_b_a, "lru_w_x": lru_w_x, "lru_b_x": lru_b_x,
        "lru_lambda": lru_lambda, "lru_w_out": lru_w_out,
        "ffn_w_in": ffn_w_in, "ffn_w_out": ffn_w_out,
    }


def reference(x, norm_mix, norm_ffn, norm_final,
              nsa_w_in, nsa_b_gate, nsa_cmp_pos, nsa_cmp_w1, nsa_cmp_b1, nsa_cmp_w2,
              nsa_cmp_b2, nsa_w_out,
              lru_w_in, lru_conv_w, lru_conv_b, lru_w_a, lru_b_a, lru_w_x, lru_b_x,
              lru_lambda, lru_w_out,
              ffn_w_in, ffn_w_out):
    for i in range(DEPTH):
        hn = rmsnorm(x, norm_mix[i])
        j = i // 2
        if i % 2 == 0:
            x = x + nsa_mixer(hn, nsa_w_in[j], nsa_b_gate[j], nsa_cmp_pos[j], nsa_cmp_w1[j],
                              nsa_cmp_b1[j], nsa_cmp_w2[j], nsa_cmp_b2[j], nsa_w_out[j])
        else:
            x = x + lru_mixer(hn, lru_w_in[j], lru_conv_w[j], lru_conv_b[j], lru_w_a[j],
                              lru_b_a[j], lru_w_x[j], lru_b_x[j], lru_lambda[j], lru_w_out[j])
        x = x + swiglu(rmsnorm(x, norm_ffn[i]), ffn_w_in[i], ffn_w_out[i])
    return rmsnorm(x, norm_final)
```

```python
import functools

import jax
import jax.numpy as jnp
from jax import lax
from jax.experimental import pallas as pl
from jax.experimental.pallas import tpu as pltpu

F32 = jnp.float32
BF16 = jnp.bfloat16

D_MODEL = 1024
N_HEADS = 16
N_GROUPS = 4
HEADS_PER_GROUP = N_HEADS // N_GROUPS
HEAD_DIM = 64
L_CMP = 32
CMP_STRIDE = 16
L_SEL = 64
N_SELECT = 16
WINDOW = 512
NSA_Q = N_HEADS * HEAD_DIM
NSA_KV = N_GROUPS * HEAD_DIM
NSA_IN = NSA_Q + 6 * NSA_KV + 3 * N_HEADS
D_RNN = 1408
LRU_BLOCKS = 8
LRU_C = 8.0
CONV_W = 4
D_FF = 2816
EPS = 1e-6

LANES = 128
NEG = -1e30
VMEM_LIMIT = 56 * 1024 * 1024

TQ = 128
TK_SEL = 256
TK_WIN = 128
N_CMP_PAD = 256


def _cparams(*sem):
    return pltpu.CompilerParams(dimension_semantics=sem, vmem_limit_bytes=VMEM_LIMIT)


def _rms(x, g):
    ms = jnp.mean(x * x, axis=-1, keepdims=True)
    return x * lax.rsqrt(ms + EPS) * g


def _norm_matmul_kernel(x_ref, g_ref, w_ref, o_ref, xn_ref):
    @pl.when(pl.program_id(1) == 0)
    def _():
        xn_ref[...] = _rms(x_ref[...], g_ref[...]).astype(BF16)

    o_ref[...] = jnp.dot(xn_ref[...], w_ref[...], preferred_element_type=F32).astype(o_ref.dtype)


def _norm_matmul(x, g, w, *, tm, tn, name):
    m, k = x.shape
    n = w.shape[1]
    return pl.pallas_call(
        _norm_matmul_kernel,
        grid=(m // tm, n // tn),
        in_specs=[
            pl.BlockSpec((tm, k), lambda i, j: (i, 0)),
            pl.BlockSpec((1, k), lambda i, j: (0, 0)),
            pl.BlockSpec((k, tn), lambda i, j: (0, j)),
        ],
        out_specs=pl.BlockSpec((tm, tn), lambda i, j: (i, j)),
        out_shape=jax.ShapeDtypeStruct((m, n), F32),
        scratch_shapes=[pltpu.VMEM((tm, k), BF16)],
        compiler_params=_cparams("parallel", "arbitrary"),
        name=name,
    )(x, g.reshape(1, k), w)


def _matmul_res_kernel(a_ref, w_ref, r_ref, o_ref):
    o_ref[...] = r_ref[...] + jnp.dot(a_ref[...], w_ref[...], preferred_element_type=F32)


def _matmul_res(a, w, res, *, tm, name):
    m, k = a.shape
    n = w.shape[1]
    return pl.pallas_call(
        _matmul_res_kernel,
        grid=(m // tm,),
        in_specs=[
            pl.BlockSpec((tm, k), lambda i: (i, 0)),
            pl.BlockSpec((k, n), lambda i: (0, 0)),
            pl.BlockSpec((tm, n), lambda i: (i, 0)),
        ],
        out_specs=pl.BlockSpec((tm, n), lambda i: (i, 0)),
        out_shape=jax.ShapeDtypeStruct((m, n), F32),
        compiler_params=_cparams("parallel"),
        name=name,
    )(a, w, res)


def _ffn_kernel(x_ref, g_ref, wg_ref, wu_ref, wo_ref, gf_ref, o_ref, xn_ref, acc_ref, *, final_norm):
    j = pl.program_id(1)

    @pl.when(j == 0)
    def _():
        xn_ref[...] = _rms(x_ref[...], g_ref[...]).astype(BF16)
        acc_ref[...] = jnp.zeros_like(acc_ref)

    xn = xn_ref[...]
    gate = jnp.dot(xn, wg_ref[...], preferred_element_type=F32)
    up = jnp.dot(xn, wu_ref[...], preferred_element_type=F32)
    hid = gate * jax.nn.sigmoid(gate) * up
    acc_ref[...] += jnp.dot(hid.astype(BF16), wo_ref[...], preferred_element_type=F32)

    @pl.when(j == pl.num_programs(1) - 1)
    def _():
        y = x_ref[...] + acc_ref[...]
        if final_norm:
            y = _rms(y, gf_ref[...])
        o_ref[...] = y


def _ffn(x, g, w_in, w_out, g_final, *, final_norm, tm, tf, name):
    m, d = x.shape
    nf = D_FF // tf
    return pl.pallas_call(
        functools.partial(_ffn_kernel, final_norm=final_norm),
        grid=(m // tm, nf),
        in_specs=[
            pl.BlockSpec((tm, d), lambda i, j: (i, 0)),
            pl.BlockSpec((1, d), lambda i, j: (0, 0)),
            pl.BlockSpec((d, tf), lambda i, j: (0, j)),
            pl.BlockSpec((d, tf), lambda i, j: (0, j + nf)),
            pl.BlockSpec((tf, d), lambda i, j: (j, 0)),
            pl.BlockSpec((1, d), lambda i, j: (0, 0)),
        ],
        out_specs=pl.BlockSpec((tm, d), lambda i, j: (i, 0)),
        out_shape=jax.ShapeDtypeStruct((m, d), F32),
        scratch_shapes=[pltpu.VMEM((tm, d), BF16), pltpu.VMEM((tm, d), F32)],
        compiler_params=_cparams("parallel", "arbitrary"),
        name=name,
    )(x, g.reshape(1, d), w_in, w_in, w_out, g_final.reshape(1, d))


def _compress_kernel(r_ref, pos_ref, w1_ref, b1_ref, w2_ref, b2_ref, o_ref):
    half = CMP_STRIDE * HEAD_DIM
    rows = r_ref[...]
    pos = pos_ref[...]
    lo = (rows + pos[:, :half]).astype(BF16)
    hi = (rows + pos[:, half:]).astype(BF16)
    part_lo = jnp.dot(lo, w1_ref[:half, :], preferred_element_type=F32)
    part_hi = jnp.dot(hi, w1_ref[half:, :], preferred_element_type=F32)
    hid = part_lo + pltpu.roll(part_hi, N_CMP_PAD - 1, axis=0) + b1_ref[...]
    hid = jax.nn.gelu(hid, approximate=True)
    out = jnp.dot(hid.astype(BF16), w2_ref[...], preferred_element_type=F32) + b2_ref[...]
    o_ref[...] = out.astype(o_ref.dtype)


def _compress(raw, pos, w1, b1, w2, b2):
    two, b, g, n, width = raw.shape
    hidden = w1.shape[-1]
    sq = pl.Squeezed()
    return pl.pallas_call(
        _compress_kernel,
        grid=(two, b, g),
        in_specs=[
            pl.BlockSpec((sq, sq, sq, n, width), lambda c, i, j: (c, i, j, 0, 0)),
            pl.BlockSpec((sq, 1, 2 * width), lambda c, i, j: (c, 0, 0)),
            pl.BlockSpec((sq, 2 * width, hidden), lambda c, i, j: (c, 0, 0)),
            pl.BlockSpec((sq, 1, hidden), lambda c, i, j: (c, 0, 0)),
            pl.BlockSpec((sq, hidden, HEAD_DIM), lambda c, i, j: (c, 0, 0)),
            pl.BlockSpec((sq, 1, HEAD_DIM), lambda c, i, j: (c, 0, 0)),
        ],
        out_specs=pl.BlockSpec((sq, sq, sq, n, HEAD_DIM), lambda c, i, j: (c, i, j, 0, 0)),
        out_shape=jax.ShapeDtypeStruct((two, b, g, n, HEAD_DIM), BF16),
        compiler_params=_cparams("parallel", "parallel", "parallel"),
        name="nsa_compress",
    )(raw, pos, w1, b1, w2, b2)


def _nt_dot(a, b):
    return lax.dot_general(a, b, (((1,), (1,)), ((), ())), preferred_element_type=F32)


def _flash_step(s, v, mask, distf, slopes, m_ref, l_ref, acc_ref):
    for r in range(HEADS_PER_GROUP):
        rows = pl.ds(r * TQ, TQ)
        s_r = jnp.where(mask, s[r * TQ:(r + 1) * TQ] - slopes[r] * distf, NEG)
        m_old = m_ref[rows, :]
        m_new = jnp.maximum(m_old, jnp.max(s_r, axis=1, keepdims=True))
        alpha = jnp.exp(m_old - m_new)
        p = jnp.exp(s_r - m_new)
        l_ref[rows, :] = alpha * l_ref[rows, :] + jnp.sum(p, axis=1, keepdims=True)
        acc_ref[rows, :] = alpha * acc_ref[rows, :] + jnp.dot(
            p.astype(BF16), v, preferred_element_type=F32)
        m_ref[rows, :] = m_new


def _flash_reset(m_ref, l_ref, acc_ref):
    m_ref[...] = jnp.full_like(m_ref, NEG)
    l_ref[...] = jnp.zeros_like(l_ref)
    acc_ref[...] = jnp.zeros_like(acc_ref)


def _nsa_kernel(slopes_ref, q_ref, kc_ref, vc_ref, ks_ref, vs_ref, kw_ref, vw_ref, gl_ref, bg_ref,
                o_ref, score_ref, m_ref, l_ref, acc_ref):
    grp = pl.program_id(1)
    qi = pl.program_id(2)
    q0 = qi * TQ
    nrep = HEADS_PER_GROUP
    slopes = [slopes_ref[grp * nrep + r] for r in range(nrep)]

    qs = q_ref[...].reshape(nrep * TQ, HEAD_DIM) * (HEAD_DIM ** -0.5)
    t_col = q0 + lax.broadcasted_iota(jnp.int32, (TQ, 1), 0)
    gates = jax.nn.sigmoid(gl_ref[...] + bg_ref[...])

    s_c = _nt_dot(qs, kc_ref[...])
    cmp_end = CMP_STRIDE * lax.broadcasted_iota(jnp.int32, (1, N_CMP_PAD), 1) + (L_CMP - 1)
    dist_c = t_col - cmp_end
    mask_c = dist_c >= 0
    distf_c = dist_c.astype(F32)
    p_sum = jnp.zeros((TQ, N_CMP_PAD), F32)
    vc = vc_ref[...]
    for r in range(nrep):
        s_r = jnp.where(mask_c, s_c[r * TQ:(r + 1) * TQ] - slopes[r] * distf_c, NEG)
        m = jnp.max(s_r, axis=1, keepdims=True)
        e = jnp.where(mask_c, jnp.exp(s_r - m), 0.0)
        d = jnp.sum(e, axis=1, keepdims=True)
        p = e / jnp.where(d > 0, d, 1.0)
        p_sum = p_sum + p
        o_c = jnp.dot(p.astype(BF16), vc, preferred_element_type=F32)
        o_ref[r] = gates[:, 3 * r:3 * r + 1] * o_c

    n_sb = score_ref.shape[0]
    cmp_start = CMP_STRIDE * lax.broadcasted_iota(jnp.int32, (n_sb, N_CMP_PAD), 1)
    sb_start = L_SEL * lax.broadcasted_iota(jnp.int32, (n_sb, N_CMP_PAD), 0)
    overlap_t = ((cmp_start < sb_start + L_SEL) & (cmp_start + L_CMP > sb_start)).astype(F32)
    imp_t = lax.dot_general(overlap_t, p_sum, (((1,), (1,)), ((), ())),
                            precision=lax.Precision.HIGHEST, preferred_element_type=F32)
    jb = lax.broadcasted_iota(jnp.int32, (n_sb, TQ), 0)
    t_row = q0 + lax.broadcasted_iota(jnp.int32, (n_sb, TQ), 1)
    valid = jb * L_SEL <= t_row
    cur = t_row // L_SEL
    forced = (jb == 0) | (jb == cur) | (jb == cur - 1)
    score = jnp.where(valid, jnp.where(forced, jnp.inf, imp_t), -jnp.inf)
    score_ref[...] = score
    rank = jnp.zeros((n_sb, TQ), F32)
    for i in range(n_sb):
        row = score_ref[pl.ds(i, 1), :]
        rank = rank + jnp.where(jb > i, jnp.where(row >= score, 1.0, 0.0),
                                jnp.where(row > score, 1.0, 0.0))
    sel_t = jnp.where(valid & (rank < float(N_SELECT)), 1.0, 0.0)
    sel = sel_t.T.astype(BF16)

    _flash_reset(m_ref, l_ref, acc_ref)
    blk_row = lax.broadcasted_iota(jnp.int32, (n_sb, TK_SEL), 0)
    key_off = lax.broadcasted_iota(jnp.int32, (n_sb, TK_SEL), 1)
    key_row = lax.broadcasted_iota(jnp.int32, (1, TK_SEL), 1)

    def sel_step(kk, carry):
        k0 = pl.multiple_of(kk * TK_SEL, TK_SEL)
        s = _nt_dot(qs, ks_ref[pl.ds(k0, TK_SEL), :])
        expand = (blk_row == (k0 + key_off) // L_SEL).astype(BF16)
        picked = jnp.dot(sel, expand, preferred_element_type=F32)
        dist = t_col - (k0 + key_row)
        mask = (picked > 0.5) & (dist >= 0)
        _flash_step(s, vs_ref[pl.ds(k0, TK_SEL), :], mask, dist.astype(F32), slopes,
                    m_ref, l_ref, acc_ref)
        return carry

    lax.fori_loop(0, (q0 + TQ + TK_SEL - 1) // TK_SEL, sel_step, 0)
    inv_l = 1.0 / l_ref[...]
    for r in range(nrep):
        rows = pl.ds(r * TQ, TQ)
        o_ref[r] = o_ref[r] + gates[:, 3 * r + 1:3 * r + 2] * (acc_ref[rows, :] * inv_l[r * TQ:(r + 1) * TQ])

    _flash_reset(m_ref, l_ref, acc_ref)
    key_row_w = lax.broadcasted_iota(jnp.int32, (1, TK_WIN), 1)
    n_win = WINDOW // TK_WIN + 1

    def win_step(kk, carry):
        k0 = pl.multiple_of((qi - kk) * TK_WIN, TK_WIN)
        s = _nt_dot(qs, kw_ref[pl.ds(k0, TK_WIN), :])
        dist = t_col - (k0 + key_row_w)
        mask = (dist >= 0) & (dist < WINDOW)
        _flash_step(s, vw_ref[pl.ds(k0, TK_WIN), :], mask, dist.astype(F32), slopes,
                    m_ref, l_ref, acc_ref)
        return carry

    lax.fori_loop(0, jnp.minimum(n_win, qi + 1), win_step, 0)
    inv_l = 1.0 / l_ref[...]
    for r in range(nrep):
        rows = pl.ds(r * TQ, TQ)
        o_ref[r] = o_ref[r] + gates[:, 3 * r + 2:3 * r + 3] * (acc_ref[rows, :] * inv_l[r * TQ:(r + 1) * TQ])


def _nsa_attention(slopes, q, kc, vc, ks, vs, kw, vw, glogit, bgate):
    b, g, nrep, s, dh = q.shape
    n_sb = s // L_SEL
    sq = pl.Squeezed()
    kv_spec = pl.BlockSpec((sq, sq, s, dh), lambda i, j, t: (i, j, 0, 0))
    cmp_spec = pl.BlockSpec((sq, sq, N_CMP_PAD, dh), lambda i, j, t: (i, j, 0, 0))
    return pl.pallas_call(
        _nsa_kernel,
        grid=(b, g, s // TQ),
        in_specs=[
            pl.BlockSpec(memory_space=pltpu.SMEM),
            pl.BlockSpec((sq, sq, nrep, TQ, dh), lambda i, j, t: (i, j, 0, t, 0)),
            cmp_spec, cmp_spec, kv_spec, kv_spec, kv_spec, kv_spec,
            pl.BlockSpec((sq, sq, TQ, 3 * nrep), lambda i, j, t: (i, j, t, 0)),
            pl.BlockSpec((sq, 1, 3 * nrep), lambda i, j, t: (j, 0, 0)),
        ],
        out_specs=pl.BlockSpec((sq, sq, nrep, TQ, dh), lambda i, j, t: (i, j, 0, t, 0)),
        out_shape=jax.ShapeDtypeStruct((b, g, nrep, s, dh), F32),
        scratch_shapes=[
            pltpu.VMEM((n_sb, TQ), F32),
            pltpu.VMEM((nrep * TQ, 1), F32),
            pltpu.VMEM((nrep * TQ, 1), F32),
            pltpu.VMEM((nrep * TQ, dh), F32),
        ],
        compiler_params=_cparams("parallel", "parallel", "arbitrary"),
        name="nsa_attention",
    )(slopes, q, kc, vc, ks, vs, kw, vw, glogit, bgate)


def _nsa_mixer(x, norm_g, w_in, b_gate, cmp_pos, cmp_w1, cmp_b1, cmp_w2, cmp_b2, w_out):
    b, s, d = x.shape
    m = b * s
    g, nrep, dh = N_GROUPS, HEADS_PER_GROUP, HEAD_DIM
    x2 = x.reshape(m, d)
    n_pad = -(-NSA_IN // LANES) * LANES
    w_in_p = jnp.pad(w_in, ((0, 0), (0, n_pad - NSA_IN))).astype(BF16)
    proj = _norm_matmul(x2, norm_g, w_in_p, tm=512, tn=n_pad // 3, name="nsa_in_proj")

    def part(i):
        lo = NSA_Q + i * NSA_KV
        return proj[:, lo:lo + NSA_KV].reshape(b, s, g, dh).transpose(0, 2, 1, 3)

    q = proj[:, :NSA_Q].astype(BF16).reshape(b, s, g, nrep, dh).transpose(0, 2, 3, 1, 4)
    raw = jnp.stack([part(0), part(1)]).reshape(2, b, g, s // CMP_STRIDE, CMP_STRIDE * dh)
    ks, vs, kw, vw = (part(i).astype(BF16) for i in range(2, 6))
    glogit = proj[:, NSA_Q + 6 * NSA_KV:NSA_IN].reshape(b, s, g, 3 * nrep).transpose(0, 2, 1, 3)

    cmp = _compress(raw, cmp_pos.reshape(2, 1, L_CMP * dh), cmp_w1.astype(BF16),
                    cmp_b1.reshape(2, 1, -1), cmp_w2.astype(BF16), cmp_b2.reshape(2, 1, dh))
    slopes = 2.0 ** (-8.0 * jnp.arange(1, N_HEADS + 1, dtype=F32) / N_HEADS)
    o = _nsa_attention(slopes, q, cmp[0], cmp[1], ks, vs, kw, vw, glogit,
                       b_gate.reshape(g, 1, 3 * nrep))
    y = o.transpose(0, 3, 1, 2, 4).reshape(m, NSA_Q).astype(BF16)
    return _matmul_res(y, w_out.astype(BF16), x2, tm=512, name="nsa_out_proj").reshape(b, s, d)


TT = 256
SUB = 8


def _lru_kernel(gate_ref, rec_ref, cw_ref, cb_ref, wax_ref, bax_ref, lam_ref, o_ref,
                ext_ref, a_ref, u_ref, h_ref):
    ti = pl.program_id(1)

    @pl.when(ti == 0)
    def _():
        ext_ref[pl.ds(TT, SUB), :] = jnp.zeros((SUB, D_RNN), F32)
        h_ref[...] = jnp.zeros_like(h_ref)

    ext_ref[pl.ds(0, SUB), :] = ext_ref[pl.ds(TT, SUB), :]
    ext_ref[pl.ds(SUB, TT), :] = rec_ref[...]
    cw = cw_ref[...]
    xr = cb_ref[...] + sum(
        cw[w:w + 1, :] * ext_ref[pl.ds(SUB - (CONV_W - 1) + w, TT), :] for w in range(CONV_W))

    z = jnp.dot(xr.astype(BF16), wax_ref[...], preferred_element_type=F32) + bax_ref[...]
    r_gate = jax.nn.sigmoid(z[:, :D_RNN])
    i_gate = jax.nn.sigmoid(z[:, D_RNN:])
    lam = lam_ref[...]
    softplus_neg = jnp.maximum(-lam, 0.0) + jnp.log1p(jnp.exp(-jnp.abs(lam)))
    log_a = -LRU_C * r_gate * softplus_neg
    a = jnp.exp(log_a)
    a_ref[...] = a
    u_ref[...] = jnp.sqrt(jnp.maximum(1.0 - a * a, 0.0)) * (i_gate * xr)

    row = lax.broadcasted_iota(jnp.int32, (SUB, D_RNN), 0)

    def scan_rows(c, h_prev):
        r0 = pl.multiple_of(c * SUB, SUB)
        a_c = a_ref[pl.ds(r0, SUB), :]
        u_c = u_ref[pl.ds(r0, SUB), :]
        shift = 1
        while shift < SUB:
            keep = row >= shift
            u_c = u_c + a_c * jnp.where(keep, pltpu.roll(u_c, shift, axis=0), 0.0)
            a_c = a_c * jnp.where(keep, pltpu.roll(a_c, shift, axis=0), 1.0)
            shift *= 2
        h_c = u_c + a_c * h_prev
        u_ref[pl.ds(r0, SUB), :] = h_c
        return jnp.broadcast_to(h_c[SUB - 1:SUB, :], (SUB, D_RNN))

    h_ref[...] = lax.fori_loop(0, TT // SUB, scan_rows, h_ref[...])
    o_ref[...] = (u_ref[...] * jax.nn.gelu(gate_ref[...], approximate=True)).astype(o_ref.dtype)


def _lru_core(proj, conv_w, conv_b, wax, bax, lam, b, s):
    sq = pl.Squeezed()
    vec = lambda n: pl.BlockSpec((1, n), lambda i, t: (0, 0))
    return pl.pallas_call(
        _lru_kernel,
        grid=(b, s // TT),
        in_specs=[
            pl.BlockSpec((sq, TT, D_RNN), lambda i, t: (i, t, 0)),
            pl.BlockSpec((sq, TT, D_RNN), lambda i, t: (i, t, 1)),
            pl.BlockSpec((CONV_W, D_RNN), lambda i, t: (0, 0)),
            vec(D_RNN),
            pl.BlockSpec((D_RNN, 2 * D_RNN), lambda i, t: (0, 0)),
            vec(2 * D_RNN),
            vec(D_RNN),
        ],
        out_specs=pl.BlockSpec((sq, TT, D_RNN), lambda i, t: (i, t, 0)),
        out_shape=jax.ShapeDtypeStruct((b, s, D_RNN), BF16),
        scratch_shapes=[
            pltpu.VMEM((TT + SUB, D_RNN), F32),
            pltpu.VMEM((TT, D_RNN), F32),
            pltpu.VMEM((TT, D_RNN), F32),
            pltpu.VMEM((SUB, D_RNN), F32),
        ],
        compiler_params=_cparams("parallel", "arbitrary"),
        name="lru_core",
    )(proj, proj, conv_w, conv_b.reshape(1, -1), wax, bax.reshape(1, -1), lam.reshape(1, -1))


def _lru_mixer(x, norm_g, w_in, conv_w, conv_b, w_a, b_a, w_x, b_x, lam, w_out):
    b, s, d = x.shape
    m = b * s
    x2 = x.reshape(m, d)
    proj = _norm_matmul(x2, norm_g, w_in.astype(BF16), tm=512, tn=D_RNN, name="lru_in_proj")
    dense = lambda w: jax.scipy.linalg.block_diag(*[w[i] for i in range(LRU_BLOCKS)])
    wax = jnp.concatenate([dense(w_a), dense(w_x)], axis=1).astype(BF16)
    bax = jnp.concatenate([b_a, b_x])
    hg = _lru_core(proj.reshape(b, s, 2 * D_RNN), conv_w, conv_b, wax, bax, lam, b, s)
    return _matmul_res(hg.reshape(m, D_RNN), w_out.astype(BF16), x2, tm=512,
                       name="lru_out_proj").reshape(b, s, d)


def kernel(x, norm_mix, norm_ffn, norm_final, nsa_w_in, nsa_b_gate, nsa_cmp_pos, nsa_cmp_w1,
           nsa_cmp_b1, nsa_cmp_w2, nsa_cmp_b2, nsa_w_out, lru_w_in, lru_conv_w, lru_conv_b,
           lru_w_a, lru_b_a, lru_w_x, lru_b_x, lru_lambda, lru_w_out, ffn_w_in, ffn_w_out):
    b, s, d = x.shape
    ffn = functools.partial(_ffn, tm=1024, tf=256)
    x = _nsa_mixer(x, norm_mix[0], nsa_w_in[0], nsa_b_gate[0], nsa_cmp_pos[0], nsa_cmp_w1[0],
                   nsa_cmp_b1[0], nsa_cmp_w2[0], nsa_cmp_b2[0], nsa_w_out[0])
    x = ffn(x.reshape(b * s, d), norm_ffn[0], ffn_w_in[0].astype(BF16), ffn_w_out[0].astype(BF16),
            norm_final, final_norm=False, name="ffn0").reshape(b, s, d)
    x = _lru_mixer(x, norm_mix[1], lru_w_in[0], lru_conv_w[0], lru_conv_b[0], lru_w_a[0], lru_b_a[0],
                   lru_w_x[0], lru_b_x[0], lru_lambda[0], lru_w_out[0])
    x = ffn(x.reshape(b * s, d), norm_ffn[1], ffn_w_in[1].astype(BF16), ffn_w_out[1].astype(BF16),
            norm_final, final_norm=True, name="ffn1").reshape(b, s, d)
    return x
```

```python
import functools

import jax
import jax.numpy as jnp
from jax import lax
from jax.experimental import pallas as pl
from jax.experimental.pallas import tpu as pltpu

F32 = jnp.float32
BF16 = jnp.bfloat16

D_MODEL = 1024
N_HEADS = 16
N_GROUPS = 4
HEADS_PER_GROUP = N_HEADS // N_GROUPS
HEAD_DIM = 64
L_CMP = 32
CMP_STRIDE = 16
L_SEL = 64
N_SELECT = 16
WINDOW = 512
NSA_Q = N_HEADS * HEAD_DIM
NSA_KV = N_GROUPS * HEAD_DIM
NSA_IN = NSA_Q + 6 * NSA_KV + 3 * N_HEADS
D_RNN = 1408
LRU_BLOCKS = 8
LRU_C = 8.0
CONV_W = 4
D_FF = 2816
EPS = 1e-6

LANES = 128
SUBLANES = 8
NEG = -1e30
VMEM_LIMIT = 56 * 1024 * 1024

TQ = 128
TK_SEL = 256
TK_WIN = 128
N_CMP_PAD = 256


def _cparams(*sem):
    return pltpu.CompilerParams(dimension_semantics=sem, vmem_limit_bytes=VMEM_LIMIT)


def _rms(x, g):
    ms = jnp.mean(x * x, axis=-1, keepdims=True)
    return x * lax.rsqrt(ms + EPS) * g


def _norm_matmul_kernel(x_ref, g_ref, w_ref, o_ref, xn_ref):
    @pl.when(pl.program_id(1) == 0)
    def _():
        xn_ref[...] = _rms(x_ref[...], g_ref[...]).astype(BF16)

    o_ref[...] = jnp.dot(xn_ref[...], w_ref[...], preferred_element_type=F32).astype(o_ref.dtype)


def _norm_matmul(x, g, w, *, tm, tn, name):
    m, k = x.shape
    n = w.shape[1]
    return pl.pallas_call(
        _norm_matmul_kernel,
        grid=(m // tm, n // tn),
        in_specs=[
            pl.BlockSpec((tm, k), lambda i, j: (i, 0)),
            pl.BlockSpec((1, k), lambda i, j: (0, 0)),
            pl.BlockSpec((k, tn), lambda i, j: (0, j)),
        ],
        out_specs=pl.BlockSpec((tm, tn), lambda i, j: (i, j)),
        out_shape=jax.ShapeDtypeStruct((m, n), F32),
        scratch_shapes=[pltpu.VMEM((tm, k), BF16)],
        compiler_params=_cparams("parallel", "arbitrary"),
        name=name,
    )(x, g.reshape(1, k), w)


def _matmul_res_kernel(a_ref, w_ref, r_ref, o_ref):
    o_ref[...] = r_ref[...] + jnp.dot(a_ref[...], w_ref[...], preferred_element_type=F32)


def _matmul_res(a, w, res, *, tm, name):
    m, k = a.shape
    n = w.shape[1]
    return pl.pallas_call(
        _matmul_res_kernel,
        grid=(m // tm,),
        in_specs=[
            pl.BlockSpec((tm, k), lambda i: (i, 0)),
            pl.BlockSpec((k, n), lambda i: (0, 0)),
            pl.BlockSpec((tm, n), lambda i: (i, 0)),
        ],
        out_specs=pl.BlockSpec((tm, n), lambda i: (i, 0)),
        out_shape=jax.ShapeDtypeStruct((m, n), F32),
        compiler_params=_cparams("parallel"),
        name=name,
    )(a, w, res)


def _ffn_kernel(x_ref, g_ref, wg_ref, wu_ref, wo_ref, gf_ref, o_ref, xn_ref, acc_ref, *, final_norm):
    j = pl.program_id(1)

    @pl.when(j == 0)
    def _():
        xn_ref[...] = _rms(x_ref[...], g_ref[...]).astype(BF16)
        acc_ref[...] = jnp.zeros_like(acc_ref)

    xn = xn_ref[...]
    gate = jnp.dot(xn, wg_ref[...], preferred_element_type=F32)
    up = jnp.dot(xn, wu_ref[...], preferred_element_type=F32)
    hid = gate * jax.nn.sigmoid(gate) * up
    acc_ref[...] += jnp.dot(hid.astype(BF16), wo_ref[...], preferred_element_type=F32)

    @pl.when(j == pl.num_programs(1) - 1)
    def _():
        y = x_ref[...] + acc_ref[...]
        if final_norm:
            y = _rms(y, gf_ref[...])
        o_ref[...] = y


def _ffn(x, g, w_in, w_out, g_final, *, final_norm, tm, tf, name):
    m, d = x.shape
    nf = D_FF // tf
    return pl.pallas_call(
        functools.partial(_ffn_kernel, final_norm=final_norm),
        grid=(m // tm, nf),
        in_specs=[
            pl.BlockSpec((tm, d), lambda i, j: (i, 0)),
            pl.BlockSpec((1, d), lambda i, j: (0, 0)),
            pl.BlockSpec((d, tf), lambda i, j: (0, j)),
            pl.BlockSpec((d, tf), lambda i, j: (0, j + nf)),
            pl.BlockSpec((tf, d), lambda i, j: (j, 0)),
            pl.BlockSpec((1, d), lambda i, j: (0, 0)),
        ],
        out_specs=pl.BlockSpec((tm, d), lambda i, j: (i, 0)),
        out_shape=jax.ShapeDtypeStruct((m, d), F32),
        scratch_shapes=[pltpu.VMEM((tm, d), BF16), pltpu.VMEM((tm, d), F32)],
        compiler_params=_cparams("parallel", "arbitrary"),
        name=name,
    )(x, g.reshape(1, d), w_in, w_in, w_out, g_final.reshape(1, d))


def _compress_kernel(r_ref, pos_ref, w1_ref, b1_ref, w2_ref, b2_ref, o_ref):
    half = CMP_STRIDE * HEAD_DIM
    rows = r_ref[...]
    pos = pos_ref[...]
    lo = (rows + pos[:, :half]).astype(BF16)
    hi = (rows + pos[:, half:]).astype(BF16)
    part_lo = jnp.dot(lo, w1_ref[:half, :], preferred_element_type=F32)
    part_hi = jnp.dot(hi, w1_ref[half:, :], preferred_element_type=F32)
    hid = part_lo + pltpu.roll(part_hi, N_CMP_PAD - 1, axis=0) + b1_ref[...]
    hid = jax.nn.gelu(hid, approximate=True)
    out = jnp.dot(hid.astype(BF16), w2_ref[...], preferred_element_type=F32) + b2_ref[...]
    o_ref[...] = out.astype(o_ref.dtype)


def _compress(raw, pos, w1, b1, w2, b2):
    two, b, g, n, width = raw.shape
    hidden = w1.shape[-1]
    sq = pl.Squeezed()
    return pl.pallas_call(
        _compress_kernel,
        grid=(two, b, g),
        in_specs=[
            pl.BlockSpec((sq, sq, sq, n, width), lambda c, i, j: (c, i, j, 0, 0)),
            pl.BlockSpec((sq, 1, 2 * width), lambda c, i, j: (c, 0, 0)),
            pl.BlockSpec((sq, 2 * width, hidden), lambda c, i, j: (c, 0, 0)),
            pl.BlockSpec((sq, 1, hidden), lambda c, i, j: (c, 0, 0)),
            pl.BlockSpec((sq, hidden, HEAD_DIM), lambda c, i, j: (c, 0, 0)),
            pl.BlockSpec((sq, 1, HEAD_DIM), lambda c, i, j: (c, 0, 0)),
        ],
        out_specs=pl.BlockSpec((sq, sq, sq, n, HEAD_DIM), lambda c, i, j: (c, i, j, 0, 0)),
        out_shape=jax.ShapeDtypeStruct((two, b, g, n, HEAD_DIM), BF16),
        compiler_params=_cparams("parallel", "parallel", "parallel"),
        name="nsa_compress",
    )(raw, pos, w1, b1, w2, b2)


AUG = 256
SEL_ROW0 = HEAD_DIM
POS_ROW0 = 2 * HEAD_DIM
N_PIECES = 3
POS_ROWS = 16
V_ROWS = 128
NQ = HEADS_PER_GROUP * TQ
BIG = 1e30
assert TQ == TK_WIN and TK_SEL % TQ == 0 and WINDOW % TK_WIN == 0


def _flash_reset(m_ref, acc_ref):
    m_ref[...] = jnp.full_like(m_ref, NEG)
    acc_ref[...] = jnp.zeros_like(acc_ref)


def _scores(kaug, tile, tk, qat_ref):
    k0 = pl.multiple_of(tile * tk, tk)
    return jnp.dot(kaug[pl.ds(k0, tk), :], qat_ref[...], preferred_element_type=F32)


def _flash_update(s, vt_tile, mask, m_ref, acc_ref):
    if mask is not None:
        s = jnp.where(mask, s, NEG)
    m_old = m_ref[...]
    m_new = jnp.maximum(m_old, jnp.max(s, axis=0, keepdims=True))
    p = jnp.exp(s - m_new).astype(BF16)
    acc_ref[...] = jnp.exp(m_old - m_new) * acc_ref[...] + jnp.dot(
        vt_tile, p, preferred_element_type=F32)
    m_ref[...] = m_new


def _flash_result(acc_ref):
    return acc_ref[pl.ds(0, HEAD_DIM), :] * (1.0 / acc_ref[pl.ds(HEAD_DIM, 1), :])


def _nsa_kernel(sl_ref, q_ref, kc_ref, vct_ref, ks_ref, vst_ref, kw_ref, vwt_ref, gt_ref, bg_ref,
                constk_ref, constc_ref, o_ref,
                kaug_s, kaug_w, kaug_c, vt_s, vt_w, vt_c, qat_ref, score_ref, m_ref, acc_ref, out_ref,
                sa_ref, sb_ref):
    grp = pl.program_id(1)
    qi = pl.program_id(2)
    q0 = qi * TQ
    nrep = HEADS_PER_GROUP

    @pl.when(qi == 0)
    def _():
        for kaug, const, k in ((kaug_s, constk_ref, ks_ref), (kaug_w, constk_ref, kw_ref),
                               (kaug_c, constc_ref, kc_ref)):
            kaug[...] = const[...]
            kaug[:, pl.ds(0, HEAD_DIM)] = k[...]
        pad = V_ROWS - HEAD_DIM
        for vt, src in ((vt_s, vst_ref), (vt_w, vwt_ref)):
            ones_row = lax.broadcasted_iota(jnp.int32, (vt.shape[0], pad, vt.shape[2]), 1) == 0
            vt[:, pl.ds(0, HEAD_DIM), :] = src[...]
            vt[:, pl.ds(HEAD_DIM, pad), :] = jnp.where(ones_row, 1.0, 0.0).astype(BF16)
        ones_row = lax.broadcasted_iota(jnp.int32, (pad, N_CMP_PAD), 0) == 0
        vt_c[pl.ds(0, HEAD_DIM), :] = vct_ref[...]
        vt_c[pl.ds(HEAD_DIM, pad), :] = jnp.where(ones_row, 1.0, 0.0).astype(BF16)

    for half in range(nrep // 2):
        qt = q_ref[:, pl.ds(half * LANES, LANES)].astype(F32).T * (HEAD_DIM ** -0.5)
        for sub in range(2):
            qat_ref[pl.ds(0, HEAD_DIM), pl.ds((2 * half + sub) * TQ, TQ)] = (
                qt[sub * HEAD_DIM:(sub + 1) * HEAD_DIM].astype(BF16))
    qat_ref[pl.ds(SEL_ROW0, HEAD_DIM), :] = jnp.zeros((HEAD_DIM, NQ), BF16)
    piece = lax.broadcasted_iota(jnp.int32, (POS_ROWS, TQ), 0)
    for r in range(nrep):
        tile = jnp.zeros((POS_ROWS, TQ), F32)
        for i in range(2 * N_PIECES):
            tile = jnp.where(piece == i, sl_ref[(grp * nrep + r) * N_PIECES + i % N_PIECES], tile)
        qat_ref[pl.ds(POS_ROW0, POS_ROWS), pl.ds(r * TQ, TQ)] = tile.astype(BF16)
    tail0 = POS_ROW0 + POS_ROWS
    qat_ref[pl.ds(tail0, AUG - tail0), :] = jnp.zeros((AUG - tail0, NQ), BF16)

    t_row = q0 + (lax.broadcasted_iota(jnp.int32, (1, NQ), 1) & (TQ - 1))
    gates = jax.nn.sigmoid(gt_ref[...] + bg_ref[...])
    gate_row = lambda br: jnp.concatenate(
        [gates[3 * r + br:3 * r + br + 1, :] for r in range(nrep)], axis=1)

    s = jnp.dot(kaug_c[...], qat_ref[...], preferred_element_type=F32)
    cmp_end = CMP_STRIDE * lax.broadcasted_iota(jnp.int32, (N_CMP_PAD, 1), 0) + (L_CMP - 1)
    mask_c = cmp_end <= t_row
    s = jnp.where(mask_c, s, NEG)
    e = jnp.where(mask_c, jnp.exp(s - jnp.max(s, axis=0, keepdims=True)), 0.0)
    d = jnp.sum(e, axis=0, keepdims=True)
    p = e * (1.0 / jnp.where(d > 0, d, 1.0))
    p_sum = sum(p[:, r * TQ:(r + 1) * TQ] for r in range(nrep))
    o_c = jnp.dot(vt_c[...], p.astype(BF16), preferred_element_type=F32)
    out_ref[...] = gate_row(0) * o_c[:HEAD_DIM]

    n_win = WINDOW // TK_WIN
    key_w = lax.broadcasted_iota(jnp.int32, (TK_WIN, 1), 0)
    gate_w = gate_row(2)

    @pl.when(qi >= n_win)
    def _():
        tiles = [qi - n_win + j for j in range(n_win + 1)]
        s_w = [_scores(kaug_w, tile, TK_WIN, qat_ref) for tile in tiles]
        s_w[0] = jnp.where(tiles[0] * TK_WIN + key_w > t_row - WINDOW, s_w[0], NEG)
        s_w[-1] = jnp.where(q0 + key_w <= t_row, s_w[-1], NEG)
        m_w = functools.reduce(jnp.maximum, [jnp.max(x, axis=0, keepdims=True) for x in s_w])
        acc = sum(jnp.dot(vt_w[tile], jnp.exp(x - m_w).astype(BF16), preferred_element_type=F32)
                  for tile, x in zip(tiles, s_w))
        out_ref[...] += gate_w * (acc[:HEAD_DIM] * (1.0 / acc[HEAD_DIM:HEAD_DIM + 1]))

    @pl.when(qi < n_win)
    def _():
        _flash_reset(m_ref, acc_ref)
        _flash_update(_scores(kaug_w, qi, TK_WIN, qat_ref), vt_w[qi], q0 + key_w <= t_row,
                      m_ref, acc_ref)

        def earlier(j, carry):
            tile = qi - 1 - j
            _flash_update(_scores(kaug_w, tile, TK_WIN, qat_ref), vt_w[tile], None, m_ref, acc_ref)
            return carry

        lax.fori_loop(0, qi, earlier, 0)
        out_ref[...] += gate_w * _flash_result(acc_ref)

    n_sb = score_ref.shape[0]
    cmp_start = CMP_STRIDE * lax.broadcasted_iota(jnp.int32, (n_sb, N_CMP_PAD), 1)
    sb_start = L_SEL * lax.broadcasted_iota(jnp.int32, (n_sb, N_CMP_PAD), 0)
    overlap_t = ((cmp_start < sb_start + L_SEL) & (cmp_start + L_CMP > sb_start)).astype(F32)
    imp_t = jnp.dot(overlap_t, p_sum, precision=lax.Precision.HIGHEST,
                    preferred_element_type=F32)
    jb = lax.broadcasted_iota(jnp.int32, (n_sb, TQ), 0)
    t_sel = q0 + lax.broadcasted_iota(jnp.int32, (n_sb, TQ), 1)
    valid = jb * L_SEL <= t_sel
    cur = t_sel // L_SEL
    forced = (jb == 0) | (jb == cur) | (jb == cur - 1)
    score = jnp.where(valid, jnp.where(forced, jnp.inf, imp_t), -jnp.inf)
    score_ref[...] = score
    chunks = [score[c * SUBLANES:(c + 1) * SUBLANES] for c in range(n_sb // SUBLANES)]
    ranks = [jnp.zeros((SUBLANES, TQ), F32) for _ in chunks]
    sub_row = lax.broadcasted_iota(jnp.int32, (SUBLANES, TQ), 0)
    for i in range(n_sb):
        row = score_ref[pl.ds(i, 1), :]
        for c, chunk in enumerate(chunks):
            later = jnp.where(row >= chunk, 1.0, 0.0)
            earlier = jnp.where(row > chunk, 1.0, 0.0)
            if i < c * SUBLANES:
                ranks[c] = ranks[c] + later
            elif i >= (c + 1) * SUBLANES:
                ranks[c] = ranks[c] + earlier
            else:
                ranks[c] = ranks[c] + jnp.where(sub_row > i - c * SUBLANES, later, earlier)
    rank = jnp.concatenate(ranks, axis=0)
    sel_bias = jnp.where(valid & (rank < float(N_SELECT)), 0.0, -BIG).astype(BF16)
    for r in range(nrep):
        qat_ref[pl.ds(SEL_ROW0, n_sb), pl.ds(r * TQ, TQ)] = sel_bias

    _flash_reset(m_ref, acc_ref)
    n_full = q0 // TK_SEL
    key_s = lax.broadcasted_iota(jnp.int32, (TK_SEL, 1), 0)
    tail_mask = n_full * TK_SEL + key_s <= t_row
    sel_scores = lambda tile: _scores(kaug_s, tile, TK_SEL, qat_ref)
    sa_ref[...] = sel_scores(0)

    def sel_pair(j, carry):
        tile = 2 * j
        sb_ref[...] = sel_scores(tile + 1)
        _flash_update(sa_ref[...], vt_s[tile], None, m_ref, acc_ref)
        sa_ref[...] = sel_scores(tile + 2)
        _flash_update(sb_ref[...], vt_s[tile + 1], None, m_ref, acc_ref)
        return carry

    lax.fori_loop(0, n_full // 2, sel_pair, 0)

    @pl.when(n_full % 2 == 1)
    def _():
        sb_ref[...] = sel_scores(n_full)
        _flash_update(sa_ref[...], vt_s[n_full - 1], None, m_ref, acc_ref)
        _flash_update(sb_ref[...], vt_s[n_full], tail_mask, m_ref, acc_ref)

    @pl.when(n_full % 2 == 0)
    def _():
        _flash_update(sa_ref[...], vt_s[n_full], tail_mask, m_ref, acc_ref)

    out_ref[...] += gate_row(1) * _flash_result(acc_ref)

    for half in range(nrep // 2):
        slab = jnp.concatenate([out_ref[:, pl.ds(2 * half * TQ, TQ)],
                                out_ref[:, pl.ds((2 * half + 1) * TQ, TQ)]], axis=0)
        o_ref[:, pl.ds(half * LANES, LANES)] = slab.T.astype(o_ref.dtype)


def _nsa_attention(slope_pieces, q, kc, vct, ks, vst, kw, vwt, glogit_t, bgate, constk, constc):
    b, s, _ = q.shape
    g, nrep, dh = N_GROUPS, HEADS_PER_GROUP, HEAD_DIM
    n_sb = s // L_SEL
    sq = pl.Squeezed()
    k_spec = pl.BlockSpec((sq, sq, s, dh), lambda i, j, t: (i, j, 0, 0))
    vt_spec = lambda tk: pl.BlockSpec((sq, sq, s // tk, dh, tk), lambda i, j, t: (i, j, 0, 0, 0))
    full = lambda shape: pl.BlockSpec(shape, lambda i, j, t: (0,) * len(shape))
    return pl.pallas_call(
        _nsa_kernel,
        grid=(b, g, s // TQ),
        in_specs=[
            pl.BlockSpec(memory_space=pltpu.SMEM),
            pl.BlockSpec((sq, TQ, nrep * dh), lambda i, j, t: (i, t, j)),
            pl.BlockSpec((sq, sq, N_CMP_PAD, dh), lambda i, j, t: (i, j, 0, 0)),
            pl.BlockSpec((sq, sq, dh, N_CMP_PAD), lambda i, j, t: (i, j, 0, 0)),
            k_spec, vt_spec(TK_SEL), k_spec, vt_spec(TK_WIN),
            pl.BlockSpec((sq, sq, 3 * nrep, TQ), lambda i, j, t: (i, j, 0, t)),
            pl.BlockSpec((sq, 3 * nrep, 1), lambda i, j, t: (j, 0, 0)),
            full((s, AUG)), full((N_CMP_PAD, AUG)),
        ],
        out_specs=pl.BlockSpec((sq, TQ, nrep * dh), lambda i, j, t: (i, t, j)),
        out_shape=jax.ShapeDtypeStruct((b, s, NSA_Q), BF16),
        scratch_shapes=[
            pltpu.VMEM((s, AUG), BF16), pltpu.VMEM((s, AUG), BF16), pltpu.VMEM((N_CMP_PAD, AUG), BF16),
            pltpu.VMEM((s // TK_SEL, V_ROWS, TK_SEL), BF16), pltpu.VMEM((s // TK_WIN, V_ROWS, TK_WIN), BF16),
            pltpu.VMEM((V_ROWS, N_CMP_PAD), BF16),
            pltpu.VMEM((AUG, NQ), BF16),
            pltpu.VMEM((n_sb, TQ), F32),
            pltpu.VMEM((1, NQ), F32),
            pltpu.VMEM((V_ROWS, NQ), F32),
            pltpu.VMEM((dh, NQ), F32),
            pltpu.VMEM((TK_SEL, NQ), F32), pltpu.VMEM((TK_SEL, NQ), F32),
        ],
        compiler_params=_cparams("arbitrary", "arbitrary", "arbitrary"),
        name="nsa_attention",
    )(slope_pieces, q, kc, vct, ks, vst, kw, vwt, glogit_t, bgate, constk, constc)


def _position_pieces(pos):
    hi = (pos // L_SEL) * L_SEL
    return jnp.stack([hi] * N_PIECES + [pos - hi] * N_PIECES, axis=1).astype(F32)


def _nsa_constants(s):
    pos = jnp.arange(s)
    constk = jnp.zeros((s, AUG), F32)
    constk = constk.at[:, SEL_ROW0:SEL_ROW0 + s // L_SEL].set(jax.nn.one_hot(pos // L_SEL, s // L_SEL))
    constk = constk.at[:, POS_ROW0:POS_ROW0 + 2 * N_PIECES].set(_position_pieces(pos))
    cmp_end = CMP_STRIDE * jnp.arange(N_CMP_PAD) + (L_CMP - 1)
    constc = jnp.zeros((N_CMP_PAD, AUG), F32)
    constc = constc.at[:, POS_ROW0:POS_ROW0 + 2 * N_PIECES].set(_position_pieces(cmp_end))
    slopes = 2.0 ** (-8.0 * jnp.arange(1, N_HEADS + 1, dtype=F32) / N_HEADS)
    pieces, rest = [], slopes
    for _ in range(N_PIECES):
        piece = rest.astype(BF16).astype(F32)
        pieces.append(piece)
        rest = rest - piece
    return constk.astype(BF16), constc.astype(BF16), jnp.stack(pieces, axis=1).reshape(-1)


def _nsa_mixer(x, norm_g, w_in, b_gate, cmp_pos, cmp_w1, cmp_b1, cmp_w2, cmp_b2, w_out):
    b, s, d = x.shape
    m = b * s
    g, nrep, dh = N_GROUPS, HEADS_PER_GROUP, HEAD_DIM
    x2 = x.reshape(m, d)
    n_pad = -(-NSA_IN // LANES) * LANES
    w_in_p = jnp.pad(w_in, ((0, 0), (0, n_pad - NSA_IN))).astype(BF16)
    proj = _norm_matmul(x2, norm_g, w_in_p, tm=512, tn=n_pad // 3, name="nsa_in_proj")

    def part(i):
        lo = NSA_Q + i * NSA_KV
        return proj[:, lo:lo + NSA_KV].reshape(b, s, g, dh)

    k_layout = lambda z: z.astype(BF16).transpose(0, 2, 1, 3)
    vt_layout = lambda z, tk: z.astype(BF16).reshape(b, s // tk, tk, g, dh).transpose(0, 3, 1, 4, 2)
    q = proj[:, :NSA_Q].astype(BF16).reshape(b, s, NSA_Q)
    raw = jnp.stack([part(0), part(1)]).transpose(0, 1, 3, 2, 4).reshape(
        2, b, g, s // CMP_STRIDE, CMP_STRIDE * dh)
    glogit_t = proj[:, NSA_Q + 6 * NSA_KV:NSA_IN].reshape(b, s, g, 3 * nrep).transpose(0, 2, 3, 1)

    cmp = _compress(raw, cmp_pos.reshape(2, 1, L_CMP * dh), cmp_w1.astype(BF16),
                    cmp_b1.reshape(2, 1, -1), cmp_w2.astype(BF16), cmp_b2.reshape(2, 1, dh))
    constk, constc, slope_pieces = _nsa_constants(s)
    y = _nsa_attention(slope_pieces, q, cmp[0], cmp[1].transpose(0, 1, 3, 2),
                       k_layout(part(2)), vt_layout(part(3), TK_SEL),
                       k_layout(part(4)), vt_layout(part(5), TK_WIN),
                       glogit_t, b_gate.reshape(g, 3 * nrep, 1), constk, constc)
    return _matmul_res(y.reshape(m, NSA_Q), w_out.astype(BF16), x2, tm=512,
                       name="nsa_out_proj").reshape(b, s, d)


TT = 256
SUB = SUBLANES


def _lru_kernel(gate_ref, rec_ref, cw_ref, cb_ref, wax_ref, bax_ref, lam_ref, o_ref,
                ext_ref, a_ref, u_ref, h_ref):
    ti = pl.program_id(1)

    @pl.when(ti == 0)
    def _():
        ext_ref[pl.ds(TT, SUB), :] = jnp.zeros((SUB, D_RNN), F32)
        h_ref[...] = jnp.zeros_like(h_ref)

    ext_ref[pl.ds(0, SUB), :] = ext_ref[pl.ds(TT, SUB), :]
    ext_ref[pl.ds(SUB, TT), :] = rec_ref[...]
    cw = cw_ref[...]
    xr = cb_ref[...] + sum(
        cw[w:w + 1, :] * ext_ref[pl.ds(SUB - (CONV_W - 1) + w, TT), :] for w in range(CONV_W))

    z = jnp.dot(xr.astype(BF16), wax_ref[...], preferred_element_type=F32) + bax_ref[...]
    r_gate = jax.nn.sigmoid(z[:, :D_RNN])
    i_gate = jax.nn.sigmoid(z[:, D_RNN:])
    lam = lam_ref[...]
    softplus_neg = jnp.maximum(-lam, 0.0) + jnp.log1p(jnp.exp(-jnp.abs(lam)))
    log_a = -LRU_C * r_gate * softplus_neg
    a = jnp.exp(log_a)
    a_ref[...] = a
    u_ref[...] = jnp.sqrt(jnp.maximum(1.0 - a * a, 0.0)) * (i_gate * xr)

    row = lax.broadcasted_iota(jnp.int32, (SUB, D_RNN), 0)

    def scan_rows(c, h_prev):
        r0 = pl.multiple_of(c * SUB, SUB)
        a_c = a_ref[pl.ds(r0, SUB), :]
        u_c = u_ref[pl.ds(r0, SUB), :]
        shift = 1
        while shift < SUB:
            keep = row >= shift
            u_c = u_c + a_c * jnp.where(keep, pltpu.roll(u_c, shift, axis=0), 0.0)
            a_c = a_c * jnp.where(keep, pltpu.roll(a_c, shift, axis=0), 1.0)
            shift *= 2
        h_c = u_c + a_c * h_prev
        u_ref[pl.ds(r0, SUB), :] = h_c
        return jnp.broadcast_to(h_c[SUB - 1:SUB, :], (SUB, D_RNN))

    h_ref[...] = lax.fori_loop(0, TT // SUB, scan_rows, h_ref[...])
    o_ref[...] = (u_ref[...] * jax.nn.gelu(gate_ref[...], approximate=True)).astype(o_ref.dtype)


def _lru_core(proj, conv_w, conv_b, wax, bax, lam, b, s):
    sq = pl.Squeezed()
    vec = lambda n: pl.BlockSpec((1, n), lambda i, t: (0, 0))
    return pl.pallas_call(
        _lru_kernel,
        grid=(b, s // TT),
        in_specs=[
            pl.BlockSpec((sq, TT, D_RNN), lambda i, t: (i, t, 0)),
            pl.BlockSpec((sq, TT, D_RNN), lambda i, t: (i, t, 1)),
            pl.BlockSpec((CONV_W, D_RNN), lambda i, t: (0, 0)),
            vec(D_RNN),
            pl.BlockSpec((D_RNN, 2 * D_RNN), lambda i, t: (0, 0)),
            vec(2 * D_RNN),
            vec(D_RNN),
        ],
        out_specs=pl.BlockSpec((sq, TT, D_RNN), lambda i, t: (i, t, 0)),
        out_shape=jax.ShapeDtypeStruct((b, s, D_RNN), BF16),
        scratch_shapes=[
            pltpu.VMEM((TT + SUB, D_RNN), F32),
            pltpu.VMEM((TT, D_RNN), F32),
            pltpu.VMEM((TT, D_RNN), F32),
            pltpu.VMEM((SUB, D_RNN), F32),
        ],
        compiler_params=_cparams("parallel", "arbitrary"),
        name="lru_core",
    )(proj, proj, conv_w, conv_b.reshape(1, -1), wax, bax.reshape(1, -1), lam.reshape(1, -1))


def _lru_mixer(x, norm_g, w_in, conv_w, conv_b, w_a, b_a, w_x, b_x, lam, w_out):
    b, s, d = x.shape
    m = b * s
    x2 = x.reshape(m, d)
    proj = _norm_matmul(x2, norm_g, w_in.astype(BF16), tm=512, tn=D_RNN, name="lru_in_proj")
    dense = lambda w: jax.scipy.linalg.block_diag(*[w[i] for i in range(LRU_BLOCKS)])
    wax = jnp.concatenate([dense(w_a), dense(w_x)], axis=1).astype(BF16)
    bax = jnp.concatenate([b_a, b_x])
    hg = _lru_core(proj.reshape(b, s, 2 * D_RNN), conv_w, conv_b, wax, bax, lam, b, s)
    return _matmul_res(hg.reshape(m, D_RNN), w_out.astype(BF16), x2, tm=512,
                       name="lru_out_proj").reshape(b, s, d)


def kernel(x, norm_mix, norm_ffn, norm_final, nsa_w_in, nsa_b_gate, nsa_cmp_pos, nsa_cmp_w1,
           nsa_cmp_b1, nsa_cmp_w2, nsa_cmp_b2, nsa_w_out, lru_w_in, lru_conv_w, lru_conv_b,
           lru_w_a, lru_b_a, lru_w_x, lru_b_x, lru_lambda, lru_w_out, ffn_w_in, ffn_w_out):
    b, s, d = x.shape
    ffn = functools.partial(_ffn, tm=1024, tf=256)
    x = _nsa_mixer(x, norm_mix[0], nsa_w_in[0], nsa_b_gate[0], nsa_cmp_pos[0], nsa_cmp_w1[0],
                   nsa_cmp_b1[0], nsa_cmp_w2[0], nsa_cmp_b2[0], nsa_w_out[0])
    x = ffn(x.reshape(b * s, d), norm_ffn[0], ffn_w_in[0].astype(BF16), ffn_w_out[0].astype(BF16),
            norm_final, final_norm=False, name="ffn0").reshape(b, s, d)
    x = _lru_mixer(x, norm_mix[1], lru_w_in[0], lru_conv_w[0], lru_conv_b[0], lru_w_a[0], lru_b_a[0],
                   lru_w_x[0], lru_b_x[0], lru_lambda[0], lru_w_out[0])
    x = ffn(x.reshape(b * s, d), norm_ffn[1], ffn_w_in[1].astype(BF16), ffn_w_out[1].astype(BF16),
            norm_final, final_norm=True, name="ffn1").reshape(b, s, d)
    return x
```

```python
import functools

import jax
import jax.numpy as jnp
from jax import lax
from jax.experimental import pallas as pl
from jax.experimental.pallas import tpu as pltpu

F32 = jnp.float32
BF16 = jnp.bfloat16

D_MODEL = 1024
N_HEADS = 16
N_GROUPS = 4
HEADS_PER_GROUP = N_HEADS // N_GROUPS
HEAD_DIM = 64
L_CMP = 32
CMP_STRIDE = 16
L_SEL = 64
N_SELECT = 16
WINDOW = 512
NSA_Q = N_HEADS * HEAD_DIM
NSA_KV = N_GROUPS * HEAD_DIM
NSA_IN = NSA_Q + 6 * NSA_KV + 3 * N_HEADS
D_RNN = 1408
LRU_BLOCKS = 8
LRU_C = 8.0
CONV_W = 4
D_FF = 2816
EPS = 1e-6

LANES = 128
SUBLANES = 8
NEG = -1e30
VMEM_LIMIT = 56 * 1024 * 1024

TQ = 128
TK_SEL = 512
TK_WIN = 128
N_CMP_PAD = 256


def _cparams(*sem):
    return pltpu.CompilerParams(dimension_semantics=sem, vmem_limit_bytes=VMEM_LIMIT)


def _rms(x, g):
    ms = jnp.mean(x * x, axis=-1, keepdims=True)
    return x * lax.rsqrt(ms + EPS) * g


def _nt_dot(a, b):
    return lax.dot_general(a, b, (((1,), (1,)), ((), ())), preferred_element_type=F32)


def _norm_matmul_kernel(x_ref, g_ref, w_ref, o_ref, xn_ref):
    @pl.when(pl.program_id(1) == 0)
    def _():
        xn_ref[...] = _rms(x_ref[...], g_ref[...]).astype(BF16)

    o_ref[...] = jnp.dot(xn_ref[...], w_ref[...], preferred_element_type=F32).astype(o_ref.dtype)


def _norm_matmul(x, g, w, *, tm, tn, name):
    m, k = x.shape
    n = w.shape[1]
    return pl.pallas_call(
        _norm_matmul_kernel,
        grid=(m // tm, n // tn),
        in_specs=[
            pl.BlockSpec((tm, k), lambda i, j: (i, 0)),
            pl.BlockSpec((1, k), lambda i, j: (0, 0)),
            pl.BlockSpec((k, tn), lambda i, j: (0, j)),
        ],
        out_specs=pl.BlockSpec((tm, tn), lambda i, j: (i, j)),
        out_shape=jax.ShapeDtypeStruct((m, n), F32),
        scratch_shapes=[pltpu.VMEM((tm, k), BF16)],
        compiler_params=_cparams("parallel", "arbitrary"),
        name=name,
    )(x, g.reshape(1, k), w)


def _matmul_res_kernel(a_ref, w_ref, r_ref, o_ref):
    o_ref[...] = r_ref[...] + jnp.dot(a_ref[...], w_ref[...], preferred_element_type=F32)


def _matmul_res(a, w, res, *, tm, name):
    m, k = a.shape
    n = w.shape[1]
    return pl.pallas_call(
        _matmul_res_kernel,
        grid=(m // tm,),
        in_specs=[
            pl.BlockSpec((tm, k), lambda i: (i, 0)),
            pl.BlockSpec((k, n), lambda i: (0, 0)),
            pl.BlockSpec((tm, n), lambda i: (i, 0)),
        ],
        out_specs=pl.BlockSpec((tm, n), lambda i: (i, 0)),
        out_shape=jax.ShapeDtypeStruct((m, n), F32),
        compiler_params=_cparams("parallel"),
        name=name,
    )(a, w, res)


def _ffn_kernel(x_ref, g_ref, wg_ref, wu_ref, wo_ref, gf_ref, o_ref, xn_ref, acc_ref, *, final_norm):
    j = pl.program_id(1)

    @pl.when(j == 0)
    def _():
        xn_ref[...] = _rms(x_ref[...], g_ref[...]).astype(BF16)
        acc_ref[...] = jnp.zeros_like(acc_ref)

    xn = xn_ref[...]
    gate = jnp.dot(xn, wg_ref[...], preferred_element_type=F32)
    up = jnp.dot(xn, wu_ref[...], preferred_element_type=F32)
    hid = gate * jax.nn.sigmoid(gate) * up
    acc_ref[...] += jnp.dot(hid.astype(BF16), wo_ref[...], preferred_element_type=F32)

    @pl.when(j == pl.num_programs(1) - 1)
    def _():
        y = x_ref[...] + acc_ref[...]
        if final_norm:
            y = _rms(y, gf_ref[...])
        o_ref[...] = y


def _ffn(x, g, w_in, w_out, g_final, *, final_norm, tm, tf, name):
    m, d = x.shape
    nf = D_FF // tf
    return pl.pallas_call(
        functools.partial(_ffn_kernel, final_norm=final_norm),
        grid=(m // tm, nf),
        in_specs=[
            pl.BlockSpec((tm, d), lambda i, j: (i, 0)),
            pl.BlockSpec((1, d), lambda i, j: (0, 0)),
            pl.BlockSpec((d, tf), lambda i, j: (0, j)),
            pl.BlockSpec((d, tf), lambda i, j: (0, j + nf)),
            pl.BlockSpec((tf, d), lambda i, j: (j, 0)),
            pl.BlockSpec((1, d), lambda i, j: (0, 0)),
        ],
        out_specs=pl.BlockSpec((tm, d), lambda i, j: (i, 0)),
        out_shape=jax.ShapeDtypeStruct((m, d), F32),
        scratch_shapes=[pltpu.VMEM((tm, d), BF16), pltpu.VMEM((tm, d), F32)],
        compiler_params=_cparams("parallel", "arbitrary"),
        name=name,
    )(x, g.reshape(1, d), w_in, w_in, w_out, g_final.reshape(1, d))


NSA_IN_PAD = -(-NSA_IN // LANES) * LANES


def _nsa_in_proj_kernel(x_ref, g_ref, w_ref, wvt_ref, q_ref, raw_ref, k_ref, vt_ref, gl_ref):
    xn = _rms(x_ref[...], g_ref[...]).astype(BF16)
    q_ref[...] = jnp.dot(xn, w_ref[:, :NSA_Q], preferred_element_type=F32).astype(q_ref.dtype)
    for i, (dst, slot) in {0: (raw_ref, 0), 1: (raw_ref, 1), 2: (k_ref, 0), 4: (k_ref, 1)}.items():
        col0 = NSA_Q + i * NSA_KV
        part = jnp.dot(xn, w_ref[:, col0:col0 + NSA_KV], preferred_element_type=F32)
        for grp in range(N_GROUPS):
            dst[slot, grp] = part[:, grp * HEAD_DIM:(grp + 1) * HEAD_DIM].astype(dst.dtype)
    for slot in range(2):
        part_t = _nt_dot(wvt_ref[slot], xn)
        for grp in range(N_GROUPS):
            vt_ref[slot, grp] = part_t[grp * HEAD_DIM:(grp + 1) * HEAD_DIM].astype(vt_ref.dtype)
    gl_ref[...] = jnp.dot(xn, w_ref[:, NSA_Q + 6 * NSA_KV:], preferred_element_type=F32)


def _nsa_in_proj(x, g, w, wvt, *, tm):
    b, s, d = x.shape
    sq = pl.Squeezed()
    grouped = lambda dtype: jax.ShapeDtypeStruct((2, b, N_GROUPS, s, HEAD_DIM), dtype)
    grouped_spec = pl.BlockSpec((2, sq, N_GROUPS, tm, HEAD_DIM), lambda i, t: (0, i, 0, t, 0))
    return pl.pallas_call(
        _nsa_in_proj_kernel,
        grid=(b, s // tm),
        in_specs=[
            pl.BlockSpec((sq, tm, d), lambda i, t: (i, t, 0)),
            pl.BlockSpec((1, d), lambda i, t: (0, 0)),
            pl.BlockSpec((d, NSA_IN_PAD), lambda i, t: (0, 0)),
            pl.BlockSpec((2, NSA_KV, d), lambda i, t: (0, 0, 0)),
        ],
        out_specs=[
            pl.BlockSpec((sq, tm, NSA_Q), lambda i, t: (i, t, 0)),
            grouped_spec, grouped_spec,
            pl.BlockSpec((2, sq, N_GROUPS, HEAD_DIM, tm), lambda i, t: (0, i, 0, 0, t)),
            pl.BlockSpec((sq, tm, LANES), lambda i, t: (i, t, 0)),
        ],
        out_shape=[
            jax.ShapeDtypeStruct((b, s, NSA_Q), BF16),
            grouped(F32), grouped(BF16),
            jax.ShapeDtypeStruct((2, b, N_GROUPS, HEAD_DIM, s), BF16),
            jax.ShapeDtypeStruct((b, s, LANES), F32),
        ],
        compiler_params=_cparams("parallel", "parallel"),
        name="nsa_in_proj",
    )(x, g.reshape(1, d), w, wvt)


def _compress_kernel(r_ref, pos_ref, w1_ref, b1_ref, w2_ref, b2_ref, w2t_ref, b2t_ref, o_ref, ot_ref):
    half = CMP_STRIDE * HEAD_DIM
    rows = r_ref[...]
    pos = pos_ref[...]
    lo = (rows + pos[:, :half]).astype(BF16)
    hi = (rows + pos[:, half:]).astype(BF16)
    part_lo = jnp.dot(lo, w1_ref[:half, :], preferred_element_type=F32)
    part_hi = jnp.dot(hi, w1_ref[half:, :], preferred_element_type=F32)
    hid = part_lo + pltpu.roll(part_hi, N_CMP_PAD - 1, axis=0) + b1_ref[...]
    hid = jax.nn.gelu(hid, approximate=True).astype(BF16)
    o_ref[...] = (jnp.dot(hid, w2_ref[...], preferred_element_type=F32) + b2_ref[...]).astype(o_ref.dtype)
    ot_ref[...] = (_nt_dot(w2t_ref[...], hid) + b2t_ref[...]).astype(ot_ref.dtype)


def _compress(raw, pos, w1, b1, w2, b2):
    two, b, g, n, width = raw.shape
    hidden = w1.shape[-1]
    sq = pl.Squeezed()
    per_branch = lambda *shape: pl.BlockSpec((sq,) + shape, lambda c, i, j: (c,) + (0,) * len(shape))
    return pl.pallas_call(
        _compress_kernel,
        grid=(two, b, g),
        in_specs=[
            pl.BlockSpec((sq, sq, sq, n, width), lambda c, i, j: (c, i, j, 0, 0)),
            per_branch(1, 2 * width), per_branch(2 * width, hidden), per_branch(1, hidden),
            per_branch(hidden, HEAD_DIM), per_branch(1, HEAD_DIM),
            per_branch(HEAD_DIM, hidden), per_branch(HEAD_DIM, 1),
        ],
        out_specs=[
            pl.BlockSpec((sq, sq, sq, n, HEAD_DIM), lambda c, i, j: (c, i, j, 0, 0)),
            pl.BlockSpec((sq, sq, sq, HEAD_DIM, n), lambda c, i, j: (c, i, j, 0, 0)),
        ],
        out_shape=[
            jax.ShapeDtypeStruct((two, b, g, n, HEAD_DIM), BF16),
            jax.ShapeDtypeStruct((two, b, g, HEAD_DIM, n), BF16),
        ],
        compiler_params=_cparams("parallel", "parallel", "parallel"),
        name="nsa_compress",
    )(raw, pos, w1, b1.reshape(two, 1, hidden), w2, b2.reshape(two, 1, HEAD_DIM),
      w2.transpose(0, 2, 1), b2.reshape(two, HEAD_DIM, 1))


AUG = 256
SEL_ROW0 = HEAD_DIM
POS_ROW0 = 2 * HEAD_DIM
N_PIECES = 3
POS_ROWS = 16
V_ROWS = 128
NQ = HEADS_PER_GROUP * TQ
BIG = 1e30
assert TQ == TK_WIN and TK_SEL % TQ == 0 and WINDOW % TK_WIN == 0


def _flash_reset(m_ref, acc_ref):
    m_ref[...] = jnp.full_like(m_ref, NEG)
    acc_ref[...] = jnp.zeros_like(acc_ref)


def _scores(kaug, tile, tk, qat_ref):
    k0 = pl.multiple_of(tile * tk, tk)
    return jnp.dot(kaug[pl.ds(k0, tk), :], qat_ref[...], preferred_element_type=F32)


def _flash_update(s, vt_tile, mask, m_ref, acc_ref):
    if mask is not None:
        s = jnp.where(mask, s, NEG)
    m_old = m_ref[...]
    m_new = jnp.maximum(m_old, jnp.max(s, axis=0, keepdims=True))
    p = jnp.exp(s - m_new).astype(BF16)
    acc_ref[...] = jnp.exp(m_old - m_new) * acc_ref[...] + jnp.dot(
        vt_tile, p, preferred_element_type=F32)
    m_ref[...] = m_new


def _flash_result(acc_ref):
    return acc_ref[pl.ds(0, HEAD_DIM), :] * (1.0 / acc_ref[pl.ds(HEAD_DIM, 1), :])


def _nsa_kernel(sl_ref, q_ref, kc_ref, vct_ref, ks_ref, vst_ref, kw_ref, vwt_ref, gl_ref, bg_ref,
                constk_ref, constc_ref, o_ref,
                kaug_s, kaug_w, kaug_c, vt_s, vt_w, vt_c, qat_ref, score_ref, m_ref, acc_ref, out_ref,
                sa_ref, sb_ref, gt_ref):
    grp = pl.program_id(1)
    qi = pl.program_id(2)
    q0 = qi * TQ
    nrep = HEADS_PER_GROUP

    @pl.when(qi == 0)
    def _():
        for kaug, const, k in ((kaug_s, constk_ref, ks_ref), (kaug_w, constk_ref, kw_ref),
                               (kaug_c, constc_ref, kc_ref)):
            kaug[...] = const[...]
            kaug[:, pl.ds(0, HEAD_DIM)] = k[...]
        pad = V_ROWS - HEAD_DIM
        ones_rows = lambda n: jnp.where(
            lax.broadcasted_iota(jnp.int32, (pad, n), 0) == 0, 1.0, 0.0).astype(BF16)
        for c in range(vt_s.shape[0]):
            vt_s[c, pl.ds(0, HEAD_DIM), :] = vst_ref[:, pl.ds(c * TK_SEL, TK_SEL)]
            vt_s[c, pl.ds(HEAD_DIM, pad), :] = ones_rows(TK_SEL)
        for vt, src in ((vt_w, vwt_ref), (vt_c, vct_ref)):
            vt[pl.ds(0, HEAD_DIM), :] = src[...]
            vt[pl.ds(HEAD_DIM, pad), :] = ones_rows(vt.shape[1])

    for half in range(nrep // 2):
        qt = q_ref[:, pl.ds(half * LANES, LANES)].astype(F32).T * (HEAD_DIM ** -0.5)
        for sub in range(2):
            qat_ref[pl.ds(0, HEAD_DIM), pl.ds((2 * half + sub) * TQ, TQ)] = (
                qt[sub * HEAD_DIM:(sub + 1) * HEAD_DIM].astype(BF16))
    qat_ref[pl.ds(SEL_ROW0, HEAD_DIM), :] = jnp.zeros((HEAD_DIM, NQ), BF16)
    piece = lax.broadcasted_iota(jnp.int32, (POS_ROWS, TQ), 0)
    for r in range(nrep):
        tile = jnp.zeros((POS_ROWS, TQ), F32)
        for i in range(2 * N_PIECES):
            tile = jnp.where(piece == i, sl_ref[(grp * nrep + r) * N_PIECES + i % N_PIECES], tile)
        qat_ref[pl.ds(POS_ROW0, POS_ROWS), pl.ds(r * TQ, TQ)] = tile.astype(BF16)
    tail0 = POS_ROW0 + POS_ROWS
    qat_ref[pl.ds(tail0, AUG - tail0), :] = jnp.zeros((AUG - tail0, NQ), BF16)

    t_row = q0 + (lax.broadcasted_iota(jnp.int32, (1, NQ), 1) & (TQ - 1))
    gt_ref[...] = jax.nn.sigmoid(gl_ref[...] + bg_ref[...]).T
    gate_row = lambda br: jnp.concatenate(
        [gt_ref[pl.ds(3 * (grp * nrep + r) + br, 1), :] for r in range(nrep)], axis=1)

    s = jnp.dot(kaug_c[...], qat_ref[...], preferred_element_type=F32)
    cmp_end = CMP_STRIDE * lax.broadcasted_iota(jnp.int32, (N_CMP_PAD, 1), 0) + (L_CMP - 1)
    mask_c = cmp_end <= t_row
    s = jnp.where(mask_c, s, NEG)
    e = jnp.where(mask_c, jnp.exp(s - jnp.max(s, axis=0, keepdims=True)), 0.0)
    d = jnp.sum(e, axis=0, keepdims=True)
    p = e * (1.0 / jnp.where(d > 0, d, 1.0))
    p_sum = sum(p[:, r * TQ:(r + 1) * TQ] for r in range(nrep))
    o_c = jnp.dot(vt_c[...], p.astype(BF16), preferred_element_type=F32)
    out_ref[...] = gate_row(0) * o_c[:HEAD_DIM]

    n_win = WINDOW // TK_WIN
    key_w = lax.broadcasted_iota(jnp.int32, (TK_WIN, 1), 0)
    gate_w = gate_row(2)

    @pl.when(qi >= n_win)
    def _():
        k0 = pl.multiple_of(q0 - WINDOW, TK_WIN)
        s_all = jnp.dot(kaug_w[pl.ds(k0, WINDOW + TQ), :], qat_ref[...], preferred_element_type=F32)
        s_w = [s_all[j * TK_WIN:(j + 1) * TK_WIN] for j in range(n_win + 1)]
        s_w[0] = jnp.where(k0 + key_w > t_row - WINDOW, s_w[0], NEG)
        s_w[-1] = jnp.where(q0 + key_w <= t_row, s_w[-1], NEG)
        m_w = functools.reduce(jnp.maximum, [jnp.max(x, axis=0, keepdims=True) for x in s_w])
        p_w = jnp.concatenate([jnp.exp(x - m_w).astype(BF16) for x in s_w], axis=0)
        acc = jnp.dot(vt_w[:, pl.ds(k0, WINDOW + TQ)], p_w, preferred_element_type=F32)
        out_ref[...] += gate_w * (acc[:HEAD_DIM] * (1.0 / acc[HEAD_DIM:HEAD_DIM + 1]))

    win_vt = lambda tile: vt_w[:, pl.ds(pl.multiple_of(tile * TK_WIN, TK_WIN), TK_WIN)]

    @pl.when(qi < n_win)
    def _():
        _flash_reset(m_ref, acc_ref)
        _flash_update(_scores(kaug_w, qi, TK_WIN, qat_ref), win_vt(qi), q0 + key_w <= t_row,
                      m_ref, acc_ref)

        def earlier(j, carry):
            tile = qi - 1 - j
            _flash_update(_scores(kaug_w, tile, TK_WIN, qat_ref), win_vt(tile), None, m_ref, acc_ref)
            return carry

        lax.fori_loop(0, qi, earlier, 0)
        out_ref[...] += gate_w * _flash_result(acc_ref)

    n_sb = score_ref.shape[0]
    cmp_start = CMP_STRIDE * lax.broadcasted_iota(jnp.int32, (n_sb, N_CMP_PAD), 1)
    sb_start = L_SEL * lax.broadcasted_iota(jnp.int32, (n_sb, N_CMP_PAD), 0)
    overlap_t = ((cmp_start < sb_start + L_SEL) & (cmp_start + L_CMP > sb_start)).astype(F32)
    imp_t = jnp.dot(overlap_t, p_sum, precision=lax.Precision.HIGHEST,
                    preferred_element_type=F32)
    jb = lax.broadcasted_iota(jnp.int32, (n_sb, TQ), 0)
    t_sel = q0 + lax.broadcasted_iota(jnp.int32, (n_sb, TQ), 1)
    valid = jb * L_SEL <= t_sel
    cur = t_sel // L_SEL
    forced = (jb == 0) | (jb == cur) | (jb == cur - 1)
    score = jnp.where(valid, jnp.where(forced, jnp.inf, imp_t), -jnp.inf)
    score_ref[...] = score
    chunks = [score[c * SUBLANES:(c + 1) * SUBLANES] for c in range(n_sb // SUBLANES)]
    ranks = [jnp.zeros((SUBLANES, TQ), F32) for _ in chunks]
    sub_row = lax.broadcasted_iota(jnp.int32, (SUBLANES, TQ), 0)
    for i in range(n_sb):
        row = score_ref[pl.ds(i, 1), :]
        for c, chunk in enumerate(chunks):
            later = jnp.where(row >= chunk, 1.0, 0.0)
            earlier = jnp.where(row > chunk, 1.0, 0.0)
            if i < c * SUBLANES:
                ranks[c] = ranks[c] + later
            elif i >= (c + 1) * SUBLANES:
                ranks[c] = ranks[c] + earlier
            else:
                ranks[c] = ranks[c] + jnp.where(sub_row > i - c * SUBLANES, later, earlier)
    rank = jnp.concatenate(ranks, axis=0)
    sel_bias = jnp.where(valid & (rank < float(N_SELECT)), 0.0, -BIG).astype(BF16)
    for r in range(nrep):
        qat_ref[pl.ds(SEL_ROW0, n_sb), pl.ds(r * TQ, TQ)] = sel_bias

    _flash_reset(m_ref, acc_ref)
    n_full = q0 // TK_SEL
    key_s = lax.broadcasted_iota(jnp.int32, (TK_SEL, 1), 0)
    tail_mask = n_full * TK_SEL + key_s <= t_row
    sel_scores = lambda tile: _scores(kaug_s, tile, TK_SEL, qat_ref)
    sa_ref[...] = sel_scores(0)

    def sel_pair(j, carry):
        tile = 2 * j
        sb_ref[...] = sel_scores(tile + 1)
        _flash_update(sa_ref[...], vt_s[tile], None, m_ref, acc_ref)
        sa_ref[...] = sel_scores(tile + 2)
        _flash_update(sb_ref[...], vt_s[tile + 1], None, m_ref, acc_ref)
        return carry

    lax.fori_loop(0, n_full // 2, sel_pair, 0)

    @pl.when(n_full % 2 == 1)
    def _():
        sb_ref[...] = sel_scores(n_full)
        _flash_update(sa_ref[...], vt_s[n_full - 1], None, m_ref, acc_ref)
        _flash_update(sb_ref[...], vt_s[n_full], tail_mask, m_ref, acc_ref)

    @pl.when(n_full % 2 == 0)
    def _():
        _flash_update(sa_ref[...], vt_s[n_full], tail_mask, m_ref, acc_ref)

    out_ref[...] += gate_row(1) * _flash_result(acc_ref)

    for half in range(nrep // 2):
        slab = jnp.concatenate([out_ref[:, pl.ds(2 * half * TQ, TQ)],
                                out_ref[:, pl.ds((2 * half + 1) * TQ, TQ)]], axis=0)
        o_ref[:, pl.ds(half * LANES, LANES)] = slab.T.astype(o_ref.dtype)


def _nsa_attention(slope_pieces, q, cmp, cmp_t, keys, vals_t, glogit, bgate, constk, constc):
    b, s, _ = q.shape
    g, nrep, dh = N_GROUPS, HEADS_PER_GROUP, HEAD_DIM
    n_sb = s // L_SEL
    sq = pl.Squeezed()
    slot = lambda c, *shape: pl.BlockSpec((sq, sq, sq) + shape, lambda i, j, t: (c, i, j, 0, 0))
    full = lambda *shape: pl.BlockSpec(shape, lambda i, j, t: (0,) * len(shape))
    return pl.pallas_call(
        _nsa_kernel,
        grid=(b, g, s // TQ),
        in_specs=[
            pl.BlockSpec(memory_space=pltpu.SMEM),
            pl.BlockSpec((sq, TQ, nrep * dh), lambda i, j, t: (i, t, j)),
            slot(0, N_CMP_PAD, dh), slot(1, dh, N_CMP_PAD),
            slot(0, s, dh), slot(0, dh, s), slot(1, s, dh), slot(1, dh, s),
            pl.BlockSpec((sq, TQ, LANES), lambda i, j, t: (i, t, 0)),
            full(1, LANES), full(s, AUG), full(N_CMP_PAD, AUG),
        ],
        out_specs=pl.BlockSpec((sq, TQ, nrep * dh), lambda i, j, t: (i, t, j)),
        out_shape=jax.ShapeDtypeStruct((b, s, NSA_Q), BF16),
        scratch_shapes=[
            pltpu.VMEM((s, AUG), BF16), pltpu.VMEM((s, AUG), BF16), pltpu.VMEM((N_CMP_PAD, AUG), BF16),
            pltpu.VMEM((s // TK_SEL, V_ROWS, TK_SEL), BF16), pltpu.VMEM((V_ROWS, s), BF16),
            pltpu.VMEM((V_ROWS, N_CMP_PAD), BF16),
            pltpu.VMEM((AUG, NQ), BF16),
            pltpu.VMEM((n_sb, TQ), F32),
            pltpu.VMEM((1, NQ), F32),
            pltpu.VMEM((V_ROWS, NQ), F32),
            pltpu.VMEM((dh, NQ), F32),
            pltpu.VMEM((TK_SEL, NQ), F32), pltpu.VMEM((TK_SEL, NQ), F32),
            pltpu.VMEM((LANES, TQ), F32),
        ],
        compiler_params=_cparams("arbitrary", "arbitrary", "arbitrary"),
        name="nsa_attention",
    )(slope_pieces, q, cmp, cmp_t, keys, vals_t, keys, vals_t, glogit, bgate, constk, constc)


def _position_pieces(pos):
    hi = (pos // L_SEL) * L_SEL
    return jnp.stack([hi] * N_PIECES + [pos - hi] * N_PIECES, axis=1).astype(F32)


def _nsa_constants(s):
    pos = jnp.arange(s)
    constk = jnp.zeros((s, AUG), F32)
    constk = constk.at[:, SEL_ROW0:SEL_ROW0 + s // L_SEL].set(jax.nn.one_hot(pos // L_SEL, s // L_SEL))
    constk = constk.at[:, POS_ROW0:POS_ROW0 + 2 * N_PIECES].set(_position_pieces(pos))
    cmp_end = CMP_STRIDE * jnp.arange(N_CMP_PAD) + (L_CMP - 1)
    constc = jnp.zeros((N_CMP_PAD, AUG), F32)
    constc = constc.at[:, POS_ROW0:POS_ROW0 + 2 * N_PIECES].set(_position_pieces(cmp_end))
    slopes = 2.0 ** (-8.0 * jnp.arange(1, N_HEADS + 1, dtype=F32) / N_HEADS)
    pieces, rest = [], slopes
    for _ in range(N_PIECES):
        piece = rest.astype(BF16).astype(F32)
        pieces.append(piece)
        rest = rest - piece
    return constk.astype(BF16), constc.astype(BF16), jnp.stack(pieces, axis=1).reshape(-1)


def _nsa_mixer(x, norm_g, w_in, b_gate, cmp_pos, cmp_w1, cmp_b1, cmp_w2, cmp_b2, w_out):
    b, s, d = x.shape
    m = b * s
    g, nrep, dh = N_GROUPS, HEADS_PER_GROUP, HEAD_DIM
    w_in_p = jnp.pad(w_in, ((0, 0), (0, NSA_IN_PAD - NSA_IN))).astype(BF16)
    v_cols = lambda i: w_in_p[:, NSA_Q + i * NSA_KV:NSA_Q + (i + 1) * NSA_KV].T
    q, raw, keys, vals_t, glogit = _nsa_in_proj(x, norm_g, w_in_p, jnp.stack([v_cols(3), v_cols(5)]),
                                                tm=512)
    raw = raw.reshape(2, b, g, s // CMP_STRIDE, CMP_STRIDE * dh)
    cmp, cmp_t = _compress(raw, cmp_pos.reshape(2, 1, L_CMP * dh), cmp_w1.astype(BF16), cmp_b1,
                           cmp_w2.astype(BF16), cmp_b2)
    constk, constc, slope_pieces = _nsa_constants(s)
    bgate = jnp.pad(b_gate, (0, LANES - 3 * N_HEADS)).reshape(1, LANES)
    y = _nsa_attention(slope_pieces, q, cmp, cmp_t, keys, vals_t, glogit, bgate, constk, constc)
    return _matmul_res(y.reshape(m, NSA_Q), w_out.astype(BF16), x.reshape(m, d), tm=512,
                       name="nsa_out_proj").reshape(b, s, d)


TT = 256
SUB = SUBLANES


def _lru_kernel(gate_ref, rec_ref, cw_ref, cb_ref, wax_ref, bax_ref, lam_ref, o_ref,
                ext_ref, a_ref, u_ref, h_ref):
    ti = pl.program_id(1)

    @pl.when(ti == 0)
    def _():
        ext_ref[pl.ds(TT, SUB), :] = jnp.zeros((SUB, D_RNN), F32)
        h_ref[...] = jnp.zeros_like(h_ref)

    ext_ref[pl.ds(0, SUB), :] = ext_ref[pl.ds(TT, SUB), :]
    ext_ref[pl.ds(SUB, TT), :] = rec_ref[...]
    cw = cw_ref[...]
    xr = cb_ref[...] + sum(
        cw[w:w + 1, :] * ext_ref[pl.ds(SUB - (CONV_W - 1) + w, TT), :] for w in range(CONV_W))

    z = jnp.dot(xr.astype(BF16), wax_ref[...], preferred_element_type=F32) + bax_ref[...]
    r_gate = jax.nn.sigmoid(z[:, :D_RNN])
    i_gate = jax.nn.sigmoid(z[:, D_RNN:])
    lam = lam_ref[...]
    softplus_neg = jnp.maximum(-lam, 0.0) + jnp.log1p(jnp.exp(-jnp.abs(lam)))
    log_a = -LRU_C * r_gate * softplus_neg
    a = jnp.exp(log_a)
    a_ref[...] = a
    u_ref[...] = jnp.sqrt(jnp.maximum(1.0 - a * a, 0.0)) * (i_gate * xr)

    row = lax.broadcasted_iota(jnp.int32, (SUB, D_RNN), 0)

    def scan_rows(c, h_prev):
        r0 = pl.multiple_of(c * SUB, SUB)
        a_c = a_ref[pl.ds(r0, SUB), :]
        u_c = u_ref[pl.ds(r0, SUB), :]
        shift = 1
        while shift < SUB:
            keep = row >= shift
            u_c = u_c + a_c * jnp.where(keep, pltpu.roll(u_c, shift, axis=0), 0.0)
            a_c = a_c * jnp.where(keep, pltpu.roll(a_c, shift, axis=0), 1.0)
            shift *= 2
        h_c = u_c + a_c * h_prev
        u_ref[pl.ds(r0, SUB), :] = h_c
        return jnp.broadcast_to(h_c[SUB - 1:SUB, :], (SUB, D_RNN))

    h_ref[...] = lax.fori_loop(0, TT // SUB, scan_rows, h_ref[...])
    o_ref[...] = (u_ref[...] * jax.nn.gelu(gate_ref[...], approximate=True)).astype(o_ref.dtype)


def _lru_core(proj, conv_w, conv_b, wax, bax, lam, b, s):
    sq = pl.Squeezed()
    vec = lambda n: pl.BlockSpec((1, n), lambda i, t: (0, 0))
    return pl.pallas_call(
        _lru_kernel,
        grid=(b, s // TT),
        in_specs=[
            pl.BlockSpec((sq, TT, D_RNN), lambda i, t: (i, t, 0)),
            pl.BlockSpec((sq, TT, D_RNN), lambda i, t: (i, t, 1)),
            pl.BlockSpec((CONV_W, D_RNN), lambda i, t: (0, 0)),
            vec(D_RNN),
            pl.BlockSpec((D_RNN, 2 * D_RNN), lambda i, t: (0, 0)),
            vec(2 * D_RNN),
            vec(D_RNN),
        ],
        out_specs=pl.BlockSpec((sq, TT, D_RNN), lambda i, t: (i, t, 0)),
        out_shape=jax.ShapeDtypeStruct((b, s, D_RNN), BF16),
        scratch_shapes=[
            pltpu.VMEM((TT + SUB, D_RNN), F32),
            pltpu.VMEM((TT, D_RNN), F32),
            pltpu.VMEM((TT, D_RNN), F32),
            pltpu.VMEM((SUB, D_RNN), F32),
        ],
        compiler_params=_cparams("parallel", "arbitrary"),
        name="lru_core",
    )(proj, proj, conv_w, conv_b.reshape(1, -1), wax, bax.reshape(1, -1), lam.reshape(1, -1))


def _lru_mixer(x, norm_g, w_in, conv_w, conv_b, w_a, b_a, w_x, b_x, lam, w_out):
    b, s, d = x.shape
    m = b * s
    x2 = x.reshape(m, d)
    proj = _norm_matmul(x2, norm_g, w_in.astype(BF16), tm=512, tn=D_RNN, name="lru_in_proj")
    dense = lambda w: jax.scipy.linalg.block_diag(*[w[i] for i in range(LRU_BLOCKS)])
    wax = jnp.concatenate([dense(w_a), dense(w_x)], axis=1).astype(BF16)
    bax = jnp.concatenate([b_a, b_x])
    hg = _lru_core(proj.reshape(b, s, 2 * D_RNN), conv_w, conv_b, wax, bax, lam, b, s)
    return _matmul_res(hg.reshape(m, D_RNN), w_out.astype(BF16), x2, tm=512,
                       name="lru_out_proj").reshape(b, s, d)


def kernel(x, norm_mix, norm_ffn, norm_final, nsa_w_in, nsa_b_gate, nsa_cmp_pos, nsa_cmp_w1,
           nsa_cmp_b1, nsa_cmp_w2, nsa_cmp_b2, nsa_w_out, lru_w_in, lru_conv_w, lru_conv_b,
           lru_w_a, lru_b_a, lru_w_x, lru_b_x, lru_lambda, lru_w_out, ffn_w_in, ffn_w_out):
    b, s, d = x.shape
    ffn = functools.partial(_ffn, tm=1024, tf=256)
    x = _nsa_mixer(x, norm_mix[0], nsa_w_in[0], nsa_b_gate[0], nsa_cmp_pos[0], nsa_cmp_w1[0],
                   nsa_cmp_b1[0], nsa_cmp_w2[0], nsa_cmp_b2[0], nsa_w_out[0])
    x = ffn(x.reshape(b * s, d), norm_ffn[0], ffn_w_in[0].astype(BF16), ffn_w_out[0].astype(BF16),
            norm_final, final_norm=False, name="ffn0").reshape(b, s, d)
    x = _lru_mixer(x, norm_mix[1], lru_w_in[0], lru_conv_w[0], lru_conv_b[0], lru_w_a[0], lru_b_a[0],
                   lru_w_x[0], lru_b_x[0], lru_lambda[0], lru_w_out[0])
    x = ffn(x.reshape(b * s, d), norm_ffn[1], ffn_w_in[1].astype(BF16), ffn_w_out[1].astype(BF16),
            norm_final, final_norm=True, name="ffn1").reshape(b, s, d)
    return x
```

```python
import functools

import jax
import jax.numpy as jnp
from jax import lax
from jax.experimental import pallas as pl
from jax.experimental.pallas import tpu as pltpu

F32 = jnp.float32
BF16 = jnp.bfloat16

D_MODEL = 1024
N_HEADS = 16
N_GROUPS = 4
HEADS_PER_GROUP = N_HEADS // N_GROUPS
HEAD_DIM = 64
L_CMP = 32
CMP_STRIDE = 16
L_SEL = 64
N_SELECT = 16
WINDOW = 512
NSA_Q = N_HEADS * HEAD_DIM
NSA_KV = N_GROUPS * HEAD_DIM
NSA_IN = NSA_Q + 6 * NSA_KV + 3 * N_HEADS
D_RNN = 1408
LRU_BLOCKS = 8
LRU_C = 8.0
CONV_W = 4
D_FF = 2816
EPS = 1e-6

LANES = 128
SUBLANES = 8
NEG = -1e30
VMEM_LIMIT = 56 * 1024 * 1024

TQ = 128
TK_SEL = 512
TK_WIN = 128
N_CMP_PAD = 256


def _cparams(*sem):
    return pltpu.CompilerParams(dimension_semantics=sem, vmem_limit_bytes=VMEM_LIMIT)


def _rms(x, g):
    ms = jnp.mean(x * x, axis=-1, keepdims=True)
    return x * lax.rsqrt(ms + EPS) * g


def _nt_dot(a, b):
    return lax.dot_general(a, b, (((1,), (1,)), ((), ())), preferred_element_type=F32)


def _norm_matmul_kernel(x_ref, g_ref, w_ref, o_ref, xn_ref):
    @pl.when(pl.program_id(1) == 0)
    def _():
        xn_ref[...] = _rms(x_ref[...], g_ref[...]).astype(BF16)

    o_ref[...] = jnp.dot(xn_ref[...], w_ref[...], preferred_element_type=F32).astype(o_ref.dtype)


def _norm_matmul(x, g, w, *, tm, tn, name):
    m, k = x.shape
    n = w.shape[1]
    return pl.pallas_call(
        _norm_matmul_kernel,
        grid=(m // tm, n // tn),
        in_specs=[
            pl.BlockSpec((tm, k), lambda i, j: (i, 0)),
            pl.BlockSpec((1, k), lambda i, j: (0, 0)),
            pl.BlockSpec((k, tn), lambda i, j: (0, j)),
        ],
        out_specs=pl.BlockSpec((tm, tn), lambda i, j: (i, j)),
        out_shape=jax.ShapeDtypeStruct((m, n), F32),
        scratch_shapes=[pltpu.VMEM((tm, k), BF16)],
        compiler_params=_cparams("parallel", "arbitrary"),
        name=name,
    )(x, g.reshape(1, k), w)


def _matmul_res_kernel(a_ref, w_ref, r_ref, o_ref):
    o_ref[...] = r_ref[...] + jnp.dot(a_ref[...], w_ref[...], preferred_element_type=F32)


def _matmul_res(a, w, res, *, tm, name):
    m, k = a.shape
    n = w.shape[1]
    return pl.pallas_call(
        _matmul_res_kernel,
        grid=(m // tm,),
        in_specs=[
            pl.BlockSpec((tm, k), lambda i: (i, 0)),
            pl.BlockSpec((k, n), lambda i: (0, 0)),
            pl.BlockSpec((tm, n), lambda i: (i, 0)),
        ],
        out_specs=pl.BlockSpec((tm, n), lambda i: (i, 0)),
        out_shape=jax.ShapeDtypeStruct((m, n), F32),
        compiler_params=_cparams("parallel"),
        name=name,
    )(a, w, res)


def _ffn_kernel(x_ref, g_ref, wg_ref, wu_ref, wo_ref, gf_ref, o_ref, xn_ref, acc_ref, *, final_norm):
    j = pl.program_id(1)

    @pl.when(j == 0)
    def _():
        xn_ref[...] = _rms(x_ref[...], g_ref[...]).astype(BF16)
        acc_ref[...] = jnp.zeros_like(acc_ref)

    xn = xn_ref[...]
    gate = jnp.dot(xn, wg_ref[...], preferred_element_type=F32)
    up = jnp.dot(xn, wu_ref[...], preferred_element_type=F32)
    hid = gate * jax.nn.sigmoid(gate) * up
    acc_ref[...] += jnp.dot(hid.astype(BF16), wo_ref[...], preferred_element_type=F32)

    @pl.when(j == pl.num_programs(1) - 1)
    def _():
        y = x_ref[...] + acc_ref[...]
        if final_norm:
            y = _rms(y, gf_ref[...])
        o_ref[...] = y


def _ffn(x, g, w_in, w_out, g_final, *, final_norm, tm, tf, name):
    m, d = x.shape
    nf = D_FF // tf
    return pl.pallas_call(
        functools.partial(_ffn_kernel, final_norm=final_norm),
        grid=(m // tm, nf),
        in_specs=[
            pl.BlockSpec((tm, d), lambda i, j: (i, 0)),
            pl.BlockSpec((1, d), lambda i, j: (0, 0)),
            pl.BlockSpec((d, tf), lambda i, j: (0, j)),
            pl.BlockSpec((d, tf), lambda i, j: (0, j + nf)),
            pl.BlockSpec((tf, d), lambda i, j: (j, 0)),
            pl.BlockSpec((1, d), lambda i, j: (0, 0)),
        ],
        out_specs=pl.BlockSpec((tm, d), lambda i, j: (i, 0)),
        out_shape=jax.ShapeDtypeStruct((m, d), F32),
        scratch_shapes=[pltpu.VMEM((tm, d), BF16), pltpu.VMEM((tm, d), F32)],
        compiler_params=_cparams("parallel", "arbitrary"),
        name=name,
    )(x, g.reshape(1, d), w_in, w_in, w_out, g_final.reshape(1, d))


NSA_IN_PAD = -(-NSA_IN // LANES) * LANES


def _nsa_in_proj_kernel(x_ref, g_ref, w_ref, wvt_ref, q_ref, raw_ref, k_ref, vt_ref, gl_ref):
    xn = _rms(x_ref[...], g_ref[...]).astype(BF16)
    q_ref[...] = jnp.dot(xn, w_ref[:, :NSA_Q], preferred_element_type=F32).astype(q_ref.dtype)
    for i, (dst, slot) in {0: (raw_ref, 0), 1: (raw_ref, 1), 2: (k_ref, 0), 4: (k_ref, 1)}.items():
        col0 = NSA_Q + i * NSA_KV
        part = jnp.dot(xn, w_ref[:, col0:col0 + NSA_KV], preferred_element_type=F32)
        for grp in range(N_GROUPS):
            dst[slot, grp] = part[:, grp * HEAD_DIM:(grp + 1) * HEAD_DIM].astype(dst.dtype)
    for slot in range(2):
        part_t = _nt_dot(wvt_ref[slot], xn)
        for grp in range(N_GROUPS):
            vt_ref[slot, grp] = part_t[grp * HEAD_DIM:(grp + 1) * HEAD_DIM].astype(vt_ref.dtype)
    gl_ref[...] = jnp.dot(xn, w_ref[:, NSA_Q + 6 * NSA_KV:], preferred_element_type=F32)


def _nsa_in_proj(x, g, w, wvt, *, tm):
    b, s, d = x.shape
    sq = pl.Squeezed()
    grouped = lambda dtype: jax.ShapeDtypeStruct((2, b, N_GROUPS, s, HEAD_DIM), dtype)
    grouped_spec = pl.BlockSpec((2, sq, N_GROUPS, tm, HEAD_DIM), lambda i, t: (0, i, 0, t, 0))
    return pl.pallas_call(
        _nsa_in_proj_kernel,
        grid=(b, s // tm),
        in_specs=[
            pl.BlockSpec((sq, tm, d), lambda i, t: (i, t, 0)),
            pl.BlockSpec((1, d), lambda i, t: (0, 0)),
            pl.BlockSpec((d, NSA_IN_PAD), lambda i, t: (0, 0)),
            pl.BlockSpec((2, NSA_KV, d), lambda i, t: (0, 0, 0)),
        ],
        out_specs=[
            pl.BlockSpec((sq, tm, NSA_Q), lambda i, t: (i, t, 0)),
            grouped_spec, grouped_spec,
            pl.BlockSpec((2, sq, N_GROUPS, HEAD_DIM, tm), lambda i, t: (0, i, 0, 0, t)),
            pl.BlockSpec((sq, tm, LANES), lambda i, t: (i, t, 0)),
        ],
        out_shape=[
            jax.ShapeDtypeStruct((b, s, NSA_Q), BF16),
            grouped(F32), grouped(BF16),
            jax.ShapeDtypeStruct((2, b, N_GROUPS, HEAD_DIM, s), BF16),
            jax.ShapeDtypeStruct((b, s, LANES), F32),
        ],
        compiler_params=_cparams("parallel", "parallel"),
        name="nsa_in_proj",
    )(x, g.reshape(1, d), w, wvt)


def _compress_kernel(r_ref, pos_ref, w1_ref, b1_ref, w2_ref, b2_ref, w2t_ref, b2t_ref, o_ref, ot_ref):
    half = CMP_STRIDE * HEAD_DIM
    rows = r_ref[...]
    pos = pos_ref[...]
    lo = (rows + pos[:, :half]).astype(BF16)
    hi = (rows + pos[:, half:]).astype(BF16)
    part_lo = jnp.dot(lo, w1_ref[:half, :], preferred_element_type=F32)
    part_hi = jnp.dot(hi, w1_ref[half:, :], preferred_element_type=F32)
    hid = part_lo + pltpu.roll(part_hi, N_CMP_PAD - 1, axis=0) + b1_ref[...]
    hid = jax.nn.gelu(hid, approximate=True).astype(BF16)
    o_ref[...] = (jnp.dot(hid, w2_ref[...], preferred_element_type=F32) + b2_ref[...]).astype(o_ref.dtype)
    ot_ref[...] = (_nt_dot(w2t_ref[...], hid) + b2t_ref[...]).astype(ot_ref.dtype)


def _compress(raw, pos, w1, b1, w2, b2):
    two, b, g, n, width = raw.shape
    hidden = w1.shape[-1]
    sq = pl.Squeezed()
    per_branch = lambda *shape: pl.BlockSpec((sq,) + shape, lambda c, i, j: (c,) + (0,) * len(shape))
    return pl.pallas_call(
        _compress_kernel,
        grid=(two, b, g),
        in_specs=[
            pl.BlockSpec((sq, sq, sq, n, width), lambda c, i, j: (c, i, j, 0, 0)),
            per_branch(1, 2 * width), per_branch(2 * width, hidden), per_branch(1, hidden),
            per_branch(hidden, HEAD_DIM), per_branch(1, HEAD_DIM),
            per_branch(HEAD_DIM, hidden), per_branch(HEAD_DIM, 1),
        ],
        out_specs=[
            pl.BlockSpec((sq, sq, sq, n, HEAD_DIM), lambda c, i, j: (c, i, j, 0, 0)),
            pl.BlockSpec((sq, sq, sq, HEAD_DIM, n), lambda c, i, j: (c, i, j, 0, 0)),
        ],
        out_shape=[
            jax.ShapeDtypeStruct((two, b, g, n, HEAD_DIM), BF16),
            jax.ShapeDtypeStruct((two, b, g, HEAD_DIM, n), BF16),
        ],
        compiler_params=_cparams("parallel", "parallel", "parallel"),
        name="nsa_compress",
    )(raw, pos, w1, b1.reshape(two, 1, hidden), w2, b2.reshape(two, 1, HEAD_DIM),
      w2.transpose(0, 2, 1), b2.reshape(two, HEAD_DIM, 1))


AUG = 256
SEL_ROW0 = HEAD_DIM
POS_ROW0 = 2 * HEAD_DIM
N_PIECES = 3
POS_ROWS = 16
V_ROWS = 128
NQ = HEADS_PER_GROUP * TQ
BIG = 1e30
assert TQ == TK_WIN and TK_SEL % TQ == 0 and WINDOW % TK_WIN == 0


def _flash_reset(m_ref, acc_ref):
    m_ref[...] = jnp.full_like(m_ref, NEG)
    acc_ref[...] = jnp.zeros_like(acc_ref)


def _scores(kaug, tile, tk, qat_ref):
    k0 = pl.multiple_of(tile * tk, tk)
    return jnp.dot(kaug[pl.ds(k0, tk), :], qat_ref[...], preferred_element_type=F32)


def _flash_update(s, vt_tile, mask, m_ref, acc_ref):
    if mask is not None:
        s = jnp.where(mask, s, NEG)
    m_old = m_ref[...]
    m_new = jnp.maximum(m_old, jnp.max(s, axis=0, keepdims=True))
    p = jnp.exp(s - m_new).astype(BF16)
    acc_ref[...] = jnp.exp(m_old - m_new) * acc_ref[...] + jnp.dot(
        vt_tile, p, preferred_element_type=F32)
    m_ref[...] = m_new


def _flash_result(acc_ref):
    return acc_ref[pl.ds(0, HEAD_DIM), :] * (1.0 / acc_ref[pl.ds(HEAD_DIM, 1), :])


def _nsa_kernel(sl_ref, q_ref, kc_ref, vct_ref, ks_ref, vst_ref, kw_ref, vwt_ref, gl_ref, bg_ref,
                constk_ref, constc_ref, o_ref,
                kaug_s, kaug_w, kaug_c, vt_s, vt_w, vt_c, qat_ref, score_ref, m_ref, acc_ref, out_ref,
                sa_ref, sb_ref, gt_ref):
    grp = pl.program_id(1)
    qi = pl.program_id(2)
    q0 = qi * TQ
    nrep = HEADS_PER_GROUP

    @pl.when(qi == 0)
    def _():
        for kaug, const, k in ((kaug_s, constk_ref, ks_ref), (kaug_w, constk_ref, kw_ref),
                               (kaug_c, constc_ref, kc_ref)):
            kaug[...] = const[...]
            kaug[:, pl.ds(0, HEAD_DIM)] = k[...]
        pad = V_ROWS - HEAD_DIM
        ones_rows = lambda n: jnp.where(
            lax.broadcasted_iota(jnp.int32, (pad, n), 0) == 0, 1.0, 0.0).astype(BF16)
        for c in range(vt_s.shape[0]):
            vt_s[c, pl.ds(0, HEAD_DIM), :] = vst_ref[:, pl.ds(c * TK_SEL, TK_SEL)]
            vt_s[c, pl.ds(HEAD_DIM, pad), :] = ones_rows(TK_SEL)
        for vt, src in ((vt_w, vwt_ref), (vt_c, vct_ref)):
            vt[pl.ds(0, HEAD_DIM), :] = src[...]
            vt[pl.ds(HEAD_DIM, pad), :] = ones_rows(vt.shape[1])

    for half in range(nrep // 2):
        qt = q_ref[:, pl.ds(half * LANES, LANES)].astype(F32).T * (HEAD_DIM ** -0.5)
        for sub in range(2):
            qat_ref[pl.ds(0, HEAD_DIM), pl.ds((2 * half + sub) * TQ, TQ)] = (
                qt[sub * HEAD_DIM:(sub + 1) * HEAD_DIM].astype(BF16))
    qat_ref[pl.ds(SEL_ROW0, HEAD_DIM), :] = jnp.zeros((HEAD_DIM, NQ), BF16)
    piece = lax.broadcasted_iota(jnp.int32, (POS_ROWS, TQ), 0)
    for r in range(nrep):
        tile = jnp.zeros((POS_ROWS, TQ), F32)
        for i in range(2 * N_PIECES):
            tile = jnp.where(piece == i, sl_ref[(grp * nrep + r) * N_PIECES + i % N_PIECES], tile)
        qat_ref[pl.ds(POS_ROW0, POS_ROWS), pl.ds(r * TQ, TQ)] = tile.astype(BF16)
    tail0 = POS_ROW0 + POS_ROWS
    qat_ref[pl.ds(tail0, AUG - tail0), :] = jnp.zeros((AUG - tail0, NQ), BF16)

    t_row = q0 + (lax.broadcasted_iota(jnp.int32, (1, NQ), 1) & (TQ - 1))
    gt_ref[...] = jax.nn.sigmoid(gl_ref[...] + bg_ref[...]).T
    gate_row = lambda br: jnp.concatenate(
        [gt_ref[pl.ds(3 * (grp * nrep + r) + br, 1), :] for r in range(nrep)], axis=1)

    s = jnp.dot(kaug_c[...], qat_ref[...], preferred_element_type=F32)
    cmp_end = CMP_STRIDE * lax.broadcasted_iota(jnp.int32, (N_CMP_PAD, 1), 0) + (L_CMP - 1)
    mask_c = cmp_end <= t_row
    s = jnp.where(mask_c, s, NEG)
    e = jnp.where(mask_c, jnp.exp(s - jnp.max(s, axis=0, keepdims=True)), 0.0)
    d = jnp.sum(e, axis=0, keepdims=True)
    p = e * (1.0 / jnp.where(d > 0, d, 1.0))
    p_sum = sum(p[:, r * TQ:(r + 1) * TQ] for r in range(nrep))
    o_c = jnp.dot(vt_c[...], p.astype(BF16), preferred_element_type=F32)
    out_ref[...] = gate_row(0) * o_c[:HEAD_DIM]

    n_win = WINDOW // TK_WIN
    key_w = lax.broadcasted_iota(jnp.int32, (TK_WIN, 1), 0)
    gate_w = gate_row(2)

    @pl.when(qi >= n_win)
    def _():
        k0 = pl.multiple_of(q0 - WINDOW, TK_WIN)
        s_all = jnp.dot(kaug_w[pl.ds(k0, WINDOW + TQ), :], qat_ref[...], preferred_element_type=F32)
        s_w = [s_all[j * TK_WIN:(j + 1) * TK_WIN] for j in range(n_win + 1)]
        s_w[0] = jnp.where(k0 + key_w > t_row - WINDOW, s_w[0], NEG)
        s_w[-1] = jnp.where(q0 + key_w <= t_row, s_w[-1], NEG)
        m_w = functools.reduce(jnp.maximum, [jnp.max(x, axis=0, keepdims=True) for x in s_w])
        p_w = jnp.concatenate([jnp.exp(x - m_w).astype(BF16) for x in s_w], axis=0)
        acc = jnp.dot(vt_w[:, pl.ds(k0, WINDOW + TQ)], p_w, preferred_element_type=F32)
        out_ref[...] += gate_w * (acc[:HEAD_DIM] * (1.0 / acc[HEAD_DIM:HEAD_DIM + 1]))

    win_vt = lambda tile: vt_w[:, pl.ds(pl.multiple_of(tile * TK_WIN, TK_WIN), TK_WIN)]

    @pl.when(qi < n_win)
    def _():
        _flash_reset(m_ref, acc_ref)
        _flash_update(_scores(kaug_w, qi, TK_WIN, qat_ref), win_vt(qi), q0 + key_w <= t_row,
                      m_ref, acc_ref)

        def earlier(j, carry):
            tile = qi - 1 - j
            _flash_update(_scores(kaug_w, tile, TK_WIN, qat_ref), win_vt(tile), None, m_ref, acc_ref)
            return carry

        lax.fori_loop(0, qi, earlier, 0)
        out_ref[...] += gate_w * _flash_result(acc_ref)

    n_sb = score_ref.shape[0]
    cmp_start = CMP_STRIDE * lax.broadcasted_iota(jnp.int32, (n_sb, N_CMP_PAD), 1)
    sb_start = L_SEL * lax.broadcasted_iota(jnp.int32, (n_sb, N_CMP_PAD), 0)
    overlap_t = ((cmp_start < sb_start + L_SEL) & (cmp_start + L_CMP > sb_start)).astype(F32)
    imp_t = jnp.dot(overlap_t, p_sum, precision=lax.Precision.HIGHEST,
                    preferred_element_type=F32)
    jb = lax.broadcasted_iota(jnp.int32, (n_sb, TQ), 0)
    t_sel = q0 + lax.broadcasted_iota(jnp.int32, (n_sb, TQ), 1)
    valid = jb * L_SEL <= t_sel
    cur = t_sel // L_SEL
    forced = (jb == 0) | (jb == cur) | (jb == cur - 1)
    score = jnp.where(valid, jnp.where(forced, jnp.inf, imp_t), -jnp.inf)
    score_ref[...] = score
    chunks = [score[c * SUBLANES:(c + 1) * SUBLANES] for c in range(n_sb // SUBLANES)]
    ranks = [jnp.zeros((SUBLANES, TQ), F32) for _ in chunks]
    sub_row = lax.broadcasted_iota(jnp.int32, (SUBLANES, TQ), 0)
    for i in range(n_sb):
        row = score_ref[pl.ds(i, 1), :]
        for c, chunk in enumerate(chunks):
            later = jnp.where(row >= chunk, 1.0, 0.0)
            earlier = jnp.where(row > chunk, 1.0, 0.0)
            if i < c * SUBLANES:
                ranks[c] = ranks[c] + later
            elif i >= (c + 1) * SUBLANES:
                ranks[c] = ranks[c] + earlier
            else:
                ranks[c] = ranks[c] + jnp.where(sub_row > i - c * SUBLANES, later, earlier)
    rank = jnp.concatenate(ranks, axis=0)
    sel_bias = jnp.where(valid & (rank < float(N_SELECT)), 0.0, -BIG).astype(BF16)
    for r in range(nrep):
        qat_ref[pl.ds(SEL_ROW0, n_sb), pl.ds(r * TQ, TQ)] = sel_bias

    _flash_reset(m_ref, acc_ref)
    n_full = q0 // TK_SEL
    key_s = lax.broadcasted_iota(jnp.int32, (TK_SEL, 1), 0)
    tail_mask = n_full * TK_SEL + key_s <= t_row
    sel_scores = lambda tile: _scores(kaug_s, tile, TK_SEL, qat_ref)
    sa_ref[...] = sel_scores(0)

    def sel_pair(j, carry):
        tile = 2 * j
        sb_ref[...] = sel_scores(tile + 1)
        _flash_update(sa_ref[...], vt_s[tile], None, m_ref, acc_ref)
        sa_ref[...] = sel_scores(tile + 2)
        _flash_update(sb_ref[...], vt_s[tile + 1], None, m_ref, acc_ref)
        return carry

    lax.fori_loop(0, n_full // 2, sel_pair, 0)

    @pl.when(n_full % 2 == 1)
    def _():
        sb_ref[...] = sel_scores(n_full)
        _flash_update(sa_ref[...], vt_s[n_full - 1], None, m_ref, acc_ref)
        _flash_update(sb_ref[...], vt_s[n_full], tail_mask, m_ref, acc_ref)

    @pl.when(n_full % 2 == 0)
    def _():
        _flash_update(sa_ref[...], vt_s[n_full], tail_mask, m_ref, acc_ref)

    out_ref[...] += gate_row(1) * _flash_result(acc_ref)

    for half in range(nrep // 2):
        slab = jnp.concatenate([out_ref[:, pl.ds(2 * half * TQ, TQ)],
                                out_ref[:, pl.ds((2 * half + 1) * TQ, TQ)]], axis=0)
        o_ref[:, pl.ds(half * LANES, LANES)] = slab.T.astype(o_ref.dtype)


def _nsa_attention(slope_pieces, q, cmp, cmp_t, keys, vals_t, glogit, bgate, constk, constc):
    b, s, _ = q.shape
    g, nrep, dh = N_GROUPS, HEADS_PER_GROUP, HEAD_DIM
    n_sb = s // L_SEL
    sq = pl.Squeezed()
    slot = lambda c, *shape: pl.BlockSpec((sq, sq, sq) + shape, lambda i, j, t: (c, i, j, 0, 0))
    full = lambda *shape: pl.BlockSpec(shape, lambda i, j, t: (0,) * len(shape))
    return pl.pallas_call(
        _nsa_kernel,
        grid=(b, g, s // TQ),
        in_specs=[
            pl.BlockSpec(memory_space=pltpu.SMEM),
            pl.BlockSpec((sq, TQ, nrep * dh), lambda i, j, t: (i, t, j)),
            slot(0, N_CMP_PAD, dh), slot(1, dh, N_CMP_PAD),
            slot(0, s, dh), slot(0, dh, s), slot(1, s, dh), slot(1, dh, s),
            pl.BlockSpec((sq, TQ, LANES), lambda i, j, t: (i, t, 0)),
            full(1, LANES), full(s, AUG), full(N_CMP_PAD, AUG),
        ],
        out_specs=pl.BlockSpec((sq, TQ, nrep * dh), lambda i, j, t: (i, t, j)),
        out_shape=jax.ShapeDtypeStruct((b, s, NSA_Q), BF16),
        scratch_shapes=[
            pltpu.VMEM((s, AUG), BF16), pltpu.VMEM((s, AUG), BF16), pltpu.VMEM((N_CMP_PAD, AUG), BF16),
            pltpu.VMEM((s // TK_SEL, V_ROWS, TK_SEL), BF16), pltpu.VMEM((V_ROWS, s), BF16),
            pltpu.VMEM((V_ROWS, N_CMP_PAD), BF16),
            pltpu.VMEM((AUG, NQ), BF16),
            pltpu.VMEM((n_sb, TQ), F32),
            pltpu.VMEM((1, NQ), F32),
            pltpu.VMEM((V_ROWS, NQ), F32),
            pltpu.VMEM((dh, NQ), F32),
            pltpu.VMEM((TK_SEL, NQ), F32), pltpu.VMEM((TK_SEL, NQ), F32),
            pltpu.VMEM((LANES, TQ), F32),
        ],
        compiler_params=_cparams("arbitrary", "arbitrary", "arbitrary"),
        name="nsa_attention",
    )(slope_pieces, q, cmp, cmp_t, keys, vals_t, keys, vals_t, glogit, bgate, constk, constc)


def _position_pieces(pos):
    hi = (pos // L_SEL) * L_SEL
    return jnp.stack([hi] * N_PIECES + [pos - hi] * N_PIECES, axis=1).astype(F32)


def _nsa_constants(s):
    pos = jnp.arange(s)
    constk = jnp.zeros((s, AUG), F32)
    constk = constk.at[:, SEL_ROW0:SEL_ROW0 + s // L_SEL].set(jax.nn.one_hot(pos // L_SEL, s // L_SEL))
    constk = constk.at[:, POS_ROW0:POS_ROW0 + 2 * N_PIECES].set(_position_pieces(pos))
    cmp_end = CMP_STRIDE * jnp.arange(N_CMP_PAD) + (L_CMP - 1)
    constc = jnp.zeros((N_CMP_PAD, AUG), F32)
    constc = constc.at[:, POS_ROW0:POS_ROW0 + 2 * N_PIECES].set(_position_pieces(cmp_end))
    slopes = 2.0 ** (-8.0 * jnp.arange(1, N_HEADS + 1, dtype=F32) / N_HEADS)
    pieces, rest = [], slopes
    for _ in range(N_PIECES):
        piece = rest.astype(BF16).astype(F32)
        pieces.append(piece)
        rest = rest - piece
    return constk.astype(BF16), constc.astype(BF16), jnp.stack(pieces, axis=1).reshape(-1)


def _nsa_mixer(x, norm_g, w_in, b_gate, cmp_pos, cmp_w1, cmp_b1, cmp_w2, cmp_b2, w_out):
    b, s, d = x.shape
    m = b * s
    g, nrep, dh = N_GROUPS, HEADS_PER_GROUP, HEAD_DIM
    w_in_p = jnp.pad(w_in, ((0, 0), (0, NSA_IN_PAD - NSA_IN))).astype(BF16)
    v_cols = lambda i: w_in_p[:, NSA_Q + i * NSA_KV:NSA_Q + (i + 1) * NSA_KV].T
    q, raw, keys, vals_t, glogit = _nsa_in_proj(x, norm_g, w_in_p, jnp.stack([v_cols(3), v_cols(5)]),
                                                tm=512)
    raw = raw.reshape(2, b, g, s // CMP_STRIDE, CMP_STRIDE * dh)
    cmp, cmp_t = _compress(raw, cmp_pos.reshape(2, 1, L_CMP * dh), cmp_w1.astype(BF16), cmp_b1,
                           cmp_w2.astype(BF16), cmp_b2)
    constk, constc, slope_pieces = _nsa_constants(s)
    bgate = jnp.pad(b_gate, (0, LANES - 3 * N_HEADS)).reshape(1, LANES)
    y = _nsa_attention(slope_pieces, q, cmp, cmp_t, keys, vals_t, glogit, bgate, constk, constc)
    return _matmul_res(y.reshape(m, NSA_Q), w_out.astype(BF16), x.reshape(m, d), tm=512,
                       name="nsa_out_proj").reshape(b, s, d)


TT = 256
SUB = SUBLANES


GATE_BAND = 256


def _gate_bands():
    block = D_RNN // LRU_BLOCKS
    bands = []
    for c0 in range(0, D_RNN, GATE_BAND):
        width = min(GATE_BAND, D_RNN - c0)
        k_lo = (c0 // block) * block // LANES * LANES
        k_hi = -(-(((c0 + width - 1) // block + 1) * block) // LANES) * LANES
        bands.append((k_lo, min(k_hi, D_RNN), c0, width))
    return tuple(bands)


GATE_BANDS = _gate_bands()


def _pack_gate_weights(w_a, w_x):
    dense = lambda w: jax.scipy.linalg.block_diag(*[w[i] for i in range(LRU_BLOCKS)])
    slabs = []
    for k_lo, k_hi, c0, width in GATE_BANDS:
        band = [jnp.pad(dense(w)[k_lo:k_hi, c0:c0 + width], ((0, 0), (0, GATE_BAND - width)))
                for w in (w_a, w_x)]
        slabs.append(jnp.concatenate(band, axis=1))
    return jnp.concatenate(slabs, axis=0).astype(BF16)


def _sigmoid(z):
    return 0.5 * jnp.tanh(0.5 * z) + 0.5


def _lru_kernel(x_ref, g_ref, win_ref, cw_ref, cb_ref, wax_ref, bax_ref, lam_ref, wout_ref, o_ref,
                ext_ref, a_ref, u_ref, h_ref, gate_ref):
    ti = pl.program_id(1)

    @pl.when(ti == 0)
    def _():
        ext_ref[pl.ds(TT, SUB), :] = jnp.zeros((SUB, D_RNN), F32)
        h_ref[...] = jnp.zeros_like(h_ref)

    xn = _rms(x_ref[...], g_ref[...]).astype(BF16)
    gate_ref[...] = jax.nn.gelu(
        jnp.dot(xn, win_ref[:, :D_RNN], preferred_element_type=F32), approximate=True)
    ext_ref[pl.ds(0, SUB), :] = ext_ref[pl.ds(TT, SUB), :]
    ext_ref[pl.ds(SUB, TT), :] = jnp.dot(xn, win_ref[:, D_RNN:], preferred_element_type=F32)
    cw = cw_ref[...]
    xr = cb_ref[...] + sum(
        cw[w:w + 1, :] * ext_ref[pl.ds(SUB - (CONV_W - 1) + w, TT), :] for w in range(CONV_W))

    lam = lam_ref[...]
    softplus_neg = jnp.maximum(-lam, 0.0) + jnp.log1p(jnp.exp(-jnp.abs(lam)))
    xr_b = xr.astype(BF16)
    bias = bax_ref[...]
    row0 = 0
    for k_lo, k_hi, c0, width in GATE_BANDS:
        z = jnp.dot(xr_b[:, k_lo:k_hi], wax_ref[row0:row0 + k_hi - k_lo, :], preferred_element_type=F32)
        row0 += k_hi - k_lo
        r_gate = _sigmoid(z[:, :width] + bias[:, c0:c0 + width])
        i_gate = _sigmoid(z[:, GATE_BAND:GATE_BAND + width] + bias[:, D_RNN + c0:D_RNN + c0 + width])
        a = jnp.exp(-LRU_C * r_gate * softplus_neg[:, c0:c0 + width])
        a_ref[:, c0:c0 + width] = a
        u_ref[:, c0:c0 + width] = jnp.sqrt(jnp.maximum(1.0 - a * a, 0.0)) * (i_gate * xr[:, c0:c0 + width])

    row = lax.broadcasted_iota(jnp.int32, (SUB, D_RNN), 0)

    def scan_rows(c, h_prev):
        r0 = pl.multiple_of(c * SUB, SUB)
        a_c = a_ref[pl.ds(r0, SUB), :]
        u_c = u_ref[pl.ds(r0, SUB), :]
        shift = 1
        while shift < SUB:
            keep = row >= shift
            u_c = u_c + a_c * jnp.where(keep, pltpu.roll(u_c, shift, axis=0), 0.0)
            a_c = a_c * jnp.where(keep, pltpu.roll(a_c, shift, axis=0), 1.0)
            shift *= 2
        h_c = u_c + a_c * h_prev
        u_ref[pl.ds(r0, SUB), :] = h_c
        return jnp.broadcast_to(h_c[SUB - 1:SUB, :], (SUB, D_RNN))

    h_ref[...] = lax.fori_loop(0, TT // SUB, scan_rows, h_ref[...])
    gated = (u_ref[...] * gate_ref[...]).astype(BF16)
    o_ref[...] = x_ref[...] + jnp.dot(gated, wout_ref[...], preferred_element_type=F32)


def _lru_mixer(x, norm_g, w_in, conv_w, conv_b, w_a, b_a, w_x, b_x, lam, w_out):
    b, s, d = x.shape
    wax = _pack_gate_weights(w_a, w_x)
    bax = jnp.concatenate([b_a, b_x])
    sq = pl.Squeezed()
    resident = lambda *shape: pl.BlockSpec(shape, lambda i, t: (0,) * len(shape),
                                           pipeline_mode=pl.Buffered(1))
    return pl.pallas_call(
        _lru_kernel,
        grid=(b, s // TT),
        in_specs=[
            pl.BlockSpec((sq, TT, d), lambda i, t: (i, t, 0)),
            resident(1, d), resident(d, 2 * D_RNN), resident(CONV_W, D_RNN), resident(1, D_RNN),
            resident(*wax.shape), resident(1, 2 * D_RNN), resident(1, D_RNN), resident(D_RNN, d),
        ],
        out_specs=pl.BlockSpec((sq, TT, d), lambda i, t: (i, t, 0)),
        out_shape=jax.ShapeDtypeStruct((b, s, d), F32),
        scratch_shapes=[
            pltpu.VMEM((TT + SUB, D_RNN), F32),
            pltpu.VMEM((TT, D_RNN), F32),
            pltpu.VMEM((TT, D_RNN), F32),
            pltpu.VMEM((SUB, D_RNN), F32),
            pltpu.VMEM((TT, D_RNN), F32),
        ],
        compiler_params=_cparams("parallel", "arbitrary"),
        name="lru_block",
    )(x, norm_g.reshape(1, d), w_in.astype(BF16), conv_w, conv_b.reshape(1, -1), wax,
      bax.reshape(1, -1), lam.reshape(1, -1), w_out.astype(BF16))


def kernel(x, norm_mix, norm_ffn, norm_final, nsa_w_in, nsa_b_gate, nsa_cmp_pos, nsa_cmp_w1,
           nsa_cmp_b1, nsa_cmp_w2, nsa_cmp_b2, nsa_w_out, lru_w_in, lru_conv_w, lru_conv_b,
           lru_w_a, lru_b_a, lru_w_x, lru_b_x, lru_lambda, lru_w_out, ffn_w_in, ffn_w_out):
    b, s, d = x.shape
    ffn = functools.partial(_ffn, tm=1024, tf=256)
    x = _nsa_mixer(x, norm_mix[0], nsa_w_in[0], nsa_b_gate[0], nsa_cmp_pos[0], nsa_cmp_w1[0],
                   nsa_cmp_b1[0], nsa_cmp_w2[0], nsa_cmp_b2[0], nsa_w_out[0])
    x = ffn(x.reshape(b * s, d), norm_ffn[0], ffn_w_in[0].astype(BF16), ffn_w_out[0].astype(BF16),
            norm_final, final_norm=False, name="ffn0").reshape(b, s, d)
    x = _lru_mixer(x, norm_mix[1], lru_w_in[0], lru_conv_w[0], lru_conv_b[0], lru_w_a[0], lru_b_a[0],
                   lru_w_x[0], lru_b_x[0], lru_lambda[0], lru_w_out[0])
    x = ffn(x.reshape(b * s, d), norm_ffn[1], ffn_w_in[1].astype(BF16), ffn_w_out[1].astype(BF16),
            norm_final, final_norm=True, name="ffn1").reshape(b, s, d)
    return x
```

```python
import functools

import jax
import jax.numpy as jnp
from jax import lax
from jax.experimental import pallas as pl
from jax.experimental.pallas import tpu as pltpu

F32 = jnp.float32
BF16 = jnp.bfloat16

D_MODEL = 1024
N_HEADS = 16
N_GROUPS = 4
HEADS_PER_GROUP = N_HEADS // N_GROUPS
HEAD_DIM = 64
L_CMP = 32
CMP_STRIDE = 16
L_SEL = 64
N_SELECT = 16
WINDOW = 512
NSA_Q = N_HEADS * HEAD_DIM
NSA_KV = N_GROUPS * HEAD_DIM
NSA_IN = NSA_Q + 6 * NSA_KV + 3 * N_HEADS
D_RNN = 1408
LRU_BLOCKS = 8
LRU_C = 8.0
CONV_W = 4
D_FF = 2816
EPS = 1e-6

LANES = 128
SUBLANES = 8
NEG = -1e30
VMEM_LIMIT = 56 * 1024 * 1024

TQ = 128
TK_SEL = 512
TK_WIN = 128
N_CMP_PAD = 256


def _cparams(*sem):
    return pltpu.CompilerParams(dimension_semantics=sem, vmem_limit_bytes=VMEM_LIMIT)


def _rms(x, g):
    ms = jnp.mean(x * x, axis=-1, keepdims=True)
    return x * lax.rsqrt(ms + EPS) * g


def _nt_dot(a, b):
    return lax.dot_general(a, b, (((1,), (1,)), ((), ())), preferred_element_type=F32)


def _norm_matmul_kernel(x_ref, g_ref, w_ref, o_ref, xn_ref):
    @pl.when(pl.program_id(1) == 0)
    def _():
        xn_ref[...] = _rms(x_ref[...], g_ref[...]).astype(BF16)

    o_ref[...] = jnp.dot(xn_ref[...], w_ref[...], preferred_element_type=F32).astype(o_ref.dtype)


def _norm_matmul(x, g, w, *, tm, tn, name):
    m, k = x.shape
    n = w.shape[1]
    return pl.pallas_call(
        _norm_matmul_kernel,
        grid=(m // tm, n // tn),
        in_specs=[
            pl.BlockSpec((tm, k), lambda i, j: (i, 0)),
            pl.BlockSpec((1, k), lambda i, j: (0, 0)),
            pl.BlockSpec((k, tn), lambda i, j: (0, j)),
        ],
        out_specs=pl.BlockSpec((tm, tn), lambda i, j: (i, j)),
        out_shape=jax.ShapeDtypeStruct((m, n), F32),
        scratch_shapes=[pltpu.VMEM((tm, k), BF16)],
        compiler_params=_cparams("parallel", "arbitrary"),
        name=name,
    )(x, g.reshape(1, k), w)


def _matmul_res_kernel(a_ref, w_ref, r_ref, o_ref):
    o_ref[...] = r_ref[...] + jnp.dot(a_ref[...], w_ref[...], preferred_element_type=F32)


def _matmul_res(a, w, res, *, tm, name):
    m, k = a.shape
    n = w.shape[1]
    return pl.pallas_call(
        _matmul_res_kernel,
        grid=(m // tm,),
        in_specs=[
            pl.BlockSpec((tm, k), lambda i: (i, 0)),
            pl.BlockSpec((k, n), lambda i: (0, 0)),
            pl.BlockSpec((tm, n), lambda i: (i, 0)),
        ],
        out_specs=pl.BlockSpec((tm, n), lambda i: (i, 0)),
        out_shape=jax.ShapeDtypeStruct((m, n), F32),
        compiler_params=_cparams("parallel"),
        name=name,
    )(a, w, res)


def _ffn_kernel(x_ref, g_ref, wg_ref, wu_ref, wo_ref, gf_ref, o_ref, xn_ref, acc_ref, *, final_norm):
    j = pl.program_id(1)

    @pl.when(j == 0)
    def _():
        xn_ref[...] = _rms(x_ref[...], g_ref[...]).astype(BF16)
        acc_ref[...] = jnp.zeros_like(acc_ref)

    xn = xn_ref[...]
    gate = jnp.dot(xn, wg_ref[...], preferred_element_type=F32)
    up = jnp.dot(xn, wu_ref[...], preferred_element_type=F32)
    hid = gate * jax.nn.sigmoid(gate) * up
    acc_ref[...] += jnp.dot(hid.astype(BF16), wo_ref[...], preferred_element_type=F32)

    @pl.when(j == pl.num_programs(1) - 1)
    def _():
        y = x_ref[...] + acc_ref[...]
        if final_norm:
            y = _rms(y, gf_ref[...])
        o_ref[...] = y


def _ffn(x, g, w_in, w_out, g_final, *, final_norm, tm, tf, name):
    m, d = x.shape
    nf = D_FF // tf
    return pl.pallas_call(
        functools.partial(_ffn_kernel, final_norm=final_norm),
        grid=(m // tm, nf),
        in_specs=[
            pl.BlockSpec((tm, d), lambda i, j: (i, 0)),
            pl.BlockSpec((1, d), lambda i, j: (0, 0)),
            pl.BlockSpec((d, tf), lambda i, j: (0, j)),
            pl.BlockSpec((d, tf), lambda i, j: (0, j + nf)),
            pl.BlockSpec((tf, d), lambda i, j: (j, 0)),
            pl.BlockSpec((1, d), lambda i, j: (0, 0)),
        ],
        out_specs=pl.BlockSpec((tm, d), lambda i, j: (i, 0)),
        out_shape=jax.ShapeDtypeStruct((m, d), F32),
        scratch_shapes=[pltpu.VMEM((tm, d), BF16), pltpu.VMEM((tm, d), F32)],
        compiler_params=_cparams("parallel", "arbitrary"),
        name=name,
    )(x, g.reshape(1, d), w_in, w_in, w_out, g_final.reshape(1, d))


NSA_IN_PAD = -(-NSA_IN // LANES) * LANES


def _nsa_in_proj_kernel(x_ref, g_ref, w_ref, wvt_ref, q_ref, raw_ref, k_ref, vt_ref, gl_ref):
    xn = _rms(x_ref[...], g_ref[...]).astype(BF16)
    q_ref[...] = jnp.dot(xn, w_ref[:, :NSA_Q], preferred_element_type=F32).astype(q_ref.dtype)
    for i, (dst, slot) in {0: (raw_ref, 0), 1: (raw_ref, 1), 2: (k_ref, 0), 4: (k_ref, 1)}.items():
        col0 = NSA_Q + i * NSA_KV
        part = jnp.dot(xn, w_ref[:, col0:col0 + NSA_KV], preferred_element_type=F32)
        for grp in range(N_GROUPS):
            dst[slot, grp] = part[:, grp * HEAD_DIM:(grp + 1) * HEAD_DIM].astype(dst.dtype)
    for slot in range(2):
        part_t = _nt_dot(wvt_ref[slot], xn)
        for grp in range(N_GROUPS):
            vt_ref[slot, grp] = part_t[grp * HEAD_DIM:(grp + 1) * HEAD_DIM].astype(vt_ref.dtype)
    gl_ref[...] = jnp.dot(xn, w_ref[:, NSA_Q + 6 * NSA_KV:], preferred_element_type=F32)


def _nsa_in_proj(x, g, w, wvt, *, tm):
    b, s, d = x.shape
    sq = pl.Squeezed()
    grouped = lambda dtype: jax.ShapeDtypeStruct((2, b, N_GROUPS, s, HEAD_DIM), dtype)
    grouped_spec = pl.BlockSpec((2, sq, N_GROUPS, tm, HEAD_DIM), lambda i, t: (0, i, 0, t, 0))
    return pl.pallas_call(
        _nsa_in_proj_kernel,
        grid=(b, s // tm),
        in_specs=[
            pl.BlockSpec((sq, tm, d), lambda i, t: (i, t, 0)),
            pl.BlockSpec((1, d), lambda i, t: (0, 0)),
            pl.BlockSpec((d, NSA_IN_PAD), lambda i, t: (0, 0)),
            pl.BlockSpec((2, NSA_KV, d), lambda i, t: (0, 0, 0)),
        ],
        out_specs=[
            pl.BlockSpec((sq, tm, NSA_Q), lambda i, t: (i, t, 0)),
            grouped_spec, grouped_spec,
            pl.BlockSpec((2, sq, N_GROUPS, HEAD_DIM, tm), lambda i, t: (0, i, 0, 0, t)),
            pl.BlockSpec((sq, tm, LANES), lambda i, t: (i, t, 0)),
        ],
        out_shape=[
            jax.ShapeDtypeStruct((b, s, NSA_Q), BF16),
            grouped(F32), grouped(BF16),
            jax.ShapeDtypeStruct((2, b, N_GROUPS, HEAD_DIM, s), BF16),
            jax.ShapeDtypeStruct((b, s, LANES), F32),
        ],
        compiler_params=_cparams("parallel", "parallel"),
        name="nsa_in_proj",
    )(x, g.reshape(1, d), w, wvt)


def _compress_kernel(r_ref, pos_ref, w1_ref, b1_ref, w2_ref, b2_ref, w2t_ref, b2t_ref, o_ref, ot_ref):
    half = CMP_STRIDE * HEAD_DIM
    rows = r_ref[...]
    pos = pos_ref[...]
    lo = (rows + pos[:, :half]).astype(BF16)
    hi = (rows + pos[:, half:]).astype(BF16)
    part_lo = jnp.dot(lo, w1_ref[:half, :], preferred_element_type=F32)
    part_hi = jnp.dot(hi, w1_ref[half:, :], preferred_element_type=F32)
    hid = part_lo + pltpu.roll(part_hi, N_CMP_PAD - 1, axis=0) + b1_ref[...]
    hid = jax.nn.gelu(hid, approximate=True).astype(BF16)
    o_ref[...] = (jnp.dot(hid, w2_ref[...], preferred_element_type=F32) + b2_ref[...]).astype(o_ref.dtype)
    ot_ref[...] = (_nt_dot(w2t_ref[...], hid) + b2t_ref[...]).astype(ot_ref.dtype)


def _compress(raw, pos, w1, b1, w2, b2):
    two, b, g, n, width = raw.shape
    hidden = w1.shape[-1]
    sq = pl.Squeezed()
    per_branch = lambda *shape: pl.BlockSpec((sq,) + shape, lambda c, i, j: (c,) + (0,) * len(shape))
    return pl.pallas_call(
        _compress_kernel,
        grid=(two, b, g),
        in_specs=[
            pl.BlockSpec((sq, sq, sq, n, width), lambda c, i, j: (c, i, j, 0, 0)),
            per_branch(1, 2 * width), per_branch(2 * width, hidden), per_branch(1, hidden),
            per_branch(hidden, HEAD_DIM), per_branch(1, HEAD_DIM),
            per_branch(HEAD_DIM, hidden), per_branch(HEAD_DIM, 1),
        ],
        out_specs=[
            pl.BlockSpec((sq, sq, sq, n, HEAD_DIM), lambda c, i, j: (c, i, j, 0, 0)),
            pl.BlockSpec((sq, sq, sq, HEAD_DIM, n), lambda c, i, j: (c, i, j, 0, 0)),
        ],
        out_shape=[
            jax.ShapeDtypeStruct((two, b, g, n, HEAD_DIM), BF16),
            jax.ShapeDtypeStruct((two, b, g, HEAD_DIM, n), BF16),
        ],
        compiler_params=_cparams("parallel", "parallel", "parallel"),
        name="nsa_compress",
    )(raw, pos, w1, b1.reshape(two, 1, hidden), w2, b2.reshape(two, 1, HEAD_DIM),
      w2.transpose(0, 2, 1), b2.reshape(two, HEAD_DIM, 1))


AUG = 256
SEL_ROW0 = HEAD_DIM
POS_ROW0 = 2 * HEAD_DIM
N_PIECES = 3
PAD_LANE = POS_ROW0 + 2 * N_PIECES
POS_ROWS = 16
V_ROWS = 80
NQ = HEADS_PER_GROUP * TQ
BIG = 1e30
assert TQ == TK_WIN and TK_SEL % TQ == 0 and WINDOW % TK_WIN == 0


def _flash_reset(m_ref, acc_ref):
    m_ref[...] = jnp.full_like(m_ref, NEG)
    acc_ref[...] = jnp.zeros_like(acc_ref)


def _scores(kaug, tile, tk, qat_ref):
    k0 = pl.multiple_of(tile * tk, tk)
    return jnp.dot(kaug[pl.ds(k0, tk), :], qat_ref[...], preferred_element_type=F32)


def _flash_update(s, vt_tile, mask, m_ref, acc_ref):
    if mask is not None:
        s = jnp.where(mask, s, NEG)
    m_old = m_ref[...]
    m_new = jnp.maximum(m_old, jnp.max(s, axis=0, keepdims=True))
    p = jnp.exp(s - m_new).astype(BF16)
    acc_ref[...] = jnp.exp(m_old - m_new) * acc_ref[...] + jnp.dot(
        vt_tile, p, preferred_element_type=F32)
    m_ref[...] = m_new


def _flash_result(acc_ref):
    return acc_ref[pl.ds(0, HEAD_DIM), :] * (1.0 / acc_ref[pl.ds(HEAD_DIM, 1), :])


def _nsa_kernel(sl_ref, q_ref, kc_ref, vct_ref, ks_ref, vst_ref, kw_ref, vwt_ref, gl_ref, bg_ref,
                constk_ref, constc_ref, o_ref,
                kaug_s, kaug_w, kaug_c, vt_s, vt_w, vt_c, qat_ref, score_ref, m_ref, acc_ref, out_ref,
                sa_ref, sb_ref, gt_ref, qats_ref):
    grp = pl.program_id(1)
    qi = pl.program_id(2)
    q0 = qi * TQ
    nrep = HEADS_PER_GROUP

    @pl.when(qi == 0)
    def _():
        s_len = ks_ref.shape[0]
        for kaug, row0, const, k in ((kaug_s, 0, constk_ref, ks_ref), (kaug_w, WINDOW, constk_ref, kw_ref),
                                     (kaug_c, 0, constc_ref, kc_ref)):
            kaug[pl.ds(row0, k.shape[0]), :] = const[...]
            kaug[pl.ds(row0, k.shape[0]), pl.ds(0, HEAD_DIM)] = k[...]
        pad_lane = lax.broadcasted_iota(jnp.int32, (WINDOW, AUG), 1) == PAD_LANE
        kaug_w[pl.ds(0, WINDOW), :] = jnp.where(pad_lane, 1.0, 0.0).astype(BF16)
        vt_w[:, pl.ds(0, WINDOW)] = jnp.zeros((V_ROWS, WINDOW), BF16)
        pad = V_ROWS - HEAD_DIM
        ones_rows = lambda n: jnp.where(
            lax.broadcasted_iota(jnp.int32, (pad, n), 0) == 0, 1.0, 0.0).astype(BF16)
        for c in range(vt_s.shape[0]):
            vt_s[c, pl.ds(0, HEAD_DIM), :] = vst_ref[:, pl.ds(c * TK_SEL, TK_SEL)]
            vt_s[c, pl.ds(HEAD_DIM, pad), :] = ones_rows(TK_SEL)
        vt_w[pl.ds(0, HEAD_DIM), pl.ds(WINDOW, s_len)] = vwt_ref[...]
        vt_w[pl.ds(HEAD_DIM, pad), pl.ds(WINDOW, s_len)] = ones_rows(s_len)
        vt_c[pl.ds(0, HEAD_DIM), :] = vct_ref[...]
        vt_c[pl.ds(HEAD_DIM, pad), :] = ones_rows(N_CMP_PAD)

    for half in range(nrep // 2):
        qt = q_ref[:, pl.ds(half * LANES, LANES)].astype(F32).T * (HEAD_DIM ** -0.5)
        for sub in range(2):
            q_rows = qt[sub * HEAD_DIM:(sub + 1) * HEAD_DIM].astype(BF16)
            for ref in (qat_ref, qats_ref):
                ref[pl.ds(0, HEAD_DIM), pl.ds((2 * half + sub) * TQ, TQ)] = q_rows
    qat_ref[pl.ds(SEL_ROW0, HEAD_DIM), :] = jnp.zeros((HEAD_DIM, NQ), BF16)
    piece = lax.broadcasted_iota(jnp.int32, (POS_ROWS, TQ), 0)
    tail0 = POS_ROW0 + POS_ROWS
    for r in range(nrep):
        tile = jnp.where(piece == PAD_LANE - POS_ROW0, -BIG, 0.0)
        for i in range(2 * N_PIECES):
            tile = jnp.where(piece == i, sl_ref[(grp * nrep + r) * N_PIECES + i % N_PIECES], tile)
        for ref in (qat_ref, qats_ref):
            ref[pl.ds(POS_ROW0, POS_ROWS), pl.ds(r * TQ, TQ)] = tile.astype(BF16)
    for ref in (qat_ref, qats_ref):
        ref[pl.ds(tail0, AUG - tail0), :] = jnp.zeros((AUG - tail0, NQ), BF16)

    t_row = q0 + (lax.broadcasted_iota(jnp.int32, (1, NQ), 1) & (TQ - 1))
    gt_ref[...] = jax.nn.sigmoid(gl_ref[...] + bg_ref[...]).T
    gate_row = lambda br: jnp.concatenate(
        [gt_ref[pl.ds(3 * (grp * nrep + r) + br, 1), :] for r in range(nrep)], axis=1)

    s = jnp.dot(kaug_c[...], qat_ref[...], preferred_element_type=F32)
    cmp_end = CMP_STRIDE * lax.broadcasted_iota(jnp.int32, (N_CMP_PAD, 1), 0) + (L_CMP - 1)
    mask_c = cmp_end <= t_row
    s = jnp.where(mask_c, s, NEG)
    e = jnp.where(mask_c, jnp.exp(s - jnp.max(s, axis=0, keepdims=True)), 0.0)
    d = jnp.sum(e, axis=0, keepdims=True)
    p = e * (1.0 / jnp.where(d > 0, d, 1.0))
    p_sum = sum(p[:, r * TQ:(r + 1) * TQ] for r in range(nrep))
    o_c = jnp.dot(vt_c[...], p.astype(BF16), preferred_element_type=F32)
    out_ref[...] = gate_row(0) * o_c[:HEAD_DIM]

    n_win = WINDOW // TK_WIN
    k0 = pl.multiple_of(q0, TK_WIN)
    s_all = jnp.dot(kaug_w[pl.ds(k0, WINDOW + TQ), :], qat_ref[...], preferred_element_type=F32)
    s_w = [s_all[j * TK_WIN:(j + 1) * TK_WIN] for j in range(n_win + 1)]
    not_after = q0 + lax.broadcasted_iota(jnp.int32, (TK_WIN, 1), 0) <= t_row
    s_w[0] = jnp.where(not_after, NEG, s_w[0])
    s_w[-1] = jnp.where(not_after, s_w[-1], NEG)
    m_w = functools.reduce(jnp.maximum, [jnp.max(x, axis=0, keepdims=True) for x in s_w])
    p_w = jnp.concatenate([jnp.exp(x - m_w).astype(BF16) for x in s_w], axis=0)
    acc_w = jnp.dot(vt_w[:, pl.ds(k0, WINDOW + TQ)], p_w, preferred_element_type=F32)
    out_ref[...] += gate_row(2) * (acc_w[:HEAD_DIM] * (1.0 / acc_w[HEAD_DIM:HEAD_DIM + 1]))

    n_sb = score_ref.shape[0]
    jb = lax.broadcasted_iota(jnp.int32, (n_sb, TQ), 0)
    t_sel = q0 + lax.broadcasted_iota(jnp.int32, (n_sb, TQ), 1)
    valid = jb * L_SEL <= t_sel

    def write_selection(selected):
        sel_bias = jnp.where(selected, 0.0, -BIG).astype(BF16)
        for r in range(nrep):
            qats_ref[pl.ds(SEL_ROW0, n_sb), pl.ds(r * TQ, TQ)] = sel_bias

    def rank_blocks():
        cmp_start = CMP_STRIDE * lax.broadcasted_iota(jnp.int32, (n_sb, N_CMP_PAD), 1)
        sb_start = L_SEL * lax.broadcasted_iota(jnp.int32, (n_sb, N_CMP_PAD), 0)
        overlap_t = ((cmp_start < sb_start + L_SEL) & (cmp_start + L_CMP > sb_start)).astype(F32)
        imp_t = jnp.dot(overlap_t, p_sum, precision=lax.Precision.HIGHEST,
                        preferred_element_type=F32)
        cur = t_sel // L_SEL
        forced = (jb == 0) | (jb == cur) | (jb == cur - 1)
        score = jnp.where(valid, jnp.where(forced, jnp.inf, imp_t), -jnp.inf)
        score_ref[...] = score
        chunks = [score[c * SUBLANES:(c + 1) * SUBLANES] for c in range(n_sb // SUBLANES)]
        ranks = [jnp.zeros((SUBLANES, TQ), F32) for _ in chunks]
        sub_row = lax.broadcasted_iota(jnp.int32, (SUBLANES, TQ), 0)
        for i in range(n_sb):
            row = score_ref[pl.ds(i, 1), :]
            for c, chunk in enumerate(chunks):
                later = jnp.where(row >= chunk, 1.0, 0.0)
                earlier = jnp.where(row > chunk, 1.0, 0.0)
                if i < c * SUBLANES:
                    ranks[c] = ranks[c] + later
                elif i >= (c + 1) * SUBLANES:
                    ranks[c] = ranks[c] + earlier
                else:
                    ranks[c] = ranks[c] + jnp.where(sub_row > i - c * SUBLANES, later, earlier)
        rank = jnp.concatenate(ranks, axis=0)
        write_selection(valid & (rank < float(N_SELECT)))

    rank_blocks()

    _flash_reset(m_ref, acc_ref)
    n_full = q0 // TK_SEL
    key_s = lax.broadcasted_iota(jnp.int32, (TK_SEL, 1), 0)
    tail_mask = n_full * TK_SEL + key_s <= t_row
    sel_scores = lambda tile: _scores(kaug_s, tile, TK_SEL, qats_ref)
    sa_ref[...] = sel_scores(0)

    def sel_pair(j, carry):
        tile = 2 * j
        sb_ref[...] = sel_scores(tile + 1)
        _flash_update(sa_ref[...], vt_s[tile], None, m_ref, acc_ref)
        sa_ref[...] = sel_scores(tile + 2)
        _flash_update(sb_ref[...], vt_s[tile + 1], None, m_ref, acc_ref)
        return carry

    lax.fori_loop(0, n_full // 2, sel_pair, 0)

    @pl.when(n_full % 2 == 1)
    def _():
        sb_ref[...] = sel_scores(n_full)
        _flash_update(sa_ref[...], vt_s[n_full - 1], None, m_ref, acc_ref)
        _flash_update(sb_ref[...], vt_s[n_full], tail_mask, m_ref, acc_ref)

    @pl.when(n_full % 2 == 0)
    def _():
        _flash_update(sa_ref[...], vt_s[n_full], tail_mask, m_ref, acc_ref)

    out_ref[...] += gate_row(1) * _flash_result(acc_ref)

    for half in range(nrep // 2):
        slab = jnp.concatenate([out_ref[:, pl.ds(2 * half * TQ, TQ)],
                                out_ref[:, pl.ds((2 * half + 1) * TQ, TQ)]], axis=0)
        o_ref[:, pl.ds(half * LANES, LANES)] = slab.T.astype(o_ref.dtype)


def _nsa_attention(slope_pieces, q, cmp, cmp_t, keys, vals_t, glogit, bgate, constk, constc):
    b, s, _ = q.shape
    g, nrep, dh = N_GROUPS, HEADS_PER_GROUP, HEAD_DIM
    n_sb = s // L_SEL
    sq = pl.Squeezed()
    slot = lambda c, *shape: pl.BlockSpec((sq, sq, sq) + shape, lambda i, j, t: (c, i, j, 0, 0))
    full = lambda *shape: pl.BlockSpec(shape, lambda i, j, t: (0,) * len(shape))
    return pl.pallas_call(
        _nsa_kernel,
        grid=(b, g, s // TQ),
        in_specs=[
            pl.BlockSpec(memory_space=pltpu.SMEM),
            pl.BlockSpec((sq, TQ, nrep * dh), lambda i, j, t: (i, t, j)),
            slot(0, N_CMP_PAD, dh), slot(1, dh, N_CMP_PAD),
            slot(0, s, dh), slot(0, dh, s), slot(1, s, dh), slot(1, dh, s),
            pl.BlockSpec((sq, TQ, LANES), lambda i, j, t: (i, t, 0)),
            full(1, LANES), full(s, AUG), full(N_CMP_PAD, AUG),
        ],
        out_specs=pl.BlockSpec((sq, TQ, nrep * dh), lambda i, j, t: (i, t, j)),
        out_shape=jax.ShapeDtypeStruct((b, s, NSA_Q), BF16),
        scratch_shapes=[
            pltpu.VMEM((s, AUG), BF16), pltpu.VMEM((WINDOW + s, AUG), BF16),
            pltpu.VMEM((N_CMP_PAD, AUG), BF16),
            pltpu.VMEM((s // TK_SEL, V_ROWS, TK_SEL), BF16), pltpu.VMEM((V_ROWS, WINDOW + s), BF16),
            pltpu.VMEM((V_ROWS, N_CMP_PAD), BF16),
            pltpu.VMEM((AUG, NQ), BF16),
            pltpu.VMEM((n_sb, TQ), F32),
            pltpu.VMEM((1, NQ), F32),
            pltpu.VMEM((V_ROWS, NQ), F32),
            pltpu.VMEM((dh, NQ), F32),
            pltpu.VMEM((TK_SEL, NQ), F32), pltpu.VMEM((TK_SEL, NQ), F32),
            pltpu.VMEM((LANES, TQ), F32),
            pltpu.VMEM((AUG, NQ), BF16),
        ],
        compiler_params=_cparams("arbitrary", "arbitrary", "arbitrary"),
        name="nsa_attention",
    )(slope_pieces, q, cmp, cmp_t, keys, vals_t, keys, vals_t, glogit, bgate, constk, constc)


def _position_pieces(pos):
    hi = (pos // L_SEL) * L_SEL
    return jnp.stack([hi] * N_PIECES + [pos - hi] * N_PIECES, axis=1).astype(F32)


def _nsa_constants(s):
    pos = jnp.arange(s)
    constk = jnp.zeros((s, AUG), F32)
    constk = constk.at[:, SEL_ROW0:SEL_ROW0 + s // L_SEL].set(jax.nn.one_hot(pos // L_SEL, s // L_SEL))
    constk = constk.at[:, POS_ROW0:POS_ROW0 + 2 * N_PIECES].set(_position_pieces(pos))
    cmp_end = CMP_STRIDE * jnp.arange(N_CMP_PAD) + (L_CMP - 1)
    constc = jnp.zeros((N_CMP_PAD, AUG), F32)
    constc = constc.at[:, POS_ROW0:POS_ROW0 + 2 * N_PIECES].set(_position_pieces(cmp_end))
    slopes = 2.0 ** (-8.0 * jnp.arange(1, N_HEADS + 1, dtype=F32) / N_HEADS)
    pieces, rest = [], slopes
    for _ in range(N_PIECES):
        piece = rest.astype(BF16).astype(F32)
        pieces.append(piece)
        rest = rest - piece
    return constk.astype(BF16), constc.astype(BF16), jnp.stack(pieces, axis=1).reshape(-1)


def _nsa_mixer(x, norm_g, w_in, b_gate, cmp_pos, cmp_w1, cmp_b1, cmp_w2, cmp_b2, w_out):
    b, s, d = x.shape
    m = b * s
    g, nrep, dh = N_GROUPS, HEADS_PER_GROUP, HEAD_DIM
    w_in_p = jnp.pad(w_in, ((0, 0), (0, NSA_IN_PAD - NSA_IN))).astype(BF16)
    v_cols = lambda i: w_in_p[:, NSA_Q + i * NSA_KV:NSA_Q + (i + 1) * NSA_KV].T
    q, raw, keys, vals_t, glogit = _nsa_in_proj(x, norm_g, w_in_p, jnp.stack([v_cols(3), v_cols(5)]),
                                                tm=512)
    raw = raw.reshape(2, b, g, s // CMP_STRIDE, CMP_STRIDE * dh)
    cmp, cmp_t = _compress(raw, cmp_pos.reshape(2, 1, L_CMP * dh), cmp_w1.astype(BF16), cmp_b1,
                           cmp_w2.astype(BF16), cmp_b2)
    constk, constc, slope_pieces = _nsa_constants(s)
    bgate = jnp.pad(b_gate, (0, LANES - 3 * N_HEADS)).reshape(1, LANES)
    y = _nsa_attention(slope_pieces, q, cmp, cmp_t, keys, vals_t, glogit, bgate, constk, constc)
    return _matmul_res(y.reshape(m, NSA_Q), w_out.astype(BF16), x.reshape(m, d), tm=512,
                       name="nsa_out_proj").reshape(b, s, d)


TT = 256
SUB = SUBLANES


GATE_BAND = 256


def _gate_bands():
    block = D_RNN // LRU_BLOCKS
    bands = []
    for c0 in range(0, D_RNN, GATE_BAND):
        width = min(GATE_BAND, D_RNN - c0)
        k_lo = (c0 // block) * block // LANES * LANES
        k_hi = -(-(((c0 + width - 1) // block + 1) * block) // LANES) * LANES
        bands.append((k_lo, min(k_hi, D_RNN), c0, width))
    return tuple(bands)


GATE_BANDS = _gate_bands()


def _pack_gate_weights(w_a, w_x):
    dense = lambda w: jax.scipy.linalg.block_diag(*[w[i] for i in range(LRU_BLOCKS)])
    slabs = []
    for k_lo, k_hi, c0, width in GATE_BANDS:
        band = [jnp.pad(dense(w)[k_lo:k_hi, c0:c0 + width], ((0, 0), (0, GATE_BAND - width)))
                for w in (w_a, w_x)]
        slabs.append(jnp.concatenate(band, axis=1))
    return jnp.concatenate(slabs, axis=0).astype(BF16)


def _sigmoid(z):
    return 0.5 * jnp.tanh(0.5 * z) + 0.5


def _lru_kernel(x_ref, g_ref, win_ref, cw_ref, cb_ref, wax_ref, bax_ref, lam_ref, wout_ref, o_ref,
                ext_ref, a_ref, u_ref, h_ref, gate_ref):
    ti = pl.program_id(1)

    @pl.when(ti == 0)
    def _():
        ext_ref[pl.ds(TT, SUB), :] = jnp.zeros((SUB, D_RNN), F32)
        h_ref[...] = jnp.zeros_like(h_ref)

    xn = _rms(x_ref[...], g_ref[...]).astype(BF16)
    gate_ref[...] = jax.nn.gelu(
        jnp.dot(xn, win_ref[:, :D_RNN], preferred_element_type=F32), approximate=True)
    ext_ref[pl.ds(0, SUB), :] = ext_ref[pl.ds(TT, SUB), :]
    ext_ref[pl.ds(SUB, TT), :] = jnp.dot(xn, win_ref[:, D_RNN:], preferred_element_type=F32)
    cw = cw_ref[...]
    xr = cb_ref[...] + sum(
        cw[w:w + 1, :] * ext_ref[pl.ds(SUB - (CONV_W - 1) + w, TT), :] for w in range(CONV_W))

    lam = lam_ref[...]
    softplus_neg = jnp.maximum(-lam, 0.0) + jnp.log1p(jnp.exp(-jnp.abs(lam)))
    xr_b = xr.astype(BF16)
    bias = bax_ref[...]
    row0 = 0
    for k_lo, k_hi, c0, width in GATE_BANDS:
        z = jnp.dot(xr_b[:, k_lo:k_hi], wax_ref[row0:row0 + k_hi - k_lo, :], preferred_element_type=F32)
        row0 += k_hi - k_lo
        r_gate = _sigmoid(z[:, :width] + bias[:, c0:c0 + width])
        i_gate = _sigmoid(z[:, GATE_BAND:GATE_BAND + width] + bias[:, D_RNN + c0:D_RNN + c0 + width])
        a = jnp.exp(-LRU_C * r_gate * softplus_neg[:, c0:c0 + width])
        a_ref[:, c0:c0 + width] = a
        u_ref[:, c0:c0 + width] = jnp.sqrt(jnp.maximum(1.0 - a * a, 0.0)) * (i_gate * xr[:, c0:c0 + width])

    row = lax.broadcasted_iota(jnp.int32, (SUB, D_RNN), 0)

    def scan_rows(c, h_prev):
        r0 = pl.multiple_of(c * SUB, SUB)
        a_c = a_ref[pl.ds(r0, SUB), :]
        u_c = u_ref[pl.ds(r0, SUB), :]
        shift = 1
        while shift < SUB:
            keep = row >= shift
            u_c = u_c + a_c * jnp.where(keep, pltpu.roll(u_c, shift, axis=0), 0.0)
            a_c = a_c * jnp.where(keep, pltpu.roll(a_c, shift, axis=0), 1.0)
            shift *= 2
        h_c = u_c + a_c * h_prev
        u_ref[pl.ds(r0, SUB), :] = h_c
        return jnp.broadcast_to(h_c[SUB - 1:SUB, :], (SUB, D_RNN))

    h_ref[...] = lax.fori_loop(0, TT // SUB, scan_rows, h_ref[...])
    gated = (u_ref[...] * gate_ref[...]).astype(BF16)
    o_ref[...] = x_ref[...] + jnp.dot(gated, wout_ref[...], preferred_element_type=F32)


def _lru_mixer(x, norm_g, w_in, conv_w, conv_b, w_a, b_a, w_x, b_x, lam, w_out):
    b, s, d = x.shape
    wax = _pack_gate_weights(w_a, w_x)
    bax = jnp.concatenate([b_a, b_x])
    sq = pl.Squeezed()
    resident = lambda *shape: pl.BlockSpec(shape, lambda i, t: (0,) * len(shape),
                                           pipeline_mode=pl.Buffered(1))
    return pl.pallas_call(
        _lru_kernel,
        grid=(b, s // TT),
        in_specs=[
            pl.BlockSpec((sq, TT, d), lambda i, t: (i, t, 0)),
            resident(1, d), resident(d, 2 * D_RNN), resident(CONV_W, D_RNN), resident(1, D_RNN),
            resident(*wax.shape), resident(1, 2 * D_RNN), resident(1, D_RNN), resident(D_RNN, d),
        ],
        out_specs=pl.BlockSpec((sq, TT, d), lambda i, t: (i, t, 0)),
        out_shape=jax.ShapeDtypeStruct((b, s, d), F32),
        scratch_shapes=[
            pltpu.VMEM((TT + SUB, D_RNN), F32),
            pltpu.VMEM((TT, D_RNN), F32),
            pltpu.VMEM((TT, D_RNN), F32),
            pltpu.VMEM((SUB, D_RNN), F32),
            pltpu.VMEM((TT, D_RNN), F32),
        ],
        compiler_params=_cparams("parallel", "arbitrary"),
        name="lru_block",
    )(x, norm_g.reshape(1, d), w_in.astype(BF16), conv_w, conv_b.reshape(1, -1), wax,
      bax.reshape(1, -1), lam.reshape(1, -1), w_out.astype(BF16))


def kernel(x, norm_mix, norm_ffn, norm_final, nsa_w_in, nsa_b_gate, nsa_cmp_pos, nsa_cmp_w1,
           nsa_cmp_b1, nsa_cmp_w2, nsa_cmp_b2, nsa_w_out, lru_w_in, lru_conv_w, lru_conv_b,
           lru_w_a, lru_b_a, lru_w_x, lru_b_x, lru_lambda, lru_w_out, ffn_w_in, ffn_w_out):
    b, s, d = x.shape
    ffn = functools.partial(_ffn, tm=1024, tf=256)
    x = _nsa_mixer(x, norm_mix[0], nsa_w_in[0], nsa_b_gate[0], nsa_cmp_pos[0], nsa_cmp_w1[0],
                   nsa_cmp_b1[0], nsa_cmp_w2[0], nsa_cmp_b2[0], nsa_w_out[0])
    x = ffn(x.reshape(b * s, d), norm_ffn[0], ffn_w_in[0].astype(BF16), ffn_w_out[0].astype(BF16),
            norm_final, final_norm=False, name="ffn0").reshape(b, s, d)
    x = _lru_mixer(x, norm_mix[1], lru_w_in[0], lru_conv_w[0], lru_conv_b[0], lru_w_a[0], lru_b_a[0],
                   lru_w_x[0], lru_b_x[0], lru_lambda[0], lru_w_out[0])
    x = ffn(x.reshape(b * s, d), norm_ffn[1], ffn_w_in[1].astype(BF16), ffn_w_out[1].astype(BF16),
            norm_final, final_norm=True, name="ffn1").reshape(b, s, d)
    return x
```

```python
import functools

import jax
import jax.numpy as jnp
from jax import lax
from jax.experimental import pallas as pl
from jax.experimental.pallas import tpu as pltpu

F32 = jnp.float32
BF16 = jnp.bfloat16

D_MODEL = 1024
N_HEADS = 16
N_GROUPS = 4
HEADS_PER_GROUP = N_HEADS // N_GROUPS
HEAD_DIM = 64
L_CMP = 32
CMP_STRIDE = 16
L_SEL = 64
N_SELECT = 16
WINDOW = 512
NSA_Q = N_HEADS * HEAD_DIM
NSA_KV = N_GROUPS * HEAD_DIM
NSA_IN = NSA_Q + 6 * NSA_KV + 3 * N_HEADS
D_RNN = 1408
LRU_BLOCKS = 8
LRU_C = 8.0
CONV_W = 4
D_FF = 2816
EPS = 1e-6

LANES = 128
SUBLANES = 8
NEG = -1e30
VMEM_LIMIT = 56 * 1024 * 1024

TQ = 128
TK_SEL = 512
TK_WIN = 128
N_CMP_PAD = 256


def _cparams(*sem):
    return pltpu.CompilerParams(dimension_semantics=sem, vmem_limit_bytes=VMEM_LIMIT)


def _rms(x, g):
    ms = jnp.mean(x * x, axis=-1, keepdims=True)
    return x * lax.rsqrt(ms + EPS) * g


def _nt_dot(a, b):
    return lax.dot_general(a, b, (((1,), (1,)), ((), ())), preferred_element_type=F32)


def _norm_matmul_kernel(x_ref, g_ref, w_ref, o_ref, xn_ref):
    @pl.when(pl.program_id(1) == 0)
    def _():
        xn_ref[...] = _rms(x_ref[...], g_ref[...]).astype(BF16)

    o_ref[...] = jnp.dot(xn_ref[...], w_ref[...], preferred_element_type=F32).astype(o_ref.dtype)


def _norm_matmul(x, g, w, *, tm, tn, name):
    m, k = x.shape
    n = w.shape[1]
    return pl.pallas_call(
        _norm_matmul_kernel,
        grid=(m // tm, n // tn),
        in_specs=[
            pl.BlockSpec((tm, k), lambda i, j: (i, 0)),
            pl.BlockSpec((1, k), lambda i, j: (0, 0)),
            pl.BlockSpec((k, tn), lambda i, j: (0, j)),
        ],
        out_specs=pl.BlockSpec((tm, tn), lambda i, j: (i, j)),
        out_shape=jax.ShapeDtypeStruct((m, n), F32),
        scratch_shapes=[pltpu.VMEM((tm, k), BF16)],
        compiler_params=_cparams("parallel", "arbitrary"),
        name=name,
    )(x, g.reshape(1, k), w)


def _matmul_res_kernel(a_ref, w_ref, r_ref, o_ref):
    o_ref[...] = r_ref[...] + jnp.dot(a_ref[...], w_ref[...], preferred_element_type=F32)


def _matmul_res(a, w, res, *, tm, name):
    m, k = a.shape
    n = w.shape[1]
    return pl.pallas_call(
        _matmul_res_kernel,
        grid=(m // tm,),
        in_specs=[
            pl.BlockSpec((tm, k), lambda i: (i, 0)),
            pl.BlockSpec((k, n), lambda i: (0, 0)),
            pl.BlockSpec((tm, n), lambda i: (i, 0)),
        ],
        out_specs=pl.BlockSpec((tm, n), lambda i: (i, 0)),
        out_shape=jax.ShapeDtypeStruct((m, n), F32),
        compiler_params=_cparams("parallel"),
        name=name,
    )(a, w, res)


def _ffn_kernel(x_ref, g_ref, wg_ref, wu_ref, wo_ref, gf_ref, o_ref, xn_ref, acc_ref, *, final_norm):
    j = pl.program_id(1)

    @pl.when(j == 0)
    def _():
        xn_ref[...] = _rms(x_ref[...], g_ref[...]).astype(BF16)
        acc_ref[...] = jnp.zeros_like(acc_ref)

    xn = xn_ref[...]
    gate = jnp.dot(xn, wg_ref[...], preferred_element_type=F32)
    up = jnp.dot(xn, wu_ref[...], preferred_element_type=F32)
    hid = gate * jax.nn.sigmoid(gate) * up
    acc_ref[...] += jnp.dot(hid.astype(BF16), wo_ref[...], preferred_element_type=F32)

    @pl.when(j == pl.num_programs(1) - 1)
    def _():
        y = x_ref[...] + acc_ref[...]
        if final_norm:
            y = _rms(y, gf_ref[...])
        o_ref[...] = y


def _ffn(x, g, w_in, w_out, g_final, *, final_norm, tm, tf, name):
    m, d = x.shape
    nf = D_FF // tf
    return pl.pallas_call(
        functools.partial(_ffn_kernel, final_norm=final_norm),
        grid=(m // tm, nf),
        in_specs=[
            pl.BlockSpec((tm, d), lambda i, j: (i, 0)),
            pl.BlockSpec((1, d), lambda i, j: (0, 0)),
            pl.BlockSpec((d, tf), lambda i, j: (0, j)),
            pl.BlockSpec((d, tf), lambda i, j: (0, j + nf)),
            pl.BlockSpec((tf, d), lambda i, j: (j, 0)),
            pl.BlockSpec((1, d), lambda i, j: (0, 0)),
        ],
        out_specs=pl.BlockSpec((tm, d), lambda i, j: (i, 0)),
        out_shape=jax.ShapeDtypeStruct((m, d), F32),
        scratch_shapes=[pltpu.VMEM((tm, d), BF16), pltpu.VMEM((tm, d), F32)],
        compiler_params=_cparams("parallel", "arbitrary"),
        name=name,
    )(x, g.reshape(1, d), w_in, w_in, w_out, g_final.reshape(1, d))


NSA_IN_PAD = -(-NSA_IN // LANES) * LANES


def _nsa_in_proj_kernel(x_ref, g_ref, w_ref, wvt_ref, q_ref, raw_ref, k_ref, vt_ref, gl_ref):
    xn = _rms(x_ref[...], g_ref[...]).astype(BF16)
    q_ref[...] = jnp.dot(xn, w_ref[:, :NSA_Q], preferred_element_type=F32).astype(q_ref.dtype)
    for i, (dst, slot) in {0: (raw_ref, 0), 1: (raw_ref, 1), 2: (k_ref, 0), 4: (k_ref, 1)}.items():
        col0 = NSA_Q + i * NSA_KV
        part = jnp.dot(xn, w_ref[:, col0:col0 + NSA_KV], preferred_element_type=F32)
        for grp in range(N_GROUPS):
            dst[slot, grp] = part[:, grp * HEAD_DIM:(grp + 1) * HEAD_DIM].astype(dst.dtype)
    for slot in range(2):
        part_t = _nt_dot(wvt_ref[slot], xn)
        for grp in range(N_GROUPS):
            vt_ref[slot, grp] = part_t[grp * HEAD_DIM:(grp + 1) * HEAD_DIM].astype(vt_ref.dtype)
    gl_ref[...] = jnp.dot(xn, w_ref[:, NSA_Q + 6 * NSA_KV:], preferred_element_type=F32)


def _nsa_in_proj(x, g, w, wvt, *, tm):
    b, s, d = x.shape
    sq = pl.Squeezed()
    grouped = lambda dtype: jax.ShapeDtypeStruct((2, b, N_GROUPS, s, HEAD_DIM), dtype)
    grouped_spec = pl.BlockSpec((2, sq, N_GROUPS, tm, HEAD_DIM), lambda i, t: (0, i, 0, t, 0))
    return pl.pallas_call(
        _nsa_in_proj_kernel,
        grid=(b, s // tm),
        in_specs=[
            pl.BlockSpec((sq, tm, d), lambda i, t: (i, t, 0)),
            pl.BlockSpec((1, d), lambda i, t: (0, 0)),
            pl.BlockSpec((d, NSA_IN_PAD), lambda i, t: (0, 0)),
            pl.BlockSpec((2, NSA_KV, d), lambda i, t: (0, 0, 0)),
        ],
        out_specs=[
            pl.BlockSpec((sq, tm, NSA_Q), lambda i, t: (i, t, 0)),
            grouped_spec, grouped_spec,
            pl.BlockSpec((2, sq, N_GROUPS, HEAD_DIM, tm), lambda i, t: (0, i, 0, 0, t)),
            pl.BlockSpec((sq, tm, LANES), lambda i, t: (i, t, 0)),
        ],
        out_shape=[
            jax.ShapeDtypeStruct((b, s, NSA_Q), BF16),
            grouped(F32), grouped(BF16),
            jax.ShapeDtypeStruct((2, b, N_GROUPS, HEAD_DIM, s), BF16),
            jax.ShapeDtypeStruct((b, s, LANES), F32),
        ],
        compiler_params=_cparams("parallel", "parallel"),
        name="nsa_in_proj",
    )(x, g.reshape(1, d), w, wvt)


def _compress_kernel(r_ref, pos_ref, w1_ref, b1_ref, w2_ref, b2_ref, w2t_ref, b2t_ref, o_ref, ot_ref):
    half = CMP_STRIDE * HEAD_DIM
    rows = r_ref[...]
    pos = pos_ref[...]
    lo = (rows + pos[:, :half]).astype(BF16)
    hi = (rows + pos[:, half:]).astype(BF16)
    part_lo = jnp.dot(lo, w1_ref[:half, :], preferred_element_type=F32)
    part_hi = jnp.dot(hi, w1_ref[half:, :], preferred_element_type=F32)
    hid = part_lo + pltpu.roll(part_hi, N_CMP_PAD - 1, axis=0) + b1_ref[...]
    hid = jax.nn.gelu(hid, approximate=True).astype(BF16)
    o_ref[...] = (jnp.dot(hid, w2_ref[...], preferred_element_type=F32) + b2_ref[...]).astype(o_ref.dtype)
    ot_ref[...] = (_nt_dot(w2t_ref[...], hid) + b2t_ref[...]).astype(ot_ref.dtype)


def _compress(raw, pos, w1, b1, w2, b2):
    two, b, g, n, width = raw.shape
    hidden = w1.shape[-1]
    sq = pl.Squeezed()
    per_branch = lambda *shape: pl.BlockSpec((sq,) + shape, lambda c, i, j: (c,) + (0,) * len(shape))
    return pl.pallas_call(
        _compress_kernel,
        grid=(two, b, g),
        in_specs=[
            pl.BlockSpec((sq, sq, sq, n, width), lambda c, i, j: (c, i, j, 0, 0)),
            per_branch(1, 2 * width), per_branch(2 * width, hidden), per_branch(1, hidden),
            per_branch(hidden, HEAD_DIM), per_branch(1, HEAD_DIM),
            per_branch(HEAD_DIM, hidden), per_branch(HEAD_DIM, 1),
        ],
        out_specs=[
            pl.BlockSpec((sq, sq, sq, n, HEAD_DIM), lambda c, i, j: (c, i, j, 0, 0)),
            pl.BlockSpec((sq, sq, sq, HEAD_DIM, n), lambda c, i, j: (c, i, j, 0, 0)),
        ],
        out_shape=[
            jax.ShapeDtypeStruct((two, b, g, n, HEAD_DIM), BF16),
            jax.ShapeDtypeStruct((two, b, g, HEAD_DIM, n), BF16),
        ],
        compiler_params=_cparams("parallel", "parallel", "parallel"),
        name="nsa_compress",
    )(raw, pos, w1, b1.reshape(two, 1, hidden), w2, b2.reshape(two, 1, HEAD_DIM),
      w2.transpose(0, 2, 1), b2.reshape(two, HEAD_DIM, 1))


AUG = 256
SEL_ROW0 = HEAD_DIM
POS_ROW0 = 2 * HEAD_DIM
N_PIECES = 3
PAD_LANE = POS_ROW0 + 2 * N_PIECES
POS_ROWS = 16
V_ROWS = 80
NQ = HEADS_PER_GROUP * TQ
BIG = 1e30
assert TQ == TK_WIN and TK_SEL % TQ == 0 and WINDOW % TK_WIN == 0


def _flash_reset(m_ref, acc_ref):
    m_ref[...] = jnp.full_like(m_ref, NEG)
    acc_ref[...] = jnp.zeros_like(acc_ref)


def _scores(kaug, tile, tk, qat_ref):
    k0 = pl.multiple_of(tile * tk, tk)
    return jnp.dot(kaug[pl.ds(k0, tk), :], qat_ref[...], preferred_element_type=F32)


def _flash_update(s, vt_tile, mask, m_ref, acc_ref):
    if mask is not None:
        s = jnp.where(mask, s, NEG)
    m_old = m_ref[...]
    m_new = jnp.maximum(m_old, jnp.max(s, axis=0, keepdims=True))
    p = jnp.exp(s - m_new).astype(BF16)
    acc_ref[...] = jnp.exp(m_old - m_new) * acc_ref[...] + jnp.dot(
        vt_tile, p, preferred_element_type=F32)
    m_ref[...] = m_new


def _flash_result(acc_ref):
    return acc_ref[pl.ds(0, HEAD_DIM), :] * (1.0 / acc_ref[pl.ds(HEAD_DIM, 1), :])


def _nsa_kernel(sl_ref, q_ref, kc_ref, vct_ref, ks_ref, vst_ref, kw_ref, vwt_ref, gl_ref, bg_ref,
                constk_ref, constc_ref, o_ref,
                kaug_s, kaug_w, kaug_c, vt_s, vt_w, vt_c, qat_ref, score_ref, m_ref, acc_ref, out_ref,
                sa_ref, sb_ref, gt_ref, qats_ref, tiles_ref):
    grp = pl.program_id(1)
    qi = pl.program_id(2)
    q0 = qi * TQ
    nrep = HEADS_PER_GROUP

    @pl.when(qi == 0)
    def _():
        s_len = ks_ref.shape[0]
        for kaug, row0, const, k in ((kaug_s, 0, constk_ref, ks_ref), (kaug_w, WINDOW, constk_ref, kw_ref),
                                     (kaug_c, 0, constc_ref, kc_ref)):
            kaug[pl.ds(row0, k.shape[0]), :] = const[...]
            kaug[pl.ds(row0, k.shape[0]), pl.ds(0, HEAD_DIM)] = k[...]
        pad_lane = lax.broadcasted_iota(jnp.int32, (WINDOW, AUG), 1) == PAD_LANE
        kaug_w[pl.ds(0, WINDOW), :] = jnp.where(pad_lane, 1.0, 0.0).astype(BF16)
        vt_w[:, pl.ds(0, WINDOW)] = jnp.zeros((V_ROWS, WINDOW), BF16)
        pad = V_ROWS - HEAD_DIM
        ones_rows = lambda n: jnp.where(
            lax.broadcasted_iota(jnp.int32, (pad, n), 0) == 0, 1.0, 0.0).astype(BF16)
        for c in range(vt_s.shape[0]):
            vt_s[c, pl.ds(0, HEAD_DIM), :] = vst_ref[:, pl.ds(c * TK_SEL, TK_SEL)]
            vt_s[c, pl.ds(HEAD_DIM, pad), :] = ones_rows(TK_SEL)
        vt_w[pl.ds(0, HEAD_DIM), pl.ds(WINDOW, s_len)] = vwt_ref[...]
        vt_w[pl.ds(HEAD_DIM, pad), pl.ds(WINDOW, s_len)] = ones_rows(s_len)
        vt_c[pl.ds(0, HEAD_DIM), :] = vct_ref[...]
        vt_c[pl.ds(HEAD_DIM, pad), :] = ones_rows(N_CMP_PAD)

    for half in range(nrep // 2):
        qt = q_ref[:, pl.ds(half * LANES, LANES)].astype(F32).T * (HEAD_DIM ** -0.5)
        for sub in range(2):
            q_rows = qt[sub * HEAD_DIM:(sub + 1) * HEAD_DIM].astype(BF16)
            for ref in (qat_ref, qats_ref):
                ref[pl.ds(0, HEAD_DIM), pl.ds((2 * half + sub) * TQ, TQ)] = q_rows
    qat_ref[pl.ds(SEL_ROW0, HEAD_DIM), :] = jnp.zeros((HEAD_DIM, NQ), BF16)
    piece = lax.broadcasted_iota(jnp.int32, (POS_ROWS, TQ), 0)
    tail0 = POS_ROW0 + POS_ROWS
    for r in range(nrep):
        tile = jnp.where(piece == PAD_LANE - POS_ROW0, -BIG, 0.0)
        for i in range(2 * N_PIECES):
            tile = jnp.where(piece == i, sl_ref[(grp * nrep + r) * N_PIECES + i % N_PIECES], tile)
        for ref in (qat_ref, qats_ref):
            ref[pl.ds(POS_ROW0, POS_ROWS), pl.ds(r * TQ, TQ)] = tile.astype(BF16)
    for ref in (qat_ref, qats_ref):
        ref[pl.ds(tail0, AUG - tail0), :] = jnp.zeros((AUG - tail0, NQ), BF16)

    t_row = q0 + (lax.broadcasted_iota(jnp.int32, (1, NQ), 1) & (TQ - 1))
    gt_ref[...] = jax.nn.sigmoid(gl_ref[...] + bg_ref[...]).T
    gate_row = lambda br: jnp.concatenate(
        [gt_ref[pl.ds(3 * (grp * nrep + r) + br, 1), :] for r in range(nrep)], axis=1)

    s = jnp.dot(kaug_c[...], qat_ref[...], preferred_element_type=F32)
    cmp_end = CMP_STRIDE * lax.broadcasted_iota(jnp.int32, (N_CMP_PAD, 1), 0) + (L_CMP - 1)
    mask_c = cmp_end <= t_row
    s = jnp.where(mask_c, s, NEG)
    e = jnp.where(mask_c, jnp.exp(s - jnp.max(s, axis=0, keepdims=True)), 0.0)
    d = jnp.sum(e, axis=0, keepdims=True)
    p = e * (1.0 / jnp.where(d > 0, d, 1.0))
    p_sum = sum(p[:, r * TQ:(r + 1) * TQ] for r in range(nrep))
    o_c = jnp.dot(vt_c[...], p.astype(BF16), preferred_element_type=F32)
    out_ref[...] = gate_row(0) * o_c[:HEAD_DIM]

    n_win = WINDOW // TK_WIN
    k0 = pl.multiple_of(q0, TK_WIN)
    s_all = jnp.dot(kaug_w[pl.ds(k0, WINDOW + TQ), :], qat_ref[...], preferred_element_type=F32)
    s_w = [s_all[j * TK_WIN:(j + 1) * TK_WIN] for j in range(n_win + 1)]
    not_after = q0 + lax.broadcasted_iota(jnp.int32, (TK_WIN, 1), 0) <= t_row
    s_w[0] = jnp.where(not_after, NEG, s_w[0])
    s_w[-1] = jnp.where(not_after, s_w[-1], NEG)
    m_w = functools.reduce(jnp.maximum, [jnp.max(x, axis=0, keepdims=True) for x in s_w])
    p_w = jnp.concatenate([jnp.exp(x - m_w).astype(BF16) for x in s_w], axis=0)
    acc_w = jnp.dot(vt_w[:, pl.ds(k0, WINDOW + TQ)], p_w, preferred_element_type=F32)
    out_ref[...] += gate_row(2) * (acc_w[:HEAD_DIM] * (1.0 / acc_w[HEAD_DIM:HEAD_DIM + 1]))

    n_sb = score_ref.shape[0]
    jb = lax.broadcasted_iota(jnp.int32, (n_sb, TQ), 0)
    t_sel = q0 + lax.broadcasted_iota(jnp.int32, (n_sb, TQ), 1)
    valid = jb * L_SEL <= t_sel

    n_full = q0 // TK_SEL
    blocks_per_tile = TK_SEL // L_SEL

    def write_selection(selected):
        sel_bias = jnp.where(selected, 0.0, -BIG).astype(BF16)
        for r in range(nrep):
            qats_ref[pl.ds(SEL_ROW0, n_sb), pl.ds(r * TQ, TQ)] = sel_bias
        picked = jnp.where(selected, 1.0, 0.0)
        bits = sum(jnp.max(picked[k * blocks_per_tile:(k + 1) * blocks_per_tile], axis=(0, 1),
                           keepdims=True) * float(2 ** k) for k in range(n_sb // blocks_per_tile))
        bits = bits[0, 0].astype(jnp.int32)
        count = jnp.int32(0)
        for k in range(n_sb // blocks_per_tile):
            tiles_ref[count] = k
            count = count + jnp.where((k < n_full) & (((bits >> k) & 1) == 1), 1, 0)
        tiles_ref[count] = n_full
        return count

    def rank_blocks():
        cmp_start = CMP_STRIDE * lax.broadcasted_iota(jnp.int32, (n_sb, N_CMP_PAD), 1)
        sb_start = L_SEL * lax.broadcasted_iota(jnp.int32, (n_sb, N_CMP_PAD), 0)
        overlap_t = ((cmp_start < sb_start + L_SEL) & (cmp_start + L_CMP > sb_start)).astype(F32)
        imp_t = jnp.dot(overlap_t, p_sum, precision=lax.Precision.HIGHEST,
                        preferred_element_type=F32)
        cur = t_sel // L_SEL
        forced = (jb == 0) | (jb == cur) | (jb == cur - 1)
        score = jnp.where(valid, jnp.where(forced, jnp.inf, imp_t), -jnp.inf)
        score_ref[...] = score
        chunks = [score[c * SUBLANES:(c + 1) * SUBLANES] for c in range(n_sb // SUBLANES)]
        ranks = [jnp.zeros((SUBLANES, TQ), F32) for _ in chunks]
        sub_row = lax.broadcasted_iota(jnp.int32, (SUBLANES, TQ), 0)
        for i in range(n_sb):
            row = score_ref[pl.ds(i, 1), :]
            for c, chunk in enumerate(chunks):
                later = jnp.where(row >= chunk, 1.0, 0.0)
                earlier = jnp.where(row > chunk, 1.0, 0.0)
                if i < c * SUBLANES:
                    ranks[c] = ranks[c] + later
                elif i >= (c + 1) * SUBLANES:
                    ranks[c] = ranks[c] + earlier
                else:
                    ranks[c] = ranks[c] + jnp.where(sub_row > i - c * SUBLANES, later, earlier)
        rank = jnp.concatenate(ranks, axis=0)
        return write_selection(valid & (rank < float(N_SELECT)))

    n_list = rank_blocks()

    _flash_reset(m_ref, acc_ref)
    key_s = lax.broadcasted_iota(jnp.int32, (TK_SEL, 1), 0)
    tail_mask = n_full * TK_SEL + key_s <= t_row
    sel_scores = lambda i: _scores(kaug_s, tiles_ref[i], TK_SEL, qats_ref)
    sel_update = lambda s_ref, i, mask: _flash_update(s_ref[...], vt_s[tiles_ref[i]], mask, m_ref, acc_ref)
    sa_ref[...] = sel_scores(0)

    def sel_pair(j, carry):
        i = 2 * j
        sb_ref[...] = sel_scores(i + 1)
        sel_update(sa_ref, i, None)
        sa_ref[...] = sel_scores(i + 2)
        sel_update(sb_ref, i + 1, None)
        return carry

    lax.fori_loop(0, n_list // 2, sel_pair, 0)

    @pl.when(n_list % 2 == 1)
    def _():
        sb_ref[...] = sel_scores(n_list)
        sel_update(sa_ref, n_list - 1, None)
        sel_update(sb_ref, n_list, tail_mask)

    @pl.when(n_list % 2 == 0)
    def _():
        sel_update(sa_ref, n_list, tail_mask)

    out_ref[...] += gate_row(1) * _flash_result(acc_ref)

    for half in range(nrep // 2):
        slab = jnp.concatenate([out_ref[:, pl.ds(2 * half * TQ, TQ)],
                                out_ref[:, pl.ds((2 * half + 1) * TQ, TQ)]], axis=0)
        o_ref[:, pl.ds(half * LANES, LANES)] = slab.T.astype(o_ref.dtype)


def _nsa_attention(slope_pieces, q, cmp, cmp_t, keys, vals_t, glogit, bgate, constk, constc):
    b, s, _ = q.shape
    g, nrep, dh = N_GROUPS, HEADS_PER_GROUP, HEAD_DIM
    n_sb = s // L_SEL
    sq = pl.Squeezed()
    slot = lambda c, *shape: pl.BlockSpec((sq, sq, sq) + shape, lambda i, j, t: (c, i, j, 0, 0))
    full = lambda *shape: pl.BlockSpec(shape, lambda i, j, t: (0,) * len(shape))
    return pl.pallas_call(
        _nsa_kernel,
        grid=(b, g, s // TQ),
        in_specs=[
            pl.BlockSpec(memory_space=pltpu.SMEM),
            pl.BlockSpec((sq, TQ, nrep * dh), lambda i, j, t: (i, t, j)),
            slot(0, N_CMP_PAD, dh), slot(1, dh, N_CMP_PAD),
            slot(0, s, dh), slot(0, dh, s), slot(1, s, dh), slot(1, dh, s),
            pl.BlockSpec((sq, TQ, LANES), lambda i, j, t: (i, t, 0)),
            full(1, LANES), full(s, AUG), full(N_CMP_PAD, AUG),
        ],
        out_specs=pl.BlockSpec((sq, TQ, nrep * dh), lambda i, j, t: (i, t, j)),
        out_shape=jax.ShapeDtypeStruct((b, s, NSA_Q), BF16),
        scratch_shapes=[
            pltpu.VMEM((s, AUG), BF16), pltpu.VMEM((WINDOW + s, AUG), BF16),
            pltpu.VMEM((N_CMP_PAD, AUG), BF16),
            pltpu.VMEM((s // TK_SEL, V_ROWS, TK_SEL), BF16), pltpu.VMEM((V_ROWS, WINDOW + s), BF16),
            pltpu.VMEM((V_ROWS, N_CMP_PAD), BF16),
            pltpu.VMEM((AUG, NQ), BF16),
            pltpu.VMEM((n_sb, TQ), F32),
            pltpu.VMEM((1, NQ), F32),
            pltpu.VMEM((V_ROWS, NQ), F32),
            pltpu.VMEM((dh, NQ), F32),
            pltpu.VMEM((TK_SEL, NQ), F32), pltpu.VMEM((TK_SEL, NQ), F32),
            pltpu.VMEM((LANES, TQ), F32),
            pltpu.VMEM((AUG, NQ), BF16),
            pltpu.SMEM((s // TK_SEL + 1,), jnp.int32),
        ],
        compiler_params=_cparams("arbitrary", "arbitrary", "arbitrary"),
        name="nsa_attention",
    )(slope_pieces, q, cmp, cmp_t, keys, vals_t, keys, vals_t, glogit, bgate, constk, constc)


def _position_pieces(pos):
    hi = (pos // L_SEL) * L_SEL
    return jnp.stack([hi] * N_PIECES + [pos - hi] * N_PIECES, axis=1).astype(F32)


def _nsa_constants(s):
    pos = jnp.arange(s)
    constk = jnp.zeros((s, AUG), F32)
    constk = constk.at[:, SEL_ROW0:SEL_ROW0 + s // L_SEL].set(jax.nn.one_hot(pos // L_SEL, s // L_SEL))
    constk = constk.at[:, POS_ROW0:POS_ROW0 + 2 * N_PIECES].set(_position_pieces(pos))
    cmp_end = CMP_STRIDE * jnp.arange(N_CMP_PAD) + (L_CMP - 1)
    constc = jnp.zeros((N_CMP_PAD, AUG), F32)
    constc = constc.at[:, POS_ROW0:POS_ROW0 + 2 * N_PIECES].set(_position_pieces(cmp_end))
    slopes = 2.0 ** (-8.0 * jnp.arange(1, N_HEADS + 1, dtype=F32) / N_HEADS)
    pieces, rest = [], slopes
    for _ in range(N_PIECES):
        piece = rest.astype(BF16).astype(F32)
        pieces.append(piece)
        rest = rest - piece
    return constk.astype(BF16), constc.astype(BF16), jnp.stack(pieces, axis=1).reshape(-1)


def _nsa_mixer(x, norm_g, w_in, b_gate, cmp_pos, cmp_w1, cmp_b1, cmp_w2, cmp_b2, w_out):
    b, s, d = x.shape
    m = b * s
    g, nrep, dh = N_GROUPS, HEADS_PER_GROUP, HEAD_DIM
    w_in_p = jnp.pad(w_in, ((0, 0), (0, NSA_IN_PAD - NSA_IN))).astype(BF16)
    v_cols = lambda i: w_in_p[:, NSA_Q + i * NSA_KV:NSA_Q + (i + 1) * NSA_KV].T
    q, raw, keys, vals_t, glogit = _nsa_in_proj(x, norm_g, w_in_p, jnp.stack([v_cols(3), v_cols(5)]),
                                                tm=512)
    raw = raw.reshape(2, b, g, s // CMP_STRIDE, CMP_STRIDE * dh)
    cmp, cmp_t = _compress(raw, cmp_pos.reshape(2, 1, L_CMP * dh), cmp_w1.astype(BF16), cmp_b1,
                           cmp_w2.astype(BF16), cmp_b2)
    constk, constc, slope_pieces = _nsa_constants(s)
    bgate = jnp.pad(b_gate, (0, LANES - 3 * N_HEADS)).reshape(1, LANES)
    y = _nsa_attention(slope_pieces, q, cmp, cmp_t, keys, vals_t, glogit, bgate, constk, constc)
    return _matmul_res(y.reshape(m, NSA_Q), w_out.astype(BF16), x.reshape(m, d), tm=512,
                       name="nsa_out_proj").reshape(b, s, d)


TT = 256
SUB = SUBLANES


GATE_BAND = 256


def _gate_bands():
    block = D_RNN // LRU_BLOCKS
    bands = []
    for c0 in range(0, D_RNN, GATE_BAND):
        width = min(GATE_BAND, D_RNN - c0)
        k_lo = (c0 // block) * block // LANES * LANES
        k_hi = -(-(((c0 + width - 1) // block + 1) * block) // LANES) * LANES
        bands.append((k_lo, min(k_hi, D_RNN), c0, width))
    return tuple(bands)


GATE_BANDS = _gate_bands()


def _pack_gate_weights(w_a, w_x):
    dense = lambda w: jax.scipy.linalg.block_diag(*[w[i] for i in range(LRU_BLOCKS)])
    slabs = []
    for k_lo, k_hi, c0, width in GATE_BANDS:
        band = [jnp.pad(dense(w)[k_lo:k_hi, c0:c0 + width], ((0, 0), (0, GATE_BAND - width)))
                for w in (w_a, w_x)]
        slabs.append(jnp.concatenate(band, axis=1))
    return jnp.concatenate(slabs, axis=0).astype(BF16)


def _sigmoid(z):
    return 0.5 * jnp.tanh(0.5 * z) + 0.5


def _lru_kernel(x_ref, g_ref, win_ref, cw_ref, cb_ref, wax_ref, bax_ref, lam_ref, wout_ref, o_ref,
                ext_ref, a_ref, u_ref, h_ref, gate_ref):
    ti = pl.program_id(1)

    @pl.when(ti == 0)
    def _():
        ext_ref[pl.ds(TT, SUB), :] = jnp.zeros((SUB, D_RNN), F32)
        h_ref[...] = jnp.zeros_like(h_ref)

    xn = _rms(x_ref[...], g_ref[...]).astype(BF16)
    gate_ref[...] = jax.nn.gelu(
        jnp.dot(xn, win_ref[:, :D_RNN], preferred_element_type=F32), approximate=True)
    ext_ref[pl.ds(0, SUB), :] = ext_ref[pl.ds(TT, SUB), :]
    ext_ref[pl.ds(SUB, TT), :] = jnp.dot(xn, win_ref[:, D_RNN:], preferred_element_type=F32)
    cw = cw_ref[...]
    xr = cb_ref[...] + sum(
        cw[w:w + 1, :] * ext_ref[pl.ds(SUB - (CONV_W - 1) + w, TT), :] for w in range(CONV_W))

    lam = lam_ref[...]
    softplus_neg = jnp.maximum(-lam, 0.0) + jnp.log1p(jnp.exp(-jnp.abs(lam)))
    xr_b = xr.astype(BF16)
    bias = bax_ref[...]
    row0 = 0
    for k_lo, k_hi, c0, width in GATE_BANDS:
        z = jnp.dot(xr_b[:, k_lo:k_hi], wax_ref[row0:row0 + k_hi - k_lo, :], preferred_element_type=F32)
        row0 += k_hi - k_lo
        r_gate = _sigmoid(z[:, :width] + bias[:, c0:c0 + width])
        i_gate = _sigmoid(z[:, GATE_BAND:GATE_BAND + width] + bias[:, D_RNN + c0:D_RNN + c0 + width])
        a = jnp.exp(-LRU_C * r_gate * softplus_neg[:, c0:c0 + width])
        a_ref[:, c0:c0 + width] = a
        u_ref[:, c0:c0 + width] = jnp.sqrt(jnp.maximum(1.0 - a * a, 0.0)) * (i_gate * xr[:, c0:c0 + width])

    row = lax.broadcasted_iota(jnp.int32, (SUB, D_RNN), 0)

    def scan_rows(c, h_prev):
        r0 = pl.multiple_of(c * SUB, SUB)
        a_c = a_ref[pl.ds(r0, SUB), :]
        u_c = u_ref[pl.ds(r0, SUB), :]
        shift = 1
        while shift < SUB:
            keep = row >= shift
            u_c = u_c + a_c * jnp.where(keep, pltpu.roll(u_c, shift, axis=0), 0.0)
            a_c = a_c * jnp.where(keep, pltpu.roll(a_c, shift, axis=0), 1.0)
            shift *= 2
        h_c = u_c + a_c * h_prev
        u_ref[pl.ds(r0, SUB), :] = h_c
        return jnp.broadcast_to(h_c[SUB - 1:SUB, :], (SUB, D_RNN))

    h_ref[...] = lax.fori_loop(0, TT // SUB, scan_rows, h_ref[...])
    gated = (u_ref[...] * gate_ref[...]).astype(BF16)
    o_ref[...] = x_ref[...] + jnp.dot(gated, wout_ref[...], preferred_element_type=F32)


def _lru_mixer(x, norm_g, w_in, conv_w, conv_b, w_a, b_a, w_x, b_x, lam, w_out):
    b, s, d = x.shape
    wax = _pack_gate_weights(w_a, w_x)
    bax = jnp.concatenate([b_a, b_x])
    sq = pl.Squeezed()
    resident = lambda *shape: pl.BlockSpec(shape, lambda i, t: (0,) * len(shape),
                                           pipeline_mode=pl.Buffered(1))
    return pl.pallas_call(
        _lru_kernel,
        grid=(b, s // TT),
        in_specs=[
            pl.BlockSpec((sq, TT, d), lambda i, t: (i, t, 0)),
            resident(1, d), resident(d, 2 * D_RNN), resident(CONV_W, D_RNN), resident(1, D_RNN),
            resident(*wax.shape), resident(1, 2 * D_RNN), resident(1, D_RNN), resident(D_RNN, d),
        ],
        out_specs=pl.BlockSpec((sq, TT, d), lambda i, t: (i, t, 0)),
        out_shape=jax.ShapeDtypeStruct((b, s, d), F32),
        scratch_shapes=[
            pltpu.VMEM((TT + SUB, D_RNN), F32),
            pltpu.VMEM((TT, D_RNN), F32),
            pltpu.VMEM((TT, D_RNN), F32),
            pltpu.VMEM((SUB, D_RNN), F32),
            pltpu.VMEM((TT, D_RNN), F32),
        ],
        compiler_params=_cparams("parallel", "arbitrary"),
        name="lru_block",
    )(x, norm_g.reshape(1, d), w_in.astype(BF16), conv_w, conv_b.reshape(1, -1), wax,
      bax.reshape(1, -1), lam.reshape(1, -1), w_out.astype(BF16))


def kernel(x, norm_mix, norm_ffn, norm_final, nsa_w_in, nsa_b_gate, nsa_cmp_pos, nsa_cmp_w1,
           nsa_cmp_b1, nsa_cmp_w2, nsa_cmp_b2, nsa_w_out, lru_w_in, lru_conv_w, lru_conv_b,
           lru_w_a, lru_b_a, lru_w_x, lru_b_x, lru_lambda, lru_w_out, ffn_w_in, ffn_w_out):
    b, s, d = x.shape
    ffn = functools.partial(_ffn, tm=1024, tf=256)
    x = _nsa_mixer(x, norm_mix[0], nsa_w_in[0], nsa_b_gate[0], nsa_cmp_pos[0], nsa_cmp_w1[0],
                   nsa_cmp_b1[0], nsa_cmp_w2[0], nsa_cmp_b2[0], nsa_w_out[0])
    x = ffn(x.reshape(b * s, d), norm_ffn[0], ffn_w_in[0].astype(BF16), ffn_w_out[0].astype(BF16),
            norm_final, final_norm=False, name="ffn0").reshape(b, s, d)
    x = _lru_mixer(x, norm_mix[1], lru_w_in[0], lru_conv_w[0], lru_conv_b[0], lru_w_a[0], lru_b_a[0],
                   lru_w_x[0], lru_b_x[0], lru_lambda[0], lru_w_out[0])
    x = ffn(x.reshape(b * s, d), norm_ffn[1], ffn_w_in[1].astype(BF16), ffn_w_out[1].astype(BF16),
            norm_final, final_norm=True, name="ffn1").reshape(b, s, d)
    return x
```

```python
import functools

import jax
import jax.numpy as jnp
from jax import lax
from jax.experimental import pallas as pl
from jax.experimental.pallas import tpu as pltpu

F32 = jnp.float32
BF16 = jnp.bfloat16

D_MODEL = 1024
N_HEADS = 16
N_GROUPS = 4
HEADS_PER_GROUP = N_HEADS // N_GROUPS
HEAD_DIM = 64
L_CMP = 32
CMP_STRIDE = 16
L_SEL = 64
N_SELECT = 16
WINDOW = 512
NSA_Q = N_HEADS * HEAD_DIM
NSA_KV = N_GROUPS * HEAD_DIM
NSA_IN = NSA_Q + 6 * NSA_KV + 3 * N_HEADS
D_RNN = 1408
LRU_BLOCKS = 8
LRU_C = 8.0
CONV_W = 4
D_FF = 2816
EPS = 1e-6

LANES = 128
SUBLANES = 8
NEG = -1e30
VMEM_LIMIT = 56 * 1024 * 1024

TQ = 128
TK_SEL = 512
TK_WIN = 128
N_CMP_PAD = 256


def _cparams(*sem):
    return pltpu.CompilerParams(dimension_semantics=sem, vmem_limit_bytes=VMEM_LIMIT)


def _rms(x, g):
    ms = jnp.mean(x * x, axis=-1, keepdims=True)
    return x * lax.rsqrt(ms + EPS) * g


def _nt_dot(a, b):
    return lax.dot_general(a, b, (((1,), (1,)), ((), ())), preferred_element_type=F32)


def _norm_matmul_kernel(x_ref, g_ref, w_ref, o_ref, xn_ref):
    @pl.when(pl.program_id(1) == 0)
    def _():
        xn_ref[...] = _rms(x_ref[...], g_ref[...]).astype(BF16)

    o_ref[...] = jnp.dot(xn_ref[...], w_ref[...], preferred_element_type=F32).astype(o_ref.dtype)


def _norm_matmul(x, g, w, *, tm, tn, name):
    m, k = x.shape
    n = w.shape[1]
    return pl.pallas_call(
        _norm_matmul_kernel,
        grid=(m // tm, n // tn),
        in_specs=[
            pl.BlockSpec((tm, k), lambda i, j: (i, 0)),
            pl.BlockSpec((1, k), lambda i, j: (0, 0)),
            pl.BlockSpec((k, tn), lambda i, j: (0, j)),
        ],
        out_specs=pl.BlockSpec((tm, tn), lambda i, j: (i, j)),
        out_shape=jax.ShapeDtypeStruct((m, n), F32),
        scratch_shapes=[pltpu.VMEM((tm, k), BF16)],
        compiler_params=_cparams("parallel", "arbitrary"),
        name=name,
    )(x, g.reshape(1, k), w)


def _matmul_res_kernel(a_ref, w_ref, r_ref, o_ref):
    o_ref[...] = r_ref[...] + jnp.dot(a_ref[...], w_ref[...], preferred_element_type=F32)


def _matmul_res(a, w, res, *, tm, name):
    m, k = a.shape
    n = w.shape[1]
    return pl.pallas_call(
        _matmul_res_kernel,
        grid=(m // tm,),
        in_specs=[
            pl.BlockSpec((tm, k), lambda i: (i, 0)),
            pl.BlockSpec((k, n), lambda i: (0, 0)),
            pl.BlockSpec((tm, n), lambda i: (i, 0)),
        ],
        out_specs=pl.BlockSpec((tm, n), lambda i: (i, 0)),
        out_shape=jax.ShapeDtypeStruct((m, n), F32),
        compiler_params=_cparams("parallel"),
        name=name,
    )(a, w, res)


def _ffn_kernel(x_ref, g_ref, wg_ref, wu_ref, wo_ref, gf_ref, o_ref, xn_ref, acc_ref, *, final_norm):
    j = pl.program_id(1)

    @pl.when(j == 0)
    def _():
        xn_ref[...] = _rms(x_ref[...], g_ref[...]).astype(BF16)
        acc_ref[...] = jnp.zeros_like(acc_ref)

    xn = xn_ref[...]
    gate = jnp.dot(xn, wg_ref[...], preferred_element_type=F32)
    up = jnp.dot(xn, wu_ref[...], preferred_element_type=F32)
    hid = gate * jax.nn.sigmoid(gate) * up
    acc_ref[...] += jnp.dot(hid.astype(BF16), wo_ref[...], preferred_element_type=F32)

    @pl.when(j == pl.num_programs(1) - 1)
    def _():
        y = x_ref[...] + acc_ref[...]
        if final_norm:
            y = _rms(y, gf_ref[...])
        o_ref[...] = y


def _ffn(x, g, w_in, w_out, g_final, *, final_norm, tm, tf, name):
    m, d = x.shape
    nf = D_FF // tf
    return pl.pallas_call(
        functools.partial(_ffn_kernel, final_norm=final_norm),
        grid=(m // tm, nf),
        in_specs=[
            pl.BlockSpec((tm, d), lambda i, j: (i, 0)),
            pl.BlockSpec((1, d), lambda i, j: (0, 0)),
            pl.BlockSpec((d, tf), lambda i, j: (0, j)),
            pl.BlockSpec((d, tf), lambda i, j: (0, j + nf)),
            pl.BlockSpec((tf, d), lambda i, j: (j, 0)),
            pl.BlockSpec((1, d), lambda i, j: (0, 0)),
        ],
        out_specs=pl.BlockSpec((tm, d), lambda i, j: (i, 0)),
        out_shape=jax.ShapeDtypeStruct((m, d), F32),
        scratch_shapes=[pltpu.VMEM((tm, d), BF16), pltpu.VMEM((tm, d), F32)],
        compiler_params=_cparams("parallel", "arbitrary"),
        name=name,
    )(x, g.reshape(1, d), w_in, w_in, w_out, g_final.reshape(1, d))


NSA_IN_PAD = -(-NSA_IN // LANES) * LANES


def _nsa_in_proj_kernel(x_ref, g_ref, w_ref, wvt_ref, q_ref, raw_ref, k_ref, vt_ref, gl_ref):
    xn = _rms(x_ref[...], g_ref[...]).astype(BF16)
    q_ref[...] = jnp.dot(xn, w_ref[:, :NSA_Q], preferred_element_type=F32).astype(q_ref.dtype)
    for i, (dst, slot) in {0: (raw_ref, 0), 1: (raw_ref, 1), 2: (k_ref, 0), 4: (k_ref, 1)}.items():
        col0 = NSA_Q + i * NSA_KV
        part = jnp.dot(xn, w_ref[:, col0:col0 + NSA_KV], preferred_element_type=F32)
        for grp in range(N_GROUPS):
            dst[slot, grp] = part[:, grp * HEAD_DIM:(grp + 1) * HEAD_DIM].astype(dst.dtype)
    for slot in range(2):
        part_t = _nt_dot(wvt_ref[slot], xn)
        for grp in range(N_GROUPS):
            vt_ref[slot, grp] = part_t[grp * HEAD_DIM:(grp + 1) * HEAD_DIM].astype(vt_ref.dtype)
    gl_ref[...] = jnp.dot(xn, w_ref[:, NSA_Q + 6 * NSA_KV:], preferred_element_type=F32)


def _nsa_in_proj(x, g, w, wvt, *, tm):
    b, s, d = x.shape
    sq = pl.Squeezed()
    grouped = lambda dtype: jax.ShapeDtypeStruct((2, b, N_GROUPS, s, HEAD_DIM), dtype)
    grouped_spec = pl.BlockSpec((2, sq, N_GROUPS, tm, HEAD_DIM), lambda i, t: (0, i, 0, t, 0))
    return pl.pallas_call(
        _nsa_in_proj_kernel,
        grid=(b, s // tm),
        in_specs=[
            pl.BlockSpec((sq, tm, d), lambda i, t: (i, t, 0)),
            pl.BlockSpec((1, d), lambda i, t: (0, 0)),
            pl.BlockSpec((d, NSA_IN_PAD), lambda i, t: (0, 0)),
            pl.BlockSpec((2, NSA_KV, d), lambda i, t: (0, 0, 0)),
        ],
        out_specs=[
            pl.BlockSpec((sq, tm, NSA_Q), lambda i, t: (i, t, 0)),
            grouped_spec, grouped_spec,
            pl.BlockSpec((2, sq, N_GROUPS, HEAD_DIM, tm), lambda i, t: (0, i, 0, 0, t)),
            pl.BlockSpec((sq, tm, LANES), lambda i, t: (i, t, 0)),
        ],
        out_shape=[
            jax.ShapeDtypeStruct((b, s, NSA_Q), BF16),
            grouped(F32), grouped(BF16),
            jax.ShapeDtypeStruct((2, b, N_GROUPS, HEAD_DIM, s), BF16),
            jax.ShapeDtypeStruct((b, s, LANES), F32),
        ],
        compiler_params=_cparams("parallel", "parallel"),
        name="nsa_in_proj",
    )(x, g.reshape(1, d), w, wvt)


def _compress_kernel(r_ref, pos_ref, w1_ref, b1_ref, w2_ref, b2_ref, w2t_ref, b2t_ref, o_ref, ot_ref):
    half = CMP_STRIDE * HEAD_DIM
    rows = r_ref[...]
    pos = pos_ref[...]
    lo = (rows + pos[:, :half]).astype(BF16)
    hi = (rows + pos[:, half:]).astype(BF16)
    part_lo = jnp.dot(lo, w1_ref[:half, :], preferred_element_type=F32)
    part_hi = jnp.dot(hi, w1_ref[half:, :], preferred_element_type=F32)
    hid = part_lo + pltpu.roll(part_hi, N_CMP_PAD - 1, axis=0) + b1_ref[...]
    hid = jax.nn.gelu(hid, approximate=True).astype(BF16)
    o_ref[...] = (jnp.dot(hid, w2_ref[...], preferred_element_type=F32) + b2_ref[...]).astype(o_ref.dtype)
    ot_ref[...] = (_nt_dot(w2t_ref[...], hid) + b2t_ref[...]).astype(ot_ref.dtype)


def _compress(raw, pos, w1, b1, w2, b2):
    two, b, g, n, width = raw.shape
    hidden = w1.shape[-1]
    sq = pl.Squeezed()
    per_branch = lambda *shape: pl.BlockSpec((sq,) + shape, lambda c, i, j: (c,) + (0,) * len(shape))
    return pl.pallas_call(
        _compress_kernel,
        grid=(two, b, g),
        in_specs=[
            pl.BlockSpec((sq, sq, sq, n, width), lambda c, i, j: (c, i, j, 0, 0)),
            per_branch(1, 2 * width), per_branch(2 * width, hidden), per_branch(1, hidden),
            per_branch(hidden, HEAD_DIM), per_branch(1, HEAD_DIM),
            per_branch(HEAD_DIM, hidden), per_branch(HEAD_DIM, 1),
        ],
        out_specs=[
            pl.BlockSpec((sq, sq, sq, n, HEAD_DIM), lambda c, i, j: (c, i, j, 0, 0)),
            pl.BlockSpec((sq, sq, sq, HEAD_DIM, n), lambda c, i, j: (c, i, j, 0, 0)),
        ],
        out_shape=[
            jax.ShapeDtypeStruct((two, b, g, n, HEAD_DIM), BF16),
            jax.ShapeDtypeStruct((two, b, g, HEAD_DIM, n), BF16),
        ],
        compiler_params=_cparams("parallel", "parallel", "parallel"),
        name="nsa_compress",
    )(raw, pos, w1, b1.reshape(two, 1, hidden), w2, b2.reshape(two, 1, HEAD_DIM),
      w2.transpose(0, 2, 1), b2.reshape(two, HEAD_DIM, 1))


AUG = 256
SEL_ROW0 = HEAD_DIM
POS_ROW0 = 2 * HEAD_DIM
N_PIECES = 3
PAD_LANE = POS_ROW0 + 2 * N_PIECES
POS_ROWS = 16
V_ROWS = 80
NQ = HEADS_PER_GROUP * TQ
BIG = 1e30
assert TQ == TK_WIN and TK_SEL % TQ == 0 and WINDOW % TK_WIN == 0


def _flash_reset(m_ref, acc_ref):
    m_ref[...] = jnp.full_like(m_ref, NEG)
    acc_ref[...] = jnp.zeros_like(acc_ref)


def _scores(kaug, tile, tk, qat_ref):
    k0 = pl.multiple_of(tile * tk, tk)
    return jnp.dot(kaug[pl.ds(k0, tk), :], qat_ref[...], preferred_element_type=F32)


def _flash_update(s, vt_tile, mask, m_ref, acc_ref):
    if mask is not None:
        s = jnp.where(mask, s, NEG)
    m_old = m_ref[...]
    m_new = jnp.maximum(m_old, jnp.max(s, axis=0, keepdims=True))
    p = jnp.exp(s - m_new).astype(BF16)
    acc_ref[...] = jnp.exp(m_old - m_new) * acc_ref[...] + jnp.dot(
        vt_tile, p, preferred_element_type=F32)
    m_ref[...] = m_new


def _flash_result(acc_ref):
    return acc_ref[pl.ds(0, HEAD_DIM), :] * (1.0 / acc_ref[pl.ds(HEAD_DIM, 1), :])


GROUPS_PER_STEP = 2


def _nsa_kernel(sl_ref, q_ref, kc_all, vct_all, ks_all, vst_all, kw_all, vwt_all, gl_ref, bg_ref,
                constk_ref, constc_ref, o_ref, *scratch):
    qi = pl.program_id(2)
    gt_ref = scratch[-1]
    per_group = (kc_all, vct_all, ks_all, vst_all, kw_all, vwt_all) + scratch[:-1]
    groups = [tuple(ref.at[h] for ref in per_group) for h in range(GROUPS_PER_STEP)]

    @pl.when(qi == 0)
    def _():
        for refs in groups:
            _nsa_assemble(constk_ref, constc_ref, *refs)

    gt_ref[...] = jax.nn.sigmoid(gl_ref[...] + bg_ref[...]).T
    tiles = [_nsa_tile(pl.program_id(1) * GROUPS_PER_STEP + h, h * HEADS_PER_GROUP * HEAD_DIM, qi,
                       sl_ref, q_ref, o_ref, gt_ref, *refs) for h, refs in enumerate(groups)]
    heads = list(tiles)
    while heads:
        heads = [tile for tile in heads if not next(tile)]
    for tile in tiles:
        next(tile, None)


def _nsa_assemble(constk_ref, constc_ref, kc_ref, vct_ref, ks_ref, vst_ref, kw_ref, vwt_ref,
                  kaug_s, kaug_w, kaug_c, vt_s, vt_w, vt_c, *unused):
    s_len = ks_ref.shape[0]
    for kaug, row0, const, k in ((kaug_s, 0, constk_ref, ks_ref), (kaug_w, WINDOW, constk_ref, kw_ref),
                                 (kaug_c, 0, constc_ref, kc_ref)):
        kaug[pl.ds(row0, k.shape[0]), :] = const[...]
        kaug[pl.ds(row0, k.shape[0]), pl.ds(0, HEAD_DIM)] = k[...]
    pad_lane = lax.broadcasted_iota(jnp.int32, (WINDOW, AUG), 1) == PAD_LANE
    kaug_w[pl.ds(0, WINDOW), :] = jnp.where(pad_lane, 1.0, 0.0).astype(BF16)
    vt_w[:, pl.ds(0, WINDOW)] = jnp.zeros((V_ROWS, WINDOW), BF16)
    pad = V_ROWS - HEAD_DIM
    ones_rows = lambda n: jnp.where(
        lax.broadcasted_iota(jnp.int32, (pad, n), 0) == 0, 1.0, 0.0).astype(BF16)
    for c in range(vt_s.shape[0]):
        vt_s[c, pl.ds(0, HEAD_DIM), :] = vst_ref[:, pl.ds(c * TK_SEL, TK_SEL)]
        vt_s[c, pl.ds(HEAD_DIM, pad), :] = ones_rows(TK_SEL)
    vt_w[pl.ds(0, HEAD_DIM), pl.ds(WINDOW, s_len)] = vwt_ref[...]
    vt_w[pl.ds(HEAD_DIM, pad), pl.ds(WINDOW, s_len)] = ones_rows(s_len)
    vt_c[pl.ds(0, HEAD_DIM), :] = vct_ref[...]
    vt_c[pl.ds(HEAD_DIM, pad), :] = ones_rows(N_CMP_PAD)


def _nsa_tile(grp, lane0, qi, sl_ref, q_ref, o_ref, gt_ref, kc_ref, vct_ref, ks_ref, vst_ref, kw_ref,
              vwt_ref, kaug_s, kaug_w, kaug_c, vt_s, vt_w, vt_c, qat_ref, score_ref, m_ref, acc_ref,
              out_ref, sa_ref, sb_ref, qats_ref, tiles_ref):
    q0 = qi * TQ
    nrep = HEADS_PER_GROUP

    for half in range(nrep // 2):
        qt = q_ref[:, pl.ds(lane0 + half * LANES, LANES)].astype(F32).T * (HEAD_DIM ** -0.5)
        for sub in range(2):
            q_rows = qt[sub * HEAD_DIM:(sub + 1) * HEAD_DIM].astype(BF16)
            for ref in (qat_ref, qats_ref):
                ref[pl.ds(0, HEAD_DIM), pl.ds((2 * half + sub) * TQ, TQ)] = q_rows
    qat_ref[pl.ds(SEL_ROW0, HEAD_DIM), :] = jnp.zeros((HEAD_DIM, NQ), BF16)
    piece = lax.broadcasted_iota(jnp.int32, (POS_ROWS, TQ), 0)
    tail0 = POS_ROW0 + POS_ROWS
    for r in range(nrep):
        tile = jnp.where(piece == PAD_LANE - POS_ROW0, -BIG, 0.0)
        for i in range(2 * N_PIECES):
            tile = jnp.where(piece == i, sl_ref[(grp * nrep + r) * N_PIECES + i % N_PIECES], tile)
        for ref in (qat_ref, qats_ref):
            ref[pl.ds(POS_ROW0, POS_ROWS), pl.ds(r * TQ, TQ)] = tile.astype(BF16)
    for ref in (qat_ref, qats_ref):
        ref[pl.ds(tail0, AUG - tail0), :] = jnp.zeros((AUG - tail0, NQ), BF16)

    yield False
    t_row = q0 + (lax.broadcasted_iota(jnp.int32, (1, NQ), 1) & (TQ - 1))
    gate_row = lambda br: jnp.concatenate(
        [gt_ref[pl.ds(3 * (grp * nrep + r) + br, 1), :] for r in range(nrep)], axis=1)

    s = jnp.dot(kaug_c[...], qat_ref[...], preferred_element_type=F32)
    cmp_end = CMP_STRIDE * lax.broadcasted_iota(jnp.int32, (N_CMP_PAD, 1), 0) + (L_CMP - 1)
    mask_c = cmp_end <= t_row
    s = jnp.where(mask_c, s, NEG)
    e = jnp.where(mask_c, jnp.exp(s - jnp.max(s, axis=0, keepdims=True)), 0.0)
    d = jnp.sum(e, axis=0, keepdims=True)
    p = e * (1.0 / jnp.where(d > 0, d, 1.0))
    p_sum = sum(p[:, r * TQ:(r + 1) * TQ] for r in range(nrep))
    o_c = jnp.dot(vt_c[...], p.astype(BF16), preferred_element_type=F32)
    out_ref[...] = gate_row(0) * o_c[:HEAD_DIM]
    yield False

    n_win = WINDOW // TK_WIN
    k0 = pl.multiple_of(q0, TK_WIN)
    s_all = jnp.dot(kaug_w[pl.ds(k0, WINDOW + TQ), :], qat_ref[...], preferred_element_type=F32)
    s_w = [s_all[j * TK_WIN:(j + 1) * TK_WIN] for j in range(n_win + 1)]
    not_after = q0 + lax.broadcasted_iota(jnp.int32, (TK_WIN, 1), 0) <= t_row
    s_w[0] = jnp.where(not_after, NEG, s_w[0])
    s_w[-1] = jnp.where(not_after, s_w[-1], NEG)
    m_w = functools.reduce(jnp.maximum, [jnp.max(x, axis=0, keepdims=True) for x in s_w])
    p_w = jnp.concatenate([jnp.exp(x - m_w).astype(BF16) for x in s_w], axis=0)
    acc_w = jnp.dot(vt_w[:, pl.ds(k0, WINDOW + TQ)], p_w, preferred_element_type=F32)
    out_ref[...] += gate_row(2) * (acc_w[:HEAD_DIM] * (1.0 / acc_w[HEAD_DIM:HEAD_DIM + 1]))
    yield False

    n_sb = score_ref.shape[0]
    jb = lax.broadcasted_iota(jnp.int32, (n_sb, TQ), 0)
    t_sel = q0 + lax.broadcasted_iota(jnp.int32, (n_sb, TQ), 1)
    valid = jb * L_SEL <= t_sel

    n_full = q0 // TK_SEL
    blocks_per_tile = TK_SEL // L_SEL

    def write_selection(selected):
        sel_bias = jnp.where(selected, 0.0, -BIG).astype(BF16)
        for r in range(nrep):
            qats_ref[pl.ds(SEL_ROW0, n_sb), pl.ds(r * TQ, TQ)] = sel_bias
        picked = jnp.where(selected, 1.0, 0.0)
        bits = sum(jnp.max(picked[k * blocks_per_tile:(k + 1) * blocks_per_tile], axis=(0, 1),
                           keepdims=True) * float(2 ** k) for k in range(n_sb // blocks_per_tile))
        bits = bits[0, 0].astype(jnp.int32)
        count = jnp.int32(0)
        for k in range(n_sb // blocks_per_tile):
            tiles_ref[count] = k
            count = count + jnp.where((k < n_full) & (((bits >> k) & 1) == 1), 1, 0)
        tiles_ref[count] = n_full
        return count

    def rank_blocks():
        cmp_start = CMP_STRIDE * lax.broadcasted_iota(jnp.int32, (n_sb, N_CMP_PAD), 1)
        sb_start = L_SEL * lax.broadcasted_iota(jnp.int32, (n_sb, N_CMP_PAD), 0)
        overlap_t = ((cmp_start < sb_start + L_SEL) & (cmp_start + L_CMP > sb_start)).astype(F32)
        imp_t = jnp.dot(overlap_t, p_sum, precision=lax.Precision.HIGHEST,
                        preferred_element_type=F32)
        cur = t_sel // L_SEL
        forced = (jb == 0) | (jb == cur) | (jb == cur - 1)
        score = jnp.where(valid, jnp.where(forced, jnp.inf, imp_t), -jnp.inf)
        score_ref[...] = score
        chunks = [score[c * SUBLANES:(c + 1) * SUBLANES] for c in range(n_sb // SUBLANES)]
        ranks = [jnp.zeros((SUBLANES, TQ), F32) for _ in chunks]
        sub_row = lax.broadcasted_iota(jnp.int32, (SUBLANES, TQ), 0)
        for i in range(n_sb):
            row = score_ref[pl.ds(i, 1), :]
            for c, chunk in enumerate(chunks):
                later = jnp.where(row >= chunk, 1.0, 0.0)
                earlier = jnp.where(row > chunk, 1.0, 0.0)
                if i < c * SUBLANES:
                    ranks[c] = ranks[c] + later
                elif i >= (c + 1) * SUBLANES:
                    ranks[c] = ranks[c] + earlier
                else:
                    ranks[c] = ranks[c] + jnp.where(sub_row > i - c * SUBLANES, later, earlier)
        rank = jnp.concatenate(ranks, axis=0)
        return write_selection(valid & (rank < float(N_SELECT)))

    n_list = rank_blocks()
    yield False

    _flash_reset(m_ref, acc_ref)
    key_s = lax.broadcasted_iota(jnp.int32, (TK_SEL, 1), 0)
    tail_mask = n_full * TK_SEL + key_s <= t_row
    sel_scores = lambda i: _scores(kaug_s, tiles_ref[i], TK_SEL, qats_ref)
    sel_update = lambda s_ref, i, mask: _flash_update(s_ref[...], vt_s[tiles_ref[i]], mask, m_ref, acc_ref)
    sa_ref[...] = sel_scores(0)
    yield True

    def sel_pair(j, carry):
        i = 2 * j
        sb_ref[...] = sel_scores(i + 1)
        sel_update(sa_ref, i, None)
        sa_ref[...] = sel_scores(i + 2)
        sel_update(sb_ref, i + 1, None)
        return carry

    lax.fori_loop(0, n_list // 2, sel_pair, 0)

    @pl.when(n_list % 2 == 1)
    def _():
        sb_ref[...] = sel_scores(n_list)
        sel_update(sa_ref, n_list - 1, None)
        sel_update(sb_ref, n_list, tail_mask)

    @pl.when(n_list % 2 == 0)
    def _():
        sel_update(sa_ref, n_list, tail_mask)

    out_ref[...] += gate_row(1) * _flash_result(acc_ref)

    for half in range(nrep // 2):
        slab = jnp.concatenate([out_ref[:, pl.ds(2 * half * TQ, TQ)],
                                out_ref[:, pl.ds((2 * half + 1) * TQ, TQ)]], axis=0)
        o_ref[:, pl.ds(lane0 + half * LANES, LANES)] = slab.T.astype(o_ref.dtype)


def _nsa_attention(slope_pieces, q, cmp, cmp_t, keys, vals_t, glogit, bgate, constk, constc):
    b, s, _ = q.shape
    g, nrep, dh = N_GROUPS, HEADS_PER_GROUP, HEAD_DIM
    n_sb = s // L_SEL
    sq = pl.Squeezed()
    gp = GROUPS_PER_STEP
    slot = lambda c, *shape: pl.BlockSpec((sq, sq, gp) + shape, lambda i, j, t: (c, i, j, 0, 0))
    full = lambda *shape: pl.BlockSpec(shape, lambda i, j, t: (0,) * len(shape))
    per_group = lambda shape, dtype: pltpu.VMEM((gp,) + shape, dtype)
    return pl.pallas_call(
        _nsa_kernel,
        grid=(b, g // gp, s // TQ),
        in_specs=[
            pl.BlockSpec(memory_space=pltpu.SMEM),
            pl.BlockSpec((sq, TQ, gp * nrep * dh), lambda i, j, t: (i, t, j)),
            slot(0, N_CMP_PAD, dh), slot(1, dh, N_CMP_PAD),
            slot(0, s, dh), slot(0, dh, s), slot(1, s, dh), slot(1, dh, s),
            pl.BlockSpec((sq, TQ, LANES), lambda i, j, t: (i, t, 0)),
            full(1, LANES), full(s, AUG), full(N_CMP_PAD, AUG),
        ],
        out_specs=pl.BlockSpec((sq, TQ, gp * nrep * dh), lambda i, j, t: (i, t, j)),
        out_shape=jax.ShapeDtypeStruct((b, s, NSA_Q), BF16),
        scratch_shapes=[
            per_group((s, AUG), BF16), per_group((WINDOW + s, AUG), BF16), per_group((N_CMP_PAD, AUG), BF16),
            per_group((s // TK_SEL, V_ROWS, TK_SEL), BF16), per_group((V_ROWS, WINDOW + s), BF16),
            per_group((V_ROWS, N_CMP_PAD), BF16),
            per_group((AUG, NQ), BF16), per_group((n_sb, TQ), F32), per_group((1, NQ), F32),
            per_group((V_ROWS, NQ), F32), per_group((dh, NQ), F32),
            per_group((TK_SEL, NQ), F32), per_group((TK_SEL, NQ), F32),
            per_group((AUG, NQ), BF16),
            pltpu.SMEM((gp, s // TK_SEL + 1), jnp.int32),
            pltpu.VMEM((LANES, TQ), F32),
        ],
        compiler_params=_cparams("arbitrary", "arbitrary", "arbitrary"),
        name="nsa_attention",
    )(slope_pieces, q, cmp, cmp_t, keys, vals_t, keys, vals_t, glogit, bgate, constk, constc)


def _position_pieces(pos):
    hi = (pos // L_SEL) * L_SEL
    return jnp.stack([hi] * N_PIECES + [pos - hi] * N_PIECES, axis=1).astype(F32)


def _nsa_constants(s):
    pos = jnp.arange(s)
    constk = jnp.zeros((s, AUG), F32)
    constk = constk.at[:, SEL_ROW0:SEL_ROW0 + s // L_SEL].set(jax.nn.one_hot(pos // L_SEL, s // L_SEL))
    constk = constk.at[:, POS_ROW0:POS_ROW0 + 2 * N_PIECES].set(_position_pieces(pos))
    cmp_end = CMP_STRIDE * jnp.arange(N_CMP_PAD) + (L_CMP - 1)
    constc = jnp.zeros((N_CMP_PAD, AUG), F32)
    constc = constc.at[:, POS_ROW0:POS_ROW0 + 2 * N_PIECES].set(_position_pieces(cmp_end))
    slopes = 2.0 ** (-8.0 * jnp.arange(1, N_HEADS + 1, dtype=F32) / N_HEADS)
    pieces, rest = [], slopes
    for _ in range(N_PIECES):
        piece = rest.astype(BF16).astype(F32)
        pieces.append(piece)
        rest = rest - piece
    return constk.astype(BF16), constc.astype(BF16), jnp.stack(pieces, axis=1).reshape(-1)


def _nsa_mixer(x, norm_g, w_in, b_gate, cmp_pos, cmp_w1, cmp_b1, cmp_w2, cmp_b2, w_out):
    b, s, d = x.shape
    m = b * s
    g, nrep, dh = N_GROUPS, HEADS_PER_GROUP, HEAD_DIM
    w_in_p = jnp.pad(w_in, ((0, 0), (0, NSA_IN_PAD - NSA_IN))).astype(BF16)
    v_cols = lambda i: w_in_p[:, NSA_Q + i * NSA_KV:NSA_Q + (i + 1) * NSA_KV].T
    q, raw, keys, vals_t, glogit = _nsa_in_proj(x, norm_g, w_in_p, jnp.stack([v_cols(3), v_cols(5)]),
                                                tm=512)
    raw = raw.reshape(2, b, g, s // CMP_STRIDE, CMP_STRIDE * dh)
    cmp, cmp_t = _compress(raw, cmp_pos.reshape(2, 1, L_CMP * dh), cmp_w1.astype(BF16), cmp_b1,
                           cmp_w2.astype(BF16), cmp_b2)
    constk, constc, slope_pieces = _nsa_constants(s)
    bgate = jnp.pad(b_gate, (0, LANES - 3 * N_HEADS)).reshape(1, LANES)
    y = _nsa_attention(slope_pieces, q, cmp, cmp_t, keys, vals_t, glogit, bgate, constk, constc)
    return _matmul_res(y.reshape(m, NSA_Q), w_out.astype(BF16), x.reshape(m, d), tm=512,
                       name="nsa_out_proj").reshape(b, s, d)


TT = 256
SUB = SUBLANES


GATE_BAND = 256


def _gate_bands():
    block = D_RNN // LRU_BLOCKS
    bands = []
    for c0 in range(0, D_RNN, GATE_BAND):
        width = min(GATE_BAND, D_RNN - c0)
        k_lo = (c0 // block) * block // LANES * LANES
        k_hi = -(-(((c0 + width - 1) // block + 1) * block) // LANES) * LANES
        bands.append((k_lo, min(k_hi, D_RNN), c0, width))
    return tuple(bands)


GATE_BANDS = _gate_bands()


def _pack_gate_weights(w_a, w_x):
    dense = lambda w: jax.scipy.linalg.block_diag(*[w[i] for i in range(LRU_BLOCKS)])
    slabs = []
    for k_lo, k_hi, c0, width in GATE_BANDS:
        band = [jnp.pad(dense(w)[k_lo:k_hi, c0:c0 + width], ((0, 0), (0, GATE_BAND - width)))
                for w in (w_a, w_x)]
        slabs.append(jnp.concatenate(band, axis=1))
    return jnp.concatenate(slabs, axis=0).astype(BF16)


def _sigmoid(z):
    return 0.5 * jnp.tanh(0.5 * z) + 0.5


def _lru_kernel(x_ref, g_ref, win_ref, cw_ref, cb_ref, wax_ref, bax_ref, lam_ref, wout_ref, o_ref,
                ext_ref, a_ref, u_ref, h_ref, gate_ref):
    ti = pl.program_id(1)

    @pl.when(ti == 0)
    def _():
        ext_ref[pl.ds(TT, SUB), :] = jnp.zeros((SUB, D_RNN), F32)
        h_ref[...] = jnp.zeros_like(h_ref)

    xn = _rms(x_ref[...], g_ref[...]).astype(BF16)
    gate_ref[...] = jax.nn.gelu(
        jnp.dot(xn, win_ref[:, :D_RNN], preferred_element_type=F32), approximate=True)
    ext_ref[pl.ds(0, SUB), :] = ext_ref[pl.ds(TT, SUB), :]
    ext_ref[pl.ds(SUB, TT), :] = jnp.dot(xn, win_ref[:, D_RNN:], preferred_element_type=F32)
    cw = cw_ref[...]
    xr = cb_ref[...] + sum(
        cw[w:w + 1, :] * ext_ref[pl.ds(SUB - (CONV_W - 1) + w, TT), :] for w in range(CONV_W))

    lam = lam_ref[...]
    softplus_neg = jnp.maximum(-lam, 0.0) + jnp.log1p(jnp.exp(-jnp.abs(lam)))
    xr_b = xr.astype(BF16)
    bias = bax_ref[...]
    row0 = 0
    for k_lo, k_hi, c0, width in GATE_BANDS:
        z = jnp.dot(xr_b[:, k_lo:k_hi], wax_ref[row0:row0 + k_hi - k_lo, :], preferred_element_type=F32)
        row0 += k_hi - k_lo
        r_gate = _sigmoid(z[:, :width] + bias[:, c0:c0 + width])
        i_gate = _sigmoid(z[:, GATE_BAND:GATE_BAND + width] + bias[:, D_RNN + c0:D_RNN + c0 + width])
        a = jnp.exp(-LRU_C * r_gate * softplus_neg[:, c0:c0 + width])
        a_ref[:, c0:c0 + width] = a
        u_ref[:, c0:c0 + width] = jnp.sqrt(jnp.maximum(1.0 - a * a, 0.0)) * (i_gate * xr[:, c0:c0 + width])

    row = lax.broadcasted_iota(jnp.int32, (SUB, D_RNN), 0)

    def scan_rows(c, h_prev):
        r0 = pl.multiple_of(c * SUB, SUB)
        a_c = a_ref[pl.ds(r0, SUB), :]
        u_c = u_ref[pl.ds(r0, SUB), :]
        shift = 1
        while shift < SUB:
            keep = row >= shift
            u_c = u_c + a_c * jnp.where(keep, pltpu.roll(u_c, shift, axis=0), 0.0)
            a_c = a_c * jnp.where(keep, pltpu.roll(a_c, shift, axis=0), 1.0)
            shift *= 2
        h_c = u_c + a_c * h_prev
        u_ref[pl.ds(r0, SUB), :] = h_c
        return jnp.broadcast_to(h_c[SUB - 1:SUB, :], (SUB, D_RNN))

    h_ref[...] = lax.fori_loop(0, TT // SUB, scan_rows, h_ref[...])
    gated = (u_ref[...] * gate_ref[...]).astype(BF16)
    o_ref[...] = x_ref[...] + jnp.dot(gated, wout_ref[...], preferred_element_type=F32)


def _lru_mixer(x, norm_g, w_in, conv_w, conv_b, w_a, b_a, w_x, b_x, lam, w_out):
    b, s, d = x.shape
    wax = _pack_gate_weights(w_a, w_x)
    bax = jnp.concatenate([b_a, b_x])
    sq = pl.Squeezed()
    resident = lambda *shape: pl.BlockSpec(shape, lambda i, t: (0,) * len(shape),
                                           pipeline_mode=pl.Buffered(1))
    return pl.pallas_call(
        _lru_kernel,
        grid=(b, s // TT),
        in_specs=[
            pl.BlockSpec((sq, TT, d), lambda i, t: (i, t, 0)),
            resident(1, d), resident(d, 2 * D_RNN), resident(CONV_W, D_RNN), resident(1, D_RNN),
            resident(*wax.shape), resident(1, 2 * D_RNN), resident(1, D_RNN), resident(D_RNN, d),
        ],
        out_specs=pl.BlockSpec((sq, TT, d), lambda i, t: (i, t, 0)),
        out_shape=jax.ShapeDtypeStruct((b, s, d), F32),
        scratch_shapes=[
            pltpu.VMEM((TT + SUB, D_RNN), F32),
            pltpu.VMEM((TT, D_RNN), F32),
            pltpu.VMEM((TT, D_RNN), F32),
            pltpu.VMEM((SUB, D_RNN), F32),
            pltpu.VMEM((TT, D_RNN), F32),
        ],
        compiler_params=_cparams("parallel", "arbitrary"),
        name="lru_block",
    )(x, norm_g.reshape(1, d), w_in.astype(BF16), conv_w, conv_b.reshape(1, -1), wax,
      bax.reshape(1, -1), lam.reshape(1, -1), w_out.astype(BF16))


def kernel(x, norm_mix, norm_ffn, norm_final, nsa_w_in, nsa_b_gate, nsa_cmp_pos, nsa_cmp_w1,
           nsa_cmp_b1, nsa_cmp_w2, nsa_cmp_b2, nsa_w_out, lru_w_in, lru_conv_w, lru_conv_b,
           lru_w_a, lru_b_a, lru_w_x, lru_b_x, lru_lambda, lru_w_out, ffn_w_in, ffn_w_out):
    b, s, d = x.shape
    ffn = functools.partial(_ffn, tm=1024, tf=256)
    x = _nsa_mixer(x, norm_mix[0], nsa_w_in[0], nsa_b_gate[0], nsa_cmp_pos[0], nsa_cmp_w1[0],
                   nsa_cmp_b1[0], nsa_cmp_w2[0], nsa_cmp_b2[0], nsa_w_out[0])
    x = ffn(x.reshape(b * s, d), norm_ffn[0], ffn_w_in[0].astype(BF16), ffn_w_out[0].astype(BF16),
            norm_final, final_norm=False, name="ffn0").reshape(b, s, d)
    x = _lru_mixer(x, norm_mix[1], lru_w_in[0], lru_conv_w[0], lru_conv_b[0], lru_w_a[0], lru_b_a[0],
                   lru_w_x[0], lru_b_x[0], lru_lambda[0], lru_w_out[0])
    x = ffn(x.reshape(b * s, d), norm_ffn[1], ffn_w_in[1].astype(BF16), ffn_w_out[1].astype(BF16),
            norm_final, final_norm=True, name="ffn1").reshape(b, s, d)
    return x
```

```python
import functools

import jax
import jax.numpy as jnp
from jax import lax
from jax.experimental import pallas as pl
from jax.experimental.pallas import tpu as pltpu

F32 = jnp.float32
BF16 = jnp.bfloat16

D_MODEL = 1024
N_HEADS = 16
N_GROUPS = 4
HEADS_PER_GROUP = N_HEADS // N_GROUPS
HEAD_DIM = 64
L_CMP = 32
CMP_STRIDE = 16
L_SEL = 64
N_SELECT = 16
WINDOW = 512
NSA_Q = N_HEADS * HEAD_DIM
NSA_KV = N_GROUPS * HEAD_DIM
NSA_IN = NSA_Q + 6 * NSA_KV + 3 * N_HEADS
D_RNN = 1408
LRU_BLOCKS = 8
LRU_C = 8.0
CONV_W = 4
D_FF = 2816
EPS = 1e-6

LANES = 128
SUBLANES = 8
NEG = -1e30
VMEM_LIMIT = 56 * 1024 * 1024

TQ = 128
TK_SEL = 512
TK_WIN = 128
N_CMP_PAD = 256


def _cparams(*sem):
    return pltpu.CompilerParams(dimension_semantics=sem, vmem_limit_bytes=VMEM_LIMIT)


def _rms(x, g):
    ms = jnp.mean(x * x, axis=-1, keepdims=True)
    return x * lax.rsqrt(ms + EPS) * g


def _nt_dot(a, b):
    return lax.dot_general(a, b, (((1,), (1,)), ((), ())), preferred_element_type=F32)


def _ffn_kernel(*refs, final_norm, tf, has_mix):
    x_ref, refs = refs[0], refs[1:]
    x = x_ref[...]
    if has_mix:
        (mix_ref, wmix_ref), refs = refs[:2], refs[2:]
        x = x + jnp.dot(mix_ref[...], wmix_ref[...], preferred_element_type=F32)
    g_ref, win_ref, wout_ref, gf_ref, o_ref, hid_ref = refs
    xn = _rms(x, g_ref[...]).astype(BF16)
    for c0 in range(0, D_FF, tf):
        gate = jnp.dot(xn, win_ref[:, c0:c0 + tf], preferred_element_type=F32)
        up = jnp.dot(xn, win_ref[:, D_FF + c0:D_FF + c0 + tf], preferred_element_type=F32)
        hid_ref[:, c0:c0 + tf] = ((0.5 * gate) * (1.0 + jnp.tanh(0.5 * gate)) * up).astype(BF16)
    y = x + jnp.dot(hid_ref[...], wout_ref[...], preferred_element_type=F32)
    if final_norm:
        y = _rms(y, gf_ref[...])
    o_ref[...] = y


def _ffn(x, g, w_in, w_out, g_final, *, final_norm, tm, tf, name, mix=None, w_mix=None):
    m, d = x.shape
    resident = lambda *shape: pl.BlockSpec(shape, lambda i: (0,) * len(shape),
                                           pipeline_mode=pl.Buffered(1))
    rows = lambda width: pl.BlockSpec((tm, width), lambda i: (i, 0))
    if mix is None:
        mix_specs, mix_args = [], ()
    else:
        mix_specs, mix_args = [rows(mix.shape[1]), resident(*w_mix.shape)], (mix, w_mix)
    return pl.pallas_call(
        functools.partial(_ffn_kernel, final_norm=final_norm, tf=tf, has_mix=mix is not None),
        grid=(m // tm,),
        in_specs=[rows(d)] + mix_specs + [
            resident(1, d), resident(d, 2 * D_FF), resident(D_FF, d), resident(1, d)],
        out_specs=rows(d),
        out_shape=jax.ShapeDtypeStruct((m, d), F32),
        scratch_shapes=[pltpu.VMEM((tm, D_FF), BF16)],
        compiler_params=_cparams("parallel"),
        name=name,
    )(x, *mix_args, g.reshape(1, d), w_in, w_out, g_final.reshape(1, d))


NSA_IN_PAD = -(-NSA_IN // LANES) * LANES


def _nsa_in_proj_kernel(x_ref, g_ref, w_ref, wvt_ref, q_ref, raw_ref, k_ref, vt_ref, gl_ref):
    xn = _rms(x_ref[...], g_ref[...]).astype(BF16)
    q_ref[...] = jnp.dot(xn, w_ref[:, :NSA_Q], preferred_element_type=F32).astype(q_ref.dtype)
    for i, (dst, slot) in {0: (raw_ref, 0), 1: (raw_ref, 1), 2: (k_ref, 0), 4: (k_ref, 1)}.items():
        col0 = NSA_Q + i * NSA_KV
        part = jnp.dot(xn, w_ref[:, col0:col0 + NSA_KV], preferred_element_type=F32)
        for grp in range(N_GROUPS):
            dst[slot, grp] = part[:, grp * HEAD_DIM:(grp + 1) * HEAD_DIM].astype(dst.dtype)
    for slot in range(2):
        part_t = _nt_dot(wvt_ref[slot], xn)
        for grp in range(N_GROUPS):
            vt_ref[slot, grp] = part_t[grp * HEAD_DIM:(grp + 1) * HEAD_DIM].astype(vt_ref.dtype)
    gl_ref[...] = jnp.dot(xn, w_ref[:, NSA_Q + 6 * NSA_KV:], preferred_element_type=F32)


def _nsa_in_proj(x, g, w, wvt, *, tm):
    b, s, d = x.shape
    sq = pl.Squeezed()
    grouped = lambda dtype: jax.ShapeDtypeStruct((2, b, N_GROUPS, s, HEAD_DIM), dtype)
    grouped_spec = pl.BlockSpec((2, sq, N_GROUPS, tm, HEAD_DIM), lambda i, t: (0, i, 0, t, 0))
    return pl.pallas_call(
        _nsa_in_proj_kernel,
        grid=(b, s // tm),
        in_specs=[
            pl.BlockSpec((sq, tm, d), lambda i, t: (i, t, 0)),
            pl.BlockSpec((1, d), lambda i, t: (0, 0)),
            pl.BlockSpec((d, NSA_IN_PAD), lambda i, t: (0, 0)),
            pl.BlockSpec((2, NSA_KV, d), lambda i, t: (0, 0, 0)),
        ],
        out_specs=[
            pl.BlockSpec((sq, tm, NSA_Q), lambda i, t: (i, t, 0)),
            grouped_spec, grouped_spec,
            pl.BlockSpec((2, sq, N_GROUPS, HEAD_DIM, tm), lambda i, t: (0, i, 0, 0, t)),
            pl.BlockSpec((sq, tm, LANES), lambda i, t: (i, t, 0)),
        ],
        out_shape=[
            jax.ShapeDtypeStruct((b, s, NSA_Q), BF16),
            grouped(F32), grouped(BF16),
            jax.ShapeDtypeStruct((2, b, N_GROUPS, HEAD_DIM, s), BF16),
            jax.ShapeDtypeStruct((b, s, LANES), F32),
        ],
        compiler_params=_cparams("parallel", "parallel"),
        name="nsa_in_proj",
    )(x, g.reshape(1, d), w, wvt)


def _compress_kernel(r_ref, pos_ref, w1_ref, b1_ref, w2_ref, b2_ref, w2t_ref, b2t_ref, o_ref, ot_ref):
    half = CMP_STRIDE * HEAD_DIM
    rows = r_ref[...]
    pos = pos_ref[...]
    lo = (rows + pos[:, :half]).astype(BF16)
    hi = (rows + pos[:, half:]).astype(BF16)
    part_lo = jnp.dot(lo, w1_ref[:half, :], preferred_element_type=F32)
    part_hi = jnp.dot(hi, w1_ref[half:, :], preferred_element_type=F32)
    hid = part_lo + pltpu.roll(part_hi, N_CMP_PAD - 1, axis=0) + b1_ref[...]
    hid = jax.nn.gelu(hid, approximate=True).astype(BF16)
    o_ref[...] = (jnp.dot(hid, w2_ref[...], preferred_element_type=F32) + b2_ref[...]).astype(o_ref.dtype)
    ot_ref[...] = (_nt_dot(w2t_ref[...], hid) + b2t_ref[...]).astype(ot_ref.dtype)


def _compress(raw, pos, w1, b1, w2, b2):
    two, b, g, n, width = raw.shape
    hidden = w1.shape[-1]
    sq = pl.Squeezed()
    per_branch = lambda *shape: pl.BlockSpec((sq,) + shape, lambda c, i, j: (c,) + (0,) * len(shape))
    return pl.pallas_call(
        _compress_kernel,
        grid=(two, b, g),
        in_specs=[
            pl.BlockSpec((sq, sq, sq, n, width), lambda c, i, j: (c, i, j, 0, 0)),
            per_branch(1, 2 * width), per_branch(2 * width, hidden), per_branch(1, hidden),
            per_branch(hidden, HEAD_DIM), per_branch(1, HEAD_DIM),
            per_branch(HEAD_DIM, hidden), per_branch(HEAD_DIM, 1),
        ],
        out_specs=[
            pl.BlockSpec((sq, sq, sq, n, HEAD_DIM), lambda c, i, j: (c, i, j, 0, 0)),
            pl.BlockSpec((sq, sq, sq, HEAD_DIM, n), lambda c, i, j: (c, i, j, 0, 0)),
        ],
        out_shape=[
            jax.ShapeDtypeStruct((two, b, g, n, HEAD_DIM), BF16),
            jax.ShapeDtypeStruct((two, b, g, HEAD_DIM, n), BF16),
        ],
        compiler_params=_cparams("parallel", "parallel", "parallel"),
        name="nsa_compress",
    )(raw, pos, w1, b1.reshape(two, 1, hidden), w2, b2.reshape(two, 1, HEAD_DIM),
      w2.transpose(0, 2, 1), b2.reshape(two, HEAD_DIM, 1))


AUG = 256
SEL_ROW0 = HEAD_DIM
POS_ROW0 = 2 * HEAD_DIM
N_PIECES = 3
PAD_LANE = POS_ROW0 + 2 * N_PIECES
POS_ROWS = 16
V_ROWS = 80
NQ = HEADS_PER_GROUP * TQ
BIG = 1e30
assert TQ == TK_WIN and TK_SEL % TQ == 0 and WINDOW % TK_WIN == 0


def _flash_reset(m_ref, acc_ref):
    m_ref[...] = jnp.full_like(m_ref, NEG)
    acc_ref[...] = jnp.zeros_like(acc_ref)


def _scores(kaug, tile, tk, qat_ref):
    k0 = pl.multiple_of(tile * tk, tk)
    return jnp.dot(kaug[pl.ds(k0, tk), :], qat_ref[...], preferred_element_type=F32)


def _flash_update(s, vt_tile, mask, m_ref, acc_ref):
    if mask is not None:
        s = jnp.where(mask, s, NEG)
    m_old = m_ref[...]
    m_new = jnp.maximum(m_old, jnp.max(s, axis=0, keepdims=True))
    p = jnp.exp(s - m_new).astype(BF16)
    acc_ref[...] = jnp.exp(m_old - m_new) * acc_ref[...] + jnp.dot(
        vt_tile, p, preferred_element_type=F32)
    m_ref[...] = m_new


def _flash_result(acc_ref):
    return acc_ref[pl.ds(0, HEAD_DIM), :] * (1.0 / acc_ref[pl.ds(HEAD_DIM, 1), :])


GROUPS_PER_STEP = 2


def _nsa_kernel(sl_ref, q_ref, kc_all, vct_all, ks_all, vst_all, kw_all, vwt_all, gl_ref, bg_ref,
                constk_ref, constc_ref, o_ref, *scratch):
    qi = pl.program_id(2)
    gt_ref = scratch[-1]
    per_group = (kc_all, vct_all, ks_all, vst_all, kw_all, vwt_all) + scratch[:-1]
    groups = [tuple(ref.at[h] for ref in per_group) for h in range(GROUPS_PER_STEP)]

    @pl.when(qi == 0)
    def _():
        for refs in groups:
            _nsa_assemble(constk_ref, constc_ref, *refs)

    gt_ref[...] = jax.nn.sigmoid(gl_ref[...] + bg_ref[...]).T
    tiles = [_nsa_tile(pl.program_id(1) * GROUPS_PER_STEP + h, h * HEADS_PER_GROUP * HEAD_DIM, qi,
                       sl_ref, q_ref, o_ref, gt_ref, *refs) for h, refs in enumerate(groups)]
    heads = list(tiles)
    while heads:
        heads = [tile for tile in heads if not next(tile)]
    for tile in tiles:
        next(tile, None)


def _nsa_assemble(constk_ref, constc_ref, kc_ref, vct_ref, ks_ref, vst_ref, kw_ref, vwt_ref,
                  kaug_s, kaug_w, kaug_c, vt_s, vt_w, vt_c, *unused):
    s_len = ks_ref.shape[0]
    for kaug, row0, const, k in ((kaug_s, 0, constk_ref, ks_ref), (kaug_w, WINDOW, constk_ref, kw_ref),
                                 (kaug_c, 0, constc_ref, kc_ref)):
        kaug[pl.ds(row0, k.shape[0]), :] = const[...]
        kaug[pl.ds(row0, k.shape[0]), pl.ds(0, HEAD_DIM)] = k[...]
    pad_lane = lax.broadcasted_iota(jnp.int32, (WINDOW, AUG), 1) == PAD_LANE
    kaug_w[pl.ds(0, WINDOW), :] = jnp.where(pad_lane, 1.0, 0.0).astype(BF16)
    vt_w[:, pl.ds(0, WINDOW)] = jnp.zeros((V_ROWS, WINDOW), BF16)
    pad = V_ROWS - HEAD_DIM
    ones_rows = lambda n: jnp.where(
        lax.broadcasted_iota(jnp.int32, (pad, n), 0) == 0, 1.0, 0.0).astype(BF16)
    for c in range(vt_s.shape[0]):
        vt_s[c, pl.ds(0, HEAD_DIM), :] = vst_ref[:, pl.ds(c * TK_SEL, TK_SEL)]
        vt_s[c, pl.ds(HEAD_DIM, pad), :] = ones_rows(TK_SEL)
    vt_w[pl.ds(0, HEAD_DIM), pl.ds(WINDOW, s_len)] = vwt_ref[...]
    vt_w[pl.ds(HEAD_DIM, pad), pl.ds(WINDOW, s_len)] = ones_rows(s_len)
    vt_c[pl.ds(0, HEAD_DIM), :] = vct_ref[...]
    vt_c[pl.ds(HEAD_DIM, pad), :] = ones_rows(N_CMP_PAD)


def _nsa_tile(grp, lane0, qi, sl_ref, q_ref, o_ref, gt_ref, kc_ref, vct_ref, ks_ref, vst_ref, kw_ref,
              vwt_ref, kaug_s, kaug_w, kaug_c, vt_s, vt_w, vt_c, qat_ref, score_ref, m_ref, acc_ref,
              out_ref, sa_ref, sb_ref, qats_ref, tiles_ref):
    q0 = qi * TQ
    nrep = HEADS_PER_GROUP

    for half in range(nrep // 2):
        qt = q_ref[:, pl.ds(lane0 + half * LANES, LANES)].astype(F32).T * (HEAD_DIM ** -0.5)
        for sub in range(2):
            q_rows = qt[sub * HEAD_DIM:(sub + 1) * HEAD_DIM].astype(BF16)
            for ref in (qat_ref, qats_ref):
                ref[pl.ds(0, HEAD_DIM), pl.ds((2 * half + sub) * TQ, TQ)] = q_rows
    qat_ref[pl.ds(SEL_ROW0, HEAD_DIM), :] = jnp.zeros((HEAD_DIM, NQ), BF16)
    piece = lax.broadcasted_iota(jnp.int32, (POS_ROWS, TQ), 0)
    tail0 = POS_ROW0 + POS_ROWS
    for r in range(nrep):
        tile = jnp.where(piece == PAD_LANE - POS_ROW0, -BIG, 0.0)
        for i in range(2 * N_PIECES):
            tile = jnp.where(piece == i, sl_ref[(grp * nrep + r) * N_PIECES + i % N_PIECES], tile)
        for ref in (qat_ref, qats_ref):
            ref[pl.ds(POS_ROW0, POS_ROWS), pl.ds(r * TQ, TQ)] = tile.astype(BF16)
    for ref in (qat_ref, qats_ref):
        ref[pl.ds(tail0, AUG - tail0), :] = jnp.zeros((AUG - tail0, NQ), BF16)

    yield False
    t_row = q0 + (lax.broadcasted_iota(jnp.int32, (1, NQ), 1) & (TQ - 1))
    gate_row = lambda br: jnp.concatenate(
        [gt_ref[pl.ds(3 * (grp * nrep + r) + br, 1), :] for r in range(nrep)], axis=1)

    s = jnp.dot(kaug_c[...], qat_ref[...], preferred_element_type=F32)
    cmp_end = CMP_STRIDE * lax.broadcasted_iota(jnp.int32, (N_CMP_PAD, 1), 0) + (L_CMP - 1)
    mask_c = cmp_end <= t_row
    s = jnp.where(mask_c, s, NEG)
    e = jnp.where(mask_c, jnp.exp(s - jnp.max(s, axis=0, keepdims=True)), 0.0)
    d = jnp.sum(e, axis=0, keepdims=True)
    p = e * (1.0 / jnp.where(d > 0, d, 1.0))
    p_sum = sum(p[:, r * TQ:(r + 1) * TQ] for r in range(nrep))
    o_c = jnp.dot(vt_c[...], p.astype(BF16), preferred_element_type=F32)
    out_ref[...] = gate_row(0) * o_c[:HEAD_DIM]
    yield False

    n_win = WINDOW // TK_WIN
    k0 = pl.multiple_of(q0, TK_WIN)
    s_all = jnp.dot(kaug_w[pl.ds(k0, WINDOW + TQ), :], qat_ref[...], preferred_element_type=F32)
    s_w = [s_all[j * TK_WIN:(j + 1) * TK_WIN] for j in range(n_win + 1)]
    not_after = q0 + lax.broadcasted_iota(jnp.int32, (TK_WIN, 1), 0) <= t_row
    s_w[0] = jnp.where(not_after, NEG, s_w[0])
    s_w[-1] = jnp.where(not_after, s_w[-1], NEG)
    m_w = functools.reduce(jnp.maximum, [jnp.max(x, axis=0, keepdims=True) for x in s_w])
    p_w = jnp.concatenate([jnp.exp(x - m_w).astype(BF16) for x in s_w], axis=0)
    acc_w = jnp.dot(vt_w[:, pl.ds(k0, WINDOW + TQ)], p_w, preferred_element_type=F32)
    out_ref[...] += gate_row(2) * (acc_w[:HEAD_DIM] * (1.0 / acc_w[HEAD_DIM:HEAD_DIM + 1]))
    yield False

    n_sb = score_ref.shape[0]
    jb = lax.broadcasted_iota(jnp.int32, (n_sb, TQ), 0)
    t_sel = q0 + lax.broadcasted_iota(jnp.int32, (n_sb, TQ), 1)
    valid = jb * L_SEL <= t_sel

    n_full = q0 // TK_SEL
    blocks_per_tile = TK_SEL // L_SEL

    def write_selection(selected):
        sel_bias = jnp.where(selected, 0.0, -BIG).astype(BF16)
        for r in range(nrep):
            qats_ref[pl.ds(SEL_ROW0, n_sb), pl.ds(r * TQ, TQ)] = sel_bias
        picked = jnp.where(selected, 1.0, 0.0)
        bits = sum(jnp.max(picked[k * blocks_per_tile:(k + 1) * blocks_per_tile], axis=(0, 1),
                           keepdims=True) * float(2 ** k) for k in range(n_sb // blocks_per_tile))
        bits = bits[0, 0].astype(jnp.int32)
        count = jnp.int32(0)
        for k in range(n_sb // blocks_per_tile):
            tiles_ref[count] = k
            count = count + jnp.where((k < n_full) & (((bits >> k) & 1) == 1), 1, 0)
        tiles_ref[count] = n_full
        return count

    def rank_blocks():
        cmp_start = CMP_STRIDE * lax.broadcasted_iota(jnp.int32, (n_sb, N_CMP_PAD), 1)
        sb_start = L_SEL * lax.broadcasted_iota(jnp.int32, (n_sb, N_CMP_PAD), 0)
        overlap_t = ((cmp_start < sb_start + L_SEL) & (cmp_start + L_CMP > sb_start)).astype(F32)
        imp_t = jnp.dot(overlap_t, p_sum, precision=lax.Precision.HIGHEST,
                        preferred_element_type=F32)
        cur = t_sel // L_SEL
        forced = (jb == 0) | (jb == cur) | (jb == cur - 1)
        score = jnp.where(valid, jnp.where(forced, jnp.inf, imp_t), -jnp.inf)
        score_ref[...] = score
        chunks = [score[c * SUBLANES:(c + 1) * SUBLANES] for c in range(n_sb // SUBLANES)]
        ranks = [jnp.zeros((SUBLANES, TQ), F32) for _ in chunks]
        sub_row = lax.broadcasted_iota(jnp.int32, (SUBLANES, TQ), 0)
        for i in range(n_sb):
            row = score_ref[pl.ds(i, 1), :]
            for c, chunk in enumerate(chunks):
                later = jnp.where(row >= chunk, 1.0, 0.0)
                earlier = jnp.where(row > chunk, 1.0, 0.0)
                if i < c * SUBLANES:
                    ranks[c] = ranks[c] + later
                elif i >= (c + 1) * SUBLANES:
                    ranks[c] = ranks[c] + earlier
                else:
                    ranks[c] = ranks[c] + jnp.where(sub_row > i - c * SUBLANES, later, earlier)
        rank = jnp.concatenate(ranks, axis=0)
        return write_selection(valid & (rank < float(N_SELECT)))

    n_list = rank_blocks()
    yield False

    _flash_reset(m_ref, acc_ref)
    key_s = lax.broadcasted_iota(jnp.int32, (TK_SEL, 1), 0)
    tail_mask = n_full * TK_SEL + key_s <= t_row
    sel_scores = lambda i: _scores(kaug_s, tiles_ref[i], TK_SEL, qats_ref)
    sel_update = lambda s_ref, i, mask: _flash_update(s_ref[...], vt_s[tiles_ref[i]], mask, m_ref, acc_ref)
    sa_ref[...] = sel_scores(0)
    yield True

    def sel_pair(j, carry):
        i = 2 * j
        sb_ref[...] = sel_scores(i + 1)
        sel_update(sa_ref, i, None)
        sa_ref[...] = sel_scores(i + 2)
        sel_update(sb_ref, i + 1, None)
        return carry

    lax.fori_loop(0, n_list // 2, sel_pair, 0)

    @pl.when(n_list % 2 == 1)
    def _():
        sb_ref[...] = sel_scores(n_list)
        sel_update(sa_ref, n_list - 1, None)
        sel_update(sb_ref, n_list, tail_mask)

    @pl.when(n_list % 2 == 0)
    def _():
        sel_update(sa_ref, n_list, tail_mask)

    out_ref[...] += gate_row(1) * _flash_result(acc_ref)

    for half in range(nrep // 2):
        slab = jnp.concatenate([out_ref[:, pl.ds(2 * half * TQ, TQ)],
                                out_ref[:, pl.ds((2 * half + 1) * TQ, TQ)]], axis=0)
        o_ref[:, pl.ds(lane0 + half * LANES, LANES)] = slab.T.astype(o_ref.dtype)


def _nsa_attention(slope_pieces, q, cmp, cmp_t, keys, vals_t, glogit, bgate, constk, constc):
    b, s, _ = q.shape
    g, nrep, dh = N_GROUPS, HEADS_PER_GROUP, HEAD_DIM
    n_sb = s // L_SEL
    sq = pl.Squeezed()
    gp = GROUPS_PER_STEP
    slot = lambda c, *shape: pl.BlockSpec((sq, sq, gp) + shape, lambda i, j, t: (c, i, j, 0, 0))
    full = lambda *shape: pl.BlockSpec(shape, lambda i, j, t: (0,) * len(shape))
    per_group = lambda shape, dtype: pltpu.VMEM((gp,) + shape, dtype)
    return pl.pallas_call(
        _nsa_kernel,
        grid=(b, g // gp, s // TQ),
        in_specs=[
            pl.BlockSpec(memory_space=pltpu.SMEM),
            pl.BlockSpec((sq, TQ, gp * nrep * dh), lambda i, j, t: (i, t, j)),
            slot(0, N_CMP_PAD, dh), slot(1, dh, N_CMP_PAD),
            slot(0, s, dh), slot(0, dh, s), slot(1, s, dh), slot(1, dh, s),
            pl.BlockSpec((sq, TQ, LANES), lambda i, j, t: (i, t, 0)),
            full(1, LANES), full(s, AUG), full(N_CMP_PAD, AUG),
        ],
        out_specs=pl.BlockSpec((sq, TQ, gp * nrep * dh), lambda i, j, t: (i, t, j)),
        out_shape=jax.ShapeDtypeStruct((b, s, NSA_Q), BF16),
        scratch_shapes=[
            per_group((s, AUG), BF16), per_group((WINDOW + s, AUG), BF16), per_group((N_CMP_PAD, AUG), BF16),
            per_group((s // TK_SEL, V_ROWS, TK_SEL), BF16), per_group((V_ROWS, WINDOW + s), BF16),
            per_group((V_ROWS, N_CMP_PAD), BF16),
            per_group((AUG, NQ), BF16), per_group((n_sb, TQ), F32), per_group((1, NQ), F32),
            per_group((V_ROWS, NQ), F32), per_group((dh, NQ), F32),
            per_group((TK_SEL, NQ), F32), per_group((TK_SEL, NQ), F32),
            per_group((AUG, NQ), BF16),
            pltpu.SMEM((gp, s // TK_SEL + 1), jnp.int32),
            pltpu.VMEM((LANES, TQ), F32),
        ],
        compiler_params=_cparams("arbitrary", "arbitrary", "arbitrary"),
        name="nsa_attention",
    )(slope_pieces, q, cmp, cmp_t, keys, vals_t, keys, vals_t, glogit, bgate, constk, constc)


def _position_pieces(pos):
    hi = (pos // L_SEL) * L_SEL
    return jnp.stack([hi] * N_PIECES + [pos - hi] * N_PIECES, axis=1).astype(F32)


def _nsa_constants(s):
    pos = jnp.arange(s)
    constk = jnp.zeros((s, AUG), F32)
    constk = constk.at[:, SEL_ROW0:SEL_ROW0 + s // L_SEL].set(jax.nn.one_hot(pos // L_SEL, s // L_SEL))
    constk = constk.at[:, POS_ROW0:POS_ROW0 + 2 * N_PIECES].set(_position_pieces(pos))
    cmp_end = CMP_STRIDE * jnp.arange(N_CMP_PAD) + (L_CMP - 1)
    constc = jnp.zeros((N_CMP_PAD, AUG), F32)
    constc = constc.at[:, POS_ROW0:POS_ROW0 + 2 * N_PIECES].set(_position_pieces(cmp_end))
    slopes = 2.0 ** (-8.0 * jnp.arange(1, N_HEADS + 1, dtype=F32) / N_HEADS)
    pieces, rest = [], slopes
    for _ in range(N_PIECES):
        piece = rest.astype(BF16).astype(F32)
        pieces.append(piece)
        rest = rest - piece
    return constk.astype(BF16), constc.astype(BF16), jnp.stack(pieces, axis=1).reshape(-1)


def _nsa_mixer(x, norm_g, w_in, b_gate, cmp_pos, cmp_w1, cmp_b1, cmp_w2, cmp_b2):
    b, s, d = x.shape
    g, nrep, dh = N_GROUPS, HEADS_PER_GROUP, HEAD_DIM
    w_in_p = jnp.pad(w_in, ((0, 0), (0, NSA_IN_PAD - NSA_IN))).astype(BF16)
    v_cols = lambda i: w_in_p[:, NSA_Q + i * NSA_KV:NSA_Q + (i + 1) * NSA_KV].T
    q, raw, keys, vals_t, glogit = _nsa_in_proj(x, norm_g, w_in_p, jnp.stack([v_cols(3), v_cols(5)]),
                                                tm=512)
    raw = raw.reshape(2, b, g, s // CMP_STRIDE, CMP_STRIDE * dh)
    cmp, cmp_t = _compress(raw, cmp_pos.reshape(2, 1, L_CMP * dh), cmp_w1.astype(BF16), cmp_b1,
                           cmp_w2.astype(BF16), cmp_b2)
    constk, constc, slope_pieces = _nsa_constants(s)
    bgate = jnp.pad(b_gate, (0, LANES - 3 * N_HEADS)).reshape(1, LANES)
    return _nsa_attention(slope_pieces, q, cmp, cmp_t, keys, vals_t, glogit, bgate, constk, constc)


TT = 256
SUB = SUBLANES


GATE_BAND = 256


def _gate_bands():
    block = D_RNN // LRU_BLOCKS
    bands = []
    for c0 in range(0, D_RNN, GATE_BAND):
        width = min(GATE_BAND, D_RNN - c0)
        k_lo = (c0 // block) * block // LANES * LANES
        k_hi = -(-(((c0 + width - 1) // block + 1) * block) // LANES) * LANES
        bands.append((k_lo, min(k_hi, D_RNN), c0, width))
    return tuple(bands)


GATE_BANDS = _gate_bands()


def _pack_gate_weights(w_a, w_x):
    dense = lambda w: jax.scipy.linalg.block_diag(*[w[i] for i in range(LRU_BLOCKS)])
    slabs = []
    for k_lo, k_hi, c0, width in GATE_BANDS:
        band = [jnp.pad(dense(w)[k_lo:k_hi, c0:c0 + width], ((0, 0), (0, GATE_BAND - width)))
                for w in (w_a, w_x)]
        slabs.append(jnp.concatenate(band, axis=1))
    return jnp.concatenate(slabs, axis=0).astype(BF16)


def _sigmoid(z):
    return 0.5 * jnp.tanh(0.5 * z) + 0.5


def _lru_kernel(x_ref, g_ref, win_ref, cw_ref, cb_ref, wax_ref, bax_ref, lam_ref, wout_ref, o_ref,
                ext_ref, a_ref, u_ref, h_ref, gate_ref):
    ti = pl.program_id(1)

    @pl.when(ti == 0)
    def _():
        ext_ref[pl.ds(TT, SUB), :] = jnp.zeros((SUB, D_RNN), F32)
        h_ref[...] = jnp.zeros_like(h_ref)

    xn = _rms(x_ref[...], g_ref[...]).astype(BF16)
    gate_ref[...] = jax.nn.gelu(
        jnp.dot(xn, win_ref[:, :D_RNN], preferred_element_type=F32), approximate=True)
    ext_ref[pl.ds(0, SUB), :] = ext_ref[pl.ds(TT, SUB), :]
    ext_ref[pl.ds(SUB, TT), :] = jnp.dot(xn, win_ref[:, D_RNN:], preferred_element_type=F32)
    cw = cw_ref[...]
    xr = cb_ref[...] + sum(
        cw[w:w + 1, :] * ext_ref[pl.ds(SUB - (CONV_W - 1) + w, TT), :] for w in range(CONV_W))

    lam = lam_ref[...]
    softplus_neg = jnp.maximum(-lam, 0.0) + jnp.log1p(jnp.exp(-jnp.abs(lam)))
    xr_b = xr.astype(BF16)
    bias = bax_ref[...]
    row0 = 0
    for k_lo, k_hi, c0, width in GATE_BANDS:
        z = jnp.dot(xr_b[:, k_lo:k_hi], wax_ref[row0:row0 + k_hi - k_lo, :], preferred_element_type=F32)
        row0 += k_hi - k_lo
        r_gate = _sigmoid(z[:, :width] + bias[:, c0:c0 + width])
        i_gate = _sigmoid(z[:, GATE_BAND:GATE_BAND + width] + bias[:, D_RNN + c0:D_RNN + c0 + width])
        a = jnp.exp(-LRU_C * r_gate * softplus_neg[:, c0:c0 + width])
        a_ref[:, c0:c0 + width] = a
        u_ref[:, c0:c0 + width] = jnp.sqrt(jnp.maximum(1.0 - a * a, 0.0)) * (i_gate * xr[:, c0:c0 + width])

    row = lax.broadcasted_iota(jnp.int32, (SUB, D_RNN), 0)

    def scan_rows(c, h_prev):
        r0 = pl.multiple_of(c * SUB, SUB)
        a_c = a_ref[pl.ds(r0, SUB), :]
        u_c = u_ref[pl.ds(r0, SUB), :]
        shift = 1
        while shift < SUB:
            keep = row >= shift
            u_c = u_c + a_c * jnp.where(keep, pltpu.roll(u_c, shift, axis=0), 0.0)
            a_c = a_c * jnp.where(keep, pltpu.roll(a_c, shift, axis=0), 1.0)
            shift *= 2
        h_c = u_c + a_c * h_prev
        u_ref[pl.ds(r0, SUB), :] = h_c
        return jnp.broadcast_to(h_c[SUB - 1:SUB, :], (SUB, D_RNN))

    h_ref[...] = lax.fori_loop(0, TT // SUB, scan_rows, h_ref[...])
    gated = (u_ref[...] * gate_ref[...]).astype(BF16)
    o_ref[...] = x_ref[...] + jnp.dot(gated, wout_ref[...], preferred_element_type=F32)


def _lru_mixer(x, norm_g, w_in, conv_w, conv_b, w_a, b_a, w_x, b_x, lam, w_out):
    b, s, d = x.shape
    wax = _pack_gate_weights(w_a, w_x)
    bax = jnp.concatenate([b_a, b_x])
    sq = pl.Squeezed()
    resident = lambda *shape: pl.BlockSpec(shape, lambda i, t: (0,) * len(shape),
                                           pipeline_mode=pl.Buffered(1))
    return pl.pallas_call(
        _lru_kernel,
        grid=(b, s // TT),
        in_specs=[
            pl.BlockSpec((sq, TT, d), lambda i, t: (i, t, 0)),
            resident(1, d), resident(d, 2 * D_RNN), resident(CONV_W, D_RNN), resident(1, D_RNN),
            resident(*wax.shape), resident(1, 2 * D_RNN), resident(1, D_RNN), resident(D_RNN, d),
        ],
        out_specs=pl.BlockSpec((sq, TT, d), lambda i, t: (i, t, 0)),
        out_shape=jax.ShapeDtypeStruct((b, s, d), F32),
        scratch_shapes=[
            pltpu.VMEM((TT + SUB, D_RNN), F32),
            pltpu.VMEM((TT, D_RNN), F32),
            pltpu.VMEM((TT, D_RNN), F32),
            pltpu.VMEM((SUB, D_RNN), F32),
            pltpu.VMEM((TT, D_RNN), F32),
        ],
        compiler_params=_cparams("parallel", "arbitrary"),
        name="lru_block",
    )(x, norm_g.reshape(1, d), w_in.astype(BF16), conv_w, conv_b.reshape(1, -1), wax,
      bax.reshape(1, -1), lam.reshape(1, -1), w_out.astype(BF16))


def kernel(x, norm_mix, norm_ffn, norm_final, nsa_w_in, nsa_b_gate, nsa_cmp_pos, nsa_cmp_w1,
           nsa_cmp_b1, nsa_cmp_w2, nsa_cmp_b2, nsa_w_out, lru_w_in, lru_conv_w, lru_conv_b,
           lru_w_a, lru_b_a, lru_w_x, lru_b_x, lru_lambda, lru_w_out, ffn_w_in, ffn_w_out):
    b, s, d = x.shape
    ffn = functools.partial(_ffn, tm=512, tf=256)
    attn = _nsa_mixer(x, norm_mix[0], nsa_w_in[0], nsa_b_gate[0], nsa_cmp_pos[0], nsa_cmp_w1[0],
                      nsa_cmp_b1[0], nsa_cmp_w2[0], nsa_cmp_b2[0])
    x = ffn(x.reshape(b * s, d), norm_ffn[0], ffn_w_in[0].astype(BF16), ffn_w_out[0].astype(BF16),
            norm_final, final_norm=False, name="ffn0", mix=attn.reshape(b * s, NSA_Q),
            w_mix=nsa_w_out[0].astype(BF16)).reshape(b, s, d)
    x = _lru_mixer(x, norm_mix[1], lru_w_in[0], lru_conv_w[0], lru_conv_b[0], lru_w_a[0], lru_b_a[0],
                   lru_w_x[0], lru_b_x[0], lru_lambda[0], lru_w_out[0])
    x = ffn(x.reshape(b * s, d), norm_ffn[1], ffn_w_in[1].astype(BF16), ffn_w_out[1].astype(BF16),
            norm_final, final_norm=True, name="ffn1").reshape(b, s, d)
    return x
```

```python
import functools

import jax
import jax.numpy as jnp
from jax import lax
from jax.experimental import pallas as pl
from jax.experimental.pallas import tpu as pltpu

F32 = jnp.float32
BF16 = jnp.bfloat16

D_MODEL = 1024
N_HEADS = 16
N_GROUPS = 4
HEADS_PER_GROUP = N_HEADS // N_GROUPS
HEAD_DIM = 64
L_CMP = 32
CMP_STRIDE = 16
L_SEL = 64
N_SELECT = 16
WINDOW = 512
NSA_Q = N_HEADS * HEAD_DIM
NSA_KV = N_GROUPS * HEAD_DIM
NSA_IN = NSA_Q + 6 * NSA_KV + 3 * N_HEADS
D_RNN = 1408
LRU_BLOCKS = 8
LRU_C = 8.0
CONV_W = 4
D_FF = 2816
EPS = 1e-6

LANES = 128
SUBLANES = 8
NEG = -1e30
VMEM_LIMIT = 56 * 1024 * 1024

TQ = 128
TK_SEL = 512
TK_WIN = 128
N_CMP_PAD = 256


def _cparams(*sem):
    return pltpu.CompilerParams(dimension_semantics=sem, vmem_limit_bytes=VMEM_LIMIT)


def _rms(x, g):
    ms = jnp.mean(x * x, axis=-1, keepdims=True)
    return x * lax.rsqrt(ms + EPS) * g


def _nt_dot(a, b):
    return lax.dot_general(a, b, (((1,), (1,)), ((), ())), preferred_element_type=F32)


def _ffn_kernel(*refs, final_norm, tf, has_mix):
    x_ref, refs = refs[0], refs[1:]
    x = x_ref[...]
    if has_mix:
        (mix_ref, wmix_ref), refs = refs[:2], refs[2:]
        x = x + jnp.dot(mix_ref[...], wmix_ref[...], preferred_element_type=F32)
    g_ref, win_ref, wout_ref, gf_ref, o_ref, hid_ref = refs
    xn = _rms(x, g_ref[...]).astype(BF16)
    for c0 in range(0, D_FF, tf):
        gate = jnp.dot(xn, win_ref[:, c0:c0 + tf], preferred_element_type=F32)
        up = jnp.dot(xn, win_ref[:, D_FF + c0:D_FF + c0 + tf], preferred_element_type=F32)
        hid_ref[:, c0:c0 + tf] = ((0.5 * gate) * (1.0 + jnp.tanh(0.5 * gate)) * up).astype(BF16)
    y = x + jnp.dot(hid_ref[...], wout_ref[...], preferred_element_type=F32)
    if final_norm:
        y = _rms(y, gf_ref[...])
    o_ref[...] = y


def _ffn(x, g, w_in, w_out, g_final, *, final_norm, tm, tf, name, mix=None, w_mix=None):
    m, d = x.shape
    resident = lambda *shape: pl.BlockSpec(shape, lambda i: (0,) * len(shape),
                                           pipeline_mode=pl.Buffered(1))
    rows = lambda width: pl.BlockSpec((tm, width), lambda i: (i, 0))
    if mix is None:
        mix_specs, mix_args = [], ()
    else:
        mix_specs, mix_args = [rows(mix.shape[1]), resident(*w_mix.shape)], (mix, w_mix)
    return pl.pallas_call(
        functools.partial(_ffn_kernel, final_norm=final_norm, tf=tf, has_mix=mix is not None),
        grid=(m // tm,),
        in_specs=[rows(d)] + mix_specs + [
            resident(1, d), resident(d, 2 * D_FF), resident(D_FF, d), resident(1, d)],
        out_specs=rows(d),
        out_shape=jax.ShapeDtypeStruct((m, d), F32),
        scratch_shapes=[pltpu.VMEM((tm, D_FF), BF16)],
        compiler_params=_cparams("parallel"),
        name=name,
    )(x, *mix_args, g.reshape(1, d), w_in, w_out, g_final.reshape(1, d))


NSA_IN_PAD = -(-NSA_IN // LANES) * LANES


def _nsa_in_proj_kernel(x_ref, g_ref, w_ref, wvt_ref, q_ref, raw_ref, k_ref, vt_ref, gl_ref):
    xn = _rms(x_ref[...], g_ref[...]).astype(BF16)
    q_ref[...] = jnp.dot(xn, w_ref[:, :NSA_Q], preferred_element_type=F32).astype(q_ref.dtype)
    for i, (dst, slot) in {0: (raw_ref, 0), 1: (raw_ref, 1), 2: (k_ref, 0), 4: (k_ref, 1)}.items():
        col0 = NSA_Q + i * NSA_KV
        part = jnp.dot(xn, w_ref[:, col0:col0 + NSA_KV], preferred_element_type=F32)
        for grp in range(N_GROUPS):
            dst[slot, grp] = part[:, grp * HEAD_DIM:(grp + 1) * HEAD_DIM].astype(dst.dtype)
    for slot in range(2):
        part_t = _nt_dot(wvt_ref[slot], xn)
        for grp in range(N_GROUPS):
            vt_ref[slot, grp] = part_t[grp * HEAD_DIM:(grp + 1) * HEAD_DIM].astype(vt_ref.dtype)
    gl_ref[...] = jnp.dot(xn, w_ref[:, NSA_Q + 6 * NSA_KV:], preferred_element_type=F32)


def _nsa_in_proj(x, g, w, wvt, *, tm):
    b, s, d = x.shape
    sq = pl.Squeezed()
    grouped = lambda dtype: jax.ShapeDtypeStruct((2, b, N_GROUPS, s, HEAD_DIM), dtype)
    grouped_spec = pl.BlockSpec((2, sq, N_GROUPS, tm, HEAD_DIM), lambda i, t: (0, i, 0, t, 0))
    return pl.pallas_call(
        _nsa_in_proj_kernel,
        grid=(b, s // tm),
        in_specs=[
            pl.BlockSpec((sq, tm, d), lambda i, t: (i, t, 0)),
            pl.BlockSpec((1, d), lambda i, t: (0, 0)),
            pl.BlockSpec((d, NSA_IN_PAD), lambda i, t: (0, 0)),
            pl.BlockSpec((2, NSA_KV, d), lambda i, t: (0, 0, 0)),
        ],
        out_specs=[
            pl.BlockSpec((sq, tm, NSA_Q), lambda i, t: (i, t, 0)),
            grouped_spec, grouped_spec,
            pl.BlockSpec((2, sq, N_GROUPS, HEAD_DIM, tm), lambda i, t: (0, i, 0, 0, t)),
            pl.BlockSpec((sq, tm, LANES), lambda i, t: (i, t, 0)),
        ],
        out_shape=[
            jax.ShapeDtypeStruct((b, s, NSA_Q), BF16),
            grouped(F32), grouped(BF16),
            jax.ShapeDtypeStruct((2, b, N_GROUPS, HEAD_DIM, s), BF16),
            jax.ShapeDtypeStruct((b, s, LANES), F32),
        ],
        compiler_params=_cparams("parallel", "parallel"),
        name="nsa_in_proj",
    )(x, g.reshape(1, d), w, wvt)


def _compress_kernel(r_ref, pos_ref, w1_ref, b1_ref, w2_ref, b2_ref, w2t_ref, b2t_ref, o_ref, ot_ref):
    half = CMP_STRIDE * HEAD_DIM
    rows = r_ref[...]
    pos = pos_ref[...]
    lo = (rows + pos[:, :half]).astype(BF16)
    hi = (rows + pos[:, half:]).astype(BF16)
    part_lo = jnp.dot(lo, w1_ref[:half, :], preferred_element_type=F32)
    part_hi = jnp.dot(hi, w1_ref[half:, :], preferred_element_type=F32)
    hid = part_lo + pltpu.roll(part_hi, N_CMP_PAD - 1, axis=0) + b1_ref[...]
    hid = jax.nn.gelu(hid, approximate=True).astype(BF16)
    o_ref[...] = (jnp.dot(hid, w2_ref[...], preferred_element_type=F32) + b2_ref[...]).astype(o_ref.dtype)
    ot_ref[...] = (_nt_dot(w2t_ref[...], hid) + b2t_ref[...]).astype(ot_ref.dtype)


def _compress(raw, pos, w1, b1, w2, b2):
    two, b, g, n, width = raw.shape
    hidden = w1.shape[-1]
    sq = pl.Squeezed()
    per_branch = lambda *shape: pl.BlockSpec((sq,) + shape, lambda c, i, j: (c,) + (0,) * len(shape))
    return pl.pallas_call(
        _compress_kernel,
        grid=(two, b, g),
        in_specs=[
            pl.BlockSpec((sq, sq, sq, n, width), lambda c, i, j: (c, i, j, 0, 0)),
            per_branch(1, 2 * width), per_branch(2 * width, hidden), per_branch(1, hidden),
            per_branch(hidden, HEAD_DIM), per_branch(1, HEAD_DIM),
            per_branch(HEAD_DIM, hidden), per_branch(HEAD_DIM, 1),
        ],
        out_specs=[
            pl.BlockSpec((sq, sq, sq, n, HEAD_DIM), lambda c, i, j: (c, i, j, 0, 0)),
            pl.BlockSpec((sq, sq, sq, HEAD_DIM, n), lambda c, i, j: (c, i, j, 0, 0)),
        ],
        out_shape=[
            jax.ShapeDtypeStruct((two, b, g, n, HEAD_DIM), BF16),
            jax.ShapeDtypeStruct((two, b, g, HEAD_DIM, n), BF16),
        ],
        compiler_params=_cparams("parallel", "parallel", "parallel"),
        name="nsa_compress",
    )(raw, pos, w1, b1.reshape(two, 1, hidden), w2, b2.reshape(two, 1, HEAD_DIM),
      w2.transpose(0, 2, 1), b2.reshape(two, HEAD_DIM, 1))


AUG = 256
SEL_ROW0 = HEAD_DIM
POS_ROW0 = 2 * HEAD_DIM
N_PIECES = 3
PAD_LANE = POS_ROW0 + 2 * N_PIECES
POS_ROWS = 16
V_ROWS = 80
NQ = HEADS_PER_GROUP * TQ
BIG = 1e30
assert TQ == TK_WIN and TK_SEL % TQ == 0 and WINDOW % TK_WIN == 0


def _flash_reset(m_ref, acc_ref):
    m_ref[...] = jnp.full_like(m_ref, NEG)
    acc_ref[...] = jnp.zeros_like(acc_ref)


def _scores(kaug, tile, tk, qat_ref):
    k0 = pl.multiple_of(tile * tk, tk)
    return jnp.dot(kaug[pl.ds(k0, tk), :], qat_ref[...], preferred_element_type=F32)


def _flash_update(s, vt_tile, mask, m_ref, acc_ref):
    if mask is not None:
        s = jnp.where(mask, s, NEG)
    m_old = m_ref[...]
    m_new = jnp.maximum(m_old, jnp.max(s, axis=0, keepdims=True))
    p = jnp.exp(s - m_new).astype(BF16)
    acc_ref[...] = jnp.exp(m_old - m_new) * acc_ref[...] + jnp.dot(
        vt_tile, p, preferred_element_type=F32)
    m_ref[...] = m_new


def _flash_result(acc_ref):
    return acc_ref[pl.ds(0, HEAD_DIM), :] * (1.0 / acc_ref[pl.ds(HEAD_DIM, 1), :])


GROUPS_PER_STEP = 2


def _nsa_kernel(sl_ref, q_ref, kc_all, vct_all, ks_all, vst_all, kw_all, vwt_all, gl_ref, bg_ref,
                constk_ref, constc_ref, o_ref, *scratch):
    qi = pl.program_id(2)
    gt_ref = scratch[-1]
    per_group = (kc_all, vct_all, ks_all, vst_all, kw_all, vwt_all) + scratch[:-1]
    groups = [tuple(ref.at[h] for ref in per_group) for h in range(GROUPS_PER_STEP)]

    @pl.when(qi == 0)
    def _():
        for refs in groups:
            _nsa_assemble(constk_ref, constc_ref, *refs)

    gt_ref[...] = jax.nn.sigmoid(gl_ref[...] + bg_ref[...]).T
    tiles = [_nsa_tile(pl.program_id(1) * GROUPS_PER_STEP + h, h * HEADS_PER_GROUP * HEAD_DIM, qi,
                       sl_ref, q_ref, o_ref, gt_ref, *refs) for h, refs in enumerate(groups)]
    heads = list(tiles)
    while heads:
        heads = [tile for tile in heads if not next(tile)]
    for tile in tiles:
        next(tile, None)


def _nsa_assemble(constk_ref, constc_ref, kc_ref, vct_ref, ks_ref, vst_ref, kw_ref, vwt_ref,
                  kaug_s, kaug_w, kaug_c, vt_s, vt_w, vt_c, *unused):
    s_len = ks_ref.shape[0]
    for kaug, row0, const, k in ((kaug_s, 0, constk_ref, ks_ref), (kaug_w, WINDOW, constk_ref, kw_ref),
                                 (kaug_c, 0, constc_ref, kc_ref)):
        kaug[pl.ds(row0, k.shape[0]), :] = const[...]
        kaug[pl.ds(row0, k.shape[0]), pl.ds(0, HEAD_DIM)] = k[...]
    pad_lane = lax.broadcasted_iota(jnp.int32, (WINDOW, AUG), 1) == PAD_LANE
    kaug_w[pl.ds(0, WINDOW), :] = jnp.where(pad_lane, 1.0, 0.0).astype(BF16)
    vt_w[:, pl.ds(0, WINDOW)] = jnp.zeros((V_ROWS, WINDOW), BF16)
    pad = V_ROWS - HEAD_DIM
    ones_rows = lambda n: jnp.where(
        lax.broadcasted_iota(jnp.int32, (pad, n), 0) == 0, 1.0, 0.0).astype(BF16)
    for c in range(vt_s.shape[0]):
        vt_s[c, pl.ds(0, HEAD_DIM), :] = vst_ref[:, pl.ds(c * TK_SEL, TK_SEL)]
        vt_s[c, pl.ds(HEAD_DIM, pad), :] = ones_rows(TK_SEL)
    vt_w[pl.ds(0, HEAD_DIM), pl.ds(WINDOW, s_len)] = vwt_ref[...]
    vt_w[pl.ds(HEAD_DIM, pad), pl.ds(WINDOW, s_len)] = ones_rows(s_len)
    vt_c[pl.ds(0, HEAD_DIM), :] = vct_ref[...]
    vt_c[pl.ds(HEAD_DIM, pad), :] = ones_rows(N_CMP_PAD)


def _nsa_tile(grp, lane0, qi, sl_ref, q_ref, o_ref, gt_ref, kc_ref, vct_ref, ks_ref, vst_ref, kw_ref,
              vwt_ref, kaug_s, kaug_w, kaug_c, vt_s, vt_w, vt_c, qat_ref, score_ref, m_ref, acc_ref,
              out_ref, sa_ref, sb_ref, qats_ref, tiles_ref):
    q0 = qi * TQ
    nrep = HEADS_PER_GROUP

    for half in range(nrep // 2):
        qt = q_ref[:, pl.ds(lane0 + half * LANES, LANES)].astype(F32).T * (HEAD_DIM ** -0.5)
        for sub in range(2):
            q_rows = qt[sub * HEAD_DIM:(sub + 1) * HEAD_DIM].astype(BF16)
            for ref in (qat_ref, qats_ref):
                ref[pl.ds(0, HEAD_DIM), pl.ds((2 * half + sub) * TQ, TQ)] = q_rows
    qat_ref[pl.ds(SEL_ROW0, HEAD_DIM), :] = jnp.zeros((HEAD_DIM, NQ), BF16)
    piece = lax.broadcasted_iota(jnp.int32, (POS_ROWS, TQ), 0)
    tail0 = POS_ROW0 + POS_ROWS
    for r in range(nrep):
        tile = jnp.where(piece == PAD_LANE - POS_ROW0, -BIG, 0.0)
        for i in range(2 * N_PIECES):
            tile = jnp.where(piece == i, sl_ref[(grp * nrep + r) * N_PIECES + i % N_PIECES], tile)
        for ref in (qat_ref, qats_ref):
            ref[pl.ds(POS_ROW0, POS_ROWS), pl.ds(r * TQ, TQ)] = tile.astype(BF16)
    for ref in (qat_ref, qats_ref):
        ref[pl.ds(tail0, AUG - tail0), :] = jnp.zeros((AUG - tail0, NQ), BF16)

    yield False
    t_row = q0 + (lax.broadcasted_iota(jnp.int32, (1, NQ), 1) & (TQ - 1))
    gate_row = lambda br: jnp.concatenate(
        [gt_ref[pl.ds(3 * (grp * nrep + r) + br, 1), :] for r in range(nrep)], axis=1)

    s = jnp.dot(kaug_c[...], qat_ref[...], preferred_element_type=F32)
    cmp_end = CMP_STRIDE * lax.broadcasted_iota(jnp.int32, (N_CMP_PAD, 1), 0) + (L_CMP - 1)
    mask_c = cmp_end <= t_row
    s = jnp.where(mask_c, s, NEG)
    e = jnp.where(mask_c, jnp.exp(s - jnp.max(s, axis=0, keepdims=True)), 0.0)
    d = jnp.sum(e, axis=0, keepdims=True)
    p = e * (1.0 / jnp.where(d > 0, d, 1.0))
    p_sum = sum(p[:, r * TQ:(r + 1) * TQ] for r in range(nrep))
    o_c = jnp.dot(vt_c[...], p.astype(BF16), preferred_element_type=F32)
    out_ref[...] = gate_row(0) * o_c[:HEAD_DIM]
    yield False

    n_win = WINDOW // TK_WIN
    k0 = pl.multiple_of(q0, TK_WIN)
    s_all = jnp.dot(kaug_w[pl.ds(k0, WINDOW + TQ), :], qat_ref[...], preferred_element_type=F32)
    s_w = [s_all[j * TK_WIN:(j + 1) * TK_WIN] for j in range(n_win + 1)]
    not_after = q0 + lax.broadcasted_iota(jnp.int32, (TK_WIN, 1), 0) <= t_row
    s_w[0] = jnp.where(not_after, NEG, s_w[0])
    s_w[-1] = jnp.where(not_after, s_w[-1], NEG)
    m_w = functools.reduce(jnp.maximum, [jnp.max(x, axis=0, keepdims=True) for x in s_w])
    p_w = jnp.concatenate([jnp.exp(x - m_w).astype(BF16) for x in s_w], axis=0)
    acc_w = jnp.dot(vt_w[:, pl.ds(k0, WINDOW + TQ)], p_w, preferred_element_type=F32)
    out_ref[...] += gate_row(2) * (acc_w[:HEAD_DIM] * (1.0 / acc_w[HEAD_DIM:HEAD_DIM + 1]))
    yield False

    n_sb = score_ref.shape[0]
    jb = lax.broadcasted_iota(jnp.int32, (n_sb, TQ), 0)
    t_sel = q0 + lax.broadcasted_iota(jnp.int32, (n_sb, TQ), 1)
    valid = jb * L_SEL <= t_sel

    n_full = q0 // TK_SEL
    blocks_per_tile = TK_SEL // L_SEL

    def write_selection(selected):
        sel_bias = jnp.where(selected, 0.0, -BIG).astype(BF16)
        for r in range(nrep):
            qats_ref[pl.ds(SEL_ROW0, n_sb), pl.ds(r * TQ, TQ)] = sel_bias
        picked = jnp.where(selected, 1.0, 0.0)
        bits = sum(jnp.max(picked[k * blocks_per_tile:(k + 1) * blocks_per_tile], axis=(0, 1),
                           keepdims=True) * float(2 ** k) for k in range(n_sb // blocks_per_tile))
        bits = bits[0, 0].astype(jnp.int32)
        count = jnp.int32(0)
        for k in range(n_sb // blocks_per_tile):
            tiles_ref[count] = k
            count = count + jnp.where((k < n_full) & (((bits >> k) & 1) == 1), 1, 0)
        tiles_ref[count] = n_full
        return count

    def rank_blocks():
        cmp_start = CMP_STRIDE * lax.broadcasted_iota(jnp.int32, (n_sb, N_CMP_PAD), 1)
        sb_start = L_SEL * lax.broadcasted_iota(jnp.int32, (n_sb, N_CMP_PAD), 0)
        overlap_t = ((cmp_start < sb_start + L_SEL) & (cmp_start + L_CMP > sb_start)).astype(F32)
        imp_t = jnp.dot(overlap_t, p_sum, precision=lax.Precision.HIGHEST,
                        preferred_element_type=F32)
        cur = t_sel // L_SEL
        forced = (jb == 0) | (jb == cur) | (jb == cur - 1)
        score = jnp.where(valid, jnp.where(forced, jnp.inf, imp_t), -jnp.inf)
        score_ref[...] = score
        chunks = [score[c * SUBLANES:(c + 1) * SUBLANES] for c in range(n_sb // SUBLANES)]
        ranks = [jnp.zeros((SUBLANES, TQ), F32) for _ in chunks]
        sub_row = lax.broadcasted_iota(jnp.int32, (SUBLANES, TQ), 0)
        for i in range(n_sb):
            row = score_ref[pl.ds(i, 1), :]
            for c, chunk in enumerate(chunks):
                later = jnp.where(row >= chunk, 1.0, 0.0)
                earlier = jnp.where(row > chunk, 1.0, 0.0)
                if i < c * SUBLANES:
                    ranks[c] = ranks[c] + later
                elif i >= (c + 1) * SUBLANES:
                    ranks[c] = ranks[c] + earlier
                else:
                    ranks[c] = ranks[c] + jnp.where(sub_row > i - c * SUBLANES, later, earlier)
        rank = jnp.concatenate(ranks, axis=0)
        return write_selection(valid & (rank < float(N_SELECT)))

    n_list = rank_blocks()
    yield False

    _flash_reset(m_ref, acc_ref)
    key_s = lax.broadcasted_iota(jnp.int32, (TK_SEL, 1), 0)
    tail_mask = n_full * TK_SEL + key_s <= t_row
    sel_scores = lambda i: _scores(kaug_s, tiles_ref[i], TK_SEL, qats_ref)
    sel_update = lambda s_ref, i, mask: _flash_update(s_ref[...], vt_s[tiles_ref[i]], mask, m_ref, acc_ref)
    sa_ref[...] = sel_scores(0)
    yield True

    def sel_pair(j, carry):
        i = 2 * j
        sb_ref[...] = sel_scores(i + 1)
        sel_update(sa_ref, i, None)
        sa_ref[...] = sel_scores(i + 2)
        sel_update(sb_ref, i + 1, None)
        return carry

    lax.fori_loop(0, n_list // 2, sel_pair, 0)

    @pl.when(n_list % 2 == 1)
    def _():
        sb_ref[...] = sel_scores(n_list)
        sel_update(sa_ref, n_list - 1, None)
        sel_update(sb_ref, n_list, tail_mask)

    @pl.when(n_list % 2 == 0)
    def _():
        sel_update(sa_ref, n_list, tail_mask)

    out_ref[...] += gate_row(1) * _flash_result(acc_ref)

    for half in range(nrep // 2):
        slab = jnp.concatenate([out_ref[:, pl.ds(2 * half * TQ, TQ)],
                                out_ref[:, pl.ds((2 * half + 1) * TQ, TQ)]], axis=0)
        o_ref[:, pl.ds(lane0 + half * LANES, LANES)] = slab.T.astype(o_ref.dtype)


def _nsa_attention(slope_pieces, q, cmp, cmp_t, keys, vals_t, glogit, bgate, constk, constc):
    b, s, _ = q.shape
    g, nrep, dh = N_GROUPS, HEADS_PER_GROUP, HEAD_DIM
    n_sb = s // L_SEL
    sq = pl.Squeezed()
    gp = GROUPS_PER_STEP
    slot = lambda c, *shape: pl.BlockSpec((sq, sq, gp) + shape, lambda i, j, t: (c, i, j, 0, 0))
    full = lambda *shape: pl.BlockSpec(shape, lambda i, j, t: (0,) * len(shape))
    per_group = lambda shape, dtype: pltpu.VMEM((gp,) + shape, dtype)
    return pl.pallas_call(
        _nsa_kernel,
        grid=(b, g // gp, s // TQ),
        in_specs=[
            pl.BlockSpec(memory_space=pltpu.SMEM),
            pl.BlockSpec((sq, TQ, gp * nrep * dh), lambda i, j, t: (i, t, j)),
            slot(0, N_CMP_PAD, dh), slot(1, dh, N_CMP_PAD),
            slot(0, s, dh), slot(0, dh, s), slot(1, s, dh), slot(1, dh, s),
            pl.BlockSpec((sq, TQ, LANES), lambda i, j, t: (i, t, 0)),
            full(1, LANES), full(s, AUG), full(N_CMP_PAD, AUG),
        ],
        out_specs=pl.BlockSpec((sq, TQ, gp * nrep * dh), lambda i, j, t: (i, t, j)),
        out_shape=jax.ShapeDtypeStruct((b, s, NSA_Q), BF16),
        scratch_shapes=[
            per_group((s, AUG), BF16), per_group((WINDOW + s, AUG), BF16), per_group((N_CMP_PAD, AUG), BF16),
            per_group((s // TK_SEL, V_ROWS, TK_SEL), BF16), per_group((V_ROWS, WINDOW + s), BF16),
            per_group((V_ROWS, N_CMP_PAD), BF16),
            per_group((AUG, NQ), BF16), per_group((n_sb, TQ), F32), per_group((1, NQ), F32),
            per_group((V_ROWS, NQ), F32), per_group((dh, NQ), F32),
            per_group((TK_SEL, NQ), F32), per_group((TK_SEL, NQ), F32),
            per_group((AUG, NQ), BF16),
            pltpu.SMEM((gp, s // TK_SEL + 1), jnp.int32),
            pltpu.VMEM((LANES, TQ), F32),
        ],
        compiler_params=_cparams("arbitrary", "arbitrary", "arbitrary"),
        name="nsa_attention",
    )(slope_pieces, q, cmp, cmp_t, keys, vals_t, keys, vals_t, glogit, bgate, constk, constc)


def _position_pieces(pos):
    hi = (pos // L_SEL) * L_SEL
    return jnp.stack([hi] * N_PIECES + [pos - hi] * N_PIECES, axis=1).astype(F32)


def _nsa_constants(s):
    pos = jnp.arange(s)
    constk = jnp.zeros((s, AUG), F32)
    constk = constk.at[:, SEL_ROW0:SEL_ROW0 + s // L_SEL].set(jax.nn.one_hot(pos // L_SEL, s // L_SEL))
    constk = constk.at[:, POS_ROW0:POS_ROW0 + 2 * N_PIECES].set(_position_pieces(pos))
    cmp_end = CMP_STRIDE * jnp.arange(N_CMP_PAD) + (L_CMP - 1)
    constc = jnp.zeros((N_CMP_PAD, AUG), F32)
    constc = constc.at[:, POS_ROW0:POS_ROW0 + 2 * N_PIECES].set(_position_pieces(cmp_end))
    slopes = 2.0 ** (-8.0 * jnp.arange(1, N_HEADS + 1, dtype=F32) / N_HEADS)
    pieces, rest = [], slopes
    for _ in range(N_PIECES):
        piece = rest.astype(BF16).astype(F32)
        pieces.append(piece)
        rest = rest - piece
    return constk.astype(BF16), constc.astype(BF16), jnp.stack(pieces, axis=1).reshape(-1)


def _nsa_mixer(x, norm_g, w_in, b_gate, cmp_pos, cmp_w1, cmp_b1, cmp_w2, cmp_b2):
    b, s, d = x.shape
    g, nrep, dh = N_GROUPS, HEADS_PER_GROUP, HEAD_DIM
    w_in_p = jnp.pad(w_in, ((0, 0), (0, NSA_IN_PAD - NSA_IN))).astype(BF16)
    v_cols = lambda i: w_in_p[:, NSA_Q + i * NSA_KV:NSA_Q + (i + 1) * NSA_KV].T
    q, raw, keys, vals_t, glogit = _nsa_in_proj(x, norm_g, w_in_p, jnp.stack([v_cols(3), v_cols(5)]),
                                                tm=512)
    raw = raw.reshape(2, b, g, s // CMP_STRIDE, CMP_STRIDE * dh)
    cmp, cmp_t = _compress(raw, cmp_pos.reshape(2, 1, L_CMP * dh), cmp_w1.astype(BF16), cmp_b1,
                           cmp_w2.astype(BF16), cmp_b2)
    constk, constc, slope_pieces = _nsa_constants(s)
    bgate = jnp.pad(b_gate, (0, LANES - 3 * N_HEADS)).reshape(1, LANES)
    return _nsa_attention(slope_pieces, q, cmp, cmp_t, keys, vals_t, glogit, bgate, constk, constc)


TT = 256
SUB = SUBLANES


GATE_BAND = 256


def _gate_bands():
    block = D_RNN // LRU_BLOCKS
    bands = []
    for c0 in range(0, D_RNN, GATE_BAND):
        width = min(GATE_BAND, D_RNN - c0)
        k_lo = (c0 // block) * block // LANES * LANES
        k_hi = -(-(((c0 + width - 1) // block + 1) * block) // LANES) * LANES
        bands.append((k_lo, min(k_hi, D_RNN), c0, width))
    return tuple(bands)


GATE_BANDS = _gate_bands()


def _pack_gate_weights(w_a, w_x):
    dense = lambda w: jax.scipy.linalg.block_diag(*[w[i] for i in range(LRU_BLOCKS)])
    slabs = []
    for k_lo, k_hi, c0, width in GATE_BANDS:
        band = [jnp.pad(dense(w)[k_lo:k_hi, c0:c0 + width], ((0, 0), (0, GATE_BAND - width)))
                for w in (w_a, w_x)]
        slabs.append(jnp.concatenate(band, axis=1))
    return jnp.concatenate(slabs, axis=0).astype(BF16)


def _sigmoid(z):
    return 0.5 * jnp.tanh(0.5 * z) + 0.5


SEQS_PER_STEP = 2


def _lru_kernel(x_all, g_ref, win_ref, cw_ref, cb_ref, wax_ref, bax_ref, lam_ref, wout_ref, o_all,
                *scratch):
    seqs = [tuple(ref.at[h] for ref in (x_all, o_all) + scratch) for h in range(SEQS_PER_STEP)]

    @pl.when(pl.program_id(1) == 0)
    def _():
        for x_ref, o_ref, ext_ref, a_ref, u_ref, h_ref, gate_ref in seqs:
            ext_ref[pl.ds(TT, SUB), :] = jnp.zeros((SUB, D_RNN), F32)
            h_ref[...] = jnp.zeros_like(h_ref)

    tiles = [_lru_tile(g_ref, win_ref, cw_ref, cb_ref, wax_ref, bax_ref, lam_ref, wout_ref, *refs)
             for refs in seqs]
    while tiles:
        tiles = [tile for tile in tiles if next(tile, True) is not True]


def _lru_tile(g_ref, win_ref, cw_ref, cb_ref, wax_ref, bax_ref, lam_ref, wout_ref,
              x_ref, o_ref, ext_ref, a_ref, u_ref, h_ref, gate_ref):
    xn = _rms(x_ref[...], g_ref[...]).astype(BF16)
    gate_ref[...] = jax.nn.gelu(
        jnp.dot(xn, win_ref[:, :D_RNN], preferred_element_type=F32), approximate=True)
    ext_ref[pl.ds(0, SUB), :] = ext_ref[pl.ds(TT, SUB), :]
    ext_ref[pl.ds(SUB, TT), :] = jnp.dot(xn, win_ref[:, D_RNN:], preferred_element_type=F32)
    yield
    cw = cw_ref[...]
    xr = cb_ref[...] + sum(
        cw[w:w + 1, :] * ext_ref[pl.ds(SUB - (CONV_W - 1) + w, TT), :] for w in range(CONV_W))

    lam = lam_ref[...]
    softplus_neg = jnp.maximum(-lam, 0.0) + jnp.log1p(jnp.exp(-jnp.abs(lam)))
    xr_b = xr.astype(BF16)
    bias = bax_ref[...]
    row0 = 0
    for k_lo, k_hi, c0, width in GATE_BANDS:
        z = jnp.dot(xr_b[:, k_lo:k_hi], wax_ref[row0:row0 + k_hi - k_lo, :], preferred_element_type=F32)
        row0 += k_hi - k_lo
        r_gate = _sigmoid(z[:, :width] + bias[:, c0:c0 + width])
        i_gate = _sigmoid(z[:, GATE_BAND:GATE_BAND + width] + bias[:, D_RNN + c0:D_RNN + c0 + width])
        a = jnp.exp(-LRU_C * r_gate * softplus_neg[:, c0:c0 + width])
        a_ref[:, c0:c0 + width] = a
        u_ref[:, c0:c0 + width] = jnp.sqrt(jnp.maximum(1.0 - a * a, 0.0)) * (i_gate * xr[:, c0:c0 + width])
    yield

    row = lax.broadcasted_iota(jnp.int32, (SUB, D_RNN), 0)

    def scan_rows(c, h_prev):
        r0 = c * SUB
        a_c = a_ref[pl.ds(r0, SUB), :]
        u_c = u_ref[pl.ds(r0, SUB), :]
        shift = 1
        while shift < SUB:
            keep = row >= shift
            u_c = u_c + a_c * jnp.where(keep, pltpu.roll(u_c, shift, axis=0), 0.0)
            a_c = a_c * jnp.where(keep, pltpu.roll(a_c, shift, axis=0), 1.0)
            shift *= 2
        h_c = u_c + a_c * h_prev
        u_ref[pl.ds(r0, SUB), :] = h_c
        return jnp.broadcast_to(h_c[SUB - 1:SUB, :], (SUB, D_RNN))

    h_state = h_ref[...]
    for c in range(TT // SUB):
        h_state = scan_rows(c, h_state)
        if c % (TT // SUB // 4) == TT // SUB // 4 - 1:
            yield
    h_ref[...] = h_state
    gated = (u_ref[...] * gate_ref[...]).astype(BF16)
    o_ref[...] = x_ref[...] + jnp.dot(gated, wout_ref[...], preferred_element_type=F32)


def _lru_mixer(x, norm_g, w_in, conv_w, conv_b, w_a, b_a, w_x, b_x, lam, w_out):
    b, s, d = x.shape
    wax = _pack_gate_weights(w_a, w_x)
    bax = jnp.concatenate([b_a, b_x])
    nseq = SEQS_PER_STEP
    resident = lambda *shape: pl.BlockSpec(shape, lambda i, t: (0,) * len(shape),
                                           pipeline_mode=pl.Buffered(1))
    per_seq = lambda rows: pltpu.VMEM((nseq, rows, D_RNN), F32)
    return pl.pallas_call(
        _lru_kernel,
        grid=(b // nseq, s // TT),
        in_specs=[
            pl.BlockSpec((nseq, TT, d), lambda i, t: (i, t, 0)),
            resident(1, d), resident(d, 2 * D_RNN), resident(CONV_W, D_RNN), resident(1, D_RNN),
            resident(*wax.shape), resident(1, 2 * D_RNN), resident(1, D_RNN), resident(D_RNN, d),
        ],
        out_specs=pl.BlockSpec((nseq, TT, d), lambda i, t: (i, t, 0)),
        out_shape=jax.ShapeDtypeStruct((b, s, d), F32),
        scratch_shapes=[per_seq(TT + SUB), per_seq(TT), per_seq(TT), per_seq(SUB), per_seq(TT)],
        compiler_params=_cparams("parallel", "arbitrary"),
        name="lru_block",
    )(x, norm_g.reshape(1, d), w_in.astype(BF16), conv_w, conv_b.reshape(1, -1), wax,
      bax.reshape(1, -1), lam.reshape(1, -1), w_out.astype(BF16))


def kernel(x, norm_mix, norm_ffn, norm_final, nsa_w_in, nsa_b_gate, nsa_cmp_pos, nsa_cmp_w1,
           nsa_cmp_b1, nsa_cmp_w2, nsa_cmp_b2, nsa_w_out, lru_w_in, lru_conv_w, lru_conv_b,
           lru_w_a, lru_b_a, lru_w_x, lru_b_x, lru_lambda, lru_w_out, ffn_w_in, ffn_w_out):
    b, s, d = x.shape
    ffn = functools.partial(_ffn, tm=512, tf=256)
    attn = _nsa_mixer(x, norm_mix[0], nsa_w_in[0], nsa_b_gate[0], nsa_cmp_pos[0], nsa_cmp_w1[0],
                      nsa_cmp_b1[0], nsa_cmp_w2[0], nsa_cmp_b2[0])
    x = ffn(x.reshape(b * s, d), norm_ffn[0], ffn_w_in[0].astype(BF16), ffn_w_out[0].astype(BF16),
            norm_final, final_norm=False, name="ffn0", mix=attn.reshape(b * s, NSA_Q),
            w_mix=nsa_w_out[0].astype(BF16)).reshape(b, s, d)
    x = _lru_mixer(x, norm_mix[1], lru_w_in[0], lru_conv_w[0], lru_conv_b[0], lru_w_a[0], lru_b_a[0],
                   lru_w_x[0], lru_b_x[0], lru_lambda[0], lru_w_out[0])
    x = ffn(x.reshape(b * s, d), norm_ffn[1], ffn_w_in[1].astype(BF16), ffn_w_out[1].astype(BF16),
            norm_final, final_norm=True, name="ffn1").reshape(b, s, d)
    return x
```

```python
import functools

import jax
import jax.numpy as jnp
from jax import lax
from jax.experimental import pallas as pl
from jax.experimental.pallas import tpu as pltpu

F32 = jnp.float32
BF16 = jnp.bfloat16

D_MODEL = 1024
N_HEADS = 16
N_GROUPS = 4
HEADS_PER_GROUP = N_HEADS // N_GROUPS
HEAD_DIM = 64
L_CMP = 32
CMP_STRIDE = 16
L_SEL = 64
N_SELECT = 16
WINDOW = 512
NSA_Q = N_HEADS * HEAD_DIM
NSA_KV = N_GROUPS * HEAD_DIM
NSA_IN = NSA_Q + 6 * NSA_KV + 3 * N_HEADS
D_RNN = 1408
LRU_BLOCKS = 8
LRU_C = 8.0
CONV_W = 4
D_FF = 2816
EPS = 1e-6

LANES = 128
SUBLANES = 8
NEG = -1e30
VMEM_LIMIT = 56 * 1024 * 1024

TQ = 256
TK_SEL = 512
TK_WIN = 256
N_CMP_PAD = 256


def _cparams(*sem):
    return pltpu.CompilerParams(dimension_semantics=sem, vmem_limit_bytes=VMEM_LIMIT)


def _rms(x, g):
    ms = jnp.mean(x * x, axis=-1, keepdims=True)
    return x * lax.rsqrt(ms + EPS) * g


def _nt_dot(a, b):
    return lax.dot_general(a, b, (((1,), (1,)), ((), ())), preferred_element_type=F32)


def _ffn_kernel(*refs, final_norm, tf, has_mix):
    x_ref, refs = refs[0], refs[1:]
    x = x_ref[...]
    if has_mix:
        (mix_ref, wmix_ref), refs = refs[:2], refs[2:]
        x = x + jnp.dot(mix_ref[...], wmix_ref[...], preferred_element_type=F32)
    g_ref, win_ref, wout_ref, gf_ref, o_ref, hid_ref = refs
    xn = _rms(x, g_ref[...]).astype(BF16)
    for c0 in range(0, D_FF, tf):
        gate = jnp.dot(xn, win_ref[:, c0:c0 + tf], preferred_element_type=F32)
        up = jnp.dot(xn, win_ref[:, D_FF + c0:D_FF + c0 + tf], preferred_element_type=F32)
        hid_ref[:, c0:c0 + tf] = ((0.5 * gate) * (1.0 + jnp.tanh(0.5 * gate)) * up).astype(BF16)
    y = x + jnp.dot(hid_ref[...], wout_ref[...], preferred_element_type=F32)
    if final_norm:
        y = _rms(y, gf_ref[...])
    o_ref[...] = y


def _ffn(x, g, w_in, w_out, g_final, *, final_norm, tm, tf, name, mix=None, w_mix=None):
    m, d = x.shape
    resident = lambda *shape: pl.BlockSpec(shape, lambda i: (0,) * len(shape),
                                           pipeline_mode=pl.Buffered(1))
    rows = lambda width: pl.BlockSpec((tm, width), lambda i: (i, 0))
    if mix is None:
        mix_specs, mix_args = [], ()
    else:
        mix_specs, mix_args = [rows(mix.shape[1]), resident(*w_mix.shape)], (mix, w_mix)
    return pl.pallas_call(
        functools.partial(_ffn_kernel, final_norm=final_norm, tf=tf, has_mix=mix is not None),
        grid=(m // tm,),
        in_specs=[rows(d)] + mix_specs + [
            resident(1, d), resident(d, 2 * D_FF), resident(D_FF, d), resident(1, d)],
        out_specs=rows(d),
        out_shape=jax.ShapeDtypeStruct((m, d), F32),
        scratch_shapes=[pltpu.VMEM((tm, D_FF), BF16)],
        compiler_params=_cparams("parallel"),
        name=name,
    )(x, *mix_args, g.reshape(1, d), w_in, w_out, g_final.reshape(1, d))


NSA_IN_PAD = -(-NSA_IN // LANES) * LANES


def _nsa_in_proj_kernel(x_ref, g_ref, w_ref, wvt_ref, q_ref, raw_ref, k_ref, vt_ref, gl_ref):
    xn = _rms(x_ref[...], g_ref[...]).astype(BF16)
    q_ref[...] = jnp.dot(xn, w_ref[:, :NSA_Q], preferred_element_type=F32).astype(q_ref.dtype)
    for i, (dst, slot) in {0: (raw_ref, 0), 1: (raw_ref, 1), 2: (k_ref, 0), 4: (k_ref, 1)}.items():
        col0 = NSA_Q + i * NSA_KV
        part = jnp.dot(xn, w_ref[:, col0:col0 + NSA_KV], preferred_element_type=F32)
        for grp in range(N_GROUPS):
            dst[slot, grp] = part[:, grp * HEAD_DIM:(grp + 1) * HEAD_DIM].astype(dst.dtype)
    for slot in range(2):
        part_t = _nt_dot(wvt_ref[slot], xn)
        for grp in range(N_GROUPS):
            vt_ref[slot, grp] = part_t[grp * HEAD_DIM:(grp + 1) * HEAD_DIM].astype(vt_ref.dtype)
    gl_ref[...] = jnp.dot(xn, w_ref[:, NSA_Q + 6 * NSA_KV:], preferred_element_type=F32)


def _nsa_in_proj(x, g, w, wvt, *, tm):
    b, s, d = x.shape
    sq = pl.Squeezed()
    grouped = lambda dtype: jax.ShapeDtypeStruct((2, b, N_GROUPS, s, HEAD_DIM), dtype)
    grouped_spec = pl.BlockSpec((2, sq, N_GROUPS, tm, HEAD_DIM), lambda i, t: (0, i, 0, t, 0))
    return pl.pallas_call(
        _nsa_in_proj_kernel,
        grid=(b, s // tm),
        in_specs=[
            pl.BlockSpec((sq, tm, d), lambda i, t: (i, t, 0)),
            pl.BlockSpec((1, d), lambda i, t: (0, 0)),
            pl.BlockSpec((d, NSA_IN_PAD), lambda i, t: (0, 0)),
            pl.BlockSpec((2, NSA_KV, d), lambda i, t: (0, 0, 0)),
        ],
        out_specs=[
            pl.BlockSpec((sq, tm, NSA_Q), lambda i, t: (i, t, 0)),
            grouped_spec, grouped_spec,
            pl.BlockSpec((2, sq, N_GROUPS, HEAD_DIM, tm), lambda i, t: (0, i, 0, 0, t)),
            pl.BlockSpec((sq, tm, LANES), lambda i, t: (i, t, 0)),
        ],
        out_shape=[
            jax.ShapeDtypeStruct((b, s, NSA_Q), BF16),
            grouped(F32), grouped(BF16),
            jax.ShapeDtypeStruct((2, b, N_GROUPS, HEAD_DIM, s), BF16),
            jax.ShapeDtypeStruct((b, s, LANES), F32),
        ],
        compiler_params=_cparams("parallel", "parallel"),
        name="nsa_in_proj",
    )(x, g.reshape(1, d), w, wvt)


def _compress_kernel(r_ref, pos_ref, w1_ref, b1_ref, w2_ref, b2_ref, w2t_ref, b2t_ref, o_ref, ot_ref):
    half = CMP_STRIDE * HEAD_DIM
    rows = r_ref[...]
    pos = pos_ref[...]
    lo = (rows + pos[:, :half]).astype(BF16)
    hi = (rows + pos[:, half:]).astype(BF16)
    part_lo = jnp.dot(lo, w1_ref[:half, :], preferred_element_type=F32)
    part_hi = jnp.dot(hi, w1_ref[half:, :], preferred_element_type=F32)
    hid = part_lo + pltpu.roll(part_hi, N_CMP_PAD - 1, axis=0) + b1_ref[...]
    hid = jax.nn.gelu(hid, approximate=True).astype(BF16)
    o_ref[...] = (jnp.dot(hid, w2_ref[...], preferred_element_type=F32) + b2_ref[...]).astype(o_ref.dtype)
    ot_ref[...] = (_nt_dot(w2t_ref[...], hid) + b2t_ref[...]).astype(ot_ref.dtype)


def _compress(raw, pos, w1, b1, w2, b2):
    two, b, g, n, width = raw.shape
    hidden = w1.shape[-1]
    sq = pl.Squeezed()
    per_branch = lambda *shape: pl.BlockSpec((sq,) + shape, lambda c, i, j: (c,) + (0,) * len(shape))
    return pl.pallas_call(
        _compress_kernel,
        grid=(two, b, g),
        in_specs=[
            pl.BlockSpec((sq, sq, sq, n, width), lambda c, i, j: (c, i, j, 0, 0)),
            per_branch(1, 2 * width), per_branch(2 * width, hidden), per_branch(1, hidden),
            per_branch(hidden, HEAD_DIM), per_branch(1, HEAD_DIM),
            per_branch(HEAD_DIM, hidden), per_branch(HEAD_DIM, 1),
        ],
        out_specs=[
            pl.BlockSpec((sq, sq, sq, n, HEAD_DIM), lambda c, i, j: (c, i, j, 0, 0)),
            pl.BlockSpec((sq, sq, sq, HEAD_DIM, n), lambda c, i, j: (c, i, j, 0, 0)),
        ],
        out_shape=[
            jax.ShapeDtypeStruct((two, b, g, n, HEAD_DIM), BF16),
            jax.ShapeDtypeStruct((two, b, g, HEAD_DIM, n), BF16),
        ],
        compiler_params=_cparams("parallel", "parallel", "parallel"),
        name="nsa_compress",
    )(raw, pos, w1, b1.reshape(two, 1, hidden), w2, b2.reshape(two, 1, HEAD_DIM),
      w2.transpose(0, 2, 1), b2.reshape(two, HEAD_DIM, 1))


AUG = 256
SEL_ROW0 = HEAD_DIM
POS_ROW0 = 2 * HEAD_DIM
N_PIECES = 3
PAD_LANE = POS_ROW0 + 2 * N_PIECES
POS_ROWS = 16
V_ROWS = 80
NQ = HEADS_PER_GROUP * TQ
BIG = 1e30
assert TQ == TK_WIN and TK_SEL % TQ == 0 and WINDOW % TK_WIN == 0


def _flash_reset(m_ref, acc_ref):
    m_ref[...] = jnp.full_like(m_ref, NEG)
    acc_ref[...] = jnp.zeros_like(acc_ref)


def _scores(kaug, tile, tk, qat_ref):
    k0 = pl.multiple_of(tile * tk, tk)
    return jnp.dot(kaug[pl.ds(k0, tk), :], qat_ref[...], preferred_element_type=F32)


def _flash_update(s, vt_tile, mask, m_ref, acc_ref):
    if mask is not None:
        s = jnp.where(mask, s, NEG)
    m_old = m_ref[...]
    m_new = jnp.maximum(m_old, jnp.max(s, axis=0, keepdims=True))
    p = jnp.exp(s - m_new).astype(BF16)
    acc_ref[...] = jnp.exp(m_old - m_new) * acc_ref[...] + jnp.dot(
        vt_tile, p, preferred_element_type=F32)
    m_ref[...] = m_new


def _flash_result(acc_ref):
    return acc_ref[pl.ds(0, HEAD_DIM), :] * (1.0 / acc_ref[pl.ds(HEAD_DIM, 1), :])


GROUPS_PER_STEP = 2


def _nsa_kernel(sl_ref, q_ref, kc_all, vct_all, ks_all, vst_all, kw_all, vwt_all, gl_ref, bg_ref,
                constk_ref, constc_ref, o_ref, *scratch):
    qi = pl.program_id(2)
    gt_ref = scratch[-1]
    per_group = (kc_all, vct_all, ks_all, vst_all, kw_all, vwt_all) + scratch[:-1]
    groups = [tuple(ref.at[h] for ref in per_group) for h in range(GROUPS_PER_STEP)]

    @pl.when(qi == 0)
    def _():
        for refs in groups:
            _nsa_assemble(constk_ref, constc_ref, *refs)

    gt_ref[...] = jax.nn.sigmoid(gl_ref[...] + bg_ref[...]).T
    tiles = [_nsa_tile(pl.program_id(1) * GROUPS_PER_STEP + h, h * HEADS_PER_GROUP * HEAD_DIM, qi,
                       sl_ref, q_ref, o_ref, gt_ref, *refs) for h, refs in enumerate(groups)]
    heads = list(tiles)
    while heads:
        heads = [tile for tile in heads if not next(tile)]
    for tile in tiles:
        next(tile, None)


def _nsa_assemble(constk_ref, constc_ref, kc_ref, vct_ref, ks_ref, vst_ref, kw_ref, vwt_ref,
                  kaug_s, kaug_w, kaug_c, vt_s, vt_w, vt_c, *unused):
    s_len = ks_ref.shape[0]
    for kaug, row0, const, k in ((kaug_s, 0, constk_ref, ks_ref), (kaug_w, WINDOW, constk_ref, kw_ref),
                                 (kaug_c, 0, constc_ref, kc_ref)):
        kaug[pl.ds(row0, k.shape[0]), :] = const[...]
        kaug[pl.ds(row0, k.shape[0]), pl.ds(0, HEAD_DIM)] = k[...]
    pad_lane = lax.broadcasted_iota(jnp.int32, (WINDOW, AUG), 1) == PAD_LANE
    kaug_w[pl.ds(0, WINDOW), :] = jnp.where(pad_lane, 1.0, 0.0).astype(BF16)
    vt_w[:, pl.ds(0, WINDOW)] = jnp.zeros((V_ROWS, WINDOW), BF16)
    pad = V_ROWS - HEAD_DIM
    ones_rows = lambda n: jnp.where(
        lax.broadcasted_iota(jnp.int32, (pad, n), 0) == 0, 1.0, 0.0).astype(BF16)
    for c in range(vt_s.shape[0]):
        vt_s[c, pl.ds(0, HEAD_DIM), :] = vst_ref[:, pl.ds(c * TK_SEL, TK_SEL)]
        vt_s[c, pl.ds(HEAD_DIM, pad), :] = ones_rows(TK_SEL)
    vt_w[pl.ds(0, HEAD_DIM), pl.ds(WINDOW, s_len)] = vwt_ref[...]
    vt_w[pl.ds(HEAD_DIM, pad), pl.ds(WINDOW, s_len)] = ones_rows(s_len)
    vt_c[pl.ds(0, HEAD_DIM), :] = vct_ref[...]
    vt_c[pl.ds(HEAD_DIM, pad), :] = ones_rows(N_CMP_PAD)


def _nsa_tile(grp, lane0, qi, sl_ref, q_ref, o_ref, gt_ref, kc_ref, vct_ref, ks_ref, vst_ref, kw_ref,
              vwt_ref, kaug_s, kaug_w, kaug_c, vt_s, vt_w, vt_c, qat_ref, score_ref, m_ref, acc_ref,
              out_ref, sa_ref, sb_ref, qats_ref, tiles_ref):
    q0 = qi * TQ
    nrep = HEADS_PER_GROUP

    for half in range(nrep // 2):
        qt = q_ref[:, pl.ds(lane0 + half * LANES, LANES)].astype(F32).T * (HEAD_DIM ** -0.5)
        for sub in range(2):
            q_rows = qt[sub * HEAD_DIM:(sub + 1) * HEAD_DIM].astype(BF16)
            for ref in (qat_ref, qats_ref):
                ref[pl.ds(0, HEAD_DIM), pl.ds((2 * half + sub) * TQ, TQ)] = q_rows
    qat_ref[pl.ds(SEL_ROW0, HEAD_DIM), :] = jnp.zeros((HEAD_DIM, NQ), BF16)
    piece = lax.broadcasted_iota(jnp.int32, (POS_ROWS, TQ), 0)
    tail0 = POS_ROW0 + POS_ROWS
    for r in range(nrep):
        tile = jnp.where(piece == PAD_LANE - POS_ROW0, -BIG, 0.0)
        for i in range(2 * N_PIECES):
            tile = jnp.where(piece == i, sl_ref[(grp * nrep + r) * N_PIECES + i % N_PIECES], tile)
        for ref in (qat_ref, qats_ref):
            ref[pl.ds(POS_ROW0, POS_ROWS), pl.ds(r * TQ, TQ)] = tile.astype(BF16)
    for ref in (qat_ref, qats_ref):
        ref[pl.ds(tail0, AUG - tail0), :] = jnp.zeros((AUG - tail0, NQ), BF16)

    yield False
    t_row = q0 + (lax.broadcasted_iota(jnp.int32, (1, NQ), 1) & (TQ - 1))
    gate_row = lambda br: jnp.concatenate(
        [gt_ref[pl.ds(3 * (grp * nrep + r) + br, 1), :] for r in range(nrep)], axis=1)

    s = jnp.dot(kaug_c[...], qat_ref[...], preferred_element_type=F32)
    cmp_end = CMP_STRIDE * lax.broadcasted_iota(jnp.int32, (N_CMP_PAD, 1), 0) + (L_CMP - 1)
    mask_c = cmp_end <= t_row
    s = jnp.where(mask_c, s, NEG)
    e = jnp.where(mask_c, jnp.exp(s - jnp.max(s, axis=0, keepdims=True)), 0.0)
    d = jnp.sum(e, axis=0, keepdims=True)
    p = e * (1.0 / jnp.where(d > 0, d, 1.0))
    p_sum = sum(p[:, r * TQ:(r + 1) * TQ] for r in range(nrep))
    o_c = jnp.dot(vt_c[...], p.astype(BF16), preferred_element_type=F32)
    out_ref[...] = gate_row(0) * o_c[:HEAD_DIM]
    yield False

    n_win = WINDOW // TK_WIN
    k0 = pl.multiple_of(q0, TK_WIN)
    s_all = jnp.dot(kaug_w[pl.ds(k0, WINDOW + TQ), :], qat_ref[...], preferred_element_type=F32)
    s_w = [s_all[j * TK_WIN:(j + 1) * TK_WIN] for j in range(n_win + 1)]
    not_after = q0 + lax.broadcasted_iota(jnp.int32, (TK_WIN, 1), 0) <= t_row
    s_w[0] = jnp.where(not_after, NEG, s_w[0])
    s_w[-1] = jnp.where(not_after, s_w[-1], NEG)
    m_w = functools.reduce(jnp.maximum, [jnp.max(x, axis=0, keepdims=True) for x in s_w])
    p_w = jnp.concatenate([jnp.exp(x - m_w).astype(BF16) for x in s_w], axis=0)
    acc_w = jnp.dot(vt_w[:, pl.ds(k0, WINDOW + TQ)], p_w, preferred_element_type=F32)
    out_ref[...] += gate_row(2) * (acc_w[:HEAD_DIM] * (1.0 / acc_w[HEAD_DIM:HEAD_DIM + 1]))
    yield False

    n_sb = score_ref.shape[0]
    jb = lax.broadcasted_iota(jnp.int32, (n_sb, TQ), 0)
    t_sel = q0 + lax.broadcasted_iota(jnp.int32, (n_sb, TQ), 1)
    valid = jb * L_SEL <= t_sel

    n_full = q0 // TK_SEL
    blocks_per_tile = TK_SEL // L_SEL

    def write_selection(selected):
        sel_bias = jnp.where(selected, 0.0, -BIG).astype(BF16)
        for r in range(nrep):
            qats_ref[pl.ds(SEL_ROW0, n_sb), pl.ds(r * TQ, TQ)] = sel_bias
        picked = jnp.where(selected, 1.0, 0.0)
        bits = sum(jnp.max(picked[k * blocks_per_tile:(k + 1) * blocks_per_tile], axis=(0, 1),
                           keepdims=True) * float(2 ** k) for k in range(n_sb // blocks_per_tile))
        bits = bits[0, 0].astype(jnp.int32)
        count = jnp.int32(0)
        for k in range(n_sb // blocks_per_tile):
            tiles_ref[count] = k
            count = count + jnp.where((k < n_full) & (((bits >> k) & 1) == 1), 1, 0)
        tiles_ref[count] = n_full
        return count

    def rank_blocks():
        cmp_start = CMP_STRIDE * lax.broadcasted_iota(jnp.int32, (n_sb, N_CMP_PAD), 1)
        sb_start = L_SEL * lax.broadcasted_iota(jnp.int32, (n_sb, N_CMP_PAD), 0)
        overlap_t = ((cmp_start < sb_start + L_SEL) & (cmp_start + L_CMP > sb_start)).astype(F32)
        imp_t = jnp.dot(overlap_t, p_sum, precision=lax.Precision.HIGHEST,
                        preferred_element_type=F32)
        cur = t_sel // L_SEL
        forced = (jb == 0) | (jb == cur) | (jb == cur - 1)
        score = jnp.where(valid, jnp.where(forced, jnp.inf, imp_t), -jnp.inf)
        score_ref[...] = score
        chunks = [score[c * SUBLANES:(c + 1) * SUBLANES] for c in range(n_sb // SUBLANES)]
        ranks = [jnp.zeros((SUBLANES, TQ), F32) for _ in chunks]
        sub_row = lax.broadcasted_iota(jnp.int32, (SUBLANES, TQ), 0)
        for i in range(n_sb):
            row = score_ref[pl.ds(i, 1), :]
            for c, chunk in enumerate(chunks):
                later = jnp.where(row >= chunk, 1.0, 0.0)
                earlier = jnp.where(row > chunk, 1.0, 0.0)
                if i < c * SUBLANES:
                    ranks[c] = ranks[c] + later
                elif i >= (c + 1) * SUBLANES:
                    ranks[c] = ranks[c] + earlier
                else:
                    ranks[c] = ranks[c] + jnp.where(sub_row > i - c * SUBLANES, later, earlier)
        rank = jnp.concatenate(ranks, axis=0)
        return write_selection(valid & (rank < float(N_SELECT)))

    n_list = rank_blocks()
    yield False

    _flash_reset(m_ref, acc_ref)
    key_s = lax.broadcasted_iota(jnp.int32, (TK_SEL, 1), 0)
    tail_mask = n_full * TK_SEL + key_s <= t_row
    sel_scores = lambda i: _scores(kaug_s, tiles_ref[i], TK_SEL, qats_ref)
    sel_update = lambda s_ref, i, mask: _flash_update(s_ref[...], vt_s[tiles_ref[i]], mask, m_ref, acc_ref)
    sa_ref[...] = sel_scores(0)
    yield True

    def sel_pair(j, carry):
        i = 2 * j
        sb_ref[...] = sel_scores(i + 1)
        sel_update(sa_ref, i, None)
        sa_ref[...] = sel_scores(i + 2)
        sel_update(sb_ref, i + 1, None)
        return carry

    lax.fori_loop(0, n_list // 2, sel_pair, 0)

    @pl.when(n_list % 2 == 1)
    def _():
        sb_ref[...] = sel_scores(n_list)
        sel_update(sa_ref, n_list - 1, None)
        sel_update(sb_ref, n_list, tail_mask)

    @pl.when(n_list % 2 == 0)
    def _():
        sel_update(sa_ref, n_list, tail_mask)

    out_ref[...] += gate_row(1) * _flash_result(acc_ref)

    for half in range(nrep // 2):
        slab = jnp.concatenate([out_ref[:, pl.ds(2 * half * TQ, TQ)],
                                out_ref[:, pl.ds((2 * half + 1) * TQ, TQ)]], axis=0)
        o_ref[:, pl.ds(lane0 + half * LANES, LANES)] = slab.T.astype(o_ref.dtype)


def _nsa_attention(slope_pieces, q, cmp, cmp_t, keys, vals_t, glogit, bgate, constk, constc):
    b, s, _ = q.shape
    g, nrep, dh = N_GROUPS, HEADS_PER_GROUP, HEAD_DIM
    n_sb = s // L_SEL
    sq = pl.Squeezed()
    gp = GROUPS_PER_STEP
    slot = lambda c, *shape: pl.BlockSpec((sq, sq, gp) + shape, lambda i, j, t: (c, i, j, 0, 0))
    full = lambda *shape: pl.BlockSpec(shape, lambda i, j, t: (0,) * len(shape))
    per_group = lambda shape, dtype: pltpu.VMEM((gp,) + shape, dtype)
    return pl.pallas_call(
        _nsa_kernel,
        grid=(b, g // gp, s // TQ),
        in_specs=[
            pl.BlockSpec(memory_space=pltpu.SMEM),
            pl.BlockSpec((sq, TQ, gp * nrep * dh), lambda i, j, t: (i, t, j)),
            slot(0, N_CMP_PAD, dh), slot(1, dh, N_CMP_PAD),
            slot(0, s, dh), slot(0, dh, s), slot(1, s, dh), slot(1, dh, s),
            pl.BlockSpec((sq, TQ, LANES), lambda i, j, t: (i, t, 0)),
            full(1, LANES), full(s, AUG), full(N_CMP_PAD, AUG),
        ],
        out_specs=pl.BlockSpec((sq, TQ, gp * nrep * dh), lambda i, j, t: (i, t, j)),
        out_shape=jax.ShapeDtypeStruct((b, s, NSA_Q), BF16),
        scratch_shapes=[
            per_group((s, AUG), BF16), per_group((WINDOW + s, AUG), BF16), per_group((N_CMP_PAD, AUG), BF16),
            per_group((s // TK_SEL, V_ROWS, TK_SEL), BF16), per_group((V_ROWS, WINDOW + s), BF16),
            per_group((V_ROWS, N_CMP_PAD), BF16),
            per_group((AUG, NQ), BF16), per_group((n_sb, TQ), F32), per_group((1, NQ), F32),
            per_group((V_ROWS, NQ), F32), per_group((dh, NQ), F32),
            per_group((TK_SEL, NQ), F32), per_group((TK_SEL, NQ), F32),
            per_group((AUG, NQ), BF16),
            pltpu.SMEM((gp, s // TK_SEL + 1), jnp.int32),
            pltpu.VMEM((LANES, TQ), F32),
        ],
        compiler_params=_cparams("arbitrary", "arbitrary", "arbitrary"),
        name="nsa_attention",
    )(slope_pieces, q, cmp, cmp_t, keys, vals_t, keys, vals_t, glogit, bgate, constk, constc)


def _position_pieces(pos):
    hi = (pos // L_SEL) * L_SEL
    return jnp.stack([hi] * N_PIECES + [pos - hi] * N_PIECES, axis=1).astype(F32)


def _nsa_constants(s):
    pos = jnp.arange(s)
    constk = jnp.zeros((s, AUG), F32)
    constk = constk.at[:, SEL_ROW0:SEL_ROW0 + s // L_SEL].set(jax.nn.one_hot(pos // L_SEL, s // L_SEL))
    constk = constk.at[:, POS_ROW0:POS_ROW0 + 2 * N_PIECES].set(_position_pieces(pos))
    cmp_end = CMP_STRIDE * jnp.arange(N_CMP_PAD) + (L_CMP - 1)
    constc = jnp.zeros((N_CMP_PAD, AUG), F32)
    constc = constc.at[:, POS_ROW0:POS_ROW0 + 2 * N_PIECES].set(_position_pieces(cmp_end))
    slopes = 2.0 ** (-8.0 * jnp.arange(1, N_HEADS + 1, dtype=F32) / N_HEADS)
    pieces, rest = [], slopes
    for _ in range(N_PIECES):
        piece = rest.astype(BF16).astype(F32)
        pieces.append(piece)
        rest = rest - piece
    return constk.astype(BF16), constc.astype(BF16), jnp.stack(pieces, axis=1).reshape(-1)


def _nsa_mixer(x, norm_g, w_in, b_gate, cmp_pos, cmp_w1, cmp_b1, cmp_w2, cmp_b2):
    b, s, d = x.shape
    g, nrep, dh = N_GROUPS, HEADS_PER_GROUP, HEAD_DIM
    w_in_p = jnp.pad(w_in, ((0, 0), (0, NSA_IN_PAD - NSA_IN))).astype(BF16)
    v_cols = lambda i: w_in_p[:, NSA_Q + i * NSA_KV:NSA_Q + (i + 1) * NSA_KV].T
    q, raw, keys, vals_t, glogit = _nsa_in_proj(x, norm_g, w_in_p, jnp.stack([v_cols(3), v_cols(5)]),
                                                tm=512)
    raw = raw.reshape(2, b, g, s // CMP_STRIDE, CMP_STRIDE * dh)
    cmp, cmp_t = _compress(raw, cmp_pos.reshape(2, 1, L_CMP * dh), cmp_w1.astype(BF16), cmp_b1,
                           cmp_w2.astype(BF16), cmp_b2)
    constk, constc, slope_pieces = _nsa_constants(s)
    bgate = jnp.pad(b_gate, (0, LANES - 3 * N_HEADS)).reshape(1, LANES)
    return _nsa_attention(slope_pieces, q, cmp, cmp_t, keys, vals_t, glogit, bgate, constk, constc)


TT = 256
SUB = SUBLANES


GATE_BAND = 256


def _gate_bands():
    block = D_RNN // LRU_BLOCKS
    bands = []
    for c0 in range(0, D_RNN, GATE_BAND):
        width = min(GATE_BAND, D_RNN - c0)
        k_lo = (c0 // block) * block // LANES * LANES
        k_hi = -(-(((c0 + width - 1) // block + 1) * block) // LANES) * LANES
        bands.append((k_lo, min(k_hi, D_RNN), c0, width))
    return tuple(bands)


GATE_BANDS = _gate_bands()


def _pack_gate_weights(w_a, w_x):
    dense = lambda w: jax.scipy.linalg.block_diag(*[w[i] for i in range(LRU_BLOCKS)])
    slabs = []
    for k_lo, k_hi, c0, width in GATE_BANDS:
        band = [jnp.pad(dense(w)[k_lo:k_hi, c0:c0 + width], ((0, 0), (0, GATE_BAND - width)))
                for w in (w_a, w_x)]
        slabs.append(jnp.concatenate(band, axis=1))
    return jnp.concatenate(slabs, axis=0).astype(BF16)


def _sigmoid(z):
    return 0.5 * jnp.tanh(0.5 * z) + 0.5


SEQS_PER_STEP = 2


def _lru_kernel(x_all, g_ref, win_ref, cw_ref, cb_ref, wax_ref, bax_ref, lam_ref, wout_ref, o_all,
                *scratch):
    seqs = [tuple(ref.at[h] for ref in (x_all, o_all) + scratch) for h in range(SEQS_PER_STEP)]

    @pl.when(pl.program_id(1) == 0)
    def _():
        for x_ref, o_ref, ext_ref, a_ref, u_ref, h_ref, gate_ref in seqs:
            ext_ref[pl.ds(TT, SUB), :] = jnp.zeros((SUB, D_RNN), F32)
            h_ref[...] = jnp.zeros_like(h_ref)

    tiles = [_lru_tile(g_ref, win_ref, cw_ref, cb_ref, wax_ref, bax_ref, lam_ref, wout_ref, *refs)
             for refs in seqs]
    while tiles:
        tiles = [tile for tile in tiles if next(tile, True) is not True]


def _lru_tile(g_ref, win_ref, cw_ref, cb_ref, wax_ref, bax_ref, lam_ref, wout_ref,
              x_ref, o_ref, ext_ref, a_ref, u_ref, h_ref, gate_ref):
    xn = _rms(x_ref[...], g_ref[...]).astype(BF16)
    gate_ref[...] = jax.nn.gelu(
        jnp.dot(xn, win_ref[:, :D_RNN], preferred_element_type=F32), approximate=True)
    ext_ref[pl.ds(0, SUB), :] = ext_ref[pl.ds(TT, SUB), :]
    ext_ref[pl.ds(SUB, TT), :] = jnp.dot(xn, win_ref[:, D_RNN:], preferred_element_type=F32)
    yield
    cw = cw_ref[...]
    xr = cb_ref[...] + sum(
        cw[w:w + 1, :] * ext_ref[pl.ds(SUB - (CONV_W - 1) + w, TT), :] for w in range(CONV_W))

    lam = lam_ref[...]
    softplus_neg = jnp.maximum(-lam, 0.0) + jnp.log1p(jnp.exp(-jnp.abs(lam)))
    xr_b = xr.astype(BF16)
    bias = bax_ref[...]
    row0 = 0
    for k_lo, k_hi, c0, width in GATE_BANDS:
        z = jnp.dot(xr_b[:, k_lo:k_hi], wax_ref[row0:row0 + k_hi - k_lo, :], preferred_element_type=F32)
        row0 += k_hi - k_lo
        r_gate = _sigmoid(z[:, :width] + bias[:, c0:c0 + width])
        i_gate = _sigmoid(z[:, GATE_BAND:GATE_BAND + width] + bias[:, D_RNN + c0:D_RNN + c0 + width])
        a = jnp.exp(-LRU_C * r_gate * softplus_neg[:, c0:c0 + width])
        a_ref[:, c0:c0 + width] = a
        u_ref[:, c0:c0 + width] = jnp.sqrt(jnp.maximum(1.0 - a * a, 0.0)) * (i_gate * xr[:, c0:c0 + width])
    yield

    row = lax.broadcasted_iota(jnp.int32, (SUB, D_RNN), 0)

    def scan_rows(c, h_prev):
        r0 = c * SUB
        a_c = a_ref[pl.ds(r0, SUB), :]
        u_c = u_ref[pl.ds(r0, SUB), :]
        shift = 1
        while shift < SUB:
            keep = row >= shift
            u_c = u_c + a_c * jnp.where(keep, pltpu.roll(u_c, shift, axis=0), 0.0)
            a_c = a_c * jnp.where(keep, pltpu.roll(a_c, shift, axis=0), 1.0)
            shift *= 2
        h_c = u_c + a_c * h_prev
        u_ref[pl.ds(r0, SUB), :] = h_c
        return jnp.broadcast_to(h_c[SUB - 1:SUB, :], (SUB, D_RNN))

    h_state = h_ref[...]
    for c in range(TT // SUB):
        h_state = scan_rows(c, h_state)
        if c % (TT // SUB // 4) == TT // SUB // 4 - 1:
            yield
    h_ref[...] = h_state
    gated = (u_ref[...] * gate_ref[...]).astype(BF16)
    o_ref[...] = x_ref[...] + jnp.dot(gated, wout_ref[...], preferred_element_type=F32)


def _lru_mixer(x, norm_g, w_in, conv_w, conv_b, w_a, b_a, w_x, b_x, lam, w_out):
    b, s, d = x.shape
    wax = _pack_gate_weights(w_a, w_x)
    bax = jnp.concatenate([b_a, b_x])
    nseq = SEQS_PER_STEP
    resident = lambda *shape: pl.BlockSpec(shape, lambda i, t: (0,) * len(shape),
                                           pipeline_mode=pl.Buffered(1))
    per_seq = lambda rows: pltpu.VMEM((nseq, rows, D_RNN), F32)
    return pl.pallas_call(
        _lru_kernel,
        grid=(b // nseq, s // TT),
        in_specs=[
            pl.BlockSpec((nseq, TT, d), lambda i, t: (i, t, 0)),
            resident(1, d), resident(d, 2 * D_RNN), resident(CONV_W, D_RNN), resident(1, D_RNN),
            resident(*wax.shape), resident(1, 2 * D_RNN), resident(1, D_RNN), resident(D_RNN, d),
        ],
        out_specs=pl.BlockSpec((nseq, TT, d), lambda i, t: (i, t, 0)),
        out_shape=jax.ShapeDtypeStruct((b, s, d), F32),
        scratch_shapes=[per_seq(TT + SUB), per_seq(TT), per_seq(TT), per_seq(SUB), per_seq(TT)],
        compiler_params=_cparams("parallel", "arbitrary"),
        name="lru_block",
    )(x, norm_g.reshape(1, d), w_in.astype(BF16), conv_w, conv_b.reshape(1, -1), wax,
      bax.reshape(1, -1), lam.reshape(1, -1), w_out.astype(BF16))


def kernel(x, norm_mix, norm_ffn, norm_final, nsa_w_in, nsa_b_gate, nsa_cmp_pos, nsa_cmp_w1,
           nsa_cmp_b1, nsa_cmp_w2, nsa_cmp_b2, nsa_w_out, lru_w_in, lru_conv_w, lru_conv_b,
           lru_w_a, lru_b_a, lru_w_x, lru_b_x, lru_lambda, lru_w_out, ffn_w_in, ffn_w_out):
    b, s, d = x.shape
    ffn = functools.partial(_ffn, tm=512, tf=256)
    attn = _nsa_mixer(x, norm_mix[0], nsa_w_in[0], nsa_b_gate[0], nsa_cmp_pos[0], nsa_cmp_w1[0],
                      nsa_cmp_b1[0], nsa_cmp_w2[0], nsa_cmp_b2[0])
    x = ffn(x.reshape(b * s, d), norm_ffn[0], ffn_w_in[0].astype(BF16), ffn_w_out[0].astype(BF16),
            norm_final, final_norm=False, name="ffn0", mix=attn.reshape(b * s, NSA_Q),
            w_mix=nsa_w_out[0].astype(BF16)).reshape(b, s, d)
    x = _lru_mixer(x, norm_mix[1], lru_w_in[0], lru_conv_w[0], lru_conv_b[0], lru_w_a[0], lru_b_a[0],
                   lru_w_x[0], lru_b_x[0], lru_lambda[0], lru_w_out[0])
    x = ffn(x.reshape(b * s, d), norm_ffn[1], ffn_w_in[1].astype(BF16), ffn_w_out[1].astype(BF16),
            norm_final, final_norm=True, name="ffn1").reshape(b, s, d)
    return x
```

```python
import functools

import jax
import jax.numpy as jnp
from jax import lax
from jax.experimental import pallas as pl
from jax.experimental.pallas import tpu as pltpu

F32 = jnp.float32
BF16 = jnp.bfloat16

D_MODEL = 1024
N_HEADS = 16
N_GROUPS = 4
HEADS_PER_GROUP = N_HEADS // N_GROUPS
HEAD_DIM = 64
L_CMP = 32
CMP_STRIDE = 16
L_SEL = 64
N_SELECT = 16
WINDOW = 512
NSA_Q = N_HEADS * HEAD_DIM
NSA_KV = N_GROUPS * HEAD_DIM
NSA_IN = NSA_Q + 6 * NSA_KV + 3 * N_HEADS
D_RNN = 1408
LRU_BLOCKS = 8
LRU_C = 8.0
CONV_W = 4
D_FF = 2816
EPS = 1e-6

LANES = 128
SUBLANES = 8
NEG = -1e30
VMEM_LIMIT = 56 * 1024 * 1024

TQ = 256
TK_SEL = 512
TK_WIN = 256
N_CMP_PAD = 256


def _cparams(*sem):
    return pltpu.CompilerParams(dimension_semantics=sem, vmem_limit_bytes=VMEM_LIMIT)


def _rms(x, g):
    ms = jnp.mean(x * x, axis=-1, keepdims=True)
    return x * lax.rsqrt(ms + EPS) * g


def _gelu_tanh(x):
    c = (2.0 / jnp.pi) ** 0.5
    half_x = 0.5 * x
    return half_x * jnp.tanh(x * (c + (c * 0.044715) * (x * x))) + half_x


def _nt_dot(a, b):
    return lax.dot_general(a, b, (((1,), (1,)), ((), ())), preferred_element_type=F32)


def _ffn_kernel(*refs, final_norm, tf, has_mix):
    x_ref, refs = refs[0], refs[1:]
    x = x_ref[...]
    if has_mix:
        (mix_ref, wmix_ref), refs = refs[:2], refs[2:]
        x = x + jnp.dot(mix_ref[...], wmix_ref[...], preferred_element_type=F32)
    g_ref, win_ref, wout_ref, gf_ref, o_ref, hid_ref = refs
    xn = _rms(x, g_ref[...]).astype(BF16)
    for c0 in range(0, D_FF, tf):
        gate = jnp.dot(xn, win_ref[:, c0:c0 + tf], preferred_element_type=F32)
        up = jnp.dot(xn, win_ref[:, D_FF + c0:D_FF + c0 + tf], preferred_element_type=F32)
        hid_ref[:, c0:c0 + tf] = ((0.5 * gate) * (1.0 + jnp.tanh(0.5 * gate)) * up).astype(BF16)
    y = x + jnp.dot(hid_ref[...], wout_ref[...], preferred_element_type=F32)
    if final_norm:
        y = _rms(y, gf_ref[...])
    o_ref[...] = y


def _ffn(x, g, w_in, w_out, g_final, *, final_norm, tm, tf, name, mix=None, w_mix=None):
    m, d = x.shape
    resident = lambda *shape: pl.BlockSpec(shape, lambda i: (0,) * len(shape),
                                           pipeline_mode=pl.Buffered(1))
    rows = lambda width: pl.BlockSpec((tm, width), lambda i: (i, 0))
    if mix is None:
        mix_specs, mix_args = [], ()
    else:
        mix_specs, mix_args = [rows(mix.shape[1]), resident(*w_mix.shape)], (mix, w_mix)
    return pl.pallas_call(
        functools.partial(_ffn_kernel, final_norm=final_norm, tf=tf, has_mix=mix is not None),
        grid=(m // tm,),
        in_specs=[rows(d)] + mix_specs + [
            resident(1, d), resident(d, 2 * D_FF), resident(D_FF, d), resident(1, d)],
        out_specs=rows(d),
        out_shape=jax.ShapeDtypeStruct((m, d), F32),
        scratch_shapes=[pltpu.VMEM((tm, D_FF), BF16)],
        compiler_params=_cparams("parallel"),
        name=name,
    )(x, *mix_args, g.reshape(1, d), w_in, w_out, g_final.reshape(1, d))


NSA_IN_PAD = -(-NSA_IN // LANES) * LANES


def _nsa_in_proj_kernel(x_ref, g_ref, w_ref, wvt_ref, q_ref, raw_ref, k_ref, vt_ref, gl_ref):
    xn = _rms(x_ref[...], g_ref[...]).astype(BF16)
    q_ref[...] = jnp.dot(xn, w_ref[:, :NSA_Q], preferred_element_type=F32).astype(q_ref.dtype)
    for i, (dst, slot) in {0: (raw_ref, 0), 1: (raw_ref, 1), 2: (k_ref, 0), 4: (k_ref, 1)}.items():
        col0 = NSA_Q + i * NSA_KV
        part = jnp.dot(xn, w_ref[:, col0:col0 + NSA_KV], preferred_element_type=F32)
        for grp in range(N_GROUPS):
            dst[slot, grp] = part[:, grp * HEAD_DIM:(grp + 1) * HEAD_DIM].astype(dst.dtype)
    for slot in range(2):
        part_t = _nt_dot(wvt_ref[slot], xn)
        for grp in range(N_GROUPS):
            vt_ref[slot, grp] = part_t[grp * HEAD_DIM:(grp + 1) * HEAD_DIM].astype(vt_ref.dtype)
    gl_ref[...] = jnp.dot(xn, w_ref[:, NSA_Q + 6 * NSA_KV:], preferred_element_type=F32)


def _nsa_in_proj(x, g, w, wvt, *, tm):
    b, s, d = x.shape
    sq = pl.Squeezed()
    grouped = lambda dtype: jax.ShapeDtypeStruct((2, b, N_GROUPS, s, HEAD_DIM), dtype)
    grouped_spec = pl.BlockSpec((2, sq, N_GROUPS, tm, HEAD_DIM), lambda i, t: (0, i, 0, t, 0))
    return pl.pallas_call(
        _nsa_in_proj_kernel,
        grid=(b, s // tm),
        in_specs=[
            pl.BlockSpec((sq, tm, d), lambda i, t: (i, t, 0)),
            pl.BlockSpec((1, d), lambda i, t: (0, 0)),
            pl.BlockSpec((d, NSA_IN_PAD), lambda i, t: (0, 0)),
            pl.BlockSpec((2, NSA_KV, d), lambda i, t: (0, 0, 0)),
        ],
        out_specs=[
            pl.BlockSpec((sq, tm, NSA_Q), lambda i, t: (i, t, 0)),
            grouped_spec, grouped_spec,
            pl.BlockSpec((2, sq, N_GROUPS, HEAD_DIM, tm), lambda i, t: (0, i, 0, 0, t)),
            pl.BlockSpec((sq, tm, LANES), lambda i, t: (i, t, 0)),
        ],
        out_shape=[
            jax.ShapeDtypeStruct((b, s, NSA_Q), BF16),
            grouped(BF16), grouped(BF16),
            jax.ShapeDtypeStruct((2, b, N_GROUPS, HEAD_DIM, s), BF16),
            jax.ShapeDtypeStruct((b, s, LANES), F32),
        ],
        compiler_params=_cparams("parallel", "parallel"),
        name="nsa_in_proj",
    )(x, g.reshape(1, d), w, wvt)


def _compress_kernel(r_ref, pos_ref, w1_ref, b1_ref, w2_ref, b2_ref, w2t_ref, b2t_ref, o_ref, ot_ref,
                     bias_ref):
    half = CMP_STRIDE * HEAD_DIM
    rows = r_ref[...]
    part_lo = jnp.dot(rows, w1_ref[:half, :], preferred_element_type=F32)
    part_hi = jnp.dot(rows, w1_ref[half:, :], preferred_element_type=F32)
    @pl.when((pl.program_id(1) == 0) & (pl.program_id(2) == 0))
    def _():
        bias_ref[...] = b1_ref[...] + jnp.dot(
            pos_ref[...], w1_ref[...].astype(F32), precision=lax.Precision.HIGHEST,
            preferred_element_type=F32)

    hid = part_lo + pltpu.roll(part_hi, N_CMP_PAD - 1, axis=0) + bias_ref[...]
    hid = jax.nn.gelu(hid, approximate=True).astype(BF16)
    o_ref[...] = (jnp.dot(hid, w2_ref[...], preferred_element_type=F32) + b2_ref[...]).astype(o_ref.dtype)
    ot_ref[...] = (_nt_dot(w2t_ref[...], hid) + b2t_ref[...]).astype(ot_ref.dtype)


def _compress(raw, pos, w1, b1, w2, b2):
    two, b, g, n, width = raw.shape
    hidden = w1.shape[-1]
    sq = pl.Squeezed()
    per_branch = lambda *shape: pl.BlockSpec((sq,) + shape, lambda c, i, j: (c,) + (0,) * len(shape))
    return pl.pallas_call(
        _compress_kernel,
        grid=(two, b, g),
        in_specs=[
            pl.BlockSpec((sq, sq, sq, n, width), lambda c, i, j: (c, i, j, 0, 0)),
            per_branch(1, 2 * width), per_branch(2 * width, hidden), per_branch(1, hidden),
            per_branch(hidden, HEAD_DIM), per_branch(1, HEAD_DIM),
            per_branch(HEAD_DIM, hidden), per_branch(HEAD_DIM, 1),
        ],
        out_specs=[
            pl.BlockSpec((sq, sq, sq, n, HEAD_DIM), lambda c, i, j: (c, i, j, 0, 0)),
            pl.BlockSpec((sq, sq, sq, HEAD_DIM, n), lambda c, i, j: (c, i, j, 0, 0)),
        ],
        out_shape=[
            jax.ShapeDtypeStruct((two, b, g, n, HEAD_DIM), BF16),
            jax.ShapeDtypeStruct((two, b, g, HEAD_DIM, n), BF16),
        ],
        scratch_shapes=[pltpu.VMEM((1, hidden), F32)],
        compiler_params=_cparams("arbitrary", "arbitrary", "arbitrary"),
        name="nsa_compress",
    )(raw, pos, w1, b1.reshape(two, 1, hidden), w2, b2.reshape(two, 1, HEAD_DIM),
      w2.transpose(0, 2, 1), b2.reshape(two, HEAD_DIM, 1))


AUG = 256
SEL_ROW0 = HEAD_DIM
POS_ROW0 = 2 * HEAD_DIM
N_PIECES = 3
PAD_LANE = POS_ROW0 + 2 * N_PIECES
POS_ROWS = 16
V_ROWS = 80
NQ = HEADS_PER_GROUP * TQ
BIG = 1e30
assert TQ == TK_WIN and TK_SEL % TQ == 0 and WINDOW % TK_WIN == 0


def _flash_reset(m_ref, acc_ref):
    m_ref[...] = jnp.full_like(m_ref, NEG)
    acc_ref[...] = jnp.zeros_like(acc_ref)


def _scores(kaug, tile, tk, qat_ref):
    k0 = pl.multiple_of(tile * tk, tk)
    return jnp.dot(kaug[pl.ds(k0, tk), :], qat_ref[...], preferred_element_type=F32)


def _flash_update(s, vt_tile, mask, m_ref, acc_ref):
    if mask is not None:
        s = jnp.where(mask, s, NEG)
    m_old = m_ref[...]
    m_new = jnp.maximum(m_old, jnp.max(s, axis=0, keepdims=True))
    p = jnp.exp(s - m_new).astype(BF16)
    acc_ref[...] = jnp.exp(m_old - m_new) * acc_ref[...] + jnp.dot(
        vt_tile, p, preferred_element_type=F32)
    m_ref[...] = m_new


def _flash_result(acc_ref):
    return acc_ref[pl.ds(0, HEAD_DIM), :] * (1.0 / acc_ref[pl.ds(HEAD_DIM, 1), :])


GROUPS_PER_STEP = 2


def _nsa_kernel(sl_ref, q_ref, kc_all, vct_all, ks_all, vst_all, kw_all, vwt_all, gl_ref, bg_ref,
                constk_ref, constc_ref, o_ref, *scratch):
    qi = pl.program_id(2)
    gt_ref = scratch[-1]
    per_group = (kc_all, vct_all, ks_all, vst_all, kw_all, vwt_all) + scratch[:-1]
    groups = [tuple(ref.at[h] for ref in per_group) for h in range(GROUPS_PER_STEP)]

    @pl.when(qi == 0)
    def _():
        for refs in groups:
            _nsa_assemble(constk_ref, constc_ref, *refs)

    gt_ref[...] = jax.nn.sigmoid(gl_ref[...] + bg_ref[...]).T
    tiles = [_nsa_tile(pl.program_id(1) * GROUPS_PER_STEP + h, h * HEADS_PER_GROUP * HEAD_DIM, qi,
                       sl_ref, q_ref, o_ref, gt_ref, *refs) for h, refs in enumerate(groups)]
    heads = list(tiles)
    while heads:
        heads = [tile for tile in heads if not next(tile)]
    for tile in tiles:
        next(tile, None)


def _nsa_assemble(constk_ref, constc_ref, kc_ref, vct_ref, ks_ref, vst_ref, kw_ref, vwt_ref,
                  kaug_s, kaug_w, kaug_c, vt_s, vt_w, vt_c, *unused):
    s_len = ks_ref.shape[0]
    for kaug, row0, const, k in ((kaug_s, 0, constk_ref, ks_ref), (kaug_w, WINDOW, constk_ref, kw_ref),
                                 (kaug_c, 0, constc_ref, kc_ref)):
        kaug[pl.ds(row0, k.shape[0]), :] = const[...]
        kaug[pl.ds(row0, k.shape[0]), pl.ds(0, HEAD_DIM)] = k[...]
    pad_lane = lax.broadcasted_iota(jnp.int32, (WINDOW, AUG), 1) == PAD_LANE
    kaug_w[pl.ds(0, WINDOW), :] = jnp.where(pad_lane, 1.0, 0.0).astype(BF16)
    vt_w[:, pl.ds(0, WINDOW)] = jnp.zeros((V_ROWS, WINDOW), BF16)
    pad = V_ROWS - HEAD_DIM
    ones_rows = lambda n: jnp.where(
        lax.broadcasted_iota(jnp.int32, (pad, n), 0) == 0, 1.0, 0.0).astype(BF16)
    for c in range(vt_s.shape[0]):
        vt_s[c, pl.ds(0, HEAD_DIM), :] = vst_ref[:, pl.ds(c * TK_SEL, TK_SEL)]
        vt_s[c, pl.ds(HEAD_DIM, pad), :] = ones_rows(TK_SEL)
    vt_w[pl.ds(0, HEAD_DIM), pl.ds(WINDOW, s_len)] = vwt_ref[...]
    vt_w[pl.ds(HEAD_DIM, pad), pl.ds(WINDOW, s_len)] = ones_rows(s_len)
    vt_c[pl.ds(0, HEAD_DIM), :] = vct_ref[...]
    vt_c[pl.ds(HEAD_DIM, pad), :] = ones_rows(N_CMP_PAD)


def _nsa_tile(grp, lane0, qi, sl_ref, q_ref, o_ref, gt_ref, kc_ref, vct_ref, ks_ref, vst_ref, kw_ref,
              vwt_ref, kaug_s, kaug_w, kaug_c, vt_s, vt_w, vt_c, qat_ref, score_ref, m_ref, acc_ref,
              out_ref, sa_ref, sb_ref, qats_ref, tiles_ref):
    q0 = qi * TQ
    nrep = HEADS_PER_GROUP

    for half in range(nrep // 2):
        qt = q_ref[:, pl.ds(lane0 + half * LANES, LANES)].astype(F32).T * (HEAD_DIM ** -0.5)
        for sub in range(2):
            q_rows = qt[sub * HEAD_DIM:(sub + 1) * HEAD_DIM].astype(BF16)
            for ref in (qat_ref, qats_ref):
                ref[pl.ds(0, HEAD_DIM), pl.ds((2 * half + sub) * TQ, TQ)] = q_rows
    qat_ref[pl.ds(SEL_ROW0, HEAD_DIM), :] = jnp.zeros((HEAD_DIM, NQ), BF16)
    piece = lax.broadcasted_iota(jnp.int32, (POS_ROWS, TQ), 0)
    tail0 = POS_ROW0 + POS_ROWS
    for r in range(nrep):
        tile = jnp.where(piece == PAD_LANE - POS_ROW0, -BIG, 0.0)
        for i in range(2 * N_PIECES):
            tile = jnp.where(piece == i, sl_ref[(grp * nrep + r) * N_PIECES + i % N_PIECES], tile)
        for ref in (qat_ref, qats_ref):
            ref[pl.ds(POS_ROW0, POS_ROWS), pl.ds(r * TQ, TQ)] = tile.astype(BF16)
    for ref in (qat_ref, qats_ref):
        ref[pl.ds(tail0, AUG - tail0), :] = jnp.zeros((AUG - tail0, NQ), BF16)

    yield False
    t_row = q0 + (lax.broadcasted_iota(jnp.int32, (1, NQ), 1) & (TQ - 1))
    gate_row = lambda br: jnp.concatenate(
        [gt_ref[pl.ds(3 * (grp * nrep + r) + br, 1), :] for r in range(nrep)], axis=1)

    s = jnp.dot(kaug_c[...], qat_ref[...], preferred_element_type=F32)
    cmp_end = CMP_STRIDE * lax.broadcasted_iota(jnp.int32, (N_CMP_PAD, 1), 0) + (L_CMP - 1)
    mask_c = cmp_end <= t_row
    s = jnp.where(mask_c, s, NEG)
    e = jnp.where(mask_c, jnp.exp(s - jnp.max(s, axis=0, keepdims=True)), 0.0)
    d = jnp.sum(e, axis=0, keepdims=True)
    p = e * (1.0 / jnp.where(d > 0, d, 1.0))
    p_sum = sum(p[:, r * TQ:(r + 1) * TQ] for r in range(nrep))
    o_c = jnp.dot(vt_c[...], p.astype(BF16), preferred_element_type=F32)
    out_ref[...] = gate_row(0) * o_c[:HEAD_DIM]
    yield False

    n_win = WINDOW // TK_WIN
    k0 = pl.multiple_of(q0, TK_WIN)
    s_all = jnp.dot(kaug_w[pl.ds(k0, WINDOW + TQ), :], qat_ref[...], preferred_element_type=F32)
    s_w = [s_all[j * TK_WIN:(j + 1) * TK_WIN] for j in range(n_win + 1)]
    not_after = q0 + lax.broadcasted_iota(jnp.int32, (TK_WIN, 1), 0) <= t_row
    s_w[0] = jnp.where(not_after, NEG, s_w[0])
    s_w[-1] = jnp.where(not_after, s_w[-1], NEG)
    m_w = functools.reduce(jnp.maximum, [jnp.max(x, axis=0, keepdims=True) for x in s_w])
    p_w = jnp.concatenate([jnp.exp(x - m_w).astype(BF16) for x in s_w], axis=0)
    acc_w = jnp.dot(vt_w[:, pl.ds(k0, WINDOW + TQ)], p_w, preferred_element_type=F32)
    out_ref[...] += gate_row(2) * (acc_w[:HEAD_DIM] * (1.0 / acc_w[HEAD_DIM:HEAD_DIM + 1]))
    yield False

    n_sb = score_ref.shape[0]
    jb = lax.broadcasted_iota(jnp.int32, (n_sb, TQ), 0)
    t_sel = q0 + lax.broadcasted_iota(jnp.int32, (n_sb, TQ), 1)
    valid = jb * L_SEL <= t_sel

    n_full = q0 // TK_SEL
    blocks_per_tile = TK_SEL // L_SEL

    def write_selection(selected):
        sel_bias = jnp.where(selected, 0.0, -BIG).astype(BF16)
        for r in range(nrep):
            qats_ref[pl.ds(SEL_ROW0, n_sb), pl.ds(r * TQ, TQ)] = sel_bias
        picked = jnp.where(selected, 1.0, 0.0)
        bits = sum(jnp.max(picked[k * blocks_per_tile:(k + 1) * blocks_per_tile], axis=(0, 1),
                           keepdims=True) * float(2 ** k) for k in range(n_sb // blocks_per_tile))
        bits = bits[0, 0].astype(jnp.int32)
        count = jnp.int32(0)
        for k in range(n_sb // blocks_per_tile):
            tiles_ref[count] = k
            count = count + jnp.where((k < n_full) & (((bits >> k) & 1) == 1), 1, 0)
        tiles_ref[count] = n_full
        return count

    def rank_blocks():
        cmp_start = CMP_STRIDE * lax.broadcasted_iota(jnp.int32, (n_sb, N_CMP_PAD), 1)
        sb_start = L_SEL * lax.broadcasted_iota(jnp.int32, (n_sb, N_CMP_PAD), 0)
        overlap_t = ((cmp_start < sb_start + L_SEL) & (cmp_start + L_CMP > sb_start)).astype(F32)
        imp_t = jnp.dot(overlap_t, p_sum, precision=lax.Precision.HIGHEST,
                        preferred_element_type=F32)
        cur = t_sel // L_SEL
        forced = (jb == 0) | (jb == cur) | (jb == cur - 1)
        score = jnp.where(valid, jnp.where(forced, jnp.inf, imp_t), -jnp.inf)
        score_ref[...] = score
        chunks = [score[c * SUBLANES:(c + 1) * SUBLANES] for c in range(n_sb // SUBLANES)]
        ranks = [jnp.zeros((SUBLANES, TQ), F32) for _ in chunks]
        sub_row = lax.broadcasted_iota(jnp.int32, (SUBLANES, TQ), 0)
        for i in range(n_sb):
            row = score_ref[pl.ds(i, 1), :]
            for c, chunk in enumerate(chunks):
                later = jnp.where(row >= chunk, 1.0, 0.0)
                earlier = jnp.where(row > chunk, 1.0, 0.0)
                if i < c * SUBLANES:
                    ranks[c] = ranks[c] + later
                elif i >= (c + 1) * SUBLANES:
                    ranks[c] = ranks[c] + earlier
                else:
                    ranks[c] = ranks[c] + jnp.where(sub_row > i - c * SUBLANES, later, earlier)
        rank = jnp.concatenate(ranks, axis=0)
        return write_selection(valid & (rank < float(N_SELECT)))

    n_list = rank_blocks()
    yield False

    _flash_reset(m_ref, acc_ref)
    key_s = lax.broadcasted_iota(jnp.int32, (TK_SEL, 1), 0)
    tail_mask = n_full * TK_SEL + key_s <= t_row
    sel_scores = lambda i: _scores(kaug_s, tiles_ref[i], TK_SEL, qats_ref)
    sel_update = lambda s_ref, i, mask: _flash_update(s_ref[...], vt_s[tiles_ref[i]], mask, m_ref, acc_ref)
    sa_ref[...] = sel_scores(0)
    yield True

    def sel_pair(j, carry):
        i = 2 * j
        sb_ref[...] = sel_scores(i + 1)
        sel_update(sa_ref, i, None)
        sa_ref[...] = sel_scores(i + 2)
        sel_update(sb_ref, i + 1, None)
        return carry

    lax.fori_loop(0, n_list // 2, sel_pair, 0)

    @pl.when(n_list % 2 == 1)
    def _():
        sb_ref[...] = sel_scores(n_list)
        sel_update(sa_ref, n_list - 1, None)
        sel_update(sb_ref, n_list, tail_mask)

    @pl.when(n_list % 2 == 0)
    def _():
        sel_update(sa_ref, n_list, tail_mask)

    out_ref[...] += gate_row(1) * _flash_result(acc_ref)

    for half in range(nrep // 2):
        slab = jnp.concatenate([out_ref[:, pl.ds(2 * half * TQ, TQ)],
                                out_ref[:, pl.ds((2 * half + 1) * TQ, TQ)]], axis=0)
        o_ref[:, pl.ds(lane0 + half * LANES, LANES)] = slab.T.astype(o_ref.dtype)


def _nsa_attention(slope_pieces, q, cmp, cmp_t, keys, vals_t, glogit, bgate, constk, constc):
    b, s, _ = q.shape
    g, nrep, dh = N_GROUPS, HEADS_PER_GROUP, HEAD_DIM
    n_sb = s // L_SEL
    sq = pl.Squeezed()
    gp = GROUPS_PER_STEP
    slot = lambda c, *shape: pl.BlockSpec((sq, sq, gp) + shape, lambda i, j, t: (c, i, j, 0, 0))
    full = lambda *shape: pl.BlockSpec(shape, lambda i, j, t: (0,) * len(shape))
    per_group = lambda shape, dtype: pltpu.VMEM((gp,) + shape, dtype)
    return pl.pallas_call(
        _nsa_kernel,
        grid=(b, g // gp, s // TQ),
        in_specs=[
            pl.BlockSpec(memory_space=pltpu.SMEM),
            pl.BlockSpec((sq, TQ, gp * nrep * dh), lambda i, j, t: (i, t, j)),
            slot(0, N_CMP_PAD, dh), slot(1, dh, N_CMP_PAD),
            slot(0, s, dh), slot(0, dh, s), slot(1, s, dh), slot(1, dh, s),
            pl.BlockSpec((sq, TQ, LANES), lambda i, j, t: (i, t, 0)),
            full(1, LANES), full(s, AUG), full(N_CMP_PAD, AUG),
        ],
        out_specs=pl.BlockSpec((sq, TQ, gp * nrep * dh), lambda i, j, t: (i, t, j)),
        out_shape=jax.ShapeDtypeStruct((b, s, NSA_Q), BF16),
        scratch_shapes=[
            per_group((s, AUG), BF16), per_group((WINDOW + s, AUG), BF16), per_group((N_CMP_PAD, AUG), BF16),
            per_group((s // TK_SEL, V_ROWS, TK_SEL), BF16), per_group((V_ROWS, WINDOW + s), BF16),
            per_group((V_ROWS, N_CMP_PAD), BF16),
            per_group((AUG, NQ), BF16), per_group((n_sb, TQ), F32), per_group((1, NQ), F32),
            per_group((V_ROWS, NQ), F32), per_group((dh, NQ), F32),
            per_group((TK_SEL, NQ), F32), per_group((TK_SEL, NQ), F32),
            per_group((AUG, NQ), BF16),
            pltpu.SMEM((gp, s // TK_SEL + 1), jnp.int32),
            pltpu.VMEM((LANES, TQ), F32),
        ],
        compiler_params=_cparams("arbitrary", "arbitrary", "arbitrary"),
        name="nsa_attention",
    )(slope_pieces, q, cmp, cmp_t, keys, vals_t, keys, vals_t, glogit, bgate, constk, constc)


def _position_pieces(pos):
    hi = (pos // L_SEL) * L_SEL
    return jnp.stack([hi] * N_PIECES + [pos - hi] * N_PIECES, axis=1).astype(F32)


def _nsa_constants(s):
    pos = jnp.arange(s)
    constk = jnp.zeros((s, AUG), F32)
    constk = constk.at[:, SEL_ROW0:SEL_ROW0 + s // L_SEL].set(jax.nn.one_hot(pos // L_SEL, s // L_SEL))
    constk = constk.at[:, POS_ROW0:POS_ROW0 + 2 * N_PIECES].set(_position_pieces(pos))
    cmp_end = CMP_STRIDE * jnp.arange(N_CMP_PAD) + (L_CMP - 1)
    constc = jnp.zeros((N_CMP_PAD, AUG), F32)
    constc = constc.at[:, POS_ROW0:POS_ROW0 + 2 * N_PIECES].set(_position_pieces(cmp_end))
    slopes = 2.0 ** (-8.0 * jnp.arange(1, N_HEADS + 1, dtype=F32) / N_HEADS)
    pieces, rest = [], slopes
    for _ in range(N_PIECES):
        piece = rest.astype(BF16).astype(F32)
        pieces.append(piece)
        rest = rest - piece
    return constk.astype(BF16), constc.astype(BF16), jnp.stack(pieces, axis=1).reshape(-1)


def _nsa_mixer(x, norm_g, w_in, b_gate, cmp_pos, cmp_w1, cmp_b1, cmp_w2, cmp_b2):
    b, s, d = x.shape
    g, nrep, dh = N_GROUPS, HEADS_PER_GROUP, HEAD_DIM
    w_in_p = jnp.pad(w_in, ((0, 0), (0, NSA_IN_PAD - NSA_IN))).astype(BF16)
    v_cols = lambda i: w_in_p[:, NSA_Q + i * NSA_KV:NSA_Q + (i + 1) * NSA_KV].T
    q, raw, keys, vals_t, glogit = _nsa_in_proj(x, norm_g, w_in_p, jnp.stack([v_cols(3), v_cols(5)]),
                                                tm=512)
    raw = raw.reshape(2, b, g, s // CMP_STRIDE, CMP_STRIDE * dh)
    cmp, cmp_t = _compress(raw, cmp_pos.reshape(2, 1, L_CMP * dh), cmp_w1.astype(BF16), cmp_b1,
                           cmp_w2.astype(BF16), cmp_b2)
    constk, constc, slope_pieces = _nsa_constants(s)
    bgate = jnp.pad(b_gate, (0, LANES - 3 * N_HEADS)).reshape(1, LANES)
    return _nsa_attention(slope_pieces, q, cmp, cmp_t, keys, vals_t, glogit, bgate, constk, constc)


TT = 256
SUB = SUBLANES


GATE_BAND = 256


def _gate_bands():
    block = D_RNN // LRU_BLOCKS
    bands = []
    for c0 in range(0, D_RNN, GATE_BAND):
        width = min(GATE_BAND, D_RNN - c0)
        k_lo = (c0 // block) * block // LANES * LANES
        k_hi = -(-(((c0 + width - 1) // block + 1) * block) // LANES) * LANES
        bands.append((k_lo, min(k_hi, D_RNN), c0, width))
    return tuple(bands)


GATE_BANDS = _gate_bands()


def _pack_gate_weights(w_a, w_x):
    dense = lambda w: jax.scipy.linalg.block_diag(*[w[i] for i in range(LRU_BLOCKS)])
    slabs = []
    for k_lo, k_hi, c0, width in GATE_BANDS:
        band = [jnp.pad(dense(w)[k_lo:k_hi, c0:c0 + width], ((0, 0), (0, GATE_BAND - width)))
                for w in (w_a, w_x)]
        slabs.append(jnp.concatenate(band, axis=1))
    return jnp.concatenate(slabs, axis=0).astype(BF16)


SEQS_PER_STEP = 2


def _lru_kernel(x_all, g_ref, win_ref, cw_ref, cb_ref, wax_ref, bax_ref, lam_ref, wout_ref, o_all,
                *scratch):
    seqs = [tuple(ref.at[h] for ref in (x_all, o_all) + scratch) for h in range(SEQS_PER_STEP)]

    @pl.when(pl.program_id(1) == 0)
    def _():
        for x_ref, o_ref, ext_ref, a_ref, u_ref, h_ref, gate_ref in seqs:
            ext_ref[pl.ds(TT, SUB), :] = jnp.zeros((SUB, D_RNN), F32)
            h_ref[...] = jnp.zeros_like(h_ref)

    tiles = [_lru_tile(g_ref, win_ref, cw_ref, cb_ref, wax_ref, bax_ref, lam_ref, wout_ref, *refs)
             for refs in seqs]
    while tiles:
        tiles = [tile for tile in tiles if next(tile, True) is not True]


def _lru_tile(g_ref, win_ref, cw_ref, cb_ref, wax_ref, bax_ref, lam_ref, wout_ref,
              x_ref, o_ref, ext_ref, a_ref, u_ref, h_ref, gate_ref):
    xn = _rms(x_ref[...], g_ref[...]).astype(BF16)
    gate_ref[...] = _gelu_tanh(jnp.dot(xn, win_ref[:, :D_RNN], preferred_element_type=F32))
    ext_ref[pl.ds(0, SUB), :] = ext_ref[pl.ds(TT, SUB), :]
    ext_ref[pl.ds(SUB, TT), :] = jnp.dot(xn, win_ref[:, D_RNN:], preferred_element_type=F32)
    yield
    cw = cw_ref[...]
    xr = cb_ref[...] + sum(
        cw[w:w + 1, :] * ext_ref[pl.ds(SUB - (CONV_W - 1) + w, TT), :] for w in range(CONV_W))

    lam = lam_ref[...]
    softplus_neg = jnp.maximum(-lam, 0.0) + jnp.log1p(jnp.exp(-jnp.abs(lam)))
    half_log_scale = (-0.5 * LRU_C) * softplus_neg
    xr_b = xr.astype(BF16)
    half_bias = bax_ref[...]
    row0 = 0
    for k_lo, k_hi, c0, width in GATE_BANDS:
        z = jnp.dot(xr_b[:, k_lo:k_hi], wax_ref[row0:row0 + k_hi - k_lo, :], preferred_element_type=F32)
        row0 += k_hi - k_lo
        tanh_r = jnp.tanh(z[:, :width] + half_bias[:, c0:c0 + width])
        tanh_i = jnp.tanh(z[:, GATE_BAND:GATE_BAND + width] + half_bias[:, D_RNN + c0:D_RNN + c0 + width])
        scale = half_log_scale[:, c0:c0 + width]
        a = jnp.exp(scale * tanh_r + scale)
        a_ref[:, c0:c0 + width] = a
        half_xr = 0.5 * xr[:, c0:c0 + width]
        u_ref[:, c0:c0 + width] = jnp.sqrt(jnp.maximum(1.0 - a * a, 0.0)) * (half_xr * tanh_i + half_xr)
    yield

    row = lax.broadcasted_iota(jnp.int32, (SUB, D_RNN), 0)

    def scan_rows(c, h_prev):
        r0 = c * SUB
        a_c = a_ref[pl.ds(r0, SUB), :]
        u_c = u_ref[pl.ds(r0, SUB), :]
        shift = 1
        while shift < SUB:
            keep = row >= shift
            u_c = u_c + a_c * jnp.where(keep, pltpu.roll(u_c, shift, axis=0), 0.0)
            a_c = a_c * jnp.where(keep, pltpu.roll(a_c, shift, axis=0), 1.0)
            shift *= 2
        h_c = u_c + a_c * h_prev
        u_ref[pl.ds(r0, SUB), :] = h_c
        return jnp.broadcast_to(h_c[SUB - 1:SUB, :], (SUB, D_RNN))

    h_state = h_ref[...]
    for c in range(TT // SUB):
        h_state = scan_rows(c, h_state)
        if c % (TT // SUB // 4) == TT // SUB // 4 - 1:
            yield
    h_ref[...] = h_state
    gated = (u_ref[...] * gate_ref[...]).astype(BF16)
    o_ref[...] = x_ref[...] + jnp.dot(gated, wout_ref[...], preferred_element_type=F32)


def _lru_mixer(x, norm_g, w_in, conv_w, conv_b, w_a, b_a, w_x, b_x, lam, w_out):
    b, s, d = x.shape
    wax = _pack_gate_weights(0.5 * w_a, 0.5 * w_x)
    bax = 0.5 * jnp.concatenate([b_a, b_x])
    nseq = SEQS_PER_STEP
    resident = lambda *shape: pl.BlockSpec(shape, lambda i, t: (0,) * len(shape),
                                           pipeline_mode=pl.Buffered(1))
    per_seq = lambda rows: pltpu.VMEM((nseq, rows, D_RNN), F32)
    return pl.pallas_call(
        _lru_kernel,
        grid=(b // nseq, s // TT),
        in_specs=[
            pl.BlockSpec((nseq, TT, d), lambda i, t: (i, t, 0)),
            resident(1, d), resident(d, 2 * D_RNN), resident(CONV_W, D_RNN), resident(1, D_RNN),
            resident(*wax.shape), resident(1, 2 * D_RNN), resident(1, D_RNN), resident(D_RNN, d),
        ],
        out_specs=pl.BlockSpec((nseq, TT, d), lambda i, t: (i, t, 0)),
        out_shape=jax.ShapeDtypeStruct((b, s, d), F32),
        scratch_shapes=[per_seq(TT + SUB), per_seq(TT), per_seq(TT), per_seq(SUB), per_seq(TT)],
        compiler_params=_cparams("parallel", "arbitrary"),
        name="lru_block",
    )(x, norm_g.reshape(1, d), w_in.astype(BF16), conv_w, conv_b.reshape(1, -1), wax,
      bax.reshape(1, -1), lam.reshape(1, -1), w_out.astype(BF16))


def kernel(x, norm_mix, norm_ffn, norm_final, nsa_w_in, nsa_b_gate, nsa_cmp_pos, nsa_cmp_w1,
           nsa_cmp_b1, nsa_cmp_w2, nsa_cmp_b2, nsa_w_out, lru_w_in, lru_conv_w, lru_conv_b,
           lru_w_a, lru_b_a, lru_w_x, lru_b_x, lru_lambda, lru_w_out, ffn_w_in, ffn_w_out):
    b, s, d = x.shape
    ffn = functools.partial(_ffn, tm=512, tf=256)
    attn = _nsa_mixer(x, norm_mix[0], nsa_w_in[0], nsa_b_gate[0], nsa_cmp_pos[0], nsa_cmp_w1[0],
                      nsa_cmp_b1[0], nsa_cmp_w2[0], nsa_cmp_b2[0])
    x = ffn(x.reshape(b * s, d), norm_ffn[0], ffn_w_in[0].astype(BF16), ffn_w_out[0].astype(BF16),
            norm_final, final_norm=False, name="ffn0", mix=attn.reshape(b * s, NSA_Q),
            w_mix=nsa_w_out[0].astype(BF16)).reshape(b, s, d)
    x = _lru_mixer(x, norm_mix[1], lru_w_in[0], lru_conv_w[0], lru_conv_b[0], lru_w_a[0], lru_b_a[0],
                   lru_w_x[0], lru_b_x[0], lru_lambda[0], lru_w_out[0])
    x = ffn(x.reshape(b * s, d), norm_ffn[1], ffn_w_in[1].astype(BF16), ffn_w_out[1].astype(BF16),
            norm_final, final_norm=True, name="ffn1").reshape(b, s, d)
    return x
```

```python
import functools
import itertools

import jax
import jax.numpy as jnp
from jax import lax
from jax.experimental import pallas as pl
from jax.experimental.pallas import tpu as pltpu

F32 = jnp.float32
BF16 = jnp.bfloat16

D_MODEL = 1024
N_HEADS = 16
N_GROUPS = 4
HEADS_PER_GROUP = N_HEADS // N_GROUPS
HEAD_DIM = 64
L_CMP = 32
CMP_STRIDE = 16
L_SEL = 64
N_SELECT = 16
WINDOW = 512
NSA_Q = N_HEADS * HEAD_DIM
NSA_KV = N_GROUPS * HEAD_DIM
NSA_IN = NSA_Q + 6 * NSA_KV + 3 * N_HEADS
D_RNN = 1408
LRU_BLOCKS = 8
LRU_C = 8.0
CONV_W = 4
D_FF = 2816
EPS = 1e-6

LANES = 128
SUBLANES = 8
NEG = -1e30
VMEM_LIMIT = 56 * 1024 * 1024

TQ = 256
TK_SEL = 512
TK_WIN = 256
N_CMP_PAD = 256


def _cparams(*sem):
    return pltpu.CompilerParams(dimension_semantics=sem, vmem_limit_bytes=VMEM_LIMIT)


def _rms(x, g):
    ms = jnp.mean(x * x, axis=-1, keepdims=True)
    return x * lax.rsqrt(ms + EPS) * g


def _gelu_tanh(x):
    c = (2.0 / jnp.pi) ** 0.5
    half_x = 0.5 * x
    return half_x * jnp.tanh(x * (c + (c * 0.044715) * (x * x))) + half_x


def _nt_dot(a, b):
    return lax.dot_general(a, b, (((1,), (1,)), ((), ())), preferred_element_type=F32)


def _ffn_kernel(*refs, final_norm, tf, has_mix):
    x_ref, refs = refs[0], refs[1:]
    x = x_ref[...]
    if has_mix:
        (mix_ref, wmix_ref), refs = refs[:2], refs[2:]
        x = x + jnp.dot(mix_ref[...], wmix_ref[...], preferred_element_type=F32)
    g_ref, win_ref, wout_ref, gf_ref, o_ref, hid_ref = refs
    xn = _rms(x, g_ref[...]).astype(BF16)
    for c0 in range(0, D_FF, tf):
        gate = jnp.dot(xn, win_ref[:, c0:c0 + tf], preferred_element_type=F32)
        up = jnp.dot(xn, win_ref[:, D_FF + c0:D_FF + c0 + tf], preferred_element_type=F32)
        hid_ref[:, c0:c0 + tf] = ((0.5 * gate) * (1.0 + jnp.tanh(0.5 * gate)) * up).astype(BF16)
    y = x + jnp.dot(hid_ref[...], wout_ref[...], preferred_element_type=F32)
    if final_norm:
        y = _rms(y, gf_ref[...])
    o_ref[...] = y


def _ffn(x, g, w_in, w_out, g_final, *, final_norm, tm, tf, name, mix=None, w_mix=None):
    m, d = x.shape
    resident = lambda *shape: pl.BlockSpec(shape, lambda i: (0,) * len(shape),
                                           pipeline_mode=pl.Buffered(1))
    rows = lambda width: pl.BlockSpec((tm, width), lambda i: (i, 0))
    if mix is None:
        mix_specs, mix_args = [], ()
    else:
        mix_specs, mix_args = [rows(mix.shape[1]), resident(*w_mix.shape)], (mix, w_mix)
    return pl.pallas_call(
        functools.partial(_ffn_kernel, final_norm=final_norm, tf=tf, has_mix=mix is not None),
        grid=(m // tm,),
        in_specs=[rows(d)] + mix_specs + [
            resident(1, d), resident(d, 2 * D_FF), resident(D_FF, d), resident(1, d)],
        out_specs=rows(d),
        out_shape=jax.ShapeDtypeStruct((m, d), F32),
        scratch_shapes=[pltpu.VMEM((tm, D_FF), BF16)],
        compiler_params=_cparams("parallel"),
        name=name,
    )(x, *mix_args, g.reshape(1, d), w_in, w_out, g_final.reshape(1, d))


NSA_IN_PAD = -(-NSA_IN // LANES) * LANES


def _nsa_in_proj_kernel(x_ref, g_ref, w_ref, wvt_ref, q_ref, raw_ref, k_ref, vt_ref, gl_ref):
    xn = _rms(x_ref[...], g_ref[...]).astype(BF16)
    q_ref[...] = jnp.dot(xn, w_ref[:, :NSA_Q], preferred_element_type=F32).astype(q_ref.dtype)
    for i, (dst, slot) in {0: (raw_ref, 0), 1: (raw_ref, 1), 2: (k_ref, 0), 4: (k_ref, 1)}.items():
        col0 = NSA_Q + i * NSA_KV
        part = jnp.dot(xn, w_ref[:, col0:col0 + NSA_KV], preferred_element_type=F32)
        for grp in range(N_GROUPS):
            dst[slot, grp] = part[:, grp * HEAD_DIM:(grp + 1) * HEAD_DIM].astype(dst.dtype)
    for slot in range(2):
        part_t = _nt_dot(wvt_ref[slot], xn)
        for grp in range(N_GROUPS):
            vt_ref[slot, grp] = part_t[grp * HEAD_DIM:(grp + 1) * HEAD_DIM].astype(vt_ref.dtype)
    gl_ref[...] = jnp.dot(xn, w_ref[:, NSA_Q + 6 * NSA_KV:], preferred_element_type=F32)


def _nsa_in_proj(x, g, w, wvt, *, tm):
    b, s, d = x.shape
    sq = pl.Squeezed()
    grouped = lambda dtype: jax.ShapeDtypeStruct((2, b, N_GROUPS, s, HEAD_DIM), dtype)
    grouped_spec = pl.BlockSpec((2, sq, N_GROUPS, tm, HEAD_DIM), lambda i, t: (0, i, 0, t, 0))
    return pl.pallas_call(
        _nsa_in_proj_kernel,
        grid=(b, s // tm),
        in_specs=[
            pl.BlockSpec((sq, tm, d), lambda i, t: (i, t, 0)),
            pl.BlockSpec((1, d), lambda i, t: (0, 0)),
            pl.BlockSpec((d, NSA_IN_PAD), lambda i, t: (0, 0)),
            pl.BlockSpec((2, NSA_KV, d), lambda i, t: (0, 0, 0)),
        ],
        out_specs=[
            pl.BlockSpec((sq, tm, NSA_Q), lambda i, t: (i, t, 0)),
            grouped_spec, grouped_spec,
            pl.BlockSpec((2, sq, N_GROUPS, HEAD_DIM, tm), lambda i, t: (0, i, 0, 0, t)),
            pl.BlockSpec((sq, tm, LANES), lambda i, t: (i, t, 0)),
        ],
        out_shape=[
            jax.ShapeDtypeStruct((b, s, NSA_Q), BF16),
            grouped(BF16), grouped(BF16),
            jax.ShapeDtypeStruct((2, b, N_GROUPS, HEAD_DIM, s), BF16),
            jax.ShapeDtypeStruct((b, s, LANES), F32),
        ],
        compiler_params=_cparams("parallel", "parallel"),
        name="nsa_in_proj",
    )(x, g.reshape(1, d), w, wvt)


def _compress_kernel(r_ref, pos_ref, w1_ref, b1_ref, w2_ref, b2_ref, w2t_ref, b2t_ref, o_ref, ot_ref,
                     bias_ref):
    half = CMP_STRIDE * HEAD_DIM
    rows = r_ref[...]
    part_lo = jnp.dot(rows, w1_ref[:half, :], preferred_element_type=F32)
    part_hi = jnp.dot(rows, w1_ref[half:, :], preferred_element_type=F32)
    @pl.when((pl.program_id(1) == 0) & (pl.program_id(2) == 0))
    def _():
        bias_ref[...] = b1_ref[...] + jnp.dot(
            pos_ref[...], w1_ref[...].astype(F32), precision=lax.Precision.HIGHEST,
            preferred_element_type=F32)

    hid = part_lo + pltpu.roll(part_hi, N_CMP_PAD - 1, axis=0) + bias_ref[...]
    hid = jax.nn.gelu(hid, approximate=True).astype(BF16)
    o_ref[...] = (jnp.dot(hid, w2_ref[...], preferred_element_type=F32) + b2_ref[...]).astype(o_ref.dtype)
    ot_ref[...] = (_nt_dot(w2t_ref[...], hid) + b2t_ref[...]).astype(ot_ref.dtype)


def _compress(raw, pos, w1, b1, w2, b2):
    two, b, g, n, width = raw.shape
    hidden = w1.shape[-1]
    sq = pl.Squeezed()
    per_branch = lambda *shape: pl.BlockSpec((sq,) + shape, lambda c, i, j: (c,) + (0,) * len(shape))
    return pl.pallas_call(
        _compress_kernel,
        grid=(two, b, g),
        in_specs=[
            pl.BlockSpec((sq, sq, sq, n, width), lambda c, i, j: (c, i, j, 0, 0)),
            per_branch(1, 2 * width), per_branch(2 * width, hidden), per_branch(1, hidden),
            per_branch(hidden, HEAD_DIM), per_branch(1, HEAD_DIM),
            per_branch(HEAD_DIM, hidden), per_branch(HEAD_DIM, 1),
        ],
        out_specs=[
            pl.BlockSpec((sq, sq, sq, n, HEAD_DIM), lambda c, i, j: (c, i, j, 0, 0)),
            pl.BlockSpec((sq, sq, sq, HEAD_DIM, n), lambda c, i, j: (c, i, j, 0, 0)),
        ],
        out_shape=[
            jax.ShapeDtypeStruct((two, b, g, n, HEAD_DIM), BF16),
            jax.ShapeDtypeStruct((two, b, g, HEAD_DIM, n), BF16),
        ],
        scratch_shapes=[pltpu.VMEM((1, hidden), F32)],
        compiler_params=_cparams("arbitrary", "arbitrary", "arbitrary"),
        name="nsa_compress",
    )(raw, pos, w1, b1.reshape(two, 1, hidden), w2, b2.reshape(two, 1, HEAD_DIM),
      w2.transpose(0, 2, 1), b2.reshape(two, HEAD_DIM, 1))


AUG = 256
SEL_ROW0 = HEAD_DIM
POS_ROW0 = 2 * HEAD_DIM
N_PIECES = 3
PAD_LANE = POS_ROW0 + 2 * N_PIECES
POS_ROWS = 16
V_ROWS = 80
NQ = HEADS_PER_GROUP * TQ
BIG = 1e30
assert TQ == TK_WIN and TK_SEL % TQ == 0 and WINDOW % TK_WIN == 0


def _flash_reset(m_ref, acc_ref):
    m_ref[...] = jnp.full_like(m_ref, NEG)
    acc_ref[...] = jnp.zeros_like(acc_ref)


def _scores(kaug, tile, tk, qat_ref):
    k0 = pl.multiple_of(tile * tk, tk)
    return jnp.dot(kaug[pl.ds(k0, tk), :], qat_ref[...], preferred_element_type=F32)


def _flash_update_steps(s, vt_tile, mask, m_ref, acc_ref):
    if mask is not None:
        s = jnp.where(mask, s, NEG)
    m_old = m_ref[...]
    m_new = jnp.maximum(m_old, jnp.max(s, axis=0, keepdims=True))
    yield
    p = jnp.exp(s - m_new).astype(BF16)
    yield
    acc_ref[...] = jnp.exp(m_old - m_new) * acc_ref[...] + jnp.dot(
        vt_tile, p, preferred_element_type=F32)
    m_ref[...] = m_new


def _flash_update(*args):
    for _ in _flash_update_steps(*args):
        pass


def _round_robin(generators):
    generators = list(generators)
    while generators:
        generators = [g for g in generators if next(g, True) is not True]


def _flash_result(acc_ref):
    return acc_ref[pl.ds(0, HEAD_DIM), :] * (1.0 / acc_ref[pl.ds(HEAD_DIM, 1), :])


GROUPS_PER_STEP = 2


def _nsa_kernel(sl_ref, q_ref, kc_all, vct_all, ks_all, vst_all, kw_all, vwt_all, gl_ref, bg_ref,
                constk_ref, constc_ref, o_ref, *scratch):
    qi = pl.program_id(2)
    gt_ref = scratch[-1]
    per_group = (kc_all, vct_all, ks_all, vst_all, kw_all, vwt_all) + scratch[:-1]
    groups = [tuple(ref.at[h] for ref in per_group) for h in range(GROUPS_PER_STEP)]

    @pl.when(qi == 0)
    def _():
        for refs in groups:
            _nsa_assemble(constk_ref, constc_ref, *refs)

    gt_ref[...] = jax.nn.sigmoid(gl_ref[...] + bg_ref[...]).T
    tiles = [_nsa_tile(pl.program_id(1) * GROUPS_PER_STEP + h, h * HEADS_PER_GROUP * HEAD_DIM, qi,
                       sl_ref, q_ref, o_ref, gt_ref, *refs) for h, refs in enumerate(groups)]
    heads = list(tiles)
    while heads:
        heads = [tile for tile in heads if not next(tile)]
    ends = [next(tile) for tile in tiles]
    for parities in itertools.product((0, 1), repeat=len(tiles)):
        @pl.when(functools.reduce(jnp.logical_and, [odd == want for (odd, _), want in zip(ends, parities)]))
        def _():
            _round_robin(last(want) for (_, last), want in zip(ends, parities))
    for tile in tiles:
        next(tile, None)


def _nsa_assemble(constk_ref, constc_ref, kc_ref, vct_ref, ks_ref, vst_ref, kw_ref, vwt_ref,
                  kaug_s, kaug_w, kaug_c, vt_s, vt_w, vt_c, *unused):
    s_len = ks_ref.shape[0]
    for kaug, row0, const, k in ((kaug_s, 0, constk_ref, ks_ref), (kaug_w, WINDOW, constk_ref, kw_ref),
                                 (kaug_c, 0, constc_ref, kc_ref)):
        kaug[pl.ds(row0, k.shape[0]), :] = const[...]
        kaug[pl.ds(row0, k.shape[0]), pl.ds(0, HEAD_DIM)] = k[...]
    pad_lane = lax.broadcasted_iota(jnp.int32, (WINDOW, AUG), 1) == PAD_LANE
    kaug_w[pl.ds(0, WINDOW), :] = jnp.where(pad_lane, 1.0, 0.0).astype(BF16)
    vt_w[:, pl.ds(0, WINDOW)] = jnp.zeros((V_ROWS, WINDOW), BF16)
    pad = V_ROWS - HEAD_DIM
    ones_rows = lambda n: jnp.where(
        lax.broadcasted_iota(jnp.int32, (pad, n), 0) == 0, 1.0, 0.0).astype(BF16)
    for c in range(vt_s.shape[0]):
        vt_s[c, pl.ds(0, HEAD_DIM), :] = vst_ref[:, pl.ds(c * TK_SEL, TK_SEL)]
        vt_s[c, pl.ds(HEAD_DIM, pad), :] = ones_rows(TK_SEL)
    vt_w[pl.ds(0, HEAD_DIM), pl.ds(WINDOW, s_len)] = vwt_ref[...]
    vt_w[pl.ds(HEAD_DIM, pad), pl.ds(WINDOW, s_len)] = ones_rows(s_len)
    vt_c[pl.ds(0, HEAD_DIM), :] = vct_ref[...]
    vt_c[pl.ds(HEAD_DIM, pad), :] = ones_rows(N_CMP_PAD)


def _nsa_tile(grp, lane0, qi, sl_ref, q_ref, o_ref, gt_ref, kc_ref, vct_ref, ks_ref, vst_ref, kw_ref,
              vwt_ref, kaug_s, kaug_w, kaug_c, vt_s, vt_w, vt_c, qat_ref, score_ref, m_ref, acc_ref,
              out_ref, sa_ref, sb_ref, qats_ref, tiles_ref):
    q0 = qi * TQ
    nrep = HEADS_PER_GROUP

    for half in range(nrep // 2):
        qt = q_ref[:, pl.ds(lane0 + half * LANES, LANES)].astype(F32).T * (HEAD_DIM ** -0.5)
        for sub in range(2):
            q_rows = qt[sub * HEAD_DIM:(sub + 1) * HEAD_DIM].astype(BF16)
            for ref in (qat_ref, qats_ref):
                ref[pl.ds(0, HEAD_DIM), pl.ds((2 * half + sub) * TQ, TQ)] = q_rows
    qat_ref[pl.ds(SEL_ROW0, HEAD_DIM), :] = jnp.zeros((HEAD_DIM, NQ), BF16)
    piece = lax.broadcasted_iota(jnp.int32, (POS_ROWS, TQ), 0)
    tail0 = POS_ROW0 + POS_ROWS
    for r in range(nrep):
        tile = jnp.where(piece == PAD_LANE - POS_ROW0, -BIG, 0.0)
        for i in range(2 * N_PIECES):
            tile = jnp.where(piece == i, sl_ref[(grp * nrep + r) * N_PIECES + i % N_PIECES], tile)
        for ref in (qat_ref, qats_ref):
            ref[pl.ds(POS_ROW0, POS_ROWS), pl.ds(r * TQ, TQ)] = tile.astype(BF16)
    for ref in (qat_ref, qats_ref):
        ref[pl.ds(tail0, AUG - tail0), :] = jnp.zeros((AUG - tail0, NQ), BF16)

    yield False
    t_row = q0 + (lax.broadcasted_iota(jnp.int32, (1, NQ), 1) & (TQ - 1))
    gate_row = lambda br: jnp.concatenate(
        [gt_ref[pl.ds(3 * (grp * nrep + r) + br, 1), :] for r in range(nrep)], axis=1)

    s = jnp.dot(kaug_c[...], qat_ref[...], preferred_element_type=F32)
    cmp_end = CMP_STRIDE * lax.broadcasted_iota(jnp.int32, (N_CMP_PAD, 1), 0) + (L_CMP - 1)
    mask_c = cmp_end <= t_row
    s = jnp.where(mask_c, s, NEG)
    e = jnp.where(mask_c, jnp.exp(s - jnp.max(s, axis=0, keepdims=True)), 0.0)
    d = jnp.sum(e, axis=0, keepdims=True)
    p = e * (1.0 / jnp.where(d > 0, d, 1.0))
    p_sum = sum(p[:, r * TQ:(r + 1) * TQ] for r in range(nrep))
    o_c = jnp.dot(vt_c[...], p.astype(BF16), preferred_element_type=F32)
    out_ref[...] = gate_row(0) * o_c[:HEAD_DIM]
    yield False

    n_win = WINDOW // TK_WIN
    k0 = pl.multiple_of(q0, TK_WIN)
    s_all = jnp.dot(kaug_w[pl.ds(k0, WINDOW + TQ), :], qat_ref[...], preferred_element_type=F32)
    s_w = [s_all[j * TK_WIN:(j + 1) * TK_WIN] for j in range(n_win + 1)]
    not_after = q0 + lax.broadcasted_iota(jnp.int32, (TK_WIN, 1), 0) <= t_row
    s_w[0] = jnp.where(not_after, NEG, s_w[0])
    s_w[-1] = jnp.where(not_after, s_w[-1], NEG)
    m_w = functools.reduce(jnp.maximum, [jnp.max(x, axis=0, keepdims=True) for x in s_w])
    p_w = jnp.concatenate([jnp.exp(x - m_w).astype(BF16) for x in s_w], axis=0)
    acc_w = jnp.dot(vt_w[:, pl.ds(k0, WINDOW + TQ)], p_w, preferred_element_type=F32)
    out_ref[...] += gate_row(2) * (acc_w[:HEAD_DIM] * (1.0 / acc_w[HEAD_DIM:HEAD_DIM + 1]))
    yield False

    n_sb = score_ref.shape[0]
    jb = lax.broadcasted_iota(jnp.int32, (n_sb, TQ), 0)
    t_sel = q0 + lax.broadcasted_iota(jnp.int32, (n_sb, TQ), 1)
    valid = jb * L_SEL <= t_sel

    n_full = q0 // TK_SEL
    blocks_per_tile = TK_SEL // L_SEL

    def write_selection(selected):
        sel_bias = jnp.where(selected, 0.0, -BIG).astype(BF16)
        for r in range(nrep):
            qats_ref[pl.ds(SEL_ROW0, n_sb), pl.ds(r * TQ, TQ)] = sel_bias
        picked = jnp.where(selected, 1.0, 0.0)
        bits = sum(jnp.max(picked[k * blocks_per_tile:(k + 1) * blocks_per_tile], axis=(0, 1),
                           keepdims=True) * float(2 ** k) for k in range(n_sb // blocks_per_tile))
        bits = bits[0, 0].astype(jnp.int32)
        count = jnp.int32(0)
        for k in range(n_sb // blocks_per_tile):
            tiles_ref[count] = k
            count = count + jnp.where((k < n_full) & (((bits >> k) & 1) == 1), 1, 0)
        tiles_ref[count] = n_full
        return count

    def rank_blocks():
        cmp_start = CMP_STRIDE * lax.broadcasted_iota(jnp.int32, (n_sb, N_CMP_PAD), 1)
        sb_start = L_SEL * lax.broadcasted_iota(jnp.int32, (n_sb, N_CMP_PAD), 0)
        overlap_t = ((cmp_start < sb_start + L_SEL) & (cmp_start + L_CMP > sb_start)).astype(F32)
        imp_t = jnp.dot(overlap_t, p_sum, precision=lax.Precision.HIGHEST,
                        preferred_element_type=F32)
        cur = t_sel // L_SEL
        forced = (jb == 0) | (jb == cur) | (jb == cur - 1)
        score = jnp.where(valid, jnp.where(forced, jnp.inf, imp_t), -jnp.inf)
        score_ref[...] = score
        chunks = [score[c * SUBLANES:(c + 1) * SUBLANES] for c in range(n_sb // SUBLANES)]
        ranks = [jnp.zeros((SUBLANES, TQ), F32) for _ in chunks]
        sub_row = lax.broadcasted_iota(jnp.int32, (SUBLANES, TQ), 0)
        for i in range(n_sb):
            row = score_ref[pl.ds(i, 1), :]
            for c, chunk in enumerate(chunks):
                later = jnp.where(row >= chunk, 1.0, 0.0)
                earlier = jnp.where(row > chunk, 1.0, 0.0)
                if i < c * SUBLANES:
                    ranks[c] = ranks[c] + later
                elif i >= (c + 1) * SUBLANES:
                    ranks[c] = ranks[c] + earlier
                else:
                    ranks[c] = ranks[c] + jnp.where(sub_row > i - c * SUBLANES, later, earlier)
        rank = jnp.concatenate(ranks, axis=0)
        return write_selection(valid & (rank < float(N_SELECT)))

    n_list = rank_blocks()
    yield False

    _flash_reset(m_ref, acc_ref)
    sel_scores = lambda i: _scores(kaug_s, tiles_ref[i], TK_SEL, qats_ref)
    sel_update = lambda s_ref, i, mask: _flash_update(s_ref[...], vt_s[tiles_ref[i]], mask, m_ref, acc_ref)
    sa_ref[...] = sel_scores(0)
    yield True

    def sel_pair(j, carry):
        i = 2 * j
        sb_ref[...] = sel_scores(i + 1)
        sel_update(sa_ref, i, None)
        sa_ref[...] = sel_scores(i + 2)
        sel_update(sb_ref, i + 1, None)
        return carry

    lax.fori_loop(0, n_list // 2, sel_pair, 0)

    def last_tiles(odd):
        own = n_list
        own_rows = pl.ds(pl.multiple_of(q0 - n_full * TK_SEL, TQ), TQ)
        if odd:
            sb_ref[...] = sel_scores(own)
            yield
            yield from _flash_update_steps(sa_ref[...], vt_s[tiles_ref[own - 1]], None, m_ref, acc_ref)
            yield
        own_ref = sb_ref if odd else sa_ref
        own_ref[own_rows, :] = jnp.where(not_after, own_ref[own_rows, :], NEG)
        yield from _flash_update_steps(own_ref[...], vt_s[tiles_ref[own]], None, m_ref, acc_ref)

    yield n_list % 2, last_tiles

    out_ref[...] += gate_row(1) * _flash_result(acc_ref)

    for half in range(nrep // 2):
        slab = jnp.concatenate([out_ref[:, pl.ds(2 * half * TQ, TQ)],
                                out_ref[:, pl.ds((2 * half + 1) * TQ, TQ)]], axis=0)
        o_ref[:, pl.ds(lane0 + half * LANES, LANES)] = slab.T.astype(o_ref.dtype)


def _nsa_attention(slope_pieces, q, cmp, cmp_t, keys, vals_t, glogit, bgate, constk, constc):
    b, s, _ = q.shape
    g, nrep, dh = N_GROUPS, HEADS_PER_GROUP, HEAD_DIM
    n_sb = s // L_SEL
    sq = pl.Squeezed()
    gp = GROUPS_PER_STEP
    slot = lambda c, *shape: pl.BlockSpec((sq, sq, gp) + shape, lambda i, j, t: (c, i, j, 0, 0))
    full = lambda *shape: pl.BlockSpec(shape, lambda i, j, t: (0,) * len(shape))
    per_group = lambda shape, dtype: pltpu.VMEM((gp,) + shape, dtype)
    return pl.pallas_call(
        _nsa_kernel,
        grid=(b, g // gp, s // TQ),
        in_specs=[
            pl.BlockSpec(memory_space=pltpu.SMEM),
            pl.BlockSpec((sq, TQ, gp * nrep * dh), lambda i, j, t: (i, t, j)),
            slot(0, N_CMP_PAD, dh), slot(1, dh, N_CMP_PAD),
            slot(0, s, dh), slot(0, dh, s), slot(1, s, dh), slot(1, dh, s),
            pl.BlockSpec((sq, TQ, LANES), lambda i, j, t: (i, t, 0)),
            full(1, LANES), full(s, AUG), full(N_CMP_PAD, AUG),
        ],
        out_specs=pl.BlockSpec((sq, TQ, gp * nrep * dh), lambda i, j, t: (i, t, j)),
        out_shape=jax.ShapeDtypeStruct((b, s, NSA_Q), BF16),
        scratch_shapes=[
            per_group((s, AUG), BF16), per_group((WINDOW + s, AUG), BF16), per_group((N_CMP_PAD, AUG), BF16),
            per_group((s // TK_SEL, V_ROWS, TK_SEL), BF16), per_group((V_ROWS, WINDOW + s), BF16),
            per_group((V_ROWS, N_CMP_PAD), BF16),
            per_group((AUG, NQ), BF16), per_group((n_sb, TQ), F32), per_group((1, NQ), F32),
            per_group((V_ROWS, NQ), F32), per_group((dh, NQ), F32),
            per_group((TK_SEL, NQ), F32), per_group((TK_SEL, NQ), F32),
            per_group((AUG, NQ), BF16),
            pltpu.SMEM((gp, s // TK_SEL + 1), jnp.int32),
            pltpu.VMEM((LANES, TQ), F32),
        ],
        compiler_params=_cparams("arbitrary", "arbitrary", "arbitrary"),
        name="nsa_attention",
    )(slope_pieces, q, cmp, cmp_t, keys, vals_t, keys, vals_t, glogit, bgate, constk, constc)


def _position_pieces(pos):
    hi = (pos // L_SEL) * L_SEL
    return jnp.stack([hi] * N_PIECES + [pos - hi] * N_PIECES, axis=1).astype(F32)


def _nsa_constants(s):
    pos = jnp.arange(s)
    constk = jnp.zeros((s, AUG), F32)
    constk = constk.at[:, SEL_ROW0:SEL_ROW0 + s // L_SEL].set(jax.nn.one_hot(pos // L_SEL, s // L_SEL))
    constk = constk.at[:, POS_ROW0:POS_ROW0 + 2 * N_PIECES].set(_position_pieces(pos))
    cmp_end = CMP_STRIDE * jnp.arange(N_CMP_PAD) + (L_CMP - 1)
    constc = jnp.zeros((N_CMP_PAD, AUG), F32)
    constc = constc.at[:, POS_ROW0:POS_ROW0 + 2 * N_PIECES].set(_position_pieces(cmp_end))
    slopes = 2.0 ** (-8.0 * jnp.arange(1, N_HEADS + 1, dtype=F32) / N_HEADS)
    pieces, rest = [], slopes
    for _ in range(N_PIECES):
        piece = rest.astype(BF16).astype(F32)
        pieces.append(piece)
        rest = rest - piece
    return constk.astype(BF16), constc.astype(BF16), jnp.stack(pieces, axis=1).reshape(-1)


def _nsa_mixer(x, norm_g, w_in, b_gate, cmp_pos, cmp_w1, cmp_b1, cmp_w2, cmp_b2):
    b, s, d = x.shape
    g, nrep, dh = N_GROUPS, HEADS_PER_GROUP, HEAD_DIM
    w_in_p = jnp.pad(w_in, ((0, 0), (0, NSA_IN_PAD - NSA_IN))).astype(BF16)
    v_cols = lambda i: w_in_p[:, NSA_Q + i * NSA_KV:NSA_Q + (i + 1) * NSA_KV].T
    q, raw, keys, vals_t, glogit = _nsa_in_proj(x, norm_g, w_in_p, jnp.stack([v_cols(3), v_cols(5)]),
                                                tm=512)
    raw = raw.reshape(2, b, g, s // CMP_STRIDE, CMP_STRIDE * dh)
    cmp, cmp_t = _compress(raw, cmp_pos.reshape(2, 1, L_CMP * dh), cmp_w1.astype(BF16), cmp_b1,
                           cmp_w2.astype(BF16), cmp_b2)
    constk, constc, slope_pieces = _nsa_constants(s)
    bgate = jnp.pad(b_gate, (0, LANES - 3 * N_HEADS)).reshape(1, LANES)
    return _nsa_attention(slope_pieces, q, cmp, cmp_t, keys, vals_t, glogit, bgate, constk, constc)


TT = 256
SUB = SUBLANES


GATE_BAND = 256


def _gate_bands():
    block = D_RNN // LRU_BLOCKS
    bands = []
    for c0 in range(0, D_RNN, GATE_BAND):
        width = min(GATE_BAND, D_RNN - c0)
        k_lo = (c0 // block) * block // LANES * LANES
        k_hi = -(-(((c0 + width - 1) // block + 1) * block) // LANES) * LANES
        bands.append((k_lo, min(k_hi, D_RNN), c0, width))
    return tuple(bands)


GATE_BANDS = _gate_bands()


def _pack_gate_weights(w_a, w_x):
    dense = lambda w: jax.scipy.linalg.block_diag(*[w[i] for i in range(LRU_BLOCKS)])
    slabs = []
    for k_lo, k_hi, c0, width in GATE_BANDS:
        band = [jnp.pad(dense(w)[k_lo:k_hi, c0:c0 + width], ((0, 0), (0, GATE_BAND - width)))
                for w in (w_a, w_x)]
        slabs.append(jnp.concatenate(band, axis=1))
    return jnp.concatenate(slabs, axis=0).astype(BF16)


SEQS_PER_STEP = 2


def _lru_kernel(x_all, g_ref, win_ref, cw_ref, cb_ref, wax_ref, bax_ref, lam_ref, wout_ref, o_all,
                *scratch):
    seqs = [tuple(ref.at[h] for ref in (x_all, o_all) + scratch) for h in range(SEQS_PER_STEP)]

    @pl.when(pl.program_id(1) == 0)
    def _():
        for x_ref, o_ref, ext_ref, a_ref, u_ref, h_ref, gate_ref in seqs:
            ext_ref[pl.ds(TT, SUB), :] = jnp.zeros((SUB, D_RNN), F32)
            h_ref[...] = jnp.zeros_like(h_ref)

    tiles = [_lru_tile(g_ref, win_ref, cw_ref, cb_ref, wax_ref, bax_ref, lam_ref, wout_ref, *refs)
             for refs in seqs]
    while tiles:
        tiles = [tile for tile in tiles if next(tile, True) is not True]


def _lru_tile(g_ref, win_ref, cw_ref, cb_ref, wax_ref, bax_ref, lam_ref, wout_ref,
              x_ref, o_ref, ext_ref, a_ref, u_ref, h_ref, gate_ref):
    xn = _rms(x_ref[...], g_ref[...]).astype(BF16)
    gate_ref[...] = _gelu_tanh(jnp.dot(xn, win_ref[:, :D_RNN], preferred_element_type=F32))
    ext_ref[pl.ds(0, SUB), :] = ext_ref[pl.ds(TT, SUB), :]
    ext_ref[pl.ds(SUB, TT), :] = jnp.dot(xn, win_ref[:, D_RNN:], preferred_element_type=F32)
    yield
    cw = cw_ref[...]
    xr = cb_ref[...] + sum(
        cw[w:w + 1, :] * ext_ref[pl.ds(SUB - (CONV_W - 1) + w, TT), :] for w in range(CONV_W))

    lam = lam_ref[...]
    softplus_neg = jnp.maximum(-lam, 0.0) + jnp.log1p(jnp.exp(-jnp.abs(lam)))
    half_log_scale = (-0.5 * LRU_C) * softplus_neg
    xr_b = xr.astype(BF16)
    half_bias = bax_ref[...]
    row0 = 0
    for k_lo, k_hi, c0, width in GATE_BANDS:
        z = jnp.dot(xr_b[:, k_lo:k_hi], wax_ref[row0:row0 + k_hi - k_lo, :], preferred_element_type=F32)
        row0 += k_hi - k_lo
        tanh_r = jnp.tanh(z[:, :width] + half_bias[:, c0:c0 + width])
        tanh_i = jnp.tanh(z[:, GATE_BAND:GATE_BAND + width] + half_bias[:, D_RNN + c0:D_RNN + c0 + width])
        scale = half_log_scale[:, c0:c0 + width]
        a = jnp.exp(scale * tanh_r + scale)
        a_ref[:, c0:c0 + width] = a
        half_xr = 0.5 * xr[:, c0:c0 + width]
        u_ref[:, c0:c0 + width] = jnp.sqrt(jnp.maximum(1.0 - a * a, 0.0)) * (half_xr * tanh_i + half_xr)
    yield

    row = lax.broadcasted_iota(jnp.int32, (SUB, D_RNN), 0)

    def scan_rows(c, h_prev):
        r0 = c * SUB
        a_c = a_ref[pl.ds(r0, SUB), :]
        u_c = u_ref[pl.ds(r0, SUB), :]
        shift = 1
        while shift < SUB:
            keep = row >= shift
            u_c = u_c + a_c * jnp.where(keep, pltpu.roll(u_c, shift, axis=0), 0.0)
            a_c = a_c * jnp.where(keep, pltpu.roll(a_c, shift, axis=0), 1.0)
            shift *= 2
        h_c = u_c + a_c * h_prev
        u_ref[pl.ds(r0, SUB), :] = h_c
        return jnp.broadcast_to(h_c[SUB - 1:SUB, :], (SUB, D_RNN))

    h_state = h_ref[...]
    for c in range(TT // SUB):
        h_state = scan_rows(c, h_state)
        if c % (TT // SUB // 4) == TT // SUB // 4 - 1:
            yield
    h_ref[...] = h_state
    gated = (u_ref[...] * gate_ref[...]).astype(BF16)
    o_ref[...] = x_ref[...] + jnp.dot(gated, wout_ref[...], preferred_element_type=F32)


def _lru_mixer(x, norm_g, w_in, conv_w, conv_b, w_a, b_a, w_x, b_x, lam, w_out):
    b, s, d = x.shape
    wax = _pack_gate_weights(0.5 * w_a, 0.5 * w_x)
    bax = 0.5 * jnp.concatenate([b_a, b_x])
    nseq = SEQS_PER_STEP
    resident = lambda *shape: pl.BlockSpec(shape, lambda i, t: (0,) * len(shape),
                                           pipeline_mode=pl.Buffered(1))
    per_seq = lambda rows: pltpu.VMEM((nseq, rows, D_RNN), F32)
    return pl.pallas_call(
        _lru_kernel,
        grid=(b // nseq, s // TT),
        in_specs=[
            pl.BlockSpec((nseq, TT, d), lambda i, t: (i, t, 0)),
            resident(1, d), resident(d, 2 * D_RNN), resident(CONV_W, D_RNN), resident(1, D_RNN),
            resident(*wax.shape), resident(1, 2 * D_RNN), resident(1, D_RNN), resident(D_RNN, d),
        ],
        out_specs=pl.BlockSpec((nseq, TT, d), lambda i, t: (i, t, 0)),
        out_shape=jax.ShapeDtypeStruct((b, s, d), F32),
        scratch_shapes=[per_seq(TT + SUB), per_seq(TT), per_seq(TT), per_seq(SUB), per_seq(TT)],
        compiler_params=_cparams("parallel", "arbitrary"),
        name="lru_block",
    )(x, norm_g.reshape(1, d), w_in.astype(BF16), conv_w, conv_b.reshape(1, -1), wax,
      bax.reshape(1, -1), lam.reshape(1, -1), w_out.astype(BF16))


def kernel(x, norm_mix, norm_ffn, norm_final, nsa_w_in, nsa_b_gate, nsa_cmp_pos, nsa_cmp_w1,
           nsa_cmp_b1, nsa_cmp_w2, nsa_cmp_b2, nsa_w_out, lru_w_in, lru_conv_w, lru_conv_b,
           lru_w_a, lru_b_a, lru_w_x, lru_b_x, lru_lambda, lru_w_out, ffn_w_in, ffn_w_out):
    b, s, d = x.shape
    ffn = functools.partial(_ffn, tm=512, tf=256)
    attn = _nsa_mixer(x, norm_mix[0], nsa_w_in[0], nsa_b_gate[0], nsa_cmp_pos[0], nsa_cmp_w1[0],
                      nsa_cmp_b1[0], nsa_cmp_w2[0], nsa_cmp_b2[0])
    x = ffn(x.reshape(b * s, d), norm_ffn[0], ffn_w_in[0].astype(BF16), ffn_w_out[0].astype(BF16),
            norm_final, final_norm=False, name="ffn0", mix=attn.reshape(b * s, NSA_Q),
            w_mix=nsa_w_out[0].astype(BF16)).reshape(b, s, d)
    x = _lru_mixer(x, norm_mix[1], lru_w_in[0], lru_conv_w[0], lru_conv_b[0], lru_w_a[0], lru_b_a[0],
                   lru_w_x[0], lru_b_x[0], lru_lambda[0], lru_w_out[0])
    x = ffn(x.reshape(b * s, d), norm_ffn[1], ffn_w_in[1].astype(BF16), ffn_w_out[1].astype(BF16),
            norm_final, final_norm=True, name="ffn1").reshape(b, s, d)
    return x
```

```python
import functools
import itertools

import jax
import jax.numpy as jnp
from jax import lax
from jax.experimental import pallas as pl
from jax.experimental.pallas import tpu as pltpu

F32 = jnp.float32
BF16 = jnp.bfloat16

D_MODEL = 1024
N_HEADS = 16
N_GROUPS = 4
HEADS_PER_GROUP = N_HEADS // N_GROUPS
HEAD_DIM = 64
L_CMP = 32
CMP_STRIDE = 16
L_SEL = 64
N_SELECT = 16
WINDOW = 512
NSA_Q = N_HEADS * HEAD_DIM
NSA_KV = N_GROUPS * HEAD_DIM
NSA_IN = NSA_Q + 6 * NSA_KV + 3 * N_HEADS
D_RNN = 1408
LRU_BLOCKS = 8
LRU_C = 8.0
CONV_W = 4
D_FF = 2816
EPS = 1e-6

LANES = 128
SUBLANES = 8
NEG = -1e30
VMEM_LIMIT = 56 * 1024 * 1024

TQ = 256
TK_SEL = 512
TK_WIN = 256
N_CMP_PAD = 256


def _cparams(*sem):
    return pltpu.CompilerParams(dimension_semantics=sem, vmem_limit_bytes=VMEM_LIMIT)


def _rms(x, g):
    ms = jnp.mean(x * x, axis=-1, keepdims=True)
    return x * lax.rsqrt(ms + EPS) * g


def _gelu_tanh(x):
    c = (2.0 / jnp.pi) ** 0.5
    half_x = 0.5 * x
    return half_x * jnp.tanh(x * (c + (c * 0.044715) * (x * x))) + half_x


def _nt_dot(a, b):
    return lax.dot_general(a, b, (((1,), (1,)), ((), ())), preferred_element_type=F32)


def _ffn_kernel(*refs, final_norm, tf, has_mix):
    x_ref, refs = refs[0], refs[1:]
    x = x_ref[...]
    if has_mix:
        (mix_ref, wmix_ref), refs = refs[:2], refs[2:]
        x = x + jnp.dot(mix_ref[...], wmix_ref[...], preferred_element_type=F32)
    g_ref, win_ref, wout_ref, gf_ref, o_ref, hid_ref = refs
    xn = _rms(x, g_ref[...]).astype(BF16)
    for c0 in range(0, D_FF, tf):
        gate = jnp.dot(xn, win_ref[:, c0:c0 + tf], preferred_element_type=F32)
        up = jnp.dot(xn, win_ref[:, D_FF + c0:D_FF + c0 + tf], preferred_element_type=F32)
        hid_ref[:, c0:c0 + tf] = ((0.5 * gate) * (1.0 + jnp.tanh(0.5 * gate)) * up).astype(BF16)
    y = x + jnp.dot(hid_ref[...], wout_ref[...], preferred_element_type=F32)
    if final_norm:
        y = _rms(y, gf_ref[...])
    o_ref[...] = y


def _ffn(x, g, w_in, w_out, g_final, *, final_norm, tm, tf, name, mix=None, w_mix=None):
    m, d = x.shape
    resident = lambda *shape: pl.BlockSpec(shape, lambda i: (0,) * len(shape),
                                           pipeline_mode=pl.Buffered(1))
    rows = lambda width: pl.BlockSpec((tm, width), lambda i: (i, 0))
    if mix is None:
        mix_specs, mix_args = [], ()
    else:
        mix_specs, mix_args = [rows(mix.shape[1]), resident(*w_mix.shape)], (mix, w_mix)
    return pl.pallas_call(
        functools.partial(_ffn_kernel, final_norm=final_norm, tf=tf, has_mix=mix is not None),
        grid=(m // tm,),
        in_specs=[rows(d)] + mix_specs + [
            resident(1, d), resident(d, 2 * D_FF), resident(D_FF, d), resident(1, d)],
        out_specs=rows(d),
        out_shape=jax.ShapeDtypeStruct((m, d), F32),
        scratch_shapes=[pltpu.VMEM((tm, D_FF), BF16)],
        compiler_params=_cparams("parallel"),
        name=name,
    )(x, *mix_args, g.reshape(1, d), w_in, w_out, g_final.reshape(1, d))


NSA_IN_PAD = -(-NSA_IN // LANES) * LANES


def _nsa_in_proj_kernel(x_ref, g_ref, w_ref, wvt_ref, q_ref, raw_ref, k_ref, vt_ref, gl_ref):
    xn = _rms(x_ref[...], g_ref[...]).astype(BF16)
    q_ref[...] = jnp.dot(xn, w_ref[:, :NSA_Q], preferred_element_type=F32).astype(q_ref.dtype)
    for i, (dst, slot) in {0: (raw_ref, 0), 1: (raw_ref, 1), 2: (k_ref, 0), 4: (k_ref, 1)}.items():
        col0 = NSA_Q + i * NSA_KV
        part = jnp.dot(xn, w_ref[:, col0:col0 + NSA_KV], preferred_element_type=F32)
        for grp in range(N_GROUPS):
            dst[slot, grp] = part[:, grp * HEAD_DIM:(grp + 1) * HEAD_DIM].astype(dst.dtype)
    for slot in range(2):
        part_t = _nt_dot(wvt_ref[slot], xn)
        for grp in range(N_GROUPS):
            vt_ref[slot, grp] = part_t[grp * HEAD_DIM:(grp + 1) * HEAD_DIM].astype(vt_ref.dtype)
    gl_ref[...] = jnp.dot(xn, w_ref[:, NSA_Q + 6 * NSA_KV:], preferred_element_type=F32)


def _nsa_in_proj(x, g, w, wvt, *, tm):
    b, s, d = x.shape
    sq = pl.Squeezed()
    grouped = lambda dtype: jax.ShapeDtypeStruct((2, b, N_GROUPS, s, HEAD_DIM), dtype)
    grouped_spec = pl.BlockSpec((2, sq, N_GROUPS, tm, HEAD_DIM), lambda i, t: (0, i, 0, t, 0))
    return pl.pallas_call(
        _nsa_in_proj_kernel,
        grid=(b, s // tm),
        in_specs=[
            pl.BlockSpec((sq, tm, d), lambda i, t: (i, t, 0)),
            pl.BlockSpec((1, d), lambda i, t: (0, 0)),
            pl.BlockSpec((d, NSA_IN_PAD), lambda i, t: (0, 0)),
            pl.BlockSpec((2, NSA_KV, d), lambda i, t: (0, 0, 0)),
        ],
        out_specs=[
            pl.BlockSpec((sq, tm, NSA_Q), lambda i, t: (i, t, 0)),
            grouped_spec, grouped_spec,
            pl.BlockSpec((2, sq, N_GROUPS, HEAD_DIM, tm), lambda i, t: (0, i, 0, 0, t)),
            pl.BlockSpec((sq, tm, LANES), lambda i, t: (i, t, 0)),
        ],
        out_shape=[
            jax.ShapeDtypeStruct((b, s, NSA_Q), BF16),
            grouped(BF16), grouped(BF16),
            jax.ShapeDtypeStruct((2, b, N_GROUPS, HEAD_DIM, s), BF16),
            jax.ShapeDtypeStruct((b, s, LANES), F32),
        ],
        compiler_params=_cparams("parallel", "parallel"),
        name="nsa_in_proj",
    )(x, g.reshape(1, d), w, wvt)


def _compress_kernel(r_ref, pos_ref, w1_ref, b1_ref, w2_ref, b2_ref, w2t_ref, b2t_ref, o_ref, ot_ref,
                     bias_ref):
    half = CMP_STRIDE * HEAD_DIM
    rows = r_ref[...]
    part_lo = jnp.dot(rows, w1_ref[:half, :], preferred_element_type=F32)
    part_hi = jnp.dot(rows, w1_ref[half:, :], preferred_element_type=F32)
    @pl.when((pl.program_id(1) == 0) & (pl.program_id(2) == 0))
    def _():
        bias_ref[...] = b1_ref[...] + jnp.dot(
            pos_ref[...], w1_ref[...].astype(F32), precision=lax.Precision.HIGHEST,
            preferred_element_type=F32)

    hid = part_lo + pltpu.roll(part_hi, N_CMP_PAD - 1, axis=0) + bias_ref[...]
    hid = jax.nn.gelu(hid, approximate=True).astype(BF16)
    o_ref[...] = (jnp.dot(hid, w2_ref[...], preferred_element_type=F32) + b2_ref[...]).astype(o_ref.dtype)
    ot_ref[...] = (_nt_dot(w2t_ref[...], hid) + b2t_ref[...]).astype(ot_ref.dtype)


def _compress(raw, pos, w1, b1, w2, b2):
    two, b, g, n, width = raw.shape
    hidden = w1.shape[-1]
    sq = pl.Squeezed()
    per_branch = lambda *shape: pl.BlockSpec((sq,) + shape, lambda c, i, j: (c,) + (0,) * len(shape))
    return pl.pallas_call(
        _compress_kernel,
        grid=(two, b, g),
        in_specs=[
            pl.BlockSpec((sq, sq, sq, n, width), lambda c, i, j: (c, i, j, 0, 0)),
            per_branch(1, 2 * width), per_branch(2 * width, hidden), per_branch(1, hidden),
            per_branch(hidden, HEAD_DIM), per_branch(1, HEAD_DIM),
            per_branch(HEAD_DIM, hidden), per_branch(HEAD_DIM, 1),
        ],
        out_specs=[
            pl.BlockSpec((sq, sq, sq, n, HEAD_DIM), lambda c, i, j: (c, i, j, 0, 0)),
            pl.BlockSpec((sq, sq, sq, HEAD_DIM, n), lambda c, i, j: (c, i, j, 0, 0)),
        ],
        out_shape=[
            jax.ShapeDtypeStruct((two, b, g, n, HEAD_DIM), BF16),
            jax.ShapeDtypeStruct((two, b, g, HEAD_DIM, n), BF16),
        ],
        scratch_shapes=[pltpu.VMEM((1, hidden), F32)],
        compiler_params=_cparams("arbitrary", "arbitrary", "arbitrary"),
        name="nsa_compress",
    )(raw, pos, w1, b1.reshape(two, 1, hidden), w2, b2.reshape(two, 1, HEAD_DIM),
      w2.transpose(0, 2, 1), b2.reshape(two, HEAD_DIM, 1))


AUG = 256
SEL_ROW0 = HEAD_DIM
POS_ROW0 = 2 * HEAD_DIM
N_PIECES = 3
PAD_LANE = POS_ROW0 + 2 * N_PIECES
POS_ROWS = 16
V_ROWS = 80
NQ = HEADS_PER_GROUP * TQ
BIG = 1e30
assert TQ == TK_WIN and TK_SEL % TQ == 0 and WINDOW % TK_WIN == 0


def _flash_reset(m_ref, acc_ref):
    m_ref[...] = jnp.full_like(m_ref, NEG)
    acc_ref[...] = jnp.zeros_like(acc_ref)


def _scores(kaug, tile, tk, qat_ref):
    k0 = pl.multiple_of(tile * tk, tk)
    return jnp.dot(kaug[pl.ds(k0, tk), :], qat_ref[...], preferred_element_type=F32)


def _flash_update_steps(s, vt_tile, mask, m_ref, acc_ref):
    if mask is not None:
        s = jnp.where(mask, s, NEG)
    m_old = m_ref[...]
    m_new = jnp.maximum(m_old, jnp.max(s, axis=0, keepdims=True))
    yield
    p = jnp.exp(s - m_new).astype(BF16)
    yield
    acc_ref[...] = jnp.exp(m_old - m_new) * acc_ref[...] + jnp.dot(
        vt_tile, p, preferred_element_type=F32)
    m_ref[...] = m_new


def _flash_update(*args):
    for _ in _flash_update_steps(*args):
        pass


def _round_robin(generators):
    generators = list(generators)
    while generators:
        generators = [g for g in generators if next(g, True) is not True]


def _flash_result(acc_ref):
    return acc_ref[pl.ds(0, HEAD_DIM), :] * (1.0 / acc_ref[pl.ds(HEAD_DIM, 1), :])


GROUPS_PER_STEP = 2


def _nsa_kernel(sl_ref, q_ref, kc_all, vct_all, ks_all, vst_all, kw_all, vwt_all, gl_ref, bg_ref,
                constk_ref, constc_ref, o_ref, *scratch):
    qi = pl.program_id(2)
    gt_ref = scratch[-1]
    per_group = (kc_all, vct_all, ks_all, vst_all, kw_all, vwt_all) + scratch[:-1]
    groups = [tuple(ref.at[h] for ref in per_group) for h in range(GROUPS_PER_STEP)]

    @pl.when(qi == 0)
    def _():
        for refs in groups:
            _nsa_assemble(constk_ref, constc_ref, *refs)

    gt_ref[...] = jax.nn.sigmoid(gl_ref[...] + bg_ref[...]).T
    tiles = [_nsa_tile(pl.program_id(1) * GROUPS_PER_STEP + h, h * HEADS_PER_GROUP * HEAD_DIM, qi,
                       sl_ref, q_ref, o_ref, gt_ref, *refs) for h, refs in enumerate(groups)]
    heads = list(tiles)
    while heads:
        heads = [tile for tile in heads if not next(tile)]
    ends = [next(tile) for tile in tiles]
    for parities in itertools.product((0, 1), repeat=len(tiles)):
        @pl.when(functools.reduce(jnp.logical_and, [odd == want for (odd, _), want in zip(ends, parities)]))
        def _():
            _round_robin(last(want) for (_, last), want in zip(ends, parities))
    for tile in tiles:
        next(tile, None)


def _nsa_assemble(constk_ref, constc_ref, kc_ref, vct_ref, ks_ref, vst_ref, kw_ref, vwt_ref,
                  kaug_s, kaug_w, kaug_c, vt_s, vt_w, vt_c, *unused):
    s_len = ks_ref.shape[0]
    for kaug, row0, const, k in ((kaug_s, 0, constk_ref, ks_ref), (kaug_w, WINDOW, constk_ref, kw_ref),
                                 (kaug_c, 0, constc_ref, kc_ref)):
        kaug[pl.ds(row0, k.shape[0]), :] = const[...]
        kaug[pl.ds(row0, k.shape[0]), pl.ds(0, HEAD_DIM)] = k[...]
    pad_lane = lax.broadcasted_iota(jnp.int32, (WINDOW, AUG), 1) == PAD_LANE
    kaug_w[pl.ds(0, WINDOW), :] = jnp.where(pad_lane, 1.0, 0.0).astype(BF16)
    vt_w[:, pl.ds(0, WINDOW)] = jnp.zeros((V_ROWS, WINDOW), BF16)
    pad = V_ROWS - HEAD_DIM
    ones_rows = lambda n: jnp.where(
        lax.broadcasted_iota(jnp.int32, (pad, n), 0) == 0, 1.0, 0.0).astype(BF16)
    for c in range(vt_s.shape[0]):
        vt_s[c, pl.ds(0, HEAD_DIM), :] = vst_ref[:, pl.ds(c * TK_SEL, TK_SEL)]
        vt_s[c, pl.ds(HEAD_DIM, pad), :] = ones_rows(TK_SEL)
    vt_w[pl.ds(0, HEAD_DIM), pl.ds(WINDOW, s_len)] = vwt_ref[...]
    vt_w[pl.ds(HEAD_DIM, pad), pl.ds(WINDOW, s_len)] = ones_rows(s_len)
    vt_c[pl.ds(0, HEAD_DIM), :] = vct_ref[...]
    vt_c[pl.ds(HEAD_DIM, pad), :] = ones_rows(N_CMP_PAD)


def _nsa_tile(grp, lane0, qi, sl_ref, q_ref, o_ref, gt_ref, kc_ref, vct_ref, ks_ref, vst_ref, kw_ref,
              vwt_ref, kaug_s, kaug_w, kaug_c, vt_s, vt_w, vt_c, qat_ref, score_ref, m_ref, acc_ref,
              out_ref, sa_ref, sb_ref, qats_ref, tiles_ref):
    q0 = qi * TQ
    nrep = HEADS_PER_GROUP

    for half in range(nrep // 2):
        qt = q_ref[:, pl.ds(lane0 + half * LANES, LANES)].astype(F32).T * (HEAD_DIM ** -0.5)
        for sub in range(2):
            q_rows = qt[sub * HEAD_DIM:(sub + 1) * HEAD_DIM].astype(BF16)
            for ref in (qat_ref, qats_ref):
                ref[pl.ds(0, HEAD_DIM), pl.ds((2 * half + sub) * TQ, TQ)] = q_rows
    qat_ref[pl.ds(SEL_ROW0, HEAD_DIM), :] = jnp.zeros((HEAD_DIM, NQ), BF16)
    piece = lax.broadcasted_iota(jnp.int32, (POS_ROWS, TQ), 0)
    tail0 = POS_ROW0 + POS_ROWS
    for r in range(nrep):
        tile = jnp.where(piece == PAD_LANE - POS_ROW0, -BIG, 0.0)
        for i in range(2 * N_PIECES):
            tile = jnp.where(piece == i, sl_ref[(grp * nrep + r) * N_PIECES + i % N_PIECES], tile)
        for ref in (qat_ref, qats_ref):
            ref[pl.ds(POS_ROW0, POS_ROWS), pl.ds(r * TQ, TQ)] = tile.astype(BF16)
    for ref in (qat_ref, qats_ref):
        ref[pl.ds(tail0, AUG - tail0), :] = jnp.zeros((AUG - tail0, NQ), BF16)

    yield False
    t_row = q0 + (lax.broadcasted_iota(jnp.int32, (1, NQ), 1) & (TQ - 1))
    gate_row = lambda br: jnp.concatenate(
        [gt_ref[pl.ds(3 * (grp * nrep + r) + br, 1), :] for r in range(nrep)], axis=1)

    s = jnp.dot(kaug_c[...], qat_ref[...], preferred_element_type=F32)
    yield False
    cmp_end = CMP_STRIDE * lax.broadcasted_iota(jnp.int32, (N_CMP_PAD, 1), 0) + (L_CMP - 1)
    mask_c = cmp_end <= t_row
    s = jnp.where(mask_c, s, NEG)
    e = jnp.where(mask_c, jnp.exp(s - jnp.max(s, axis=0, keepdims=True)), 0.0)
    d = jnp.sum(e, axis=0, keepdims=True)
    p = e * (1.0 / jnp.where(d > 0, d, 1.0))
    yield False
    p_sum = sum(p[:, r * TQ:(r + 1) * TQ] for r in range(nrep))
    o_c = jnp.dot(vt_c[...], p.astype(BF16), preferred_element_type=F32)
    out_ref[...] = gate_row(0) * o_c[:HEAD_DIM]
    yield False

    n_sb = score_ref.shape[0]
    jb = lax.broadcasted_iota(jnp.int32, (n_sb, TQ), 0)
    t_sel = q0 + lax.broadcasted_iota(jnp.int32, (n_sb, TQ), 1)
    valid = jb * L_SEL <= t_sel

    n_full = q0 // TK_SEL
    blocks_per_tile = TK_SEL // L_SEL

    def write_selection(selected):
        sel_bias = jnp.where(selected, 0.0, -BIG).astype(BF16)
        for r in range(nrep):
            qats_ref[pl.ds(SEL_ROW0, n_sb), pl.ds(r * TQ, TQ)] = sel_bias
        picked = jnp.where(selected, 1.0, 0.0)
        bits = sum(jnp.max(picked[k * blocks_per_tile:(k + 1) * blocks_per_tile], axis=(0, 1),
                           keepdims=True) * float(2 ** k) for k in range(n_sb // blocks_per_tile))
        bits = bits[0, 0].astype(jnp.int32)
        count = jnp.int32(0)
        for k in range(n_sb // blocks_per_tile):
            tiles_ref[count] = k
            count = count + jnp.where((k < n_full) & (((bits >> k) & 1) == 1), 1, 0)
        tiles_ref[count] = n_full
        return count

    def rank_blocks():
        cmp_start = CMP_STRIDE * lax.broadcasted_iota(jnp.int32, (n_sb, N_CMP_PAD), 1)
        sb_start = L_SEL * lax.broadcasted_iota(jnp.int32, (n_sb, N_CMP_PAD), 0)
        overlap_t = ((cmp_start < sb_start + L_SEL) & (cmp_start + L_CMP > sb_start)).astype(F32)
        imp_t = jnp.dot(overlap_t, p_sum, precision=lax.Precision.HIGHEST,
                        preferred_element_type=F32)
        cur = t_sel // L_SEL
        forced = (jb == 0) | (jb == cur) | (jb == cur - 1)
        score = jnp.where(valid, jnp.where(forced, jnp.inf, imp_t), -jnp.inf)
        score_ref[...] = score
        chunks = [score[c * SUBLANES:(c + 1) * SUBLANES] for c in range(n_sb // SUBLANES)]
        ranks = [jnp.zeros((SUBLANES, TQ), F32) for _ in chunks]
        sub_row = lax.broadcasted_iota(jnp.int32, (SUBLANES, TQ), 0)
        for i in range(n_sb):
            row = score_ref[pl.ds(i, 1), :]
            for c, chunk in enumerate(chunks):
                later = jnp.where(row >= chunk, 1.0, 0.0)
                earlier = jnp.where(row > chunk, 1.0, 0.0)
                if i < c * SUBLANES:
                    ranks[c] = ranks[c] + later
                elif i >= (c + 1) * SUBLANES:
                    ranks[c] = ranks[c] + earlier
                else:
                    ranks[c] = ranks[c] + jnp.where(sub_row > i - c * SUBLANES, later, earlier)
        rank = jnp.concatenate(ranks, axis=0)
        return write_selection(valid & (rank < float(N_SELECT)))

    n_list = rank_blocks()
    yield False

    n_win = WINDOW // TK_WIN
    k0 = pl.multiple_of(q0, TK_WIN)
    s_all = jnp.dot(kaug_w[pl.ds(k0, WINDOW + TQ), :], qat_ref[...], preferred_element_type=F32)
    yield False
    s_w = [s_all[j * TK_WIN:(j + 1) * TK_WIN] for j in range(n_win + 1)]
    not_after = q0 + lax.broadcasted_iota(jnp.int32, (TK_WIN, 1), 0) <= t_row
    s_w[0] = jnp.where(not_after, NEG, s_w[0])
    s_w[-1] = jnp.where(not_after, s_w[-1], NEG)
    m_w = functools.reduce(jnp.maximum, [jnp.max(x, axis=0, keepdims=True) for x in s_w])
    p_w = jnp.concatenate([jnp.exp(x - m_w).astype(BF16) for x in s_w], axis=0)
    yield False
    acc_w = jnp.dot(vt_w[:, pl.ds(k0, WINDOW + TQ)], p_w, preferred_element_type=F32)
    out_ref[...] += gate_row(2) * (acc_w[:HEAD_DIM] * (1.0 / acc_w[HEAD_DIM:HEAD_DIM + 1]))
    yield False

    _flash_reset(m_ref, acc_ref)
    sel_scores = lambda i: _scores(kaug_s, tiles_ref[i], TK_SEL, qats_ref)
    sel_update = lambda s_ref, i, mask: _flash_update(s_ref[...], vt_s[tiles_ref[i]], mask, m_ref, acc_ref)
    sa_ref[...] = sel_scores(0)
    yield True

    def sel_pair(j, carry):
        i = 2 * j
        sb_ref[...] = sel_scores(i + 1)
        sel_update(sa_ref, i, None)
        sa_ref[...] = sel_scores(i + 2)
        sel_update(sb_ref, i + 1, None)
        return carry

    lax.fori_loop(0, n_list // 2, sel_pair, 0)

    def last_tiles(odd):
        own = n_list
        own_rows = pl.ds(pl.multiple_of(q0 - n_full * TK_SEL, TQ), TQ)
        if odd:
            sb_ref[...] = sel_scores(own)
            yield
            yield from _flash_update_steps(sa_ref[...], vt_s[tiles_ref[own - 1]], None, m_ref, acc_ref)
            yield
        own_ref = sb_ref if odd else sa_ref
        own_ref[own_rows, :] = jnp.where(not_after, own_ref[own_rows, :], NEG)
        yield from _flash_update_steps(own_ref[...], vt_s[tiles_ref[own]], None, m_ref, acc_ref)

    yield n_list % 2, last_tiles

    out_ref[...] += gate_row(1) * _flash_result(acc_ref)

    for half in range(nrep // 2):
        slab = jnp.concatenate([out_ref[:, pl.ds(2 * half * TQ, TQ)],
                                out_ref[:, pl.ds((2 * half + 1) * TQ, TQ)]], axis=0)
        o_ref[:, pl.ds(lane0 + half * LANES, LANES)] = slab.T.astype(o_ref.dtype)


def _nsa_attention(slope_pieces, q, cmp, cmp_t, keys, vals_t, glogit, bgate, constk, constc):
    b, s, _ = q.shape
    g, nrep, dh = N_GROUPS, HEADS_PER_GROUP, HEAD_DIM
    n_sb = s // L_SEL
    sq = pl.Squeezed()
    gp = GROUPS_PER_STEP
    slot = lambda c, *shape: pl.BlockSpec((sq, sq, gp) + shape, lambda i, j, t: (c, i, j, 0, 0))
    full = lambda *shape: pl.BlockSpec(shape, lambda i, j, t: (0,) * len(shape))
    per_group = lambda shape, dtype: pltpu.VMEM((gp,) + shape, dtype)
    return pl.pallas_call(
        _nsa_kernel,
        grid=(b, g // gp, s // TQ),
        in_specs=[
            pl.BlockSpec(memory_space=pltpu.SMEM),
            pl.BlockSpec((sq, TQ, gp * nrep * dh), lambda i, j, t: (i, t, j)),
            slot(0, N_CMP_PAD, dh), slot(1, dh, N_CMP_PAD),
            slot(0, s, dh), slot(0, dh, s), slot(1, s, dh), slot(1, dh, s),
            pl.BlockSpec((sq, TQ, LANES), lambda i, j, t: (i, t, 0)),
            full(1, LANES), full(s, AUG), full(N_CMP_PAD, AUG),
        ],
        out_specs=pl.BlockSpec((sq, TQ, gp * nrep * dh), lambda i, j, t: (i, t, j)),
        out_shape=jax.ShapeDtypeStruct((b, s, NSA_Q), BF16),
        scratch_shapes=[
            per_group((s, AUG), BF16), per_group((WINDOW + s, AUG), BF16), per_group((N_CMP_PAD, AUG), BF16),
            per_group((s // TK_SEL, V_ROWS, TK_SEL), BF16), per_group((V_ROWS, WINDOW + s), BF16),
            per_group((V_ROWS, N_CMP_PAD), BF16),
            per_group((AUG, NQ), BF16), per_group((n_sb, TQ), F32), per_group((1, NQ), F32),
            per_group((V_ROWS, NQ), F32), per_group((dh, NQ), F32),
            per_group((TK_SEL, NQ), F32), per_group((TK_SEL, NQ), F32),
            per_group((AUG, NQ), BF16),
            pltpu.SMEM((gp, s // TK_SEL + 1), jnp.int32),
            pltpu.VMEM((LANES, TQ), F32),
        ],
        compiler_params=_cparams("arbitrary", "arbitrary", "arbitrary"),
        name="nsa_attention",
    )(slope_pieces, q, cmp, cmp_t, keys, vals_t, keys, vals_t, glogit, bgate, constk, constc)


def _position_pieces(pos):
    hi = (pos // L_SEL) * L_SEL
    return jnp.stack([hi] * N_PIECES + [pos - hi] * N_PIECES, axis=1).astype(F32)


def _nsa_constants(s):
    pos = jnp.arange(s)
    constk = jnp.zeros((s, AUG), F32)
    constk = constk.at[:, SEL_ROW0:SEL_ROW0 + s // L_SEL].set(jax.nn.one_hot(pos // L_SEL, s // L_SEL))
    constk = constk.at[:, POS_ROW0:POS_ROW0 + 2 * N_PIECES].set(_position_pieces(pos))
    cmp_end = CMP_STRIDE * jnp.arange(N_CMP_PAD) + (L_CMP - 1)
    constc = jnp.zeros((N_CMP_PAD, AUG), F32)
    constc = constc.at[:, POS_ROW0:POS_ROW0 + 2 * N_PIECES].set(_position_pieces(cmp_end))
    slopes = 2.0 ** (-8.0 * jnp.arange(1, N_HEADS + 1, dtype=F32) / N_HEADS)
    pieces, rest = [], slopes
    for _ in range(N_PIECES):
        piece = rest.astype(BF16).astype(F32)
        pieces.append(piece)
        rest = rest - piece
    return constk.astype(BF16), constc.astype(BF16), jnp.stack(pieces, axis=1).reshape(-1)


def _nsa_mixer(x, norm_g, w_in, b_gate, cmp_pos, cmp_w1, cmp_b1, cmp_w2, cmp_b2):
    b, s, d = x.shape
    g, nrep, dh = N_GROUPS, HEADS_PER_GROUP, HEAD_DIM
    w_in_p = jnp.pad(w_in, ((0, 0), (0, NSA_IN_PAD - NSA_IN))).astype(BF16)
    v_cols = lambda i: w_in_p[:, NSA_Q + i * NSA_KV:NSA_Q + (i + 1) * NSA_KV].T
    q, raw, keys, vals_t, glogit = _nsa_in_proj(x, norm_g, w_in_p, jnp.stack([v_cols(3), v_cols(5)]),
                                                tm=512)
    raw = raw.reshape(2, b, g, s // CMP_STRIDE, CMP_STRIDE * dh)
    cmp, cmp_t = _compress(raw, cmp_pos.reshape(2, 1, L_CMP * dh), cmp_w1.astype(BF16), cmp_b1,
                           cmp_w2.astype(BF16), cmp_b2)
    constk, constc, slope_pieces = _nsa_constants(s)
    bgate = jnp.pad(b_gate, (0, LANES - 3 * N_HEADS)).reshape(1, LANES)
    return _nsa_attention(slope_pieces, q, cmp, cmp_t, keys, vals_t, glogit, bgate, constk, constc)


TT = 256
SUB = SUBLANES


GATE_BAND = 256


def _gate_bands():
    block = D_RNN // LRU_BLOCKS
    bands = []
    for c0 in range(0, D_RNN, GATE_BAND):
        width = min(GATE_BAND, D_RNN - c0)
        k_lo = (c0 // block) * block // LANES * LANES
        k_hi = -(-(((c0 + width - 1) // block + 1) * block) // LANES) * LANES
        bands.append((k_lo, min(k_hi, D_RNN), c0, width))
    return tuple(bands)


GATE_BANDS = _gate_bands()


def _pack_gate_weights(w_a, w_x):
    dense = lambda w: jax.scipy.linalg.block_diag(*[w[i] for i in range(LRU_BLOCKS)])
    slabs = []
    for k_lo, k_hi, c0, width in GATE_BANDS:
        band = [jnp.pad(dense(w)[k_lo:k_hi, c0:c0 + width], ((0, 0), (0, GATE_BAND - width)))
                for w in (w_a, w_x)]
        slabs.append(jnp.concatenate(band, axis=1))
    return jnp.concatenate(slabs, axis=0).astype(BF16)


SEQS_PER_STEP = 2


def _lru_kernel(x_all, g_ref, win_ref, cw_ref, cb_ref, wax_ref, bax_ref, lam_ref, wout_ref, o_all,
                *scratch):
    seqs = [tuple(ref.at[h] for ref in (x_all, o_all) + scratch) for h in range(SEQS_PER_STEP)]

    @pl.when(pl.program_id(1) == 0)
    def _():
        for x_ref, o_ref, ext_ref, a_ref, u_ref, h_ref, gate_ref in seqs:
            ext_ref[pl.ds(TT, SUB), :] = jnp.zeros((SUB, D_RNN), F32)
            h_ref[...] = jnp.zeros_like(h_ref)

    tiles = [_lru_tile(g_ref, win_ref, cw_ref, cb_ref, wax_ref, bax_ref, lam_ref, wout_ref, *refs)
             for refs in seqs]
    while tiles:
        tiles = [tile for tile in tiles if next(tile, True) is not True]


def _lru_tile(g_ref, win_ref, cw_ref, cb_ref, wax_ref, bax_ref, lam_ref, wout_ref,
              x_ref, o_ref, ext_ref, a_ref, u_ref, h_ref, gate_ref):
    xn = _rms(x_ref[...], g_ref[...]).astype(BF16)
    gate_ref[...] = _gelu_tanh(jnp.dot(xn, win_ref[:, :D_RNN], preferred_element_type=F32))
    ext_ref[pl.ds(0, SUB), :] = ext_ref[pl.ds(TT, SUB), :]
    ext_ref[pl.ds(SUB, TT), :] = jnp.dot(xn, win_ref[:, D_RNN:], preferred_element_type=F32)
    yield
    cw = cw_ref[...]
    xr = cb_ref[...] + sum(
        cw[w:w + 1, :] * ext_ref[pl.ds(SUB - (CONV_W - 1) + w, TT), :] for w in range(CONV_W))

    lam = lam_ref[...]
    softplus_neg = jnp.maximum(-lam, 0.0) + jnp.log1p(jnp.exp(-jnp.abs(lam)))
    half_log_scale = (-0.5 * LRU_C) * softplus_neg
    xr_b = xr.astype(BF16)
    half_bias = bax_ref[...]
    row0 = 0
    for k_lo, k_hi, c0, width in GATE_BANDS:
        z = jnp.dot(xr_b[:, k_lo:k_hi], wax_ref[row0:row0 + k_hi - k_lo, :], preferred_element_type=F32)
        row0 += k_hi - k_lo
        tanh_r = jnp.tanh(z[:, :width] + half_bias[:, c0:c0 + width])
        tanh_i = jnp.tanh(z[:, GATE_BAND:GATE_BAND + width] + half_bias[:, D_RNN + c0:D_RNN + c0 + width])
        scale = half_log_scale[:, c0:c0 + width]
        a = jnp.exp(scale * tanh_r + scale)
        a_ref[:, c0:c0 + width] = a
        half_xr = 0.5 * xr[:, c0:c0 + width]
        u_ref[:, c0:c0 + width] = jnp.sqrt(jnp.maximum(1.0 - a * a, 0.0)) * (half_xr * tanh_i + half_xr)
    yield

    row = lax.broadcasted_iota(jnp.int32, (SUB, D_RNN), 0)

    def scan_rows(c, h_prev):
        r0 = c * SUB
        a_c = a_ref[pl.ds(r0, SUB), :]
        u_c = u_ref[pl.ds(r0, SUB), :]
        shift = 1
        while shift < SUB:
            keep = row >= shift
            u_c = u_c + a_c * jnp.where(keep, pltpu.roll(u_c, shift, axis=0), 0.0)
            a_c = a_c * jnp.where(keep, pltpu.roll(a_c, shift, axis=0), 1.0)
            shift *= 2
        h_c = u_c + a_c * h_prev
        u_ref[pl.ds(r0, SUB), :] = h_c
        return jnp.broadcast_to(h_c[SUB - 1:SUB, :], (SUB, D_RNN))

    h_state = h_ref[...]
    for c in range(TT // SUB):
        h_state = scan_rows(c, h_state)
        if c % (TT // SUB // 4) == TT // SUB // 4 - 1:
            yield
    h_ref[...] = h_state
    gated = (u_ref[...] * gate_ref[...]).astype(BF16)
    o_ref[...] = x_ref[...] + jnp.dot(gated, wout_ref[...], preferred_element_type=F32)


def _lru_mixer(x, norm_g, w_in, conv_w, conv_b, w_a, b_a, w_x, b_x, lam, w_out):
    b, s, d = x.shape
    wax = _pack_gate_weights(0.5 * w_a, 0.5 * w_x)
    bax = 0.5 * jnp.concatenate([b_a, b_x])
    nseq = SEQS_PER_STEP
    resident = lambda *shape: pl.BlockSpec(shape, lambda i, t: (0,) * len(shape),
                                           pipeline_mode=pl.Buffered(1))
    per_seq = lambda rows: pltpu.VMEM((nseq, rows, D_RNN), F32)
    return pl.pallas_call(
        _lru_kernel,
        grid=(b // nseq, s // TT),
        in_specs=[
            pl.BlockSpec((nseq, TT, d), lambda i, t: (i, t, 0)),
            resident(1, d), resident(d, 2 * D_RNN), resident(CONV_W, D_RNN), resident(1, D_RNN),
            resident(*wax.shape), resident(1, 2 * D_RNN), resident(1, D_RNN), resident(D_RNN, d),
        ],
        out_specs=pl.BlockSpec((nseq, TT, d), lambda i, t: (i, t, 0)),
        out_shape=jax.ShapeDtypeStruct((b, s, d), F32),
        scratch_shapes=[per_seq(TT + SUB), per_seq(TT), per_seq(TT), per_seq(SUB), per_seq(TT)],
        compiler_params=_cparams("parallel", "arbitrary"),
        name="lru_block",
    )(x, norm_g.reshape(1, d), w_in.astype(BF16), conv_w, conv_b.reshape(1, -1), wax,
      bax.reshape(1, -1), lam.reshape(1, -1), w_out.astype(BF16))


def kernel(x, norm_mix, norm_ffn, norm_final, nsa_w_in, nsa_b_gate, nsa_cmp_pos, nsa_cmp_w1,
           nsa_cmp_b1, nsa_cmp_w2, nsa_cmp_b2, nsa_w_out, lru_w_in, lru_conv_w, lru_conv_b,
           lru_w_a, lru_b_a, lru_w_x, lru_b_x, lru_lambda, lru_w_out, ffn_w_in, ffn_w_out):
    b, s, d = x.shape
    ffn = functools.partial(_ffn, tm=512, tf=256)
    attn = _nsa_mixer(x, norm_mix[0], nsa_w_in[0], nsa_b_gate[0], nsa_cmp_pos[0], nsa_cmp_w1[0],
                      nsa_cmp_b1[0], nsa_cmp_w2[0], nsa_cmp_b2[0])
    x = ffn(x.reshape(b * s, d), norm_ffn[0], ffn_w_in[0].astype(BF16), ffn_w_out[0].astype(BF16),
            norm_final, final_norm=False, name="ffn0", mix=attn.reshape(b * s, NSA_Q),
            w_mix=nsa_w_out[0].astype(BF16)).reshape(b, s, d)
    x = _lru_mixer(x, norm_mix[1], lru_w_in[0], lru_conv_w[0], lru_conv_b[0], lru_w_a[0], lru_b_a[0],
                   lru_w_x[0], lru_b_x[0], lru_lambda[0], lru_w_out[0])
    x = ffn(x.reshape(b * s, d), norm_ffn[1], ffn_w_in[1].astype(BF16), ffn_w_out[1].astype(BF16),
            norm_final, final_norm=True, name="ffn1").reshape(b, s, d)
    return x
```

```python
import functools
import itertools

import jax
import jax.numpy as jnp
from jax import lax
from jax.experimental import pallas as pl
from jax.experimental.pallas import tpu as pltpu

F32 = jnp.float32
BF16 = jnp.bfloat16

D_MODEL = 1024
N_HEADS = 16
N_GROUPS = 4
HEADS_PER_GROUP = N_HEADS // N_GROUPS
HEAD_DIM = 64
L_CMP = 32
CMP_STRIDE = 16
L_SEL = 64
N_SELECT = 16
WINDOW = 512
NSA_Q = N_HEADS * HEAD_DIM
NSA_KV = N_GROUPS * HEAD_DIM
NSA_IN = NSA_Q + 6 * NSA_KV + 3 * N_HEADS
D_RNN = 1408
LRU_BLOCKS = 8
LRU_C = 8.0
CONV_W = 4
D_FF = 2816
EPS = 1e-6

LANES = 128
SUBLANES = 8
NEG = -1e30
VMEM_LIMIT = 56 * 1024 * 1024

TQ = 256
TK_SEL = 512
TK_WIN = 256
N_CMP_PAD = 256


def _cparams(*sem):
    return pltpu.CompilerParams(dimension_semantics=sem, vmem_limit_bytes=VMEM_LIMIT)


def _rms(x, g):
    ms = jnp.mean(x * x, axis=-1, keepdims=True)
    return x * lax.rsqrt(ms + EPS) * g


def _gelu_tanh(x):
    c = (2.0 / jnp.pi) ** 0.5
    half_x = 0.5 * x
    return half_x * jnp.tanh(x * (c + (c * 0.044715) * (x * x))) + half_x


def _nt_dot(a, b):
    return lax.dot_general(a, b, (((1,), (1,)), ((), ())), preferred_element_type=F32)


def _ffn_kernel(*refs, final_norm, tf, has_mix):
    x_ref, refs = refs[0], refs[1:]
    x = x_ref[...]
    if has_mix:
        (mix_ref, wmix_ref), refs = refs[:2], refs[2:]
        x = x + jnp.dot(mix_ref[...], wmix_ref[...], preferred_element_type=F32)
    g_ref, win_ref, wout_ref, gf_ref, o_ref, hid_ref = refs
    xn = _rms(x, g_ref[...]).astype(BF16)
    for c0 in range(0, D_FF, tf):
        gate = jnp.dot(xn, win_ref[:, c0:c0 + tf], preferred_element_type=F32)
        up = jnp.dot(xn, win_ref[:, D_FF + c0:D_FF + c0 + tf], preferred_element_type=F32)
        hid_ref[:, c0:c0 + tf] = ((0.5 * gate) * (1.0 + jnp.tanh(0.5 * gate)) * up).astype(BF16)
    y = x + jnp.dot(hid_ref[...], wout_ref[...], preferred_element_type=F32)
    if final_norm:
        y = _rms(y, gf_ref[...])
    o_ref[...] = y


def _ffn(x, g, w_in, w_out, g_final, *, final_norm, tm, tf, name, mix=None, w_mix=None):
    m, d = x.shape
    resident = lambda *shape: pl.BlockSpec(shape, lambda i: (0,) * len(shape),
                                           pipeline_mode=pl.Buffered(1))
    rows = lambda width: pl.BlockSpec((tm, width), lambda i: (i, 0))
    if mix is None:
        mix_specs, mix_args = [], ()
    else:
        mix_specs, mix_args = [rows(mix.shape[1]), resident(*w_mix.shape)], (mix, w_mix)
    return pl.pallas_call(
        functools.partial(_ffn_kernel, final_norm=final_norm, tf=tf, has_mix=mix is not None),
        grid=(m // tm,),
        in_specs=[rows(d)] + mix_specs + [
            resident(1, d), resident(d, 2 * D_FF), resident(D_FF, d), resident(1, d)],
        out_specs=rows(d),
        out_shape=jax.ShapeDtypeStruct((m, d), F32),
        scratch_shapes=[pltpu.VMEM((tm, D_FF), BF16)],
        compiler_params=_cparams("parallel"),
        name=name,
    )(x, *mix_args, g.reshape(1, d), w_in, w_out, g_final.reshape(1, d))


NSA_IN_PAD = -(-NSA_IN // LANES) * LANES


def _nsa_in_proj_kernel(x_ref, g_ref, w_ref, wvt_ref, q_ref, raw_ref, k_ref, vt_ref, gl_ref):
    xn = _rms(x_ref[...], g_ref[...]).astype(BF16)
    q_ref[...] = jnp.dot(xn, w_ref[:, :NSA_Q], preferred_element_type=F32).astype(q_ref.dtype)
    for i, (dst, slot) in {0: (raw_ref, 0), 1: (raw_ref, 1), 2: (k_ref, 0), 4: (k_ref, 1)}.items():
        col0 = NSA_Q + i * NSA_KV
        part = jnp.dot(xn, w_ref[:, col0:col0 + NSA_KV], preferred_element_type=F32)
        for grp in range(N_GROUPS):
            dst[slot, grp] = part[:, grp * HEAD_DIM:(grp + 1) * HEAD_DIM].astype(dst.dtype)
    for slot in range(2):
        part_t = _nt_dot(wvt_ref[slot], xn)
        for grp in range(N_GROUPS):
            vt_ref[slot, grp] = part_t[grp * HEAD_DIM:(grp + 1) * HEAD_DIM].astype(vt_ref.dtype)
    gl_ref[...] = jnp.dot(xn, w_ref[:, NSA_Q + 6 * NSA_KV:], preferred_element_type=F32)


def _nsa_in_proj(x, g, w, wvt, *, tm):
    b, s, d = x.shape
    sq = pl.Squeezed()
    grouped = lambda dtype: jax.ShapeDtypeStruct((2, b, N_GROUPS, s, HEAD_DIM), dtype)
    grouped_spec = pl.BlockSpec((2, sq, N_GROUPS, tm, HEAD_DIM), lambda i, t: (0, i, 0, t, 0))
    return pl.pallas_call(
        _nsa_in_proj_kernel,
        grid=(b, s // tm),
        in_specs=[
            pl.BlockSpec((sq, tm, d), lambda i, t: (i, t, 0)),
            pl.BlockSpec((1, d), lambda i, t: (0, 0)),
            pl.BlockSpec((d, NSA_IN_PAD), lambda i, t: (0, 0)),
            pl.BlockSpec((2, NSA_KV, d), lambda i, t: (0, 0, 0)),
        ],
        out_specs=[
            pl.BlockSpec((sq, tm, NSA_Q), lambda i, t: (i, t, 0)),
            grouped_spec, grouped_spec,
            pl.BlockSpec((2, sq, N_GROUPS, HEAD_DIM, tm), lambda i, t: (0, i, 0, 0, t)),
            pl.BlockSpec((sq, tm, LANES), lambda i, t: (i, t, 0)),
        ],
        out_shape=[
            jax.ShapeDtypeStruct((b, s, NSA_Q), BF16),
            grouped(BF16), grouped(BF16),
            jax.ShapeDtypeStruct((2, b, N_GROUPS, HEAD_DIM, s), BF16),
            jax.ShapeDtypeStruct((b, s, LANES), F32),
        ],
        compiler_params=_cparams("parallel", "parallel"),
        name="nsa_in_proj",
    )(x, g.reshape(1, d), w, wvt)


def _compress_kernel(r_ref, pos_ref, w1_ref, b1_ref, w2_ref, b2_ref, w2t_ref, b2t_ref, o_ref, ot_ref,
                     bias_ref):
    half = CMP_STRIDE * HEAD_DIM
    rows = r_ref[...]
    part_lo = jnp.dot(rows, w1_ref[:half, :], preferred_element_type=F32)
    part_hi = jnp.dot(rows, w1_ref[half:, :], preferred_element_type=F32)
    @pl.when((pl.program_id(1) == 0) & (pl.program_id(2) == 0))
    def _():
        bias_ref[...] = b1_ref[...] + jnp.dot(
            pos_ref[...], w1_ref[...].astype(F32), precision=lax.Precision.HIGHEST,
            preferred_element_type=F32)

    hid = part_lo + pltpu.roll(part_hi, N_CMP_PAD - 1, axis=0) + bias_ref[...]
    hid = jax.nn.gelu(hid, approximate=True).astype(BF16)
    o_ref[...] = (jnp.dot(hid, w2_ref[...], preferred_element_type=F32) + b2_ref[...]).astype(o_ref.dtype)
    ot_ref[...] = (_nt_dot(w2t_ref[...], hid) + b2t_ref[...]).astype(ot_ref.dtype)


def _compress(raw, pos, w1, b1, w2, b2):
    two, b, g, n, width = raw.shape
    hidden = w1.shape[-1]
    sq = pl.Squeezed()
    per_branch = lambda *shape: pl.BlockSpec((sq,) + shape, lambda c, i, j: (c,) + (0,) * len(shape))
    return pl.pallas_call(
        _compress_kernel,
        grid=(two, b, g),
        in_specs=[
            pl.BlockSpec((sq, sq, sq, n, width), lambda c, i, j: (c, i, j, 0, 0)),
            per_branch(1, 2 * width), per_branch(2 * width, hidden), per_branch(1, hidden),
            per_branch(hidden, HEAD_DIM), per_branch(1, HEAD_DIM),
            per_branch(HEAD_DIM, hidden), per_branch(HEAD_DIM, 1),
        ],
        out_specs=[
            pl.BlockSpec((sq, sq, sq, n, HEAD_DIM), lambda c, i, j: (c, i, j, 0, 0)),
            pl.BlockSpec((sq, sq, sq, HEAD_DIM, n), lambda c, i, j: (c, i, j, 0, 0)),
        ],
        out_shape=[
            jax.ShapeDtypeStruct((two, b, g, n, HEAD_DIM), BF16),
            jax.ShapeDtypeStruct((two, b, g, HEAD_DIM, n), BF16),
        ],
        scratch_shapes=[pltpu.VMEM((1, hidden), F32)],
        compiler_params=_cparams("arbitrary", "arbitrary", "arbitrary"),
        name="nsa_compress",
    )(raw, pos, w1, b1.reshape(two, 1, hidden), w2, b2.reshape(two, 1, HEAD_DIM),
      w2.transpose(0, 2, 1), b2.reshape(two, HEAD_DIM, 1))


AUG = 256
SEL_ROW0 = HEAD_DIM
POS_ROW0 = 2 * HEAD_DIM
N_PIECES = 3
PAD_LANE = POS_ROW0 + 2 * N_PIECES
POS_ROWS = 16
V_ROWS = 80
NQ = HEADS_PER_GROUP * TQ
BIG = 1e30
LOG2E = 1.4426950408889634
assert TQ == TK_WIN and TK_SEL % TQ == 0 and WINDOW % TK_WIN == 0


def _flash_reset(m_ref, acc_ref):
    m_ref[...] = jnp.full_like(m_ref, NEG)
    acc_ref[...] = jnp.zeros_like(acc_ref)


def _scores(kaug, tile, tk, qat_ref):
    k0 = pl.multiple_of(tile * tk, tk)
    return jnp.dot(kaug[pl.ds(k0, tk), :], qat_ref[...], preferred_element_type=F32)


def _flash_update_steps(s, vt_tile, mask, m_ref, acc_ref):
    if mask is not None:
        s = jnp.where(mask, s, NEG)
    m_old = m_ref[...]
    m_new = jnp.maximum(m_old, jnp.max(s, axis=0, keepdims=True))
    yield
    p = jnp.exp2(s - m_new).astype(BF16)
    yield
    acc_ref[...] = jnp.exp2(m_old - m_new) * acc_ref[...] + jnp.dot(
        vt_tile, p, preferred_element_type=F32)
    m_ref[...] = m_new


def _flash_update(*args):
    for _ in _flash_update_steps(*args):
        pass


def _round_robin(generators):
    generators = list(generators)
    while generators:
        generators = [g for g in generators if next(g, True) is not True]


def _flash_result(acc_ref):
    return acc_ref[pl.ds(0, HEAD_DIM), :] * (1.0 / acc_ref[pl.ds(HEAD_DIM, 1), :])


GROUPS_PER_STEP = 2


def _nsa_kernel(sl_ref, q_ref, kc_all, vct_all, ks_all, vst_all, kw_all, vwt_all, gl_ref, bg_ref,
                constk_ref, constc_ref, o_ref, *scratch):
    qi = pl.program_id(2)
    gt_ref = scratch[-1]
    per_group = (kc_all, vct_all, ks_all, vst_all, kw_all, vwt_all) + scratch[:-1]
    groups = [tuple(ref.at[h] for ref in per_group) for h in range(GROUPS_PER_STEP)]

    @pl.when(qi == 0)
    def _():
        for refs in groups:
            _nsa_assemble(constk_ref, constc_ref, *refs)

    gt_ref[...] = jax.nn.sigmoid(gl_ref[...] + bg_ref[...]).T
    tiles = [_nsa_tile(pl.program_id(1) * GROUPS_PER_STEP + h, h * HEADS_PER_GROUP * HEAD_DIM, qi,
                       sl_ref, q_ref, o_ref, gt_ref, *refs) for h, refs in enumerate(groups)]
    heads = list(tiles)
    while heads:
        heads = [tile for tile in heads if not next(tile)]
    ends = [next(tile) for tile in tiles]
    for parities in itertools.product((0, 1), repeat=len(tiles)):
        @pl.when(functools.reduce(jnp.logical_and, [odd == want for (odd, _), want in zip(ends, parities)]))
        def _():
            _round_robin(last(want) for (_, last), want in zip(ends, parities))
    for tile in tiles:
        next(tile, None)


def _nsa_assemble(constk_ref, constc_ref, kc_ref, vct_ref, ks_ref, vst_ref, kw_ref, vwt_ref,
                  kaug_s, kaug_w, kaug_c, vt_s, vt_w, vt_c, *unused):
    s_len = ks_ref.shape[0]
    for kaug, row0, const, k in ((kaug_s, 0, constk_ref, ks_ref), (kaug_w, WINDOW, constk_ref, kw_ref),
                                 (kaug_c, 0, constc_ref, kc_ref)):
        kaug[pl.ds(row0, k.shape[0]), :] = const[...]
        kaug[pl.ds(row0, k.shape[0]), pl.ds(0, HEAD_DIM)] = k[...]
    pad_lane = lax.broadcasted_iota(jnp.int32, (WINDOW, AUG), 1) == PAD_LANE
    kaug_w[pl.ds(0, WINDOW), :] = jnp.where(pad_lane, 1.0, 0.0).astype(BF16)
    vt_w[:, pl.ds(0, WINDOW)] = jnp.zeros((V_ROWS, WINDOW), BF16)
    pad = V_ROWS - HEAD_DIM
    ones_rows = lambda n: jnp.where(
        lax.broadcasted_iota(jnp.int32, (pad, n), 0) == 0, 1.0, 0.0).astype(BF16)
    for c in range(vt_s.shape[0]):
        vt_s[c, pl.ds(0, HEAD_DIM), :] = vst_ref[:, pl.ds(c * TK_SEL, TK_SEL)]
        vt_s[c, pl.ds(HEAD_DIM, pad), :] = ones_rows(TK_SEL)
    vt_w[pl.ds(0, HEAD_DIM), pl.ds(WINDOW, s_len)] = vwt_ref[...]
    vt_w[pl.ds(HEAD_DIM, pad), pl.ds(WINDOW, s_len)] = ones_rows(s_len)
    vt_c[pl.ds(0, HEAD_DIM), :] = vct_ref[...]
    vt_c[pl.ds(HEAD_DIM, pad), :] = ones_rows(N_CMP_PAD)


def _nsa_tile(grp, lane0, qi, sl_ref, q_ref, o_ref, gt_ref, kc_ref, vct_ref, ks_ref, vst_ref, kw_ref,
              vwt_ref, kaug_s, kaug_w, kaug_c, vt_s, vt_w, vt_c, qat_ref, score_ref, m_ref, acc_ref,
              out_ref, sa_ref, sb_ref, qats_ref, tiles_ref):
    q0 = qi * TQ
    nrep = HEADS_PER_GROUP

    for half in range(nrep // 2):
        qt = q_ref[:, pl.ds(lane0 + half * LANES, LANES)].astype(F32).T * (HEAD_DIM ** -0.5 * LOG2E)
        for sub in range(2):
            q_rows = qt[sub * HEAD_DIM:(sub + 1) * HEAD_DIM].astype(BF16)
            for ref in (qat_ref, qats_ref):
                ref[pl.ds(0, HEAD_DIM), pl.ds((2 * half + sub) * TQ, TQ)] = q_rows
    qat_ref[pl.ds(SEL_ROW0, HEAD_DIM), :] = jnp.zeros((HEAD_DIM, NQ), BF16)
    piece = lax.broadcasted_iota(jnp.int32, (POS_ROWS, TQ), 0)
    tail0 = POS_ROW0 + POS_ROWS
    for r in range(nrep):
        tile = jnp.where(piece == PAD_LANE - POS_ROW0, -BIG, 0.0)
        for i in range(2 * N_PIECES):
            tile = jnp.where(piece == i, sl_ref[(grp * nrep + r) * N_PIECES + i % N_PIECES], tile)
        for ref in (qat_ref, qats_ref):
            ref[pl.ds(POS_ROW0, POS_ROWS), pl.ds(r * TQ, TQ)] = tile.astype(BF16)
    for ref in (qat_ref, qats_ref):
        ref[pl.ds(tail0, AUG - tail0), :] = jnp.zeros((AUG - tail0, NQ), BF16)

    yield False
    t_row = q0 + (lax.broadcasted_iota(jnp.int32, (1, NQ), 1) & (TQ - 1))
    gate_row = lambda br: jnp.concatenate(
        [gt_ref[pl.ds(3 * (grp * nrep + r) + br, 1), :] for r in range(nrep)], axis=1)

    s = jnp.dot(kaug_c[...], qat_ref[...], preferred_element_type=F32)
    yield False
    cmp_end = CMP_STRIDE * lax.broadcasted_iota(jnp.int32, (N_CMP_PAD, 1), 0) + (L_CMP - 1)
    mask_c = cmp_end <= t_row
    s = jnp.where(mask_c, s, NEG)
    e = jnp.where(mask_c, jnp.exp2(s - jnp.max(s, axis=0, keepdims=True)), 0.0)
    d = jnp.sum(e, axis=0, keepdims=True)
    p = e * (1.0 / jnp.where(d > 0, d, 1.0))
    yield False
    p_sum = sum(p[:, r * TQ:(r + 1) * TQ] for r in range(nrep))
    o_c = jnp.dot(vt_c[...], p.astype(BF16), preferred_element_type=F32)
    out_ref[...] = gate_row(0) * o_c[:HEAD_DIM]
    yield False

    n_sb = score_ref.shape[0]
    jb = lax.broadcasted_iota(jnp.int32, (n_sb, TQ), 0)
    t_sel = q0 + lax.broadcasted_iota(jnp.int32, (n_sb, TQ), 1)
    valid = jb * L_SEL <= t_sel

    n_full = q0 // TK_SEL
    blocks_per_tile = TK_SEL // L_SEL

    def write_selection(selected):
        sel_bias = jnp.where(selected, 0.0, -BIG).astype(BF16)
        for r in range(nrep):
            qats_ref[pl.ds(SEL_ROW0, n_sb), pl.ds(r * TQ, TQ)] = sel_bias
        picked = jnp.where(selected, 1.0, 0.0)
        bits = sum(jnp.max(picked[k * blocks_per_tile:(k + 1) * blocks_per_tile], axis=(0, 1),
                           keepdims=True) * float(2 ** k) for k in range(n_sb // blocks_per_tile))
        bits = bits[0, 0].astype(jnp.int32)
        count = jnp.int32(0)
        for k in range(n_sb // blocks_per_tile):
            tiles_ref[count] = k
            count = count + jnp.where((k < n_full) & (((bits >> k) & 1) == 1), 1, 0)
        tiles_ref[count] = n_full
        return count

    def rank_blocks():
        cmp_start = CMP_STRIDE * lax.broadcasted_iota(jnp.int32, (n_sb, N_CMP_PAD), 1)
        sb_start = L_SEL * lax.broadcasted_iota(jnp.int32, (n_sb, N_CMP_PAD), 0)
        overlap_t = ((cmp_start < sb_start + L_SEL) & (cmp_start + L_CMP > sb_start)).astype(F32)
        imp_t = jnp.dot(overlap_t, p_sum, precision=lax.Precision.HIGHEST,
                        preferred_element_type=F32)
        cur = t_sel // L_SEL
        forced = (jb == 0) | (jb == cur) | (jb == cur - 1)
        score = jnp.where(valid, jnp.where(forced, jnp.inf, imp_t), -jnp.inf)
        score_ref[...] = score
        chunks = [score[c * SUBLANES:(c + 1) * SUBLANES] for c in range(n_sb // SUBLANES)]
        ranks = [jnp.zeros((SUBLANES, TQ), F32) for _ in chunks]
        sub_row = lax.broadcasted_iota(jnp.int32, (SUBLANES, TQ), 0)
        for i in range(n_sb):
            row = score_ref[pl.ds(i, 1), :]
            for c, chunk in enumerate(chunks):
                later = jnp.where(row >= chunk, 1.0, 0.0)
                earlier = jnp.where(row > chunk, 1.0, 0.0)
                if i < c * SUBLANES:
                    ranks[c] = ranks[c] + later
                elif i >= (c + 1) * SUBLANES:
                    ranks[c] = ranks[c] + earlier
                else:
                    ranks[c] = ranks[c] + jnp.where(sub_row > i - c * SUBLANES, later, earlier)
        rank = jnp.concatenate(ranks, axis=0)
        return write_selection(valid & (rank < float(N_SELECT)))

    n_win = WINDOW // TK_WIN
    k0 = pl.multiple_of(q0, TK_WIN)
    s_all = jnp.dot(kaug_w[pl.ds(k0, WINDOW + TQ), :], qat_ref[...], preferred_element_type=F32)
    yield False
    n_list = rank_blocks()
    yield False

    _flash_reset(m_ref, acc_ref)
    sel_scores = lambda i: _scores(kaug_s, tiles_ref[i], TK_SEL, qats_ref)
    sel_update = lambda s_ref, i, mask: _flash_update(s_ref[...], vt_s[tiles_ref[i]], mask, m_ref, acc_ref)
    sa_ref[...] = sel_scores(0)
    yield False

    s_w = [s_all[j * TK_WIN:(j + 1) * TK_WIN] for j in range(n_win + 1)]
    not_after = q0 + lax.broadcasted_iota(jnp.int32, (TK_WIN, 1), 0) <= t_row
    s_w[0] = jnp.where(not_after, NEG, s_w[0])
    s_w[-1] = jnp.where(not_after, s_w[-1], NEG)
    m_w = functools.reduce(jnp.maximum, [jnp.max(x, axis=0, keepdims=True) for x in s_w])
    p_w = jnp.concatenate([jnp.exp2(x - m_w).astype(BF16) for x in s_w], axis=0)
    yield False
    acc_w = jnp.dot(vt_w[:, pl.ds(k0, WINDOW + TQ)], p_w, preferred_element_type=F32)
    out_ref[...] += gate_row(2) * (acc_w[:HEAD_DIM] * (1.0 / acc_w[HEAD_DIM:HEAD_DIM + 1]))
    yield True

    def sel_pair(j, carry):
        i = 2 * j
        sb_ref[...] = sel_scores(i + 1)
        sel_update(sa_ref, i, None)
        sa_ref[...] = sel_scores(i + 2)
        sel_update(sb_ref, i + 1, None)
        return carry

    lax.fori_loop(0, n_list // 2, sel_pair, 0)

    def last_tiles(odd):
        own = n_list
        own_rows = pl.ds(pl.multiple_of(q0 - n_full * TK_SEL, TQ), TQ)
        if odd:
            sb_ref[...] = sel_scores(own)
            yield
            yield from _flash_update_steps(sa_ref[...], vt_s[tiles_ref[own - 1]], None, m_ref, acc_ref)
            yield
        own_ref = sb_ref if odd else sa_ref
        own_ref[own_rows, :] = jnp.where(not_after, own_ref[own_rows, :], NEG)
        yield from _flash_update_steps(own_ref[...], vt_s[tiles_ref[own]], None, m_ref, acc_ref)

    yield n_list % 2, last_tiles

    out_ref[...] += gate_row(1) * _flash_result(acc_ref)

    for half in range(nrep // 2):
        slab = jnp.concatenate([out_ref[:, pl.ds(2 * half * TQ, TQ)],
                                out_ref[:, pl.ds((2 * half + 1) * TQ, TQ)]], axis=0)
        o_ref[:, pl.ds(lane0 + half * LANES, LANES)] = slab.T.astype(o_ref.dtype)


def _nsa_attention(slope_pieces, q, cmp, cmp_t, keys, vals_t, glogit, bgate, constk, constc):
    b, s, _ = q.shape
    g, nrep, dh = N_GROUPS, HEADS_PER_GROUP, HEAD_DIM
    n_sb = s // L_SEL
    sq = pl.Squeezed()
    gp = GROUPS_PER_STEP
    slot = lambda c, *shape: pl.BlockSpec((sq, sq, gp) + shape, lambda i, j, t: (c, i, j, 0, 0))
    full = lambda *shape: pl.BlockSpec(shape, lambda i, j, t: (0,) * len(shape))
    per_group = lambda shape, dtype: pltpu.VMEM((gp,) + shape, dtype)
    return pl.pallas_call(
        _nsa_kernel,
        grid=(b, g // gp, s // TQ),
        in_specs=[
            pl.BlockSpec(memory_space=pltpu.SMEM),
            pl.BlockSpec((sq, TQ, gp * nrep * dh), lambda i, j, t: (i, t, j)),
            slot(0, N_CMP_PAD, dh), slot(1, dh, N_CMP_PAD),
            slot(0, s, dh), slot(0, dh, s), slot(1, s, dh), slot(1, dh, s),
            pl.BlockSpec((sq, TQ, LANES), lambda i, j, t: (i, t, 0)),
            full(1, LANES), full(s, AUG), full(N_CMP_PAD, AUG),
        ],
        out_specs=pl.BlockSpec((sq, TQ, gp * nrep * dh), lambda i, j, t: (i, t, j)),
        out_shape=jax.ShapeDtypeStruct((b, s, NSA_Q), BF16),
        scratch_shapes=[
            per_group((s, AUG), BF16), per_group((WINDOW + s, AUG), BF16), per_group((N_CMP_PAD, AUG), BF16),
            per_group((s // TK_SEL, V_ROWS, TK_SEL), BF16), per_group((V_ROWS, WINDOW + s), BF16),
            per_group((V_ROWS, N_CMP_PAD), BF16),
            per_group((AUG, NQ), BF16), per_group((n_sb, TQ), F32), per_group((1, NQ), F32),
            per_group((V_ROWS, NQ), F32), per_group((dh, NQ), F32),
            per_group((TK_SEL, NQ), F32), per_group((TK_SEL, NQ), F32),
            per_group((AUG, NQ), BF16),
            pltpu.SMEM((gp, s // TK_SEL + 1), jnp.int32),
            pltpu.VMEM((LANES, TQ), F32),
        ],
        compiler_params=_cparams("arbitrary", "arbitrary", "arbitrary"),
        name="nsa_attention",
    )(slope_pieces, q, cmp, cmp_t, keys, vals_t, keys, vals_t, glogit, bgate, constk, constc)


def _position_pieces(pos):
    hi = (pos // L_SEL) * L_SEL
    return jnp.stack([hi] * N_PIECES + [pos - hi] * N_PIECES, axis=1).astype(F32)


def _nsa_constants(s):
    pos = jnp.arange(s)
    constk = jnp.zeros((s, AUG), F32)
    constk = constk.at[:, SEL_ROW0:SEL_ROW0 + s // L_SEL].set(jax.nn.one_hot(pos // L_SEL, s // L_SEL))
    constk = constk.at[:, POS_ROW0:POS_ROW0 + 2 * N_PIECES].set(_position_pieces(pos))
    cmp_end = CMP_STRIDE * jnp.arange(N_CMP_PAD) + (L_CMP - 1)
    constc = jnp.zeros((N_CMP_PAD, AUG), F32)
    constc = constc.at[:, POS_ROW0:POS_ROW0 + 2 * N_PIECES].set(_position_pieces(cmp_end))
    slopes = 2.0 ** (-8.0 * jnp.arange(1, N_HEADS + 1, dtype=F32) / N_HEADS)
    pieces, rest = [], slopes * LOG2E
    for _ in range(N_PIECES):
        piece = rest.astype(BF16).astype(F32)
        pieces.append(piece)
        rest = rest - piece
    return constk.astype(BF16), constc.astype(BF16), jnp.stack(pieces, axis=1).reshape(-1)


def _nsa_mixer(x, norm_g, w_in, b_gate, cmp_pos, cmp_w1, cmp_b1, cmp_w2, cmp_b2):
    b, s, d = x.shape
    g, nrep, dh = N_GROUPS, HEADS_PER_GROUP, HEAD_DIM
    w_in_p = jnp.pad(w_in, ((0, 0), (0, NSA_IN_PAD - NSA_IN))).astype(BF16)
    v_cols = lambda i: w_in_p[:, NSA_Q + i * NSA_KV:NSA_Q + (i + 1) * NSA_KV].T
    q, raw, keys, vals_t, glogit = _nsa_in_proj(x, norm_g, w_in_p, jnp.stack([v_cols(3), v_cols(5)]),
                                                tm=512)
    raw = raw.reshape(2, b, g, s // CMP_STRIDE, CMP_STRIDE * dh)
    cmp, cmp_t = _compress(raw, cmp_pos.reshape(2, 1, L_CMP * dh), cmp_w1.astype(BF16), cmp_b1,
                           cmp_w2.astype(BF16), cmp_b2)
    constk, constc, slope_pieces = _nsa_constants(s)
    bgate = jnp.pad(b_gate, (0, LANES - 3 * N_HEADS)).reshape(1, LANES)
    return _nsa_attention(slope_pieces, q, cmp, cmp_t, keys, vals_t, glogit, bgate, constk, constc)


TT = 256
SUB = SUBLANES


GATE_BAND = 256


def _gate_bands():
    block = D_RNN // LRU_BLOCKS
    bands = []
    for c0 in range(0, D_RNN, GATE_BAND):
        width = min(GATE_BAND, D_RNN - c0)
        k_lo = (c0 // block) * block // LANES * LANES
        k_hi = -(-(((c0 + width - 1) // block + 1) * block) // LANES) * LANES
        bands.append((k_lo, min(k_hi, D_RNN), c0, width))
    return tuple(bands)


GATE_BANDS = _gate_bands()


def _pack_gate_weights(w_a, w_x):
    dense = lambda w: jax.scipy.linalg.block_diag(*[w[i] for i in range(LRU_BLOCKS)])
    slabs = []
    for k_lo, k_hi, c0, width in GATE_BANDS:
        band = [jnp.pad(dense(w)[k_lo:k_hi, c0:c0 + width], ((0, 0), (0, GATE_BAND - width)))
                for w in (w_a, w_x)]
        slabs.append(jnp.concatenate(band, axis=1))
    return jnp.concatenate(slabs, axis=0).astype(BF16)


SEQS_PER_STEP = 2


def _lru_kernel(x_all, g_ref, win_ref, cw_ref, cb_ref, wax_ref, bax_ref, lam_ref, wout_ref, o_all,
                *scratch):
    seqs = [tuple(ref.at[h] for ref in (x_all, o_all) + scratch) for h in range(SEQS_PER_STEP)]

    @pl.when(pl.program_id(1) == 0)
    def _():
        for x_ref, o_ref, ext_ref, a_ref, u_ref, h_ref, gate_ref in seqs:
            ext_ref[pl.ds(TT, SUB), :] = jnp.zeros((SUB, D_RNN), F32)
            h_ref[...] = jnp.zeros_like(h_ref)

    tiles = [_lru_tile(g_ref, win_ref, cw_ref, cb_ref, wax_ref, bax_ref, lam_ref, wout_ref, *refs)
             for refs in seqs]
    while tiles:
        tiles = [tile for tile in tiles if next(tile, True) is not True]


def _lru_tile(g_ref, win_ref, cw_ref, cb_ref, wax_ref, bax_ref, lam_ref, wout_ref,
              x_ref, o_ref, ext_ref, a_ref, u_ref, h_ref, gate_ref):
    xn = _rms(x_ref[...], g_ref[...]).astype(BF16)
    gate_ref[...] = _gelu_tanh(jnp.dot(xn, win_ref[:, :D_RNN], preferred_element_type=F32))
    ext_ref[pl.ds(0, SUB), :] = ext_ref[pl.ds(TT, SUB), :]
    ext_ref[pl.ds(SUB, TT), :] = jnp.dot(xn, win_ref[:, D_RNN:], preferred_element_type=F32)
    yield
    cw = cw_ref[...]
    xr = cb_ref[...] + sum(
        cw[w:w + 1, :] * ext_ref[pl.ds(SUB - (CONV_W - 1) + w, TT), :] for w in range(CONV_W))

    lam = lam_ref[...]
    softplus_neg = jnp.maximum(-lam, 0.0) + jnp.log1p(jnp.exp(-jnp.abs(lam)))
    half_log_scale = (-0.5 * LRU_C) * softplus_neg
    xr_b = xr.astype(BF16)
    half_bias = bax_ref[...]
    row0 = 0
    for k_lo, k_hi, c0, width in GATE_BANDS:
        z = jnp.dot(xr_b[:, k_lo:k_hi], wax_ref[row0:row0 + k_hi - k_lo, :], preferred_element_type=F32)
        row0 += k_hi - k_lo
        tanh_r = jnp.tanh(z[:, :width] + half_bias[:, c0:c0 + width])
        tanh_i = jnp.tanh(z[:, GATE_BAND:GATE_BAND + width] + half_bias[:, D_RNN + c0:D_RNN + c0 + width])
        scale = half_log_scale[:, c0:c0 + width]
        a = jnp.exp(scale * tanh_r + scale)
        a_ref[:, c0:c0 + width] = a
        half_xr = 0.5 * xr[:, c0:c0 + width]
        u_ref[:, c0:c0 + width] = jnp.sqrt(jnp.maximum(1.0 - a * a, 0.0)) * (half_xr * tanh_i + half_xr)
    yield

    row = lax.broadcasted_iota(jnp.int32, (SUB, D_RNN), 0)

    def scan_rows(c, h_prev):
        r0 = c * SUB
        a_c = a_ref[pl.ds(r0, SUB), :]
        u_c = u_ref[pl.ds(r0, SUB), :]
        shift = 1
        while shift < SUB:
            keep = row >= shift
            u_c = u_c + a_c * jnp.where(keep, pltpu.roll(u_c, shift, axis=0), 0.0)
            a_c = a_c * jnp.where(keep, pltpu.roll(a_c, shift, axis=0), 1.0)
            shift *= 2
        h_c = u_c + a_c * h_prev
        u_ref[pl.ds(r0, SUB), :] = h_c
        return jnp.broadcast_to(h_c[SUB - 1:SUB, :], (SUB, D_RNN))

    h_state = h_ref[...]
    for c in range(TT // SUB):
        h_state = scan_rows(c, h_state)
        if c % (TT // SUB // 4) == TT // SUB // 4 - 1:
            yield
    h_ref[...] = h_state
    gated = (u_ref[...] * gate_ref[...]).astype(BF16)
    o_ref[...] = x_ref[...] + jnp.dot(gated, wout_ref[...], preferred_element_type=F32)


def _lru_mixer(x, norm_g, w_in, conv_w, conv_b, w_a, b_a, w_x, b_x, lam, w_out):
    b, s, d = x.shape
    wax = _pack_gate_weights(0.5 * w_a, 0.5 * w_x)
    bax = 0.5 * jnp.concatenate([b_a, b_x])
    nseq = SEQS_PER_STEP
    resident = lambda *shape: pl.BlockSpec(shape, lambda i, t: (0,) * len(shape),
                                           pipeline_mode=pl.Buffered(1))
    per_seq = lambda rows: pltpu.VMEM((nseq, rows, D_RNN), F32)
    return pl.pallas_call(
        _lru_kernel,
        grid=(b // nseq, s // TT),
        in_specs=[
            pl.BlockSpec((nseq, TT, d), lambda i, t: (i, t, 0)),
            resident(1, d), resident(d, 2 * D_RNN), resident(CONV_W, D_RNN), resident(1, D_RNN),
            resident(*wax.shape), resident(1, 2 * D_RNN), resident(1, D_RNN), resident(D_RNN, d),
        ],
        out_specs=pl.BlockSpec((nseq, TT, d), lambda i, t: (i, t, 0)),
        out_shape=jax.ShapeDtypeStruct((b, s, d), F32),
        scratch_shapes=[per_seq(TT + SUB), per_seq(TT), per_seq(TT), per_seq(SUB), per_seq(TT)],
        compiler_params=_cparams("parallel", "arbitrary"),
        name="lru_block",
    )(x, norm_g.reshape(1, d), w_in.astype(BF16), conv_w, conv_b.reshape(1, -1), wax,
      bax.reshape(1, -1), lam.reshape(1, -1), w_out.astype(BF16))


def kernel(x, norm_mix, norm_ffn, norm_final, nsa_w_in, nsa_b_gate, nsa_cmp_pos, nsa_cmp_w1,
           nsa_cmp_b1, nsa_cmp_w2, nsa_cmp_b2, nsa_w_out, lru_w_in, lru_conv_w, lru_conv_b,
           lru_w_a, lru_b_a, lru_w_x, lru_b_x, lru_lambda, lru_w_out, ffn_w_in, ffn_w_out):
    b, s, d = x.shape
    ffn = functools.partial(_ffn, tm=512, tf=256)
    attn = _nsa_mixer(x, norm_mix[0], nsa_w_in[0], nsa_b_gate[0], nsa_cmp_pos[0], nsa_cmp_w1[0],
                      nsa_cmp_b1[0], nsa_cmp_w2[0], nsa_cmp_b2[0])
    x = ffn(x.reshape(b * s, d), norm_ffn[0], ffn_w_in[0].astype(BF16), ffn_w_out[0].astype(BF16),
            norm_final, final_norm=False, name="ffn0", mix=attn.reshape(b * s, NSA_Q),
            w_mix=nsa_w_out[0].astype(BF16)).reshape(b, s, d)
    x = _lru_mixer(x, norm_mix[1], lru_w_in[0], lru_conv_w[0], lru_conv_b[0], lru_w_a[0], lru_b_a[0],
                   lru_w_x[0], lru_b_x[0], lru_lambda[0], lru_w_out[0])
    x = ffn(x.reshape(b * s, d), norm_ffn[1], ffn_w_in[1].astype(BF16), ffn_w_out[1].astype(BF16),
            norm_final, final_norm=True, name="ffn1").reshape(b, s, d)
    return x
```

```python
import functools
import itertools

import jax
import jax.numpy as jnp
from jax import lax
from jax.experimental import pallas as pl
from jax.experimental.pallas import tpu as pltpu

F32 = jnp.float32
BF16 = jnp.bfloat16

D_MODEL = 1024
N_HEADS = 16
N_GROUPS = 4
HEADS_PER_GROUP = N_HEADS // N_GROUPS
HEAD_DIM = 64
L_CMP = 32
CMP_STRIDE = 16
L_SEL = 64
N_SELECT = 16
WINDOW = 512
NSA_Q = N_HEADS * HEAD_DIM
NSA_KV = N_GROUPS * HEAD_DIM
NSA_IN = NSA_Q + 6 * NSA_KV + 3 * N_HEADS
D_RNN = 1408
LRU_BLOCKS = 8
LRU_C = 8.0
CONV_W = 4
D_FF = 2816
EPS = 1e-6

LANES = 128
SUBLANES = 8
NEG = -1e30
VMEM_LIMIT = 56 * 1024 * 1024

TQ = 256
TK_SEL = 512
TK_WIN = 256
N_CMP_PAD = 256


def _cparams(*sem):
    return pltpu.CompilerParams(dimension_semantics=sem, vmem_limit_bytes=VMEM_LIMIT)


def _rms(x, g):
    ms = jnp.mean(x * x, axis=-1, keepdims=True)
    return x * lax.rsqrt(ms + EPS) * g


def _gelu_tanh(x):
    c = (2.0 / jnp.pi) ** 0.5
    half_x = 0.5 * x
    return half_x * jnp.tanh(x * (c + (c * 0.044715) * (x * x))) + half_x


def _nt_dot(a, b):
    return lax.dot_general(a, b, (((1,), (1,)), ((), ())), preferred_element_type=F32)


def _ffn_kernel(*refs, final_norm, tf, has_mix):
    x_ref, refs = refs[0], refs[1:]
    x = x_ref[...]
    if has_mix:
        (mix_ref, wmix_ref), refs = refs[:2], refs[2:]
        x = x + jnp.dot(mix_ref[...], wmix_ref[...], preferred_element_type=F32)
    g_ref, win_ref, wout_ref, gf_ref, o_ref, hid_ref = refs
    xn = _rms(x, g_ref[...]).astype(BF16)
    for c0 in range(0, D_FF, tf):
        gate = jnp.dot(xn, win_ref[:, c0:c0 + tf], preferred_element_type=F32)
        up = jnp.dot(xn, win_ref[:, D_FF + c0:D_FF + c0 + tf], preferred_element_type=F32)
        hid_ref[:, c0:c0 + tf] = ((0.5 * gate) * (1.0 + jnp.tanh(0.5 * gate)) * up).astype(BF16)
    y = x + jnp.dot(hid_ref[...], wout_ref[...], preferred_element_type=F32)
    if final_norm:
        y = _rms(y, gf_ref[...])
    o_ref[...] = y


def _ffn(x, g, w_in, w_out, g_final, *, final_norm, tm, tf, name, mix=None, w_mix=None):
    m, d = x.shape
    resident = lambda *shape: pl.BlockSpec(shape, lambda i: (0,) * len(shape),
                                           pipeline_mode=pl.Buffered(1))
    rows = lambda width: pl.BlockSpec((tm, width), lambda i: (i, 0))
    if mix is None:
        mix_specs, mix_args = [], ()
    else:
        mix_specs, mix_args = [rows(mix.shape[1]), resident(*w_mix.shape)], (mix, w_mix)
    return pl.pallas_call(
        functools.partial(_ffn_kernel, final_norm=final_norm, tf=tf, has_mix=mix is not None),
        grid=(m // tm,),
        in_specs=[rows(d)] + mix_specs + [
            resident(1, d), resident(d, 2 * D_FF), resident(D_FF, d), resident(1, d)],
        out_specs=rows(d),
        out_shape=jax.ShapeDtypeStruct((m, d), F32),
        scratch_shapes=[pltpu.VMEM((tm, D_FF), BF16)],
        compiler_params=_cparams("parallel"),
        name=name,
    )(x, *mix_args, g.reshape(1, d), w_in, w_out, g_final.reshape(1, d))


NSA_IN_PAD = -(-NSA_IN // LANES) * LANES


def _nsa_in_proj_kernel(x_ref, g_ref, w_ref, wvt_ref, q_ref, raw_ref, k_ref, vt_ref, gl_ref):
    xn = _rms(x_ref[...], g_ref[...]).astype(BF16)
    q_ref[...] = jnp.dot(xn, w_ref[:, :NSA_Q], preferred_element_type=F32).astype(q_ref.dtype)
    for i, (dst, slot) in {0: (raw_ref, 0), 1: (raw_ref, 1), 2: (k_ref, 0), 4: (k_ref, 1)}.items():
        col0 = NSA_Q + i * NSA_KV
        part = jnp.dot(xn, w_ref[:, col0:col0 + NSA_KV], preferred_element_type=F32)
        for grp in range(N_GROUPS):
            dst[slot, grp] = part[:, grp * HEAD_DIM:(grp + 1) * HEAD_DIM].astype(dst.dtype)
    for slot in range(2):
        part_t = _nt_dot(wvt_ref[slot], xn)
        for grp in range(N_GROUPS):
            vt_ref[slot, grp] = part_t[grp * HEAD_DIM:(grp + 1) * HEAD_DIM].astype(vt_ref.dtype)
    gl_ref[...] = jnp.dot(xn, w_ref[:, NSA_Q + 6 * NSA_KV:], preferred_element_type=F32)


def _nsa_in_proj(x, g, w, wvt, *, tm):
    b, s, d = x.shape
    sq = pl.Squeezed()
    grouped = lambda dtype: jax.ShapeDtypeStruct((2, b, N_GROUPS, s, HEAD_DIM), dtype)
    grouped_spec = pl.BlockSpec((2, sq, N_GROUPS, tm, HEAD_DIM), lambda i, t: (0, i, 0, t, 0))
    return pl.pallas_call(
        _nsa_in_proj_kernel,
        grid=(b, s // tm),
        in_specs=[
            pl.BlockSpec((sq, tm, d), lambda i, t: (i, t, 0)),
            pl.BlockSpec((1, d), lambda i, t: (0, 0)),
            pl.BlockSpec((d, NSA_IN_PAD), lambda i, t: (0, 0)),
            pl.BlockSpec((2, NSA_KV, d), lambda i, t: (0, 0, 0)),
        ],
        out_specs=[
            pl.BlockSpec((sq, tm, NSA_Q), lambda i, t: (i, t, 0)),
            grouped_spec, grouped_spec,
            pl.BlockSpec((2, sq, N_GROUPS, HEAD_DIM, tm), lambda i, t: (0, i, 0, 0, t)),
            pl.BlockSpec((sq, tm, LANES), lambda i, t: (i, t, 0)),
        ],
        out_shape=[
            jax.ShapeDtypeStruct((b, s, NSA_Q), BF16),
            grouped(BF16), grouped(BF16),
            jax.ShapeDtypeStruct((2, b, N_GROUPS, HEAD_DIM, s), BF16),
            jax.ShapeDtypeStruct((b, s, LANES), F32),
        ],
        compiler_params=_cparams("parallel", "parallel"),
        name="nsa_in_proj",
    )(x, g.reshape(1, d), w, wvt)


def _compress_kernel(r_ref, pos_ref, w1_ref, b1_ref, w2_ref, b2_ref, w2t_ref, b2t_ref, o_ref, ot_ref,
                     bias_ref):
    half = CMP_STRIDE * HEAD_DIM
    rows = r_ref[...]
    part_lo = jnp.dot(rows, w1_ref[:half, :], preferred_element_type=F32)
    part_hi = jnp.dot(rows, w1_ref[half:, :], preferred_element_type=F32)
    @pl.when((pl.program_id(1) == 0) & (pl.program_id(2) == 0))
    def _():
        bias_ref[...] = b1_ref[...] + jnp.dot(
            pos_ref[...], w1_ref[...].astype(F32), precision=lax.Precision.HIGHEST,
            preferred_element_type=F32)

    hid = part_lo + pltpu.roll(part_hi, N_CMP_PAD - 1, axis=0) + bias_ref[...]
    hid = jax.nn.gelu(hid, approximate=True).astype(BF16)
    o_ref[...] = (jnp.dot(hid, w2_ref[...], preferred_element_type=F32) + b2_ref[...]).astype(o_ref.dtype)
    ot_ref[...] = (_nt_dot(w2t_ref[...], hid) + b2t_ref[...]).astype(ot_ref.dtype)


def _compress(raw, pos, w1, b1, w2, b2):
    two, b, g, n, width = raw.shape
    hidden = w1.shape[-1]
    sq = pl.Squeezed()
    per_branch = lambda *shape: pl.BlockSpec((sq,) + shape, lambda c, i, j: (c,) + (0,) * len(shape))
    return pl.pallas_call(
        _compress_kernel,
        grid=(two, b, g),
        in_specs=[
            pl.BlockSpec((sq, sq, sq, n, width), lambda c, i, j: (c, i, j, 0, 0)),
            per_branch(1, 2 * width), per_branch(2 * width, hidden), per_branch(1, hidden),
            per_branch(hidden, HEAD_DIM), per_branch(1, HEAD_DIM),
            per_branch(HEAD_DIM, hidden), per_branch(HEAD_DIM, 1),
        ],
        out_specs=[
            pl.BlockSpec((sq, sq, sq, n, HEAD_DIM), lambda c, i, j: (c, i, j, 0, 0)),
            pl.BlockSpec((sq, sq, sq, HEAD_DIM, n), lambda c, i, j: (c, i, j, 0, 0)),
        ],
        out_shape=[
            jax.ShapeDtypeStruct((two, b, g, n, HEAD_DIM), BF16),
            jax.ShapeDtypeStruct((two, b, g, HEAD_DIM, n), BF16),
        ],
        scratch_shapes=[pltpu.VMEM((1, hidden), F32)],
        compiler_params=_cparams("arbitrary", "arbitrary", "arbitrary"),
        name="nsa_compress",
    )(raw, pos, w1, b1.reshape(two, 1, hidden), w2, b2.reshape(two, 1, HEAD_DIM),
      w2.transpose(0, 2, 1), b2.reshape(two, HEAD_DIM, 1))


AUG = 256
SEL_ROW0 = HEAD_DIM
POS_ROW0 = 2 * HEAD_DIM
N_PIECES = 3
PAD_LANE = POS_ROW0 + 2 * N_PIECES
POS_ROWS = 16
V_ROWS = 80
NQ = HEADS_PER_GROUP * TQ
BIG = 1e30
LOG2E = 1.4426950408889634
assert TQ == TK_WIN and TK_SEL % TQ == 0 and WINDOW % TK_WIN == 0


def _flash_reset(m_ref, acc_ref):
    m_ref[...] = jnp.full_like(m_ref, NEG)
    acc_ref[...] = jnp.zeros_like(acc_ref)


def _scores(kaug, tile, tk, qat_ref):
    k0 = pl.multiple_of(tile * tk, tk)
    return jnp.dot(kaug[pl.ds(k0, tk), :], qat_ref[...], preferred_element_type=F32)


def _flash_update_steps(s, vt_tile, mask, m_ref, acc_ref):
    if mask is not None:
        s = jnp.where(mask, s, NEG)
    m_old = m_ref[...]
    m_new = jnp.maximum(m_old, jnp.max(s, axis=0, keepdims=True))
    yield
    p = jnp.exp2(s - m_new).astype(BF16)
    yield
    acc_ref[...] = jnp.exp2(m_old - m_new) * acc_ref[...] + jnp.dot(
        vt_tile, p, preferred_element_type=F32)
    m_ref[...] = m_new


def _flash_update(*args):
    for _ in _flash_update_steps(*args):
        pass


def _round_robin(generators):
    generators = list(generators)
    while generators:
        generators = [g for g in generators if next(g, True) is not True]


def _flash_result(acc_ref):
    return acc_ref[pl.ds(0, HEAD_DIM), :] * (1.0 / acc_ref[pl.ds(HEAD_DIM, 1), :])


GROUPS_PER_STEP = 2


def _nsa_kernel(sl_ref, q_ref, kc_all, vct_all, ks_all, vst_all, kw_all, vwt_all, gl_ref, bg_ref,
                constk_ref, constc_ref, o_ref, *scratch):
    qi = pl.program_id(2)
    gt_ref = scratch[-1]
    per_group = (kc_all, vct_all, ks_all, vst_all, kw_all, vwt_all) + scratch[:-1]
    groups = [tuple(ref.at[h] for ref in per_group) for h in range(GROUPS_PER_STEP)]

    @pl.when(qi == 0)
    def _():
        for refs in groups:
            _nsa_assemble(constk_ref, constc_ref, *refs)

    gt_ref[...] = jax.nn.sigmoid(gl_ref[...] + bg_ref[...]).T
    tiles = [_nsa_tile(pl.program_id(1) * GROUPS_PER_STEP + h, h * HEADS_PER_GROUP * HEAD_DIM, qi,
                       sl_ref, q_ref, o_ref, gt_ref, *refs) for h, refs in enumerate(groups)]
    heads = list(tiles)
    while heads:
        heads = [tile for tile in heads if not next(tile)]
    ends = [next(tile) for tile in tiles]
    for parities in itertools.product((0, 1), repeat=len(tiles)):
        @pl.when(functools.reduce(jnp.logical_and, [odd == want for (odd, _), want in zip(ends, parities)]))
        def _():
            _round_robin(last(want) for (_, last), want in zip(ends, parities))
    for tile in tiles:
        next(tile, None)


def _nsa_assemble(constk_ref, constc_ref, kc_ref, vct_ref, ks_ref, vst_ref, kw_ref, vwt_ref,
                  kaug_s, kaug_w, kaug_c, vt_s, vt_w, vt_c, *unused):
    s_len = ks_ref.shape[0]
    for kaug, row0, const, k in ((kaug_s, 0, constk_ref, ks_ref), (kaug_w, WINDOW, constk_ref, kw_ref),
                                 (kaug_c, 0, constc_ref, kc_ref)):
        kaug[pl.ds(row0, k.shape[0]), :] = const[...]
        kaug[pl.ds(row0, k.shape[0]), pl.ds(0, HEAD_DIM)] = k[...]
    pad_lane = lax.broadcasted_iota(jnp.int32, (WINDOW, AUG), 1) == PAD_LANE
    kaug_w[pl.ds(0, WINDOW), :] = jnp.where(pad_lane, 1.0, 0.0).astype(BF16)
    vt_w[:, pl.ds(0, WINDOW)] = jnp.zeros((V_ROWS, WINDOW), BF16)
    pad = V_ROWS - HEAD_DIM
    ones_rows = lambda n: jnp.where(
        lax.broadcasted_iota(jnp.int32, (pad, n), 0) == 0, 1.0, 0.0).astype(BF16)
    for c in range(vt_s.shape[0]):
        vt_s[c, pl.ds(0, HEAD_DIM), :] = vst_ref[:, pl.ds(c * TK_SEL, TK_SEL)]
        vt_s[c, pl.ds(HEAD_DIM, pad), :] = ones_rows(TK_SEL)
    vt_w[pl.ds(0, HEAD_DIM), pl.ds(WINDOW, s_len)] = vwt_ref[...]
    vt_w[pl.ds(HEAD_DIM, pad), pl.ds(WINDOW, s_len)] = ones_rows(s_len)
    vt_c[pl.ds(0, HEAD_DIM), :] = vct_ref[...]
    vt_c[pl.ds(HEAD_DIM, pad), :] = ones_rows(N_CMP_PAD)


def _nsa_tile(grp, lane0, qi, sl_ref, q_ref, o_ref, gt_ref, kc_ref, vct_ref, ks_ref, vst_ref, kw_ref,
              vwt_ref, kaug_s, kaug_w, kaug_c, vt_s, vt_w, vt_c, qat_ref, score_ref, m_ref, acc_ref,
              out_ref, sa_ref, sb_ref, qats_ref, tiles_ref):
    q0 = qi * TQ
    nrep = HEADS_PER_GROUP

    for half in range(nrep // 2):
        qt = q_ref[:, pl.ds(lane0 + half * LANES, LANES)].astype(F32).T * (HEAD_DIM ** -0.5 * LOG2E)
        for sub in range(2):
            q_rows = qt[sub * HEAD_DIM:(sub + 1) * HEAD_DIM].astype(BF16)
            for ref in (qat_ref, qats_ref):
                ref[pl.ds(0, HEAD_DIM), pl.ds((2 * half + sub) * TQ, TQ)] = q_rows
    qat_ref[pl.ds(SEL_ROW0, HEAD_DIM), :] = jnp.zeros((HEAD_DIM, NQ), BF16)
    piece = lax.broadcasted_iota(jnp.int32, (POS_ROWS, TQ), 0)
    tail0 = POS_ROW0 + POS_ROWS
    for r in range(nrep):
        tile = jnp.where(piece == PAD_LANE - POS_ROW0, -BIG, 0.0)
        for i in range(2 * N_PIECES):
            tile = jnp.where(piece == i, sl_ref[(grp * nrep + r) * N_PIECES + i % N_PIECES], tile)
        for ref in (qat_ref, qats_ref):
            ref[pl.ds(POS_ROW0, POS_ROWS), pl.ds(r * TQ, TQ)] = tile.astype(BF16)
    for ref in (qat_ref, qats_ref):
        ref[pl.ds(tail0, AUG - tail0), :] = jnp.zeros((AUG - tail0, NQ), BF16)

    yield False
    t_row = q0 + (lax.broadcasted_iota(jnp.int32, (1, NQ), 1) & (TQ - 1))
    gate_row = lambda br: jnp.concatenate(
        [gt_ref[pl.ds(3 * (grp * nrep + r) + br, 1), :] for r in range(nrep)], axis=1)

    s = jnp.dot(kaug_c[...], qat_ref[...], preferred_element_type=F32)
    yield False
    cmp_end = CMP_STRIDE * lax.broadcasted_iota(jnp.int32, (N_CMP_PAD, 1), 0) + (L_CMP - 1)
    mask_c = cmp_end <= t_row
    s = jnp.where(mask_c, s, NEG)
    e = jnp.exp2(s - jnp.max(s, axis=0, keepdims=True))
    d = jnp.sum(e, axis=0, keepdims=True)
    p = e * jnp.where(t_row >= L_CMP - 1, 1.0 / d, 0.0)
    yield False
    p_sum = sum(p[:, r * TQ:(r + 1) * TQ] for r in range(nrep))
    o_c = jnp.dot(vt_c[...], p.astype(BF16), preferred_element_type=F32)
    out_ref[...] = gate_row(0) * o_c[:HEAD_DIM]
    yield False

    n_sb = score_ref.shape[0]
    jb = lax.broadcasted_iota(jnp.int32, (n_sb, TQ), 0)
    t_sel = q0 + lax.broadcasted_iota(jnp.int32, (n_sb, TQ), 1)
    valid = jb * L_SEL <= t_sel

    n_full = q0 // TK_SEL
    blocks_per_tile = TK_SEL // L_SEL

    def write_selection(selected):
        sel_bias = jnp.where(selected, 0.0, -BIG).astype(BF16)
        for r in range(nrep):
            qats_ref[pl.ds(SEL_ROW0, n_sb), pl.ds(r * TQ, TQ)] = sel_bias
        picked = jnp.where(selected, 1.0, 0.0)
        bits = sum(jnp.max(picked[k * blocks_per_tile:(k + 1) * blocks_per_tile], axis=(0, 1),
                           keepdims=True) * float(2 ** k) for k in range(n_sb // blocks_per_tile))
        bits = bits[0, 0].astype(jnp.int32)
        count = jnp.int32(0)
        for k in range(n_sb // blocks_per_tile):
            tiles_ref[count] = k
            count = count + jnp.where((k < n_full) & (((bits >> k) & 1) == 1), 1, 0)
        tiles_ref[count] = n_full
        return count

    def rank_blocks():
        cmp_start = CMP_STRIDE * lax.broadcasted_iota(jnp.int32, (n_sb, N_CMP_PAD), 1)
        sb_start = L_SEL * lax.broadcasted_iota(jnp.int32, (n_sb, N_CMP_PAD), 0)
        overlap_t = ((cmp_start < sb_start + L_SEL) & (cmp_start + L_CMP > sb_start)).astype(F32)
        imp_t = jnp.dot(overlap_t, p_sum, precision=lax.Precision.HIGHEST,
                        preferred_element_type=F32)
        cur = t_sel // L_SEL
        forced = (jb == 0) | (jb == cur) | (jb == cur - 1)
        score = jnp.where(valid, jnp.where(forced, jnp.inf, imp_t), -jnp.inf)
        score_ref[...] = score
        chunks = [score[c * SUBLANES:(c + 1) * SUBLANES] for c in range(n_sb // SUBLANES)]
        ranks = [jnp.zeros((SUBLANES, TQ), F32) for _ in chunks]
        sub_row = lax.broadcasted_iota(jnp.int32, (SUBLANES, TQ), 0)
        for i in range(n_sb):
            row = score_ref[pl.ds(i, 1), :]
            for c, chunk in enumerate(chunks):
                later = jnp.where(row >= chunk, 1.0, 0.0)
                earlier = jnp.where(row > chunk, 1.0, 0.0)
                if i < c * SUBLANES:
                    ranks[c] = ranks[c] + later
                elif i >= (c + 1) * SUBLANES:
                    ranks[c] = ranks[c] + earlier
                else:
                    ranks[c] = ranks[c] + jnp.where(sub_row > i - c * SUBLANES, later, earlier)
        rank = jnp.concatenate(ranks, axis=0)
        return write_selection(valid & (rank < float(N_SELECT)))

    n_win = WINDOW // TK_WIN
    k0 = pl.multiple_of(q0, TK_WIN)
    s_all = jnp.dot(kaug_w[pl.ds(k0, WINDOW + TQ), :], qat_ref[...], preferred_element_type=F32)
    yield False
    n_list = rank_blocks()
    yield False

    _flash_reset(m_ref, acc_ref)
    sel_scores = lambda i: _scores(kaug_s, tiles_ref[i], TK_SEL, qats_ref)
    sel_update = lambda s_ref, i, mask: _flash_update(s_ref[...], vt_s[tiles_ref[i]], mask, m_ref, acc_ref)
    sa_ref[...] = sel_scores(0)
    yield False

    s_w = [s_all[j * TK_WIN:(j + 1) * TK_WIN] for j in range(n_win + 1)]
    not_after = q0 + lax.broadcasted_iota(jnp.int32, (TK_WIN, 1), 0) <= t_row
    s_w[0] = jnp.where(not_after, NEG, s_w[0])
    s_w[-1] = jnp.where(not_after, s_w[-1], NEG)
    m_w = functools.reduce(jnp.maximum, [jnp.max(x, axis=0, keepdims=True) for x in s_w])
    p_w = jnp.concatenate([jnp.exp2(x - m_w).astype(BF16) for x in s_w], axis=0)
    yield False
    acc_w = jnp.dot(vt_w[:, pl.ds(k0, WINDOW + TQ)], p_w, preferred_element_type=F32)
    out_ref[...] += gate_row(2) * (acc_w[:HEAD_DIM] * (1.0 / acc_w[HEAD_DIM:HEAD_DIM + 1]))
    yield True

    def sel_pair(j, carry):
        i = 2 * j
        sb_ref[...] = sel_scores(i + 1)
        sel_update(sa_ref, i, None)
        sa_ref[...] = sel_scores(i + 2)
        sel_update(sb_ref, i + 1, None)
        return carry

    lax.fori_loop(0, n_list // 2, sel_pair, 0)

    def last_tiles(odd):
        own = n_list
        own_rows = pl.ds(pl.multiple_of(q0 - n_full * TK_SEL, TQ), TQ)
        if odd:
            sb_ref[...] = sel_scores(own)
            yield
            yield from _flash_update_steps(sa_ref[...], vt_s[tiles_ref[own - 1]], None, m_ref, acc_ref)
            yield
        own_ref = sb_ref if odd else sa_ref
        own_ref[own_rows, :] = jnp.where(not_after, own_ref[own_rows, :], NEG)
        yield from _flash_update_steps(own_ref[...], vt_s[tiles_ref[own]], None, m_ref, acc_ref)

    yield n_list % 2, last_tiles

    out_ref[...] += gate_row(1) * _flash_result(acc_ref)

    for half in range(nrep // 2):
        slab = jnp.concatenate([out_ref[:, pl.ds(2 * half * TQ, TQ)],
                                out_ref[:, pl.ds((2 * half + 1) * TQ, TQ)]], axis=0)
        o_ref[:, pl.ds(lane0 + half * LANES, LANES)] = slab.T.astype(o_ref.dtype)


def _nsa_attention(slope_pieces, q, cmp, cmp_t, keys, vals_t, glogit, bgate, constk, constc):
    b, s, _ = q.shape
    g, nrep, dh = N_GROUPS, HEADS_PER_GROUP, HEAD_DIM
    n_sb = s // L_SEL
    sq = pl.Squeezed()
    gp = GROUPS_PER_STEP
    slot = lambda c, *shape: pl.BlockSpec((sq, sq, gp) + shape, lambda i, j, t: (c, i, j, 0, 0))
    full = lambda *shape: pl.BlockSpec(shape, lambda i, j, t: (0,) * len(shape))
    per_group = lambda shape, dtype: pltpu.VMEM((gp,) + shape, dtype)
    return pl.pallas_call(
        _nsa_kernel,
        grid=(b, g // gp, s // TQ),
        in_specs=[
            pl.BlockSpec(memory_space=pltpu.SMEM),
            pl.BlockSpec((sq, TQ, gp * nrep * dh), lambda i, j, t: (i, t, j)),
            slot(0, N_CMP_PAD, dh), slot(1, dh, N_CMP_PAD),
            slot(0, s, dh), slot(0, dh, s), slot(1, s, dh), slot(1, dh, s),
            pl.BlockSpec((sq, TQ, LANES), lambda i, j, t: (i, t, 0)),
            full(1, LANES), full(s, AUG), full(N_CMP_PAD, AUG),
        ],
        out_specs=pl.BlockSpec((sq, TQ, gp * nrep * dh), lambda i, j, t: (i, t, j)),
        out_shape=jax.ShapeDtypeStruct((b, s, NSA_Q), BF16),
        scratch_shapes=[
            per_group((s, AUG), BF16), per_group((WINDOW + s, AUG), BF16), per_group((N_CMP_PAD, AUG), BF16),
            per_group((s // TK_SEL, V_ROWS, TK_SEL), BF16), per_group((V_ROWS, WINDOW + s), BF16),
            per_group((V_ROWS, N_CMP_PAD), BF16),
            per_group((AUG, NQ), BF16), per_group((n_sb, TQ), F32), per_group((1, NQ), F32),
            per_group((V_ROWS, NQ), F32), per_group((dh, NQ), F32),
            per_group((TK_SEL, NQ), F32), per_group((TK_SEL, NQ), F32),
            per_group((AUG, NQ), BF16),
            pltpu.SMEM((gp, s // TK_SEL + 1), jnp.int32),
            pltpu.VMEM((LANES, TQ), F32),
        ],
        compiler_params=_cparams("arbitrary", "arbitrary", "arbitrary"),
        name="nsa_attention",
    )(slope_pieces, q, cmp, cmp_t, keys, vals_t, keys, vals_t, glogit, bgate, constk, constc)


def _position_pieces(pos):
    hi = (pos // L_SEL) * L_SEL
    return jnp.stack([hi] * N_PIECES + [pos - hi] * N_PIECES, axis=1).astype(F32)


def _nsa_constants(s):
    pos = jnp.arange(s)
    constk = jnp.zeros((s, AUG), F32)
    constk = constk.at[:, SEL_ROW0:SEL_ROW0 + s // L_SEL].set(jax.nn.one_hot(pos // L_SEL, s // L_SEL))
    constk = constk.at[:, POS_ROW0:POS_ROW0 + 2 * N_PIECES].set(_position_pieces(pos))
    cmp_end = CMP_STRIDE * jnp.arange(N_CMP_PAD) + (L_CMP - 1)
    constc = jnp.zeros((N_CMP_PAD, AUG), F32)
    constc = constc.at[:, POS_ROW0:POS_ROW0 + 2 * N_PIECES].set(_position_pieces(cmp_end))
    slopes = 2.0 ** (-8.0 * jnp.arange(1, N_HEADS + 1, dtype=F32) / N_HEADS)
    pieces, rest = [], slopes * LOG2E
    for _ in range(N_PIECES):
        piece = rest.astype(BF16).astype(F32)
        pieces.append(piece)
        rest = rest - piece
    return constk.astype(BF16), constc.astype(BF16), jnp.stack(pieces, axis=1).reshape(-1)


def _nsa_mixer(x, norm_g, w_in, b_gate, cmp_pos, cmp_w1, cmp_b1, cmp_w2, cmp_b2):
    b, s, d = x.shape
    g, nrep, dh = N_GROUPS, HEADS_PER_GROUP, HEAD_DIM
    w_in_p = jnp.pad(w_in, ((0, 0), (0, NSA_IN_PAD - NSA_IN))).astype(BF16)
    v_cols = lambda i: w_in_p[:, NSA_Q + i * NSA_KV:NSA_Q + (i + 1) * NSA_KV].T
    q, raw, keys, vals_t, glogit = _nsa_in_proj(x, norm_g, w_in_p, jnp.stack([v_cols(3), v_cols(5)]),
                                                tm=512)
    raw = raw.reshape(2, b, g, s // CMP_STRIDE, CMP_STRIDE * dh)
    cmp, cmp_t = _compress(raw, cmp_pos.reshape(2, 1, L_CMP * dh), cmp_w1.astype(BF16), cmp_b1,
                           cmp_w2.astype(BF16), cmp_b2)
    constk, constc, slope_pieces = _nsa_constants(s)
    bgate = jnp.pad(b_gate, (0, LANES - 3 * N_HEADS)).reshape(1, LANES)
    return _nsa_attention(slope_pieces, q, cmp, cmp_t, keys, vals_t, glogit, bgate, constk, constc)


TT = 256
SUB = SUBLANES


GATE_BAND = 256


def _gate_bands():
    block = D_RNN // LRU_BLOCKS
    bands = []
    for c0 in range(0, D_RNN, GATE_BAND):
        width = min(GATE_BAND, D_RNN - c0)
        k_lo = (c0 // block) * block // LANES * LANES
        k_hi = -(-(((c0 + width - 1) // block + 1) * block) // LANES) * LANES
        bands.append((k_lo, min(k_hi, D_RNN), c0, width))
    return tuple(bands)


GATE_BANDS = _gate_bands()


def _pack_gate_weights(w_a, w_x):
    dense = lambda w: jax.scipy.linalg.block_diag(*[w[i] for i in range(LRU_BLOCKS)])
    slabs = []
    for k_lo, k_hi, c0, width in GATE_BANDS:
        band = [jnp.pad(dense(w)[k_lo:k_hi, c0:c0 + width], ((0, 0), (0, GATE_BAND - width)))
                for w in (w_a, w_x)]
        slabs.append(jnp.concatenate(band, axis=1))
    return jnp.concatenate(slabs, axis=0).astype(BF16)


SEQS_PER_STEP = 2


def _lru_kernel(x_all, g_ref, win_ref, cw_ref, cb_ref, wax_ref, bax_ref, lam_ref, wout_ref, o_all,
                *scratch):
    seqs = [tuple(ref.at[h] for ref in (x_all, o_all) + scratch) for h in range(SEQS_PER_STEP)]

    @pl.when(pl.program_id(1) == 0)
    def _():
        for x_ref, o_ref, ext_ref, a_ref, u_ref, h_ref, gate_ref in seqs:
            ext_ref[pl.ds(TT, SUB), :] = jnp.zeros((SUB, D_RNN), F32)
            h_ref[...] = jnp.zeros_like(h_ref)

    tiles = [_lru_tile(g_ref, win_ref, cw_ref, cb_ref, wax_ref, bax_ref, lam_ref, wout_ref, *refs)
             for refs in seqs]
    while tiles:
        tiles = [tile for tile in tiles if next(tile, True) is not True]


def _lru_tile(g_ref, win_ref, cw_ref, cb_ref, wax_ref, bax_ref, lam_ref, wout_ref,
              x_ref, o_ref, ext_ref, a_ref, u_ref, h_ref, gate_ref):
    xn = _rms(x_ref[...], g_ref[...]).astype(BF16)
    gate_ref[...] = _gelu_tanh(jnp.dot(xn, win_ref[:, :D_RNN], preferred_element_type=F32))
    ext_ref[pl.ds(0, SUB), :] = ext_ref[pl.ds(TT, SUB), :]
    ext_ref[pl.ds(SUB, TT), :] = jnp.dot(xn, win_ref[:, D_RNN:], preferred_element_type=F32)
    yield
    cw = cw_ref[...]
    xr = cb_ref[...] + sum(
        cw[w:w + 1, :] * ext_ref[pl.ds(SUB - (CONV_W - 1) + w, TT), :] for w in range(CONV_W))

    lam = lam_ref[...]
    softplus_neg = jnp.maximum(-lam, 0.0) + jnp.log1p(jnp.exp(-jnp.abs(lam)))
    half_log_scale = (-0.5 * LRU_C * LOG2E) * softplus_neg
    xr_b = xr.astype(BF16)
    half_bias = bax_ref[...]
    row0 = 0
    for k_lo, k_hi, c0, width in GATE_BANDS:
        z = jnp.dot(xr_b[:, k_lo:k_hi], wax_ref[row0:row0 + k_hi - k_lo, :], preferred_element_type=F32)
        row0 += k_hi - k_lo
        tanh_r = jnp.tanh(z[:, :width] + half_bias[:, c0:c0 + width])
        tanh_i = jnp.tanh(z[:, GATE_BAND:GATE_BAND + width] + half_bias[:, D_RNN + c0:D_RNN + c0 + width])
        scale = half_log_scale[:, c0:c0 + width]
        a = jnp.exp2(scale * tanh_r + scale)
        a_ref[:, c0:c0 + width] = a
        half_xr = 0.5 * xr[:, c0:c0 + width]
        u_ref[:, c0:c0 + width] = jnp.sqrt(jnp.maximum(1.0 - a * a, 0.0)) * (half_xr * tanh_i + half_xr)
    yield

    row = lax.broadcasted_iota(jnp.int32, (SUB, D_RNN), 0)

    def scan_rows(c, h_prev):
        r0 = c * SUB
        a_c = a_ref[pl.ds(r0, SUB), :]
        u_c = u_ref[pl.ds(r0, SUB), :]
        shift = 1
        while shift < SUB:
            keep = row >= shift
            u_c = u_c + a_c * jnp.where(keep, pltpu.roll(u_c, shift, axis=0), 0.0)
            a_c = a_c * jnp.where(keep, pltpu.roll(a_c, shift, axis=0), 1.0)
            shift *= 2
        h_c = u_c + a_c * h_prev
        u_ref[pl.ds(r0, SUB), :] = h_c
        return jnp.broadcast_to(h_c[SUB - 1:SUB, :], (SUB, D_RNN))

    h_state = h_ref[...]
    for c in range(TT // SUB):
        h_state = scan_rows(c, h_state)
        if c % (TT // SUB // 4) == TT // SUB // 4 - 1:
            yield
    h_ref[...] = h_state
    gated = (u_ref[...] * gate_ref[...]).astype(BF16)
    o_ref[...] = x_ref[...] + jnp.dot(gated, wout_ref[...], preferred_element_type=F32)


def _lru_mixer(x, norm_g, w_in, conv_w, conv_b, w_a, b_a, w_x, b_x, lam, w_out):
    b, s, d = x.shape
    wax = _pack_gate_weights(0.5 * w_a, 0.5 * w_x)
    bax = 0.5 * jnp.concatenate([b_a, b_x])
    nseq = SEQS_PER_STEP
    resident = lambda *shape: pl.BlockSpec(shape, lambda i, t: (0,) * len(shape),
                                           pipeline_mode=pl.Buffered(1))
    per_seq = lambda rows: pltpu.VMEM((nseq, rows, D_RNN), F32)
    return pl.pallas_call(
        _lru_kernel,
        grid=(b // nseq, s // TT),
        in_specs=[
            pl.BlockSpec((nseq, TT, d), lambda i, t: (i, t, 0)),
            resident(1, d), resident(d, 2 * D_RNN), resident(CONV_W, D_RNN), resident(1, D_RNN),
            resident(*wax.shape), resident(1, 2 * D_RNN), resident(1, D_RNN), resident(D_RNN, d),
        ],
        out_specs=pl.BlockSpec((nseq, TT, d), lambda i, t: (i, t, 0)),
        out_shape=jax.ShapeDtypeStruct((b, s, d), F32),
        scratch_shapes=[per_seq(TT + SUB), per_seq(TT), per_seq(TT), per_seq(SUB), per_seq(TT)],
        compiler_params=_cparams("parallel", "arbitrary"),
        name="lru_block",
    )(x, norm_g.reshape(1, d), w_in.astype(BF16), conv_w, conv_b.reshape(1, -1), wax,
      bax.reshape(1, -1), lam.reshape(1, -1), w_out.astype(BF16))


def kernel(x, norm_mix, norm_ffn, norm_final, nsa_w_in, nsa_b_gate, nsa_cmp_pos, nsa_cmp_w1,
           nsa_cmp_b1, nsa_cmp_w2, nsa_cmp_b2, nsa_w_out, lru_w_in, lru_conv_w, lru_conv_b,
           lru_w_a, lru_b_a, lru_w_x, lru_b_x, lru_lambda, lru_w_out, ffn_w_in, ffn_w_out):
    b, s, d = x.shape
    ffn = functools.partial(_ffn, tm=512, tf=256)
    attn = _nsa_mixer(x, norm_mix[0], nsa_w_in[0], nsa_b_gate[0], nsa_cmp_pos[0], nsa_cmp_w1[0],
                      nsa_cmp_b1[0], nsa_cmp_w2[0], nsa_cmp_b2[0])
    x = ffn(x.reshape(b * s, d), norm_ffn[0], ffn_w_in[0].astype(BF16), ffn_w_out[0].astype(BF16),
            norm_final, final_norm=False, name="ffn0", mix=attn.reshape(b * s, NSA_Q),
            w_mix=nsa_w_out[0].astype(BF16)).reshape(b, s, d)
    x = _lru_mixer(x, norm_mix[1], lru_w_in[0], lru_conv_w[0], lru_conv_b[0], lru_w_a[0], lru_b_a[0],
                   lru_w_x[0], lru_b_x[0], lru_lambda[0], lru_w_out[0])
    x = ffn(x.reshape(b * s, d), norm_ffn[1], ffn_w_in[1].astype(BF16), ffn_w_out[1].astype(BF16),
            norm_final, final_norm=True, name="ffn1").reshape(b, s, d)
    return x
```

```python
import functools
import itertools

import jax
import jax.numpy as jnp
from jax import lax
from jax.experimental import pallas as pl
from jax.experimental.pallas import tpu as pltpu

F32 = jnp.float32
BF16 = jnp.bfloat16

D_MODEL = 1024
N_HEADS = 16
N_GROUPS = 4
HEADS_PER_GROUP = N_HEADS // N_GROUPS
HEAD_DIM = 64
L_CMP = 32
CMP_STRIDE = 16
L_SEL = 64
N_SELECT = 16
WINDOW = 512
NSA_Q = N_HEADS * HEAD_DIM
NSA_KV = N_GROUPS * HEAD_DIM
NSA_IN = NSA_Q + 6 * NSA_KV + 3 * N_HEADS
D_RNN = 1408
LRU_BLOCKS = 8
LRU_C = 8.0
CONV_W = 4
D_FF = 2816
EPS = 1e-6

LANES = 128
SUBLANES = 8
NEG = -1e30
VMEM_LIMIT = 56 * 1024 * 1024

TQ = 256
TK_SEL = 512
TK_WIN = 256
N_CMP_PAD = 256


def _cparams(*sem):
    return pltpu.CompilerParams(dimension_semantics=sem, vmem_limit_bytes=VMEM_LIMIT)


def _rms(x, g):
    ms = jnp.mean(x * x, axis=-1, keepdims=True)
    return x * lax.rsqrt(ms + EPS) * g


def _gelu_tanh(x):
    c = (2.0 / jnp.pi) ** 0.5
    half_x = 0.5 * x
    return half_x * jnp.tanh(x * (c + (c * 0.044715) * (x * x))) + half_x


def _nt_dot(a, b):
    return lax.dot_general(a, b, (((1,), (1,)), ((), ())), preferred_element_type=F32)


def _ffn_kernel(*refs, final_norm, tf, has_mix):
    x_ref, refs = refs[0], refs[1:]
    x = x_ref[...]
    if has_mix:
        (mix_ref, wmix_ref), refs = refs[:2], refs[2:]
        x = x + jnp.dot(mix_ref[...], wmix_ref[...], preferred_element_type=F32)
    g_ref, win_ref, wout_ref, gf_ref, o_ref, hid_ref = refs
    xn = _rms(x, g_ref[...]).astype(BF16)
    for c0 in range(0, D_FF, tf):
        gate = jnp.dot(xn, win_ref[:, c0:c0 + tf], preferred_element_type=F32)
        up = jnp.dot(xn, win_ref[:, D_FF + c0:D_FF + c0 + tf], preferred_element_type=F32)
        hid_ref[:, c0:c0 + tf] = ((0.5 * gate) * (1.0 + jnp.tanh(0.5 * gate)) * up).astype(BF16)
    y = x + jnp.dot(hid_ref[...], wout_ref[...], preferred_element_type=F32)
    if final_norm:
        y = _rms(y, gf_ref[...])
    o_ref[...] = y


def _ffn(x, g, w_in, w_out, g_final, *, final_norm, tm, tf, name, mix=None, w_mix=None):
    m, d = x.shape
    resident = lambda *shape: pl.BlockSpec(shape, lambda i: (0,) * len(shape),
                                           pipeline_mode=pl.Buffered(1))
    rows = lambda width: pl.BlockSpec((tm, width), lambda i: (i, 0))
    if mix is None:
        mix_specs, mix_args = [], ()
    else:
        mix_specs, mix_args = [rows(mix.shape[1]), resident(*w_mix.shape)], (mix, w_mix)
    return pl.pallas_call(
        functools.partial(_ffn_kernel, final_norm=final_norm, tf=tf, has_mix=mix is not None),
        grid=(m // tm,),
        in_specs=[rows(d)] + mix_specs + [
            resident(1, d), resident(d, 2 * D_FF), resident(D_FF, d), resident(1, d)],
        out_specs=rows(d),
        out_shape=jax.ShapeDtypeStruct((m, d), F32),
        scratch_shapes=[pltpu.VMEM((tm, D_FF), BF16)],
        compiler_params=_cparams("parallel"),
        name=name,
    )(x, *mix_args, g.reshape(1, d), w_in, w_out, g_final.reshape(1, d))


NSA_IN_PAD = -(-NSA_IN // LANES) * LANES


def _nsa_in_proj_kernel(x_ref, g_ref, w_ref, wvt_ref, q_ref, raw_ref, k_ref, vt_ref, gl_ref):
    xn = _rms(x_ref[...], g_ref[...]).astype(BF16)
    q_ref[...] = jnp.dot(xn, w_ref[:, :NSA_Q], preferred_element_type=F32).astype(q_ref.dtype)
    for i, (dst, slot) in {0: (raw_ref, 0), 1: (raw_ref, 1), 2: (k_ref, 0), 4: (k_ref, 1)}.items():
        col0 = NSA_Q + i * NSA_KV
        part = jnp.dot(xn, w_ref[:, col0:col0 + NSA_KV], preferred_element_type=F32)
        for grp in range(N_GROUPS):
            dst[slot, grp] = part[:, grp * HEAD_DIM:(grp + 1) * HEAD_DIM].astype(dst.dtype)
    for slot in range(2):
        part_t = _nt_dot(wvt_ref[slot], xn)
        for grp in range(N_GROUPS):
            vt_ref[slot, grp] = part_t[grp * HEAD_DIM:(grp + 1) * HEAD_DIM].astype(vt_ref.dtype)
    gl_ref[...] = jnp.dot(xn, w_ref[:, NSA_Q + 6 * NSA_KV:], preferred_element_type=F32)


def _nsa_in_proj(x, g, w, wvt, *, tm):
    b, s, d = x.shape
    sq = pl.Squeezed()
    grouped = lambda dtype: jax.ShapeDtypeStruct((2, b, N_GROUPS, s, HEAD_DIM), dtype)
    grouped_spec = pl.BlockSpec((2, sq, N_GROUPS, tm, HEAD_DIM), lambda i, t: (0, i, 0, t, 0))
    return pl.pallas_call(
        _nsa_in_proj_kernel,
        grid=(b, s // tm),
        in_specs=[
            pl.BlockSpec((sq, tm, d), lambda i, t: (i, t, 0)),
            pl.BlockSpec((1, d), lambda i, t: (0, 0)),
            pl.BlockSpec((d, NSA_IN_PAD), lambda i, t: (0, 0)),
            pl.BlockSpec((2, NSA_KV, d), lambda i, t: (0, 0, 0)),
        ],
        out_specs=[
            pl.BlockSpec((sq, tm, NSA_Q), lambda i, t: (i, t, 0)),
            grouped_spec, grouped_spec,
            pl.BlockSpec((2, sq, N_GROUPS, HEAD_DIM, tm), lambda i, t: (0, i, 0, 0, t)),
            pl.BlockSpec((sq, tm, LANES), lambda i, t: (i, t, 0)),
        ],
        out_shape=[
            jax.ShapeDtypeStruct((b, s, NSA_Q), BF16),
            grouped(BF16), grouped(BF16),
            jax.ShapeDtypeStruct((2, b, N_GROUPS, HEAD_DIM, s), BF16),
            jax.ShapeDtypeStruct((b, s, LANES), F32),
        ],
        compiler_params=_cparams("parallel", "parallel"),
        name="nsa_in_proj",
    )(x, g.reshape(1, d), w, wvt)


def _compress_kernel(r_ref, pos_ref, w1_ref, b1_ref, w2_ref, b2_ref, w2t_ref, b2t_ref, o_ref, ot_ref,
                     bias_ref):
    half = CMP_STRIDE * HEAD_DIM
    rows = r_ref[...]
    part_lo = jnp.dot(rows, w1_ref[:half, :], preferred_element_type=F32)
    part_hi = jnp.dot(rows, w1_ref[half:, :], preferred_element_type=F32)
    @pl.when((pl.program_id(1) == 0) & (pl.program_id(2) == 0))
    def _():
        bias_ref[...] = b1_ref[...] + jnp.dot(
            pos_ref[...], w1_ref[...].astype(F32), precision=lax.Precision.HIGHEST,
            preferred_element_type=F32)

    hid = part_lo + pltpu.roll(part_hi, N_CMP_PAD - 1, axis=0) + bias_ref[...]
    hid = jax.nn.gelu(hid, approximate=True).astype(BF16)
    o_ref[...] = (jnp.dot(hid, w2_ref[...], preferred_element_type=F32) + b2_ref[...]).astype(o_ref.dtype)
    ot_ref[...] = (_nt_dot(w2t_ref[...], hid) + b2t_ref[...]).astype(ot_ref.dtype)


def _compress(raw, pos, w1, b1, w2, b2):
    two, b, g, n, width = raw.shape
    hidden = w1.shape[-1]
    sq = pl.Squeezed()
    per_branch = lambda *shape: pl.BlockSpec((sq,) + shape, lambda c, i, j: (c,) + (0,) * len(shape))
    return pl.pallas_call(
        _compress_kernel,
        grid=(two, b, g),
        in_specs=[
            pl.BlockSpec((sq, sq, sq, n, width), lambda c, i, j: (c, i, j, 0, 0)),
            per_branch(1, 2 * width), per_branch(2 * width, hidden), per_branch(1, hidden),
            per_branch(hidden, HEAD_DIM), per_branch(1, HEAD_DIM),
            per_branch(HEAD_DIM, hidden), per_branch(HEAD_DIM, 1),
        ],
        out_specs=[
            pl.BlockSpec((sq, sq, sq, n, HEAD_DIM), lambda c, i, j: (c, i, j, 0, 0)),
            pl.BlockSpec((sq, sq, sq, HEAD_DIM, n), lambda c, i, j: (c, i, j, 0, 0)),
        ],
        out_shape=[
            jax.ShapeDtypeStruct((two, b, g, n, HEAD_DIM), BF16),
            jax.ShapeDtypeStruct((two, b, g, HEAD_DIM, n), BF16),
        ],
        scratch_shapes=[pltpu.VMEM((1, hidden), F32)],
        compiler_params=_cparams("arbitrary", "arbitrary", "arbitrary"),
        name="nsa_compress",
    )(raw, pos, w1, b1.reshape(two, 1, hidden), w2, b2.reshape(two, 1, HEAD_DIM),
      w2.transpose(0, 2, 1), b2.reshape(two, HEAD_DIM, 1))


AUG = 256
SEL_ROW0 = HEAD_DIM
POS_ROW0 = 2 * HEAD_DIM
N_PIECES = 3
PAD_LANE = POS_ROW0 + 2 * N_PIECES
POS_ROWS = 16
V_ROWS = 80
NQ = HEADS_PER_GROUP * TQ
BIG = 1e30
LOG2E = 1.4426950408889634
assert TQ == TK_WIN and TK_SEL % TQ == 0 and WINDOW % TK_WIN == 0


def _flash_reset(m_ref, acc_ref):
    m_ref[...] = jnp.full_like(m_ref, NEG)
    acc_ref[...] = jnp.zeros_like(acc_ref)


def _scores(kaug, tile, tk, qat_ref):
    k0 = pl.multiple_of(tile * tk, tk)
    return jnp.dot(kaug[pl.ds(k0, tk), :], qat_ref[...], preferred_element_type=F32)


def _flash_update_steps(s, vt_tile, mask, m_ref, acc_ref):
    if mask is not None:
        s = jnp.where(mask, s, NEG)
    m_old = m_ref[...]
    m_new = jnp.maximum(m_old, jnp.max(s, axis=0, keepdims=True))
    yield
    p = jnp.exp2(s - m_new).astype(BF16)
    yield
    acc_ref[...] = jnp.exp2(m_old - m_new) * acc_ref[...] + jnp.dot(
        vt_tile, p, preferred_element_type=F32)
    m_ref[...] = m_new


def _flash_update(*args):
    for _ in _flash_update_steps(*args):
        pass


def _round_robin(generators):
    generators = list(generators)
    while generators:
        generators = [g for g in generators if next(g, True) is not True]


def _flash_result(acc_ref):
    return acc_ref[pl.ds(0, HEAD_DIM), :] * (1.0 / acc_ref[pl.ds(HEAD_DIM, 1), :])


GROUPS_PER_STEP = 2


def _nsa_kernel(sl_ref, q_ref, kc_all, vct_all, ks_all, vst_all, kw_all, vwt_all, gl_ref, bg_ref,
                constk_ref, constc_ref, o_ref, *scratch):
    qi = pl.program_id(2)
    gt_ref = scratch[-1]
    per_group = (kc_all, vct_all, ks_all, vst_all, kw_all, vwt_all) + scratch[:-1]
    groups = [tuple(ref.at[h] for ref in per_group) for h in range(GROUPS_PER_STEP)]

    @pl.when(qi == 0)
    def _():
        for refs in groups:
            _nsa_assemble(constk_ref, constc_ref, *refs)

    gt_ref[...] = jax.nn.sigmoid(gl_ref[...] + bg_ref[...]).T
    tiles = [_nsa_tile(pl.program_id(1) * GROUPS_PER_STEP + h, h * HEADS_PER_GROUP * HEAD_DIM, qi,
                       sl_ref, q_ref, o_ref, gt_ref, *refs) for h, refs in enumerate(groups)]
    heads = list(tiles)
    while heads:
        heads = [tile for tile in heads if not next(tile)]
    ends = [next(tile) for tile in tiles]
    for parities in itertools.product((0, 1), repeat=len(tiles)):
        @pl.when(functools.reduce(jnp.logical_and, [odd == want for (odd, _), want in zip(ends, parities)]))
        def _():
            _round_robin(last(want) for (_, last), want in zip(ends, parities))
    for tile in tiles:
        next(tile, None)


def _nsa_assemble(constk_ref, constc_ref, kc_ref, vct_ref, ks_ref, vst_ref, kw_ref, vwt_ref,
                  kaug_s, kaug_w, kaug_c, vt_s, vt_w, vt_c, *unused):
    s_len = ks_ref.shape[0]
    for kaug, row0, const, k in ((kaug_s, 0, constk_ref, ks_ref), (kaug_w, WINDOW, constk_ref, kw_ref),
                                 (kaug_c, 0, constc_ref, kc_ref)):
        kaug[pl.ds(row0, k.shape[0]), :] = const[...]
        kaug[pl.ds(row0, k.shape[0]), pl.ds(0, HEAD_DIM)] = k[...]
    pad_lane = lax.broadcasted_iota(jnp.int32, (WINDOW, AUG), 1) == PAD_LANE
    kaug_w[pl.ds(0, WINDOW), :] = jnp.where(pad_lane, 1.0, 0.0).astype(BF16)
    vt_w[:, pl.ds(0, WINDOW)] = jnp.zeros((V_ROWS, WINDOW), BF16)
    pad = V_ROWS - HEAD_DIM
    ones_rows = lambda n: jnp.where(
        lax.broadcasted_iota(jnp.int32, (pad, n), 0) == 0, 1.0, 0.0).astype(BF16)
    for c in range(vt_s.shape[0]):
        vt_s[c, pl.ds(0, HEAD_DIM), :] = vst_ref[:, pl.ds(c * TK_SEL, TK_SEL)]
        vt_s[c, pl.ds(HEAD_DIM, pad), :] = ones_rows(TK_SEL)
    vt_w[pl.ds(0, HEAD_DIM), pl.ds(WINDOW, s_len)] = vwt_ref[...]
    vt_w[pl.ds(HEAD_DIM, pad), pl.ds(WINDOW, s_len)] = ones_rows(s_len)
    vt_c[pl.ds(0, HEAD_DIM), :] = vct_ref[...]
    vt_c[pl.ds(HEAD_DIM, pad), :] = ones_rows(N_CMP_PAD)


def _nsa_tile(grp, lane0, qi, sl_ref, q_ref, o_ref, gt_ref, kc_ref, vct_ref, ks_ref, vst_ref, kw_ref,
              vwt_ref, kaug_s, kaug_w, kaug_c, vt_s, vt_w, vt_c, qat_ref, score_ref, m_ref, acc_ref,
              out_ref, sa_ref, sb_ref, qats_ref, tiles_ref):
    q0 = qi * TQ
    nrep = HEADS_PER_GROUP

    for half in range(nrep // 2):
        qt = q_ref[:, pl.ds(lane0 + half * LANES, LANES)].astype(F32).T * (HEAD_DIM ** -0.5 * LOG2E)
        for sub in range(2):
            q_rows = qt[sub * HEAD_DIM:(sub + 1) * HEAD_DIM].astype(BF16)
            for ref in (qat_ref, qats_ref):
                ref[pl.ds(0, HEAD_DIM), pl.ds((2 * half + sub) * TQ, TQ)] = q_rows
    qat_ref[pl.ds(SEL_ROW0, HEAD_DIM), :] = jnp.zeros((HEAD_DIM, NQ), BF16)
    piece = lax.broadcasted_iota(jnp.int32, (POS_ROWS, TQ), 0)
    tail0 = POS_ROW0 + POS_ROWS
    for r in range(nrep):
        tile = jnp.where(piece == PAD_LANE - POS_ROW0, -BIG, 0.0)
        for i in range(2 * N_PIECES):
            tile = jnp.where(piece == i, sl_ref[(grp * nrep + r) * N_PIECES + i % N_PIECES], tile)
        for ref in (qat_ref, qats_ref):
            ref[pl.ds(POS_ROW0, POS_ROWS), pl.ds(r * TQ, TQ)] = tile.astype(BF16)
    for ref in (qat_ref, qats_ref):
        ref[pl.ds(tail0, AUG - tail0), :] = jnp.zeros((AUG - tail0, NQ), BF16)

    yield False
    t_row = q0 + (lax.broadcasted_iota(jnp.int32, (1, NQ), 1) & (TQ - 1))
    gate_row = lambda br: jnp.concatenate(
        [gt_ref[pl.ds(3 * (grp * nrep + r) + br, 1), :] for r in range(nrep)], axis=1)

    s = jnp.dot(kaug_c[...], qat_ref[...], preferred_element_type=F32)
    yield False
    cmp_end = CMP_STRIDE * lax.broadcasted_iota(jnp.int32, (N_CMP_PAD, 1), 0) + (L_CMP - 1)
    mask_c = cmp_end <= t_row
    s = jnp.where(mask_c, s, NEG)
    e = jnp.exp2(s - jnp.max(s, axis=0, keepdims=True))
    d = jnp.sum(e, axis=0, keepdims=True)
    p = e * jnp.where(t_row >= L_CMP - 1, 1.0 / d, 0.0)
    yield False
    p_sum = sum(p[:, r * TQ:(r + 1) * TQ] for r in range(nrep))
    o_c = jnp.dot(vt_c[...], p.astype(BF16), preferred_element_type=F32)
    out_ref[...] = gate_row(0) * o_c[:HEAD_DIM]
    yield False

    n_sb = score_ref.shape[0]
    jb = lax.broadcasted_iota(jnp.int32, (n_sb, TQ), 0)
    t_sel = q0 + lax.broadcasted_iota(jnp.int32, (n_sb, TQ), 1)
    valid = jb * L_SEL <= t_sel

    n_full = q0 // TK_SEL
    blocks_per_tile = TK_SEL // L_SEL

    def write_selection(selected):
        sel_bias = jnp.where(selected, 0.0, -BIG).astype(BF16)
        for r in range(nrep):
            qats_ref[pl.ds(SEL_ROW0, n_sb), pl.ds(r * TQ, TQ)] = sel_bias
        picked = jnp.where(selected, 1.0, 0.0)
        bits = sum(jnp.max(picked[k * blocks_per_tile:(k + 1) * blocks_per_tile], axis=(0, 1),
                           keepdims=True) * float(2 ** k) for k in range(n_sb // blocks_per_tile))
        bits = bits[0, 0].astype(jnp.int32)
        count = jnp.int32(0)
        for k in range(n_sb // blocks_per_tile):
            tiles_ref[count] = k
            count = count + jnp.where((k < n_full) & (((bits >> k) & 1) == 1), 1, 0)
        tiles_ref[count] = n_full
        return count

    def rank_blocks():
        cmp_start = CMP_STRIDE * lax.broadcasted_iota(jnp.int32, (n_sb, N_CMP_PAD), 1)
        sb_start = L_SEL * lax.broadcasted_iota(jnp.int32, (n_sb, N_CMP_PAD), 0)
        overlap_t = ((cmp_start < sb_start + L_SEL) & (cmp_start + L_CMP > sb_start)).astype(F32)
        imp_t = jnp.dot(overlap_t, p_sum, precision=lax.Precision.HIGHEST,
                        preferred_element_type=F32)
        cur = t_sel // L_SEL
        forced = (jb == 0) | (jb == cur) | (jb == cur - 1)
        score = jnp.where(valid, jnp.where(forced, jnp.inf, imp_t), -jnp.inf)
        score_ref[...] = score
        chunks = [score[c * SUBLANES:(c + 1) * SUBLANES] for c in range(n_sb // SUBLANES)]
        ranks = [jnp.zeros((SUBLANES, TQ), F32) for _ in chunks]
        sub_row = lax.broadcasted_iota(jnp.int32, (SUBLANES, TQ), 0)
        for i in range(n_sb):
            row = score_ref[pl.ds(i, 1), :]
            for c, chunk in enumerate(chunks):
                later = jnp.where(row >= chunk, 1.0, 0.0)
                earlier = jnp.where(row > chunk, 1.0, 0.0)
                if i < c * SUBLANES:
                    ranks[c] = ranks[c] + later
                elif i >= (c + 1) * SUBLANES:
                    ranks[c] = ranks[c] + earlier
                else:
                    ranks[c] = ranks[c] + jnp.where(sub_row > i - c * SUBLANES, later, earlier)
        rank = jnp.concatenate(ranks, axis=0)
        return write_selection(valid & (rank < float(N_SELECT)))

    n_win = WINDOW // TK_WIN
    k0 = pl.multiple_of(q0, TK_WIN)
    s_all = jnp.dot(kaug_w[pl.ds(k0, WINDOW + TQ), :], qat_ref[...], preferred_element_type=F32)
    yield False
    n_list = rank_blocks()
    yield False

    _flash_reset(m_ref, acc_ref)
    sel_scores = lambda i: _scores(kaug_s, tiles_ref[i], TK_SEL, qats_ref)
    sel_update = lambda s_ref, i, mask: _flash_update(s_ref[...], vt_s[tiles_ref[i]], mask, m_ref, acc_ref)
    sa_ref[...] = sel_scores(0)
    yield False

    s_w = [s_all[j * TK_WIN:(j + 1) * TK_WIN] for j in range(n_win + 1)]
    not_after = q0 + lax.broadcasted_iota(jnp.int32, (TK_WIN, 1), 0) <= t_row
    s_w[0] = jnp.where(not_after, NEG, s_w[0])
    s_w[-1] = jnp.where(not_after, s_w[-1], NEG)
    m_w = functools.reduce(jnp.maximum, [jnp.max(x, axis=0, keepdims=True) for x in s_w])
    p_w = jnp.concatenate([jnp.exp2(x - m_w).astype(BF16) for x in s_w], axis=0)
    yield False
    acc_w = jnp.dot(vt_w[:, pl.ds(k0, WINDOW + TQ)], p_w, preferred_element_type=F32)
    out_ref[...] += gate_row(2) * (acc_w[:HEAD_DIM] * (1.0 / acc_w[HEAD_DIM:HEAD_DIM + 1]))
    yield True

    def sel_pair(j, carry):
        i = 2 * j
        sb_ref[...] = sel_scores(i + 1)
        sel_update(sa_ref, i, None)
        sa_ref[...] = sel_scores(i + 2)
        sel_update(sb_ref, i + 1, None)
        return carry

    lax.fori_loop(0, n_list // 2, sel_pair, 0)

    def last_tiles(odd):
        own = n_list
        own_rows = pl.ds(pl.multiple_of(q0 - n_full * TK_SEL, TQ), TQ)
        if odd:
            sb_ref[...] = sel_scores(own)
            yield
            yield from _flash_update_steps(sa_ref[...], vt_s[tiles_ref[own - 1]], None, m_ref, acc_ref)
            yield
        own_ref = sb_ref if odd else sa_ref
        own_ref[own_rows, :] = jnp.where(not_after, own_ref[own_rows, :], NEG)
        yield from _flash_update_steps(own_ref[...], vt_s[tiles_ref[own]], None, m_ref, acc_ref)

    yield n_list % 2, last_tiles

    out_ref[...] += gate_row(1) * _flash_result(acc_ref)

    for half in range(nrep // 2):
        slab = jnp.concatenate([out_ref[:, pl.ds(2 * half * TQ, TQ)],
                                out_ref[:, pl.ds((2 * half + 1) * TQ, TQ)]], axis=0)
        o_ref[:, pl.ds(lane0 + half * LANES, LANES)] = slab.T.astype(o_ref.dtype)


def _nsa_attention(slope_pieces, q, cmp, cmp_t, keys, vals_t, glogit, bgate, constk, constc):
    b, s, _ = q.shape
    g, nrep, dh = N_GROUPS, HEADS_PER_GROUP, HEAD_DIM
    n_sb = s // L_SEL
    sq = pl.Squeezed()
    gp = GROUPS_PER_STEP
    slot = lambda c, *shape: pl.BlockSpec((sq, sq, gp) + shape, lambda i, j, t: (c, i, j, 0, 0))
    full = lambda *shape: pl.BlockSpec(shape, lambda i, j, t: (0,) * len(shape))
    per_group = lambda shape, dtype: pltpu.VMEM((gp,) + shape, dtype)
    return pl.pallas_call(
        _nsa_kernel,
        grid=(b, g // gp, s // TQ),
        in_specs=[
            pl.BlockSpec(memory_space=pltpu.SMEM),
            pl.BlockSpec((sq, TQ, gp * nrep * dh), lambda i, j, t: (i, t, j)),
            slot(0, N_CMP_PAD, dh), slot(1, dh, N_CMP_PAD),
            slot(0, s, dh), slot(0, dh, s), slot(1, s, dh), slot(1, dh, s),
            pl.BlockSpec((sq, TQ, LANES), lambda i, j, t: (i, t, 0)),
            full(1, LANES), full(s, AUG), full(N_CMP_PAD, AUG),
        ],
        out_specs=pl.BlockSpec((sq, TQ, gp * nrep * dh), lambda i, j, t: (i, t, j)),
        out_shape=jax.ShapeDtypeStruct((b, s, NSA_Q), BF16),
        scratch_shapes=[
            per_group((s, AUG), BF16), per_group((WINDOW + s, AUG), BF16), per_group((N_CMP_PAD, AUG), BF16),
            per_group((s // TK_SEL, V_ROWS, TK_SEL), BF16), per_group((V_ROWS, WINDOW + s), BF16),
            per_group((V_ROWS, N_CMP_PAD), BF16),
            per_group((AUG, NQ), BF16), per_group((n_sb, TQ), F32), per_group((1, NQ), F32),
            per_group((V_ROWS, NQ), F32), per_group((dh, NQ), F32),
            per_group((TK_SEL, NQ), F32), per_group((TK_SEL, NQ), F32),
            per_group((AUG, NQ), BF16),
            pltpu.SMEM((gp, s // TK_SEL + 1), jnp.int32),
            pltpu.VMEM((LANES, TQ), F32),
        ],
        compiler_params=_cparams("arbitrary", "arbitrary", "arbitrary"),
        name="nsa_attention",
    )(slope_pieces, q, cmp, cmp_t, keys, vals_t, keys, vals_t, glogit, bgate, constk, constc)


def _position_pieces(pos):
    hi = (pos // L_SEL) * L_SEL
    return jnp.stack([hi] * N_PIECES + [pos - hi] * N_PIECES, axis=1).astype(F32)


def _nsa_constants(s):
    pos = jnp.arange(s)
    constk = jnp.zeros((s, AUG), F32)
    constk = constk.at[:, SEL_ROW0:SEL_ROW0 + s // L_SEL].set(jax.nn.one_hot(pos // L_SEL, s // L_SEL))
    constk = constk.at[:, POS_ROW0:POS_ROW0 + 2 * N_PIECES].set(_position_pieces(pos))
    cmp_end = CMP_STRIDE * jnp.arange(N_CMP_PAD) + (L_CMP - 1)
    constc = jnp.zeros((N_CMP_PAD, AUG), F32)
    constc = constc.at[:, POS_ROW0:POS_ROW0 + 2 * N_PIECES].set(_position_pieces(cmp_end))
    slopes = 2.0 ** (-8.0 * jnp.arange(1, N_HEADS + 1, dtype=F32) / N_HEADS)
    pieces, rest = [], slopes * LOG2E
    for _ in range(N_PIECES):
        piece = rest.astype(BF16).astype(F32)
        pieces.append(piece)
        rest = rest - piece
    return constk.astype(BF16), constc.astype(BF16), jnp.stack(pieces, axis=1).reshape(-1)


def _nsa_mixer(x, norm_g, w_in, b_gate, cmp_pos, cmp_w1, cmp_b1, cmp_w2, cmp_b2):
    b, s, d = x.shape
    g, nrep, dh = N_GROUPS, HEADS_PER_GROUP, HEAD_DIM
    w_in_p = jnp.pad(w_in, ((0, 0), (0, NSA_IN_PAD - NSA_IN))).astype(BF16)
    v_cols = lambda i: w_in_p[:, NSA_Q + i * NSA_KV:NSA_Q + (i + 1) * NSA_KV].T
    q, raw, keys, vals_t, glogit = _nsa_in_proj(x, norm_g, w_in_p, jnp.stack([v_cols(3), v_cols(5)]),
                                                tm=512)
    raw = raw.reshape(2, b, g, s // CMP_STRIDE, CMP_STRIDE * dh)
    cmp, cmp_t = _compress(raw, cmp_pos.reshape(2, 1, L_CMP * dh), cmp_w1.astype(BF16), cmp_b1,
                           cmp_w2.astype(BF16), cmp_b2)
    constk, constc, slope_pieces = _nsa_constants(s)
    bgate = jnp.pad(b_gate, (0, LANES - 3 * N_HEADS)).reshape(1, LANES)
    return _nsa_attention(slope_pieces, q, cmp, cmp_t, keys, vals_t, glogit, bgate, constk, constc)


TT = 256
SUB = SUBLANES


GATE_BAND = 256


def _gate_bands():
    block = D_RNN // LRU_BLOCKS
    bands = []
    for c0 in range(0, D_RNN, GATE_BAND):
        width = min(GATE_BAND, D_RNN - c0)
        k_lo = (c0 // block) * block // LANES * LANES
        k_hi = -(-(((c0 + width - 1) // block + 1) * block) // LANES) * LANES
        bands.append((k_lo, min(k_hi, D_RNN), c0, width))
    return tuple(bands)


GATE_BANDS = _gate_bands()


def _pack_gate_weights(w_a, w_x):
    dense = lambda w: jax.scipy.linalg.block_diag(*[w[i] for i in range(LRU_BLOCKS)])
    slabs = []
    for k_lo, k_hi, c0, width in GATE_BANDS:
        band = [jnp.pad(dense(w)[k_lo:k_hi, c0:c0 + width], ((0, 0), (0, GATE_BAND - width)))
                for w in (w_a, w_x)]
        slabs.append(jnp.concatenate(band, axis=1))
    return jnp.concatenate(slabs, axis=0).astype(BF16)


SEQS_PER_STEP = 2


def _lru_kernel(x_all, g_ref, win_ref, cw_ref, cb_ref, wax_ref, bax_ref, lam_ref, wout_ref, o_all,
                *scratch):
    seqs = [tuple(ref.at[h] for ref in (x_all, o_all) + scratch) for h in range(SEQS_PER_STEP)]

    @pl.when(pl.program_id(1) == 0)
    def _():
        for x_ref, o_ref, ext_ref, a_ref, u_ref, h_ref, gate_ref in seqs:
            ext_ref[pl.ds(TT, SUB), :] = jnp.zeros((SUB, D_RNN), F32)
            h_ref[...] = jnp.zeros_like(h_ref)

    tiles = [_lru_tile(g_ref, win_ref, cw_ref, cb_ref, wax_ref, bax_ref, lam_ref, wout_ref, *refs)
             for refs in seqs]
    while tiles:
        tiles = [tile for tile in tiles if next(tile, True) is not True]


def _lru_tile(g_ref, win_ref, cw_ref, cb_ref, wax_ref, bax_ref, lam_ref, wout_ref,
              x_ref, o_ref, ext_ref, a_ref, u_ref, h_ref, gate_ref):
    xn = _rms(x_ref[...], g_ref[...]).astype(BF16)
    gate_ref[...] = _gelu_tanh(jnp.dot(xn, win_ref[:, :D_RNN], preferred_element_type=F32))
    ext_ref[pl.ds(0, SUB), :] = ext_ref[pl.ds(TT, SUB), :]
    ext_ref[pl.ds(SUB, TT), :] = jnp.dot(xn, win_ref[:, D_RNN:], preferred_element_type=F32)
    yield
    cw = cw_ref[...]
    xr = cb_ref[...] + sum(
        cw[w:w + 1, :] * ext_ref[pl.ds(SUB - (CONV_W - 1) + w, TT), :] for w in range(CONV_W))

    lam = lam_ref[...]
    softplus_neg = jnp.maximum(-lam, 0.0) + jnp.log1p(jnp.exp(-jnp.abs(lam)))
    half_log_scale = (-0.5 * LRU_C * LOG2E) * softplus_neg
    xr_b = xr.astype(BF16)
    half_bias = bax_ref[...]
    row0 = 0
    for k_lo, k_hi, c0, width in GATE_BANDS:
        z = jnp.dot(xr_b[:, k_lo:k_hi], wax_ref[row0:row0 + k_hi - k_lo, :], preferred_element_type=F32)
        row0 += k_hi - k_lo
        tanh_r = jnp.tanh(z[:, :width] + half_bias[:, c0:c0 + width])
        tanh_i = jnp.tanh(z[:, GATE_BAND:GATE_BAND + width] + half_bias[:, D_RNN + c0:D_RNN + c0 + width])
        scale = half_log_scale[:, c0:c0 + width]
        a = jnp.exp2(scale * tanh_r + scale)
        a_ref[:, c0:c0 + width] = a
        half_xr = 0.5 * xr[:, c0:c0 + width]
        u_ref[:, c0:c0 + width] = jnp.sqrt(jnp.maximum(1.0 - a * a, 0.0)) * (half_xr * tanh_i + half_xr)
    yield

    row = lax.broadcasted_iota(jnp.int32, (SUB, D_RNN), 0)

    def scan_rows(c, h_prev):
        r0 = c * SUB
        a_c = a_ref[pl.ds(r0, SUB), :]
        u_c = u_ref[pl.ds(r0, SUB), :]
        shift = 1
        while shift < SUB:
            keep = row >= shift
            u_c = u_c + a_c * jnp.where(keep, pltpu.roll(u_c, shift, axis=0), 0.0)
            a_c = a_c * jnp.where(keep, pltpu.roll(a_c, shift, axis=0), 1.0)
            shift *= 2
        h_c = u_c + a_c * h_prev
        u_ref[pl.ds(r0, SUB), :] = h_c
        return jnp.broadcast_to(h_c[SUB - 1:SUB, :], (SUB, D_RNN))

    h_state = h_ref[...]
    for c in range(TT // SUB):
        h_state = scan_rows(c, h_state)
        if c % (TT // SUB // 4) == TT // SUB // 4 - 1:
            yield
    h_ref[...] = h_state
    gated = (u_ref[...] * gate_ref[...]).astype(BF16)
    o_ref[...] = x_ref[...] + jnp.dot(gated, wout_ref[...], preferred_element_type=F32)


def _lru_mixer(x, norm_g, w_in, conv_w, conv_b, w_a, b_a, w_x, b_x, lam, w_out):
    b, s, d = x.shape
    wax = _pack_gate_weights(0.5 * w_a, 0.5 * w_x)
    bax = 0.5 * jnp.concatenate([b_a, b_x])
    nseq = SEQS_PER_STEP
    resident = lambda *shape: pl.BlockSpec(shape, lambda i, t: (0,) * len(shape),
                                           pipeline_mode=pl.Buffered(1))
    per_seq = lambda rows: pltpu.VMEM((nseq, rows, D_RNN), F32)
    return pl.pallas_call(
        _lru_kernel,
        grid=(b // nseq, s // TT),
        in_specs=[
            pl.BlockSpec((nseq, TT, d), lambda i, t: (i, t, 0)),
            resident(1, d), resident(d, 2 * D_RNN), resident(CONV_W, D_RNN), resident(1, D_RNN),
            resident(*wax.shape), resident(1, 2 * D_RNN), resident(1, D_RNN), resident(D_RNN, d),
        ],
        out_specs=pl.BlockSpec((nseq, TT, d), lambda i, t: (i, t, 0)),
        out_shape=jax.ShapeDtypeStruct((b, s, d), F32),
        scratch_shapes=[per_seq(TT + SUB), per_seq(TT), per_seq(TT), per_seq(SUB), per_seq(TT)],
        compiler_params=_cparams("parallel", "arbitrary"),
        name="lru_block",
    )(x, norm_g.reshape(1, d), w_in.astype(BF16), conv_w, conv_b.reshape(1, -1), wax,
      bax.reshape(1, -1), lam.reshape(1, -1), w_out.astype(BF16))


def _layer_to_bf16_kernel(w_ref, o_ref):
    o_ref[...] = w_ref[...].astype(o_ref.dtype)


def _layer_to_bf16(w, layer, *, n_blocks=8):
    _, rows, cols = w.shape
    tr = rows // n_blocks
    return pl.pallas_call(
        _layer_to_bf16_kernel,
        grid=(n_blocks,),
        in_specs=[pl.BlockSpec((pl.Squeezed(), tr, cols), lambda i: (layer, i, 0))],
        out_specs=pl.BlockSpec((tr, cols), lambda i: (i, 0)),
        out_shape=jax.ShapeDtypeStruct((rows, cols), BF16),
        compiler_params=_cparams("parallel"),
        name="weights_to_bf16",
    )(w)


def kernel(x, norm_mix, norm_ffn, norm_final, nsa_w_in, nsa_b_gate, nsa_cmp_pos, nsa_cmp_w1,
           nsa_cmp_b1, nsa_cmp_w2, nsa_cmp_b2, nsa_w_out, lru_w_in, lru_conv_w, lru_conv_b,
           lru_w_a, lru_b_a, lru_w_x, lru_b_x, lru_lambda, lru_w_out, ffn_w_in, ffn_w_out):
    b, s, d = x.shape
    ffn = functools.partial(_ffn, tm=512, tf=256)
    attn = _nsa_mixer(x, norm_mix[0], nsa_w_in[0], nsa_b_gate[0], nsa_cmp_pos[0], nsa_cmp_w1[0],
                      nsa_cmp_b1[0], nsa_cmp_w2[0], nsa_cmp_b2[0])
    x = ffn(x.reshape(b * s, d), norm_ffn[0], _layer_to_bf16(ffn_w_in, 0), _layer_to_bf16(ffn_w_out, 0),
            norm_final, final_norm=False, name="ffn0", mix=attn.reshape(b * s, NSA_Q),
            w_mix=nsa_w_out[0].astype(BF16)).reshape(b, s, d)
    x = _lru_mixer(x, norm_mix[1], lru_w_in[0], lru_conv_w[0], lru_conv_b[0], lru_w_a[0], lru_b_a[0],
                   lru_w_x[0], lru_b_x[0], lru_lambda[0], lru_w_out[0])
    x = ffn(x.reshape(b * s, d), norm_ffn[1], _layer_to_bf16(ffn_w_in, 1), _layer_to_bf16(ffn_w_out, 1),
            norm_final, final_norm=True, name="ffn1").reshape(b, s, d)
    return x
```

```python
import functools
import itertools

import jax
import jax.numpy as jnp
from jax import lax
from jax.experimental import pallas as pl
from jax.experimental.pallas import tpu as pltpu

F32 = jnp.float32
BF16 = jnp.bfloat16

D_MODEL = 1024
N_HEADS = 16
N_GROUPS = 4
HEADS_PER_GROUP = N_HEADS // N_GROUPS
HEAD_DIM = 64
L_CMP = 32
CMP_STRIDE = 16
L_SEL = 64
N_SELECT = 16
WINDOW = 512
NSA_Q = N_HEADS * HEAD_DIM
NSA_KV = N_GROUPS * HEAD_DIM
NSA_IN = NSA_Q + 6 * NSA_KV + 3 * N_HEADS
D_RNN = 1408
LRU_BLOCKS = 8
LRU_C = 8.0
CONV_W = 4
D_FF = 2816
EPS = 1e-6

LANES = 128
SUBLANES = 8
NEG = -1e30
VMEM_LIMIT = 56 * 1024 * 1024

TQ = 256
TK_SEL = 512
TK_WIN = 256
N_CMP_PAD = 256


def _cparams(*sem):
    return pltpu.CompilerParams(dimension_semantics=sem, vmem_limit_bytes=VMEM_LIMIT)


def _rms(x, g):
    ms = jnp.mean(x * x, axis=-1, keepdims=True)
    return x * lax.rsqrt(ms + EPS) * g


def _gelu_tanh(x):
    c = (2.0 / jnp.pi) ** 0.5
    half_x = 0.5 * x
    return half_x * jnp.tanh(x * (c + (c * 0.044715) * (x * x))) + half_x


def _nt_dot(a, b):
    return lax.dot_general(a, b, (((1,), (1,)), ((), ())), preferred_element_type=F32)


def _ffn_kernel(*refs, final_norm, tf, has_mix):
    x_ref, refs = refs[0], refs[1:]
    x = x_ref[...]
    if has_mix:
        (mix_ref, wmix_ref), refs = refs[:2], refs[2:]
        x = x + jnp.dot(mix_ref[...], wmix_ref[...], preferred_element_type=F32)
    g_ref, win_ref, wout_ref, gf_ref, o_ref, hid_ref = refs
    xn = _rms(x, g_ref[...]).astype(BF16)
    for c0 in range(0, D_FF, tf):
        gate = jnp.dot(xn, win_ref[:, c0:c0 + tf], preferred_element_type=F32)
        up = jnp.dot(xn, win_ref[:, D_FF + c0:D_FF + c0 + tf], preferred_element_type=F32)
        hid_ref[:, c0:c0 + tf] = ((0.5 * gate) * (1.0 + jnp.tanh(0.5 * gate)) * up).astype(BF16)
    y = x + jnp.dot(hid_ref[...], wout_ref[...], preferred_element_type=F32)
    if final_norm:
        y = _rms(y, gf_ref[...])
    o_ref[...] = y


def _ffn(x, g, w_in, w_out, g_final, *, final_norm, tm, tf, name, mix=None, w_mix=None):
    m, d = x.shape
    resident = lambda *shape: pl.BlockSpec(shape, lambda i: (0,) * len(shape),
                                           pipeline_mode=pl.Buffered(1))
    rows = lambda width: pl.BlockSpec((tm, width), lambda i: (i, 0))
    if mix is None:
        mix_specs, mix_args = [], ()
    else:
        mix_specs, mix_args = [rows(mix.shape[1]), resident(*w_mix.shape)], (mix, w_mix)
    return pl.pallas_call(
        functools.partial(_ffn_kernel, final_norm=final_norm, tf=tf, has_mix=mix is not None),
        grid=(m // tm,),
        in_specs=[rows(d)] + mix_specs + [
            resident(1, d), resident(d, 2 * D_FF), resident(D_FF, d), resident(1, d)],
        out_specs=rows(d),
        out_shape=jax.ShapeDtypeStruct((m, d), F32),
        scratch_shapes=[pltpu.VMEM((tm, D_FF), BF16)],
        compiler_params=_cparams("parallel"),
        name=name,
    )(x, *mix_args, g.reshape(1, d), w_in, w_out, g_final.reshape(1, d))


NSA_IN_PAD = -(-NSA_IN // LANES) * LANES


def _nsa_in_proj_kernel(x_ref, g_ref, w_ref, wvt_ref, q_ref, raw_ref, k_ref, vt_ref, gl_ref):
    xn = _rms(x_ref[...], g_ref[...]).astype(BF16)
    q_ref[...] = jnp.dot(xn, w_ref[:, :NSA_Q], preferred_element_type=F32).astype(q_ref.dtype)
    raw_ref[...] = jnp.dot(xn, w_ref[:, NSA_Q:NSA_Q + 2 * NSA_KV],
                           preferred_element_type=F32).astype(raw_ref.dtype)
    for i, (dst, slot) in {2: (k_ref, 0), 4: (k_ref, 1)}.items():
        col0 = NSA_Q + i * NSA_KV
        part = jnp.dot(xn, w_ref[:, col0:col0 + NSA_KV], preferred_element_type=F32)
        for grp in range(N_GROUPS):
            dst[slot, grp] = part[:, grp * HEAD_DIM:(grp + 1) * HEAD_DIM].astype(dst.dtype)
    for slot in range(2):
        part_t = _nt_dot(wvt_ref[slot], xn)
        for grp in range(N_GROUPS):
            vt_ref[slot, grp] = part_t[grp * HEAD_DIM:(grp + 1) * HEAD_DIM].astype(vt_ref.dtype)
    gl_ref[...] = jnp.dot(xn, w_ref[:, NSA_Q + 6 * NSA_KV:], preferred_element_type=F32)


def _nsa_in_proj(x, g, w, wvt, *, tm):
    b, s, d = x.shape
    sq = pl.Squeezed()
    grouped = lambda dtype: jax.ShapeDtypeStruct((2, b, N_GROUPS, s, HEAD_DIM), dtype)
    grouped_spec = pl.BlockSpec((2, sq, N_GROUPS, tm, HEAD_DIM), lambda i, t: (0, i, 0, t, 0))
    return pl.pallas_call(
        _nsa_in_proj_kernel,
        grid=(b, s // tm),
        in_specs=[
            pl.BlockSpec((sq, tm, d), lambda i, t: (i, t, 0)),
            pl.BlockSpec((1, d), lambda i, t: (0, 0)),
            pl.BlockSpec((d, NSA_IN_PAD), lambda i, t: (0, 0)),
            pl.BlockSpec((2, NSA_KV, d), lambda i, t: (0, 0, 0)),
        ],
        out_specs=[
            pl.BlockSpec((sq, tm, NSA_Q), lambda i, t: (i, t, 0)),
            pl.BlockSpec((sq, tm, 2 * NSA_KV), lambda i, t: (i, t, 0)), grouped_spec,
            pl.BlockSpec((2, sq, N_GROUPS, HEAD_DIM, tm), lambda i, t: (0, i, 0, 0, t)),
            pl.BlockSpec((sq, tm, LANES), lambda i, t: (i, t, 0)),
        ],
        out_shape=[
            jax.ShapeDtypeStruct((b, s, NSA_Q), BF16),
            jax.ShapeDtypeStruct((b, s, 2 * NSA_KV), BF16), grouped(BF16),
            jax.ShapeDtypeStruct((2, b, N_GROUPS, HEAD_DIM, s), BF16),
            jax.ShapeDtypeStruct((b, s, LANES), F32),
        ],
        compiler_params=_cparams("parallel", "parallel"),
        name="nsa_in_proj",
    )(x, g.reshape(1, d), w, wvt)


def _compress_kernel(r_ref, pos_ref, w1_ref, b1_ref, w2_ref, b2_ref, w2t_ref, b2t_ref, o_ref, ot_ref,
                     bias_ref):
    half = CMP_STRIDE * HEAD_DIM
    rows = r_ref[...]
    part_lo = jnp.dot(rows, w1_ref[:half, :], preferred_element_type=F32)
    part_hi = jnp.dot(rows, w1_ref[half:, :], preferred_element_type=F32)
    @pl.when((pl.program_id(1) == 0) & (pl.program_id(2) == 0))
    def _():
        bias_ref[...] = b1_ref[...] + jnp.dot(
            pos_ref[...], w1_ref[...].astype(F32), precision=lax.Precision.HIGHEST,
            preferred_element_type=F32)

    hid = part_lo + pltpu.roll(part_hi, N_CMP_PAD - 1, axis=0) + bias_ref[...]
    hid = jax.nn.gelu(hid, approximate=True).astype(BF16)
    o_ref[...] = (jnp.dot(hid, w2_ref[...], preferred_element_type=F32) + b2_ref[...]).astype(o_ref.dtype)
    ot_ref[...] = (_nt_dot(w2t_ref[...], hid) + b2t_ref[...]).astype(ot_ref.dtype)


def _compress(raw, pos, w1, b1, w2, b2):
    two, b, g, n, width = raw.shape
    hidden = w1.shape[-1]
    sq = pl.Squeezed()
    per_branch = lambda *shape: pl.BlockSpec((sq,) + shape, lambda c, i, j: (c,) + (0,) * len(shape))
    return pl.pallas_call(
        _compress_kernel,
        grid=(two, b, g),
        in_specs=[
            pl.BlockSpec((sq, sq, sq, n, width), lambda c, i, j: (c, i, j, 0, 0)),
            per_branch(1, 2 * width), per_branch(2 * width, hidden), per_branch(1, hidden),
            per_branch(hidden, HEAD_DIM), per_branch(1, HEAD_DIM),
            per_branch(HEAD_DIM, hidden), per_branch(HEAD_DIM, 1),
        ],
        out_specs=[
            pl.BlockSpec((sq, sq, sq, n, HEAD_DIM), lambda c, i, j: (c, i, j, 0, 0)),
            pl.BlockSpec((sq, sq, sq, HEAD_DIM, n), lambda c, i, j: (c, i, j, 0, 0)),
        ],
        out_shape=[
            jax.ShapeDtypeStruct((two, b, g, n, HEAD_DIM), BF16),
            jax.ShapeDtypeStruct((two, b, g, HEAD_DIM, n), BF16),
        ],
        scratch_shapes=[pltpu.VMEM((1, hidden), F32)],
        compiler_params=_cparams("arbitrary", "arbitrary", "arbitrary"),
        name="nsa_compress",
    )(raw, pos, w1, b1.reshape(two, 1, hidden), w2, b2.reshape(two, 1, HEAD_DIM),
      w2.transpose(0, 2, 1), b2.reshape(two, HEAD_DIM, 1))


AUG = 256
SEL_ROW0 = HEAD_DIM
POS_ROW0 = 2 * HEAD_DIM
N_PIECES = 3
PAD_LANE = POS_ROW0 + 2 * N_PIECES
POS_ROWS = 16
V_ROWS = 80
NQ = HEADS_PER_GROUP * TQ
BIG = 1e30
LOG2E = 1.4426950408889634
assert TQ == TK_WIN and TK_SEL % TQ == 0 and WINDOW % TK_WIN == 0


def _flash_reset(m_ref, acc_ref):
    m_ref[...] = jnp.full_like(m_ref, NEG)
    acc_ref[...] = jnp.zeros_like(acc_ref)


def _scores(kaug, tile, tk, qat_ref):
    k0 = pl.multiple_of(tile * tk, tk)
    return jnp.dot(kaug[pl.ds(k0, tk), :], qat_ref[...], preferred_element_type=F32)


def _flash_update_steps(s, vt_tile, mask, m_ref, acc_ref):
    if mask is not None:
        s = jnp.where(mask, s, NEG)
    m_old = m_ref[...]
    m_new = jnp.maximum(m_old, jnp.max(s, axis=0, keepdims=True))
    yield
    p = jnp.exp2(s - m_new).astype(BF16)
    yield
    acc_ref[...] = jnp.exp2(m_old - m_new) * acc_ref[...] + jnp.dot(
        vt_tile, p, preferred_element_type=F32)
    m_ref[...] = m_new


def _flash_update(*args):
    for _ in _flash_update_steps(*args):
        pass


def _round_robin(generators):
    generators = list(generators)
    while generators:
        generators = [g for g in generators if next(g, True) is not True]


def _flash_result(acc_ref):
    return acc_ref[pl.ds(0, HEAD_DIM), :] * (1.0 / acc_ref[pl.ds(HEAD_DIM, 1), :])


GROUPS_PER_STEP = 2


def _nsa_kernel(sl_ref, q_ref, kc_all, vct_all, ks_all, vst_all, kw_all, vwt_all, gl_ref, bg_ref,
                constk_ref, constc_ref, o_ref, *scratch):
    qi = pl.program_id(2)
    gt_ref = scratch[-1]
    per_group = (kc_all, vct_all, ks_all, vst_all, kw_all, vwt_all) + scratch[:-1]
    groups = [tuple(ref.at[h] for ref in per_group) for h in range(GROUPS_PER_STEP)]

    @pl.when(qi == 0)
    def _():
        for refs in groups:
            _nsa_assemble(constk_ref, constc_ref, *refs)

    gt_ref[...] = jax.nn.sigmoid(gl_ref[...] + bg_ref[...]).T
    tiles = [_nsa_tile(pl.program_id(1) * GROUPS_PER_STEP + h, h * HEADS_PER_GROUP * HEAD_DIM, qi,
                       sl_ref, q_ref, o_ref, gt_ref, *refs) for h, refs in enumerate(groups)]
    heads = list(tiles)
    while heads:
        heads = [tile for tile in heads if not next(tile)]
    ends = [next(tile) for tile in tiles]
    for parities in itertools.product((0, 1), repeat=len(tiles)):
        @pl.when(functools.reduce(jnp.logical_and, [odd == want for (odd, _), want in zip(ends, parities)]))
        def _():
            _round_robin(last(want) for (_, last), want in zip(ends, parities))
    for tile in tiles:
        next(tile, None)


def _nsa_assemble(constk_ref, constc_ref, kc_ref, vct_ref, ks_ref, vst_ref, kw_ref, vwt_ref,
                  kaug_s, kaug_w, kaug_c, vt_s, vt_w, vt_c, *unused):
    s_len = ks_ref.shape[0]
    for kaug, row0, const, k in ((kaug_s, 0, constk_ref, ks_ref), (kaug_w, WINDOW, constk_ref, kw_ref),
                                 (kaug_c, 0, constc_ref, kc_ref)):
        kaug[pl.ds(row0, k.shape[0]), :] = const[...]
        kaug[pl.ds(row0, k.shape[0]), pl.ds(0, HEAD_DIM)] = k[...]
    pad_lane = lax.broadcasted_iota(jnp.int32, (WINDOW, AUG), 1) == PAD_LANE
    kaug_w[pl.ds(0, WINDOW), :] = jnp.where(pad_lane, 1.0, 0.0).astype(BF16)
    vt_w[:, pl.ds(0, WINDOW)] = jnp.zeros((V_ROWS, WINDOW), BF16)
    pad = V_ROWS - HEAD_DIM
    ones_rows = lambda n: jnp.where(
        lax.broadcasted_iota(jnp.int32, (pad, n), 0) == 0, 1.0, 0.0).astype(BF16)
    for c in range(vt_s.shape[0]):
        vt_s[c, pl.ds(0, HEAD_DIM), :] = vst_ref[:, pl.ds(c * TK_SEL, TK_SEL)]
        vt_s[c, pl.ds(HEAD_DIM, pad), :] = ones_rows(TK_SEL)
    vt_w[pl.ds(0, HEAD_DIM), pl.ds(WINDOW, s_len)] = vwt_ref[...]
    vt_w[pl.ds(HEAD_DIM, pad), pl.ds(WINDOW, s_len)] = ones_rows(s_len)
    vt_c[pl.ds(0, HEAD_DIM), :] = vct_ref[...]
    vt_c[pl.ds(HEAD_DIM, pad), :] = ones_rows(N_CMP_PAD)


def _nsa_tile(grp, lane0, qi, sl_ref, q_ref, o_ref, gt_ref, kc_ref, vct_ref, ks_ref, vst_ref, kw_ref,
              vwt_ref, kaug_s, kaug_w, kaug_c, vt_s, vt_w, vt_c, qat_ref, score_ref, m_ref, acc_ref,
              out_ref, sa_ref, sb_ref, qats_ref, tiles_ref):
    q0 = qi * TQ
    nrep = HEADS_PER_GROUP

    for half in range(nrep // 2):
        qt = q_ref[:, pl.ds(lane0 + half * LANES, LANES)].astype(F32).T * (HEAD_DIM ** -0.5 * LOG2E)
        for sub in range(2):
            q_rows = qt[sub * HEAD_DIM:(sub + 1) * HEAD_DIM].astype(BF16)
            for ref in (qat_ref, qats_ref):
                ref[pl.ds(0, HEAD_DIM), pl.ds((2 * half + sub) * TQ, TQ)] = q_rows
    qat_ref[pl.ds(SEL_ROW0, HEAD_DIM), :] = jnp.zeros((HEAD_DIM, NQ), BF16)
    piece = lax.broadcasted_iota(jnp.int32, (POS_ROWS, TQ), 0)
    tail0 = POS_ROW0 + POS_ROWS
    for r in range(nrep):
        tile = jnp.where(piece == PAD_LANE - POS_ROW0, -BIG, 0.0)
        for i in range(2 * N_PIECES):
            tile = jnp.where(piece == i, sl_ref[(grp * nrep + r) * N_PIECES + i % N_PIECES], tile)
        for ref in (qat_ref, qats_ref):
            ref[pl.ds(POS_ROW0, POS_ROWS), pl.ds(r * TQ, TQ)] = tile.astype(BF16)
    for ref in (qat_ref, qats_ref):
        ref[pl.ds(tail0, AUG - tail0), :] = jnp.zeros((AUG - tail0, NQ), BF16)

    yield False
    t_row = q0 + (lax.broadcasted_iota(jnp.int32, (1, NQ), 1) & (TQ - 1))
    gate_row = lambda br: jnp.concatenate(
        [gt_ref[pl.ds(3 * (grp * nrep + r) + br, 1), :] for r in range(nrep)], axis=1)

    s = jnp.dot(kaug_c[...], qat_ref[...], preferred_element_type=F32)
    yield False
    cmp_end = CMP_STRIDE * lax.broadcasted_iota(jnp.int32, (N_CMP_PAD, 1), 0) + (L_CMP - 1)
    mask_c = cmp_end <= t_row
    s = jnp.where(mask_c, s, NEG)
    e = jnp.exp2(s - jnp.max(s, axis=0, keepdims=True))
    d = jnp.sum(e, axis=0, keepdims=True)
    p = e * jnp.where(t_row >= L_CMP - 1, 1.0 / d, 0.0)
    yield False
    p_sum = sum(p[:, r * TQ:(r + 1) * TQ] for r in range(nrep))
    o_c = jnp.dot(vt_c[...], p.astype(BF16), preferred_element_type=F32)
    out_ref[...] = gate_row(0) * o_c[:HEAD_DIM]
    yield False

    n_sb = score_ref.shape[0]
    jb = lax.broadcasted_iota(jnp.int32, (n_sb, TQ), 0)
    t_sel = q0 + lax.broadcasted_iota(jnp.int32, (n_sb, TQ), 1)
    valid = jb * L_SEL <= t_sel

    n_full = q0 // TK_SEL
    blocks_per_tile = TK_SEL // L_SEL

    def write_selection(selected):
        sel_bias = jnp.where(selected, 0.0, -BIG).astype(BF16)
        for r in range(nrep):
            qats_ref[pl.ds(SEL_ROW0, n_sb), pl.ds(r * TQ, TQ)] = sel_bias
        picked = jnp.where(selected, 1.0, 0.0)
        bits = sum(jnp.max(picked[k * blocks_per_tile:(k + 1) * blocks_per_tile], axis=(0, 1),
                           keepdims=True) * float(2 ** k) for k in range(n_sb // blocks_per_tile))
        bits = bits[0, 0].astype(jnp.int32)
        count = jnp.int32(0)
        for k in range(n_sb // blocks_per_tile):
            tiles_ref[count] = k
            count = count + jnp.where((k < n_full) & (((bits >> k) & 1) == 1), 1, 0)
        tiles_ref[count] = n_full
        return count

    def rank_blocks():
        cmp_start = CMP_STRIDE * lax.broadcasted_iota(jnp.int32, (n_sb, N_CMP_PAD), 1)
        sb_start = L_SEL * lax.broadcasted_iota(jnp.int32, (n_sb, N_CMP_PAD), 0)
        overlap_t = ((cmp_start < sb_start + L_SEL) & (cmp_start + L_CMP > sb_start)).astype(F32)
        imp_t = jnp.dot(overlap_t, p_sum, precision=lax.Precision.HIGHEST,
                        preferred_element_type=F32)
        cur = t_sel // L_SEL
        forced = (jb == 0) | (jb == cur) | (jb == cur - 1)
        score = jnp.where(valid, jnp.where(forced, jnp.inf, imp_t), -jnp.inf)
        score_ref[...] = score
        chunks = [score[c * SUBLANES:(c + 1) * SUBLANES] for c in range(n_sb // SUBLANES)]
        ranks = [jnp.zeros((SUBLANES, TQ), F32) for _ in chunks]
        sub_row = lax.broadcasted_iota(jnp.int32, (SUBLANES, TQ), 0)
        for i in range(n_sb):
            row = score_ref[pl.ds(i, 1), :]
            for c, chunk in enumerate(chunks):
                later = jnp.where(row >= chunk, 1.0, 0.0)
                earlier = jnp.where(row > chunk, 1.0, 0.0)
                if i < c * SUBLANES:
                    ranks[c] = ranks[c] + later
                elif i >= (c + 1) * SUBLANES:
                    ranks[c] = ranks[c] + earlier
                else:
                    ranks[c] = ranks[c] + jnp.where(sub_row > i - c * SUBLANES, later, earlier)
        rank = jnp.concatenate(ranks, axis=0)
        return write_selection(valid & (rank < float(N_SELECT)))

    n_win = WINDOW // TK_WIN
    k0 = pl.multiple_of(q0, TK_WIN)
    s_all = jnp.dot(kaug_w[pl.ds(k0, WINDOW + TQ), :], qat_ref[...], preferred_element_type=F32)
    yield False
    n_list = rank_blocks()
    yield False

    _flash_reset(m_ref, acc_ref)
    sel_scores = lambda i: _scores(kaug_s, tiles_ref[i], TK_SEL, qats_ref)
    sel_update = lambda s_ref, i, mask: _flash_update(s_ref[...], vt_s[tiles_ref[i]], mask, m_ref, acc_ref)
    sa_ref[...] = sel_scores(0)
    yield False

    s_w = [s_all[j * TK_WIN:(j + 1) * TK_WIN] for j in range(n_win + 1)]
    not_after = q0 + lax.broadcasted_iota(jnp.int32, (TK_WIN, 1), 0) <= t_row
    s_w[0] = jnp.where(not_after, NEG, s_w[0])
    s_w[-1] = jnp.where(not_after, s_w[-1], NEG)
    m_w = functools.reduce(jnp.maximum, [jnp.max(x, axis=0, keepdims=True) for x in s_w])
    p_w = jnp.concatenate([jnp.exp2(x - m_w).astype(BF16) for x in s_w], axis=0)
    yield False
    acc_w = jnp.dot(vt_w[:, pl.ds(k0, WINDOW + TQ)], p_w, preferred_element_type=F32)
    out_ref[...] += gate_row(2) * (acc_w[:HEAD_DIM] * (1.0 / acc_w[HEAD_DIM:HEAD_DIM + 1]))
    yield True

    def sel_pair(j, carry):
        i = 2 * j
        sb_ref[...] = sel_scores(i + 1)
        sel_update(sa_ref, i, None)
        sa_ref[...] = sel_scores(i + 2)
        sel_update(sb_ref, i + 1, None)
        return carry

    lax.fori_loop(0, n_list // 2, sel_pair, 0)

    def last_tiles(odd):
        own = n_list
        own_rows = pl.ds(pl.multiple_of(q0 - n_full * TK_SEL, TQ), TQ)
        if odd:
            sb_ref[...] = sel_scores(own)
            yield
            yield from _flash_update_steps(sa_ref[...], vt_s[tiles_ref[own - 1]], None, m_ref, acc_ref)
            yield
        own_ref = sb_ref if odd else sa_ref
        own_ref[own_rows, :] = jnp.where(not_after, own_ref[own_rows, :], NEG)
        yield from _flash_update_steps(own_ref[...], vt_s[tiles_ref[own]], None, m_ref, acc_ref)

    yield n_list % 2, last_tiles

    out_ref[...] += gate_row(1) * _flash_result(acc_ref)

    for half in range(nrep // 2):
        slab = jnp.concatenate([out_ref[:, pl.ds(2 * half * TQ, TQ)],
                                out_ref[:, pl.ds((2 * half + 1) * TQ, TQ)]], axis=0)
        o_ref[:, pl.ds(lane0 + half * LANES, LANES)] = slab.T.astype(o_ref.dtype)


def _nsa_attention(slope_pieces, q, cmp, cmp_t, keys, vals_t, glogit, bgate, constk, constc):
    b, s, _ = q.shape
    g, nrep, dh = N_GROUPS, HEADS_PER_GROUP, HEAD_DIM
    n_sb = s // L_SEL
    sq = pl.Squeezed()
    gp = GROUPS_PER_STEP
    slot = lambda c, *shape: pl.BlockSpec((sq, sq, gp) + shape, lambda i, j, t: (c, i, j, 0, 0))
    full = lambda *shape: pl.BlockSpec(shape, lambda i, j, t: (0,) * len(shape))
    per_group = lambda shape, dtype: pltpu.VMEM((gp,) + shape, dtype)
    return pl.pallas_call(
        _nsa_kernel,
        grid=(b, g // gp, s // TQ),
        in_specs=[
            pl.BlockSpec(memory_space=pltpu.SMEM),
            pl.BlockSpec((sq, TQ, gp * nrep * dh), lambda i, j, t: (i, t, j)),
            slot(0, N_CMP_PAD, dh), slot(1, dh, N_CMP_PAD),
            slot(0, s, dh), slot(0, dh, s), slot(1, s, dh), slot(1, dh, s),
            pl.BlockSpec((sq, TQ, LANES), lambda i, j, t: (i, t, 0)),
            full(1, LANES), full(s, AUG), full(N_CMP_PAD, AUG),
        ],
        out_specs=pl.BlockSpec((sq, TQ, gp * nrep * dh), lambda i, j, t: (i, t, j)),
        out_shape=jax.ShapeDtypeStruct((b, s, NSA_Q), BF16),
        scratch_shapes=[
            per_group((s, AUG), BF16), per_group((WINDOW + s, AUG), BF16), per_group((N_CMP_PAD, AUG), BF16),
            per_group((s // TK_SEL, V_ROWS, TK_SEL), BF16), per_group((V_ROWS, WINDOW + s), BF16),
            per_group((V_ROWS, N_CMP_PAD), BF16),
            per_group((AUG, NQ), BF16), per_group((n_sb, TQ), F32), per_group((1, NQ), F32),
            per_group((V_ROWS, NQ), F32), per_group((dh, NQ), F32),
            per_group((TK_SEL, NQ), F32), per_group((TK_SEL, NQ), F32),
            per_group((AUG, NQ), BF16),
            pltpu.SMEM((gp, s // TK_SEL + 1), jnp.int32),
            pltpu.VMEM((LANES, TQ), F32),
        ],
        compiler_params=_cparams("arbitrary", "arbitrary", "arbitrary"),
        name="nsa_attention",
    )(slope_pieces, q, cmp, cmp_t, keys, vals_t, keys, vals_t, glogit, bgate, constk, constc)


def _position_pieces(pos):
    hi = (pos // L_SEL) * L_SEL
    return jnp.stack([hi] * N_PIECES + [pos - hi] * N_PIECES, axis=1).astype(F32)


def _nsa_constants(s):
    pos = jnp.arange(s)
    constk = jnp.zeros((s, AUG), F32)
    constk = constk.at[:, SEL_ROW0:SEL_ROW0 + s // L_SEL].set(jax.nn.one_hot(pos // L_SEL, s // L_SEL))
    constk = constk.at[:, POS_ROW0:POS_ROW0 + 2 * N_PIECES].set(_position_pieces(pos))
    cmp_end = CMP_STRIDE * jnp.arange(N_CMP_PAD) + (L_CMP - 1)
    constc = jnp.zeros((N_CMP_PAD, AUG), F32)
    constc = constc.at[:, POS_ROW0:POS_ROW0 + 2 * N_PIECES].set(_position_pieces(cmp_end))
    slopes = 2.0 ** (-8.0 * jnp.arange(1, N_HEADS + 1, dtype=F32) / N_HEADS)
    pieces, rest = [], slopes * LOG2E
    for _ in range(N_PIECES):
        piece = rest.astype(BF16).astype(F32)
        pieces.append(piece)
        rest = rest - piece
    return constk.astype(BF16), constc.astype(BF16), jnp.stack(pieces, axis=1).reshape(-1)


def _nsa_mixer(x, norm_g, w_in, b_gate, cmp_pos, cmp_w1, cmp_b1, cmp_w2, cmp_b2):
    b, s, d = x.shape
    g, nrep, dh = N_GROUPS, HEADS_PER_GROUP, HEAD_DIM
    w_in_p = jnp.pad(w_in, ((0, 0), (0, NSA_IN_PAD - NSA_IN))).astype(BF16)
    v_cols = lambda i: w_in_p[:, NSA_Q + i * NSA_KV:NSA_Q + (i + 1) * NSA_KV].T
    q, raw, keys, vals_t, glogit = _nsa_in_proj(x, norm_g, w_in_p, jnp.stack([v_cols(3), v_cols(5)]),
                                                tm=512)
    raw = raw.reshape(b, s // CMP_STRIDE, CMP_STRIDE, 2, g, dh).transpose(3, 0, 4, 1, 2, 5).reshape(
        2, b, g, s // CMP_STRIDE, CMP_STRIDE * dh)
    cmp, cmp_t = _compress(raw, cmp_pos.reshape(2, 1, L_CMP * dh), cmp_w1.astype(BF16), cmp_b1,
                           cmp_w2.astype(BF16), cmp_b2)
    constk, constc, slope_pieces = _nsa_constants(s)
    bgate = jnp.pad(b_gate, (0, LANES - 3 * N_HEADS)).reshape(1, LANES)
    return _nsa_attention(slope_pieces, q, cmp, cmp_t, keys, vals_t, glogit, bgate, constk, constc)


TT = 256
SUB = SUBLANES


GATE_BAND = 256


def _gate_bands():
    block = D_RNN // LRU_BLOCKS
    bands = []
    for c0 in range(0, D_RNN, GATE_BAND):
        width = min(GATE_BAND, D_RNN - c0)
        k_lo = (c0 // block) * block // LANES * LANES
        k_hi = -(-(((c0 + width - 1) // block + 1) * block) // LANES) * LANES
        bands.append((k_lo, min(k_hi, D_RNN), c0, width))
    return tuple(bands)


GATE_BANDS = _gate_bands()


def _pack_gate_weights(w_a, w_x):
    dense = lambda w: jax.scipy.linalg.block_diag(*[w[i] for i in range(LRU_BLOCKS)])
    slabs = []
    for k_lo, k_hi, c0, width in GATE_BANDS:
        band = [jnp.pad(dense(w)[k_lo:k_hi, c0:c0 + width], ((0, 0), (0, GATE_BAND - width)))
                for w in (w_a, w_x)]
        slabs.append(jnp.concatenate(band, axis=1))
    return jnp.concatenate(slabs, axis=0).astype(BF16)


SEQS_PER_STEP = 2


def _lru_kernel(x_all, g_ref, win_ref, cw_ref, cb_ref, wax_ref, bax_ref, lam_ref, wout_ref, o_all,
                *scratch):
    seqs = [tuple(ref.at[h] for ref in (x_all, o_all) + scratch) for h in range(SEQS_PER_STEP)]

    @pl.when(pl.program_id(1) == 0)
    def _():
        for x_ref, o_ref, ext_ref, a_ref, u_ref, h_ref, gate_ref in seqs:
            ext_ref[pl.ds(TT, SUB), :] = jnp.zeros((SUB, D_RNN), F32)
            h_ref[...] = jnp.zeros_like(h_ref)

    tiles = [_lru_tile(g_ref, win_ref, cw_ref, cb_ref, wax_ref, bax_ref, lam_ref, wout_ref, *refs)
             for refs in seqs]
    while tiles:
        tiles = [tile for tile in tiles if next(tile, True) is not True]


def _lru_tile(g_ref, win_ref, cw_ref, cb_ref, wax_ref, bax_ref, lam_ref, wout_ref,
              x_ref, o_ref, ext_ref, a_ref, u_ref, h_ref, gate_ref):
    xn = _rms(x_ref[...], g_ref[...]).astype(BF16)
    gate_ref[...] = _gelu_tanh(jnp.dot(xn, win_ref[:, :D_RNN], preferred_element_type=F32))
    ext_ref[pl.ds(0, SUB), :] = ext_ref[pl.ds(TT, SUB), :]
    ext_ref[pl.ds(SUB, TT), :] = jnp.dot(xn, win_ref[:, D_RNN:], preferred_element_type=F32)
    yield
    cw = cw_ref[...]
    xr = cb_ref[...] + sum(
        cw[w:w + 1, :] * ext_ref[pl.ds(SUB - (CONV_W - 1) + w, TT), :] for w in range(CONV_W))

    lam = lam_ref[...]
    softplus_neg = jnp.maximum(-lam, 0.0) + jnp.log1p(jnp.exp(-jnp.abs(lam)))
    half_log_scale = (-0.5 * LRU_C * LOG2E) * softplus_neg
    xr_b = xr.astype(BF16)
    half_bias = bax_ref[...]
    row0 = 0
    for k_lo, k_hi, c0, width in GATE_BANDS:
        z = jnp.dot(xr_b[:, k_lo:k_hi], wax_ref[row0:row0 + k_hi - k_lo, :], preferred_element_type=F32)
        row0 += k_hi - k_lo
        tanh_r = jnp.tanh(z[:, :width] + half_bias[:, c0:c0 + width])
        tanh_i = jnp.tanh(z[:, GATE_BAND:GATE_BAND + width] + half_bias[:, D_RNN + c0:D_RNN + c0 + width])
        scale = half_log_scale[:, c0:c0 + width]
        a = jnp.exp2(scale * tanh_r + scale)
        a_ref[:, c0:c0 + width] = a
        half_xr = 0.5 * xr[:, c0:c0 + width]
        u_ref[:, c0:c0 + width] = jnp.sqrt(jnp.maximum(1.0 - a * a, 0.0)) * (half_xr * tanh_i + half_xr)
    yield

    row = lax.broadcasted_iota(jnp.int32, (SUB, D_RNN), 0)

    def scan_rows(c, h_prev):
        r0 = c * SUB
        a_c = a_ref[pl.ds(r0, SUB), :]
        u_c = u_ref[pl.ds(r0, SUB), :]
        shift = 1
        while shift < SUB:
            keep = row >= shift
            u_c = u_c + a_c * jnp.where(keep, pltpu.roll(u_c, shift, axis=0), 0.0)
            a_c = a_c * jnp.where(keep, pltpu.roll(a_c, shift, axis=0), 1.0)
            shift *= 2
        h_c = u_c + a_c * h_prev
        u_ref[pl.ds(r0, SUB), :] = h_c
        return jnp.broadcast_to(h_c[SUB - 1:SUB, :], (SUB, D_RNN))

    h_state = h_ref[...]
    for c in range(TT // SUB):
        h_state = scan_rows(c, h_state)
        if c % (TT // SUB // 4) == TT // SUB // 4 - 1:
            yield
    h_ref[...] = h_state
    gated = (u_ref[...] * gate_ref[...]).astype(BF16)
    o_ref[...] = x_ref[...] + jnp.dot(gated, wout_ref[...], preferred_element_type=F32)


def _lru_mixer(x, norm_g, w_in, conv_w, conv_b, w_a, b_a, w_x, b_x, lam, w_out):
    b, s, d = x.shape
    wax = _pack_gate_weights(0.5 * w_a, 0.5 * w_x)
    bax = 0.5 * jnp.concatenate([b_a, b_x])
    nseq = SEQS_PER_STEP
    resident = lambda *shape: pl.BlockSpec(shape, lambda i, t: (0,) * len(shape),
                                           pipeline_mode=pl.Buffered(1))
    per_seq = lambda rows: pltpu.VMEM((nseq, rows, D_RNN), F32)
    return pl.pallas_call(
        _lru_kernel,
        grid=(b // nseq, s // TT),
        in_specs=[
            pl.BlockSpec((nseq, TT, d), lambda i, t: (i, t, 0)),
            resident(1, d), resident(d, 2 * D_RNN), resident(CONV_W, D_RNN), resident(1, D_RNN),
            resident(*wax.shape), resident(1, 2 * D_RNN), resident(1, D_RNN), resident(D_RNN, d),
        ],
        out_specs=pl.BlockSpec((nseq, TT, d), lambda i, t: (i, t, 0)),
        out_shape=jax.ShapeDtypeStruct((b, s, d), F32),
        scratch_shapes=[per_seq(TT + SUB), per_seq(TT), per_seq(TT), per_seq(SUB), per_seq(TT)],
        compiler_params=_cparams("parallel", "arbitrary"),
        name="lru_block",
    )(x, norm_g.reshape(1, d), w_in.astype(BF16), conv_w, conv_b.reshape(1, -1), wax,
      bax.reshape(1, -1), lam.reshape(1, -1), w_out.astype(BF16))


def _layer_to_bf16_kernel(w_ref, o_ref):
    o_ref[...] = w_ref[...].astype(o_ref.dtype)


def _layer_to_bf16(w, layer, *, n_blocks=8):
    _, rows, cols = w.shape
    tr = rows // n_blocks
    return pl.pallas_call(
        _layer_to_bf16_kernel,
        grid=(n_blocks,),
        in_specs=[pl.BlockSpec((pl.Squeezed(), tr, cols), lambda i: (layer, i, 0))],
        out_specs=pl.BlockSpec((tr, cols), lambda i: (i, 0)),
        out_shape=jax.ShapeDtypeStruct((rows, cols), BF16),
        compiler_params=_cparams("parallel"),
        name="weights_to_bf16",
    )(w)


def kernel(x, norm_mix, norm_ffn, norm_final, nsa_w_in, nsa_b_gate, nsa_cmp_pos, nsa_cmp_w1,
           nsa_cmp_b1, nsa_cmp_w2, nsa_cmp_b2, nsa_w_out, lru_w_in, lru_conv_w, lru_conv_b,
           lru_w_a, lru_b_a, lru_w_x, lru_b_x, lru_lambda, lru_w_out, ffn_w_in, ffn_w_out):
    b, s, d = x.shape
    ffn = functools.partial(_ffn, tm=512, tf=256)
    attn = _nsa_mixer(x, norm_mix[0], nsa_w_in[0], nsa_b_gate[0], nsa_cmp_pos[0], nsa_cmp_w1[0],
                      nsa_cmp_b1[0], nsa_cmp_w2[0], nsa_cmp_b2[0])
    x = ffn(x.reshape(b * s, d), norm_ffn[0], _layer_to_bf16(ffn_w_in, 0), _layer_to_bf16(ffn_w_out, 0),
            norm_final, final_norm=False, name="ffn0", mix=attn.reshape(b * s, NSA_Q),
            w_mix=nsa_w_out[0].astype(BF16)).reshape(b, s, d)
    x = _lru_mixer(x, norm_mix[1], lru_w_in[0], lru_conv_w[0], lru_conv_b[0], lru_w_a[0], lru_b_a[0],
                   lru_w_x[0], lru_b_x[0], lru_lambda[0], lru_w_out[0])
    x = ffn(x.reshape(b * s, d), norm_ffn[1], _layer_to_bf16(ffn_w_in, 1), _layer_to_bf16(ffn_w_out, 1),
            norm_final, final_norm=True, name="ffn1").reshape(b, s, d)
    return x
```

```python
import functools
import itertools

import jax
import jax.numpy as jnp
from jax import lax
from jax.experimental import pallas as pl
from jax.experimental.pallas import tpu as pltpu

F32 = jnp.float32
BF16 = jnp.bfloat16

D_MODEL = 1024
N_HEADS = 16
N_GROUPS = 4
HEADS_PER_GROUP = N_HEADS // N_GROUPS
HEAD_DIM = 64
L_CMP = 32
CMP_STRIDE = 16
L_SEL = 64
N_SELECT = 16
WINDOW = 512
NSA_Q = N_HEADS * HEAD_DIM
NSA_KV = N_GROUPS * HEAD_DIM
NSA_IN = NSA_Q + 6 * NSA_KV + 3 * N_HEADS
D_RNN = 1408
LRU_BLOCKS = 8
LRU_C = 8.0
CONV_W = 4
D_FF = 2816
EPS = 1e-6

LANES = 128
SUBLANES = 8
NEG = -1e30
VMEM_LIMIT = 56 * 1024 * 1024

TQ = 256
TK_SEL = 512
TK_WIN = 256
N_CMP_PAD = 256


def _cparams(*sem):
    return pltpu.CompilerParams(dimension_semantics=sem, vmem_limit_bytes=VMEM_LIMIT)


def _rms(x, g):
    ms = jnp.mean(x * x, axis=-1, keepdims=True)
    return x * lax.rsqrt(ms + EPS) * g


def _gelu_tanh(x):
    c = (2.0 / jnp.pi) ** 0.5
    half_x = 0.5 * x
    return half_x * jnp.tanh(x * (c + (c * 0.044715) * (x * x))) + half_x


def _nt_dot(a, b):
    return lax.dot_general(a, b, (((1,), (1,)), ((), ())), preferred_element_type=F32)


def _ffn_kernel(*refs, final_norm, tf, has_mix):
    x_ref, refs = refs[0], refs[1:]
    x = x_ref[...]
    if has_mix:
        (mix_ref, wmix_ref), refs = refs[:2], refs[2:]
        x = x + jnp.dot(mix_ref[...], wmix_ref[...], preferred_element_type=F32)
    g_ref, win_ref, wout_ref, gf_ref, o_ref, hid_ref = refs
    xn = _rms(x, g_ref[...]).astype(BF16)
    for c0 in range(0, D_FF, tf):
        gate = jnp.dot(xn, win_ref[:, c0:c0 + tf], preferred_element_type=F32)
        up = jnp.dot(xn, win_ref[:, D_FF + c0:D_FF + c0 + tf], preferred_element_type=F32)
        hid_ref[:, c0:c0 + tf] = ((0.5 * gate) * (1.0 + jnp.tanh(0.5 * gate)) * up).astype(BF16)
    y = x + jnp.dot(hid_ref[...], wout_ref[...], preferred_element_type=F32)
    if final_norm:
        y = _rms(y, gf_ref[...])
    o_ref[...] = y


def _ffn(x, g, w_in, w_out, g_final, *, final_norm, tm, tf, name, mix=None, w_mix=None):
    m, d = x.shape
    resident = lambda *shape: pl.BlockSpec(shape, lambda i: (0,) * len(shape),
                                           pipeline_mode=pl.Buffered(1))
    rows = lambda width: pl.BlockSpec((tm, width), lambda i: (i, 0))
    if mix is None:
        mix_specs, mix_args = [], ()
    else:
        mix_specs, mix_args = [rows(mix.shape[1]), resident(*w_mix.shape)], (mix, w_mix)
    return pl.pallas_call(
        functools.partial(_ffn_kernel, final_norm=final_norm, tf=tf, has_mix=mix is not None),
        grid=(m // tm,),
        in_specs=[rows(d)] + mix_specs + [
            resident(1, d), resident(d, 2 * D_FF), resident(D_FF, d), resident(1, d)],
        out_specs=rows(d),
        out_shape=jax.ShapeDtypeStruct((m, d), F32),
        scratch_shapes=[pltpu.VMEM((tm, D_FF), BF16)],
        compiler_params=_cparams("parallel"),
        name=name,
    )(x, *mix_args, g.reshape(1, d), w_in, w_out, g_final.reshape(1, d))


NSA_IN_PAD = -(-NSA_IN // LANES) * LANES


def _nsa_in_proj_kernel(x_ref, g_ref, w_ref, wvt_ref, q_ref, raw_ref, k_ref, vt_ref, gl_ref):
    xn = _rms(x_ref[...], g_ref[...]).astype(BF16)
    q_ref[...] = jnp.dot(xn, w_ref[:, :NSA_Q], preferred_element_type=F32).astype(q_ref.dtype)
    for i, (dst, slot) in {0: (raw_ref, 0), 1: (raw_ref, 1), 2: (k_ref, 0), 4: (k_ref, 1)}.items():
        col0 = NSA_Q + i * NSA_KV
        part = jnp.dot(xn, w_ref[:, col0:col0 + NSA_KV], preferred_element_type=F32)
        for grp in range(N_GROUPS):
            dst[slot, grp] = part[:, grp * HEAD_DIM:(grp + 1) * HEAD_DIM].astype(dst.dtype)
    for slot in range(2):
        part_t = _nt_dot(wvt_ref[slot], xn)
        for grp in range(N_GROUPS):
            vt_ref[slot, grp] = part_t[grp * HEAD_DIM:(grp + 1) * HEAD_DIM].astype(vt_ref.dtype)
    gl_ref[...] = jnp.dot(xn, w_ref[:, NSA_Q + 6 * NSA_KV:], preferred_element_type=F32)


def _nsa_in_proj(x, g, w, wvt, *, tm):
    b, s, d = x.shape
    sq = pl.Squeezed()
    grouped = lambda dtype: jax.ShapeDtypeStruct((2, b, N_GROUPS, s, HEAD_DIM), dtype)
    grouped_spec = pl.BlockSpec((2, sq, N_GROUPS, tm, HEAD_DIM), lambda i, t: (0, i, 0, t, 0))
    return pl.pallas_call(
        _nsa_in_proj_kernel,
        grid=(b, s // tm),
        in_specs=[
            pl.BlockSpec((sq, tm, d), lambda i, t: (i, t, 0)),
            pl.BlockSpec((1, d), lambda i, t: (0, 0)),
            pl.BlockSpec((d, NSA_IN_PAD), lambda i, t: (0, 0)),
            pl.BlockSpec((2, NSA_KV, d), lambda i, t: (0, 0, 0)),
        ],
        out_specs=[
            pl.BlockSpec((sq, tm, NSA_Q), lambda i, t: (i, t, 0)),
            grouped_spec, grouped_spec,
            pl.BlockSpec((2, sq, N_GROUPS, HEAD_DIM, tm), lambda i, t: (0, i, 0, 0, t)),
            pl.BlockSpec((sq, tm, LANES), lambda i, t: (i, t, 0)),
        ],
        out_shape=[
            jax.ShapeDtypeStruct((b, s, NSA_Q), BF16),
            grouped(BF16), grouped(BF16),
            jax.ShapeDtypeStruct((2, b, N_GROUPS, HEAD_DIM, s), BF16),
            jax.ShapeDtypeStruct((b, s, LANES), F32),
        ],
        compiler_params=_cparams("parallel", "parallel"),
        name="nsa_in_proj",
    )(x, g.reshape(1, d), w, wvt)


def _compress_kernel(r_ref, pos_ref, w1_ref, b1_ref, w2_ref, b2_ref, w2t_ref, b2t_ref, o_ref, ot_ref,
                     bias_ref):
    half = CMP_STRIDE * HEAD_DIM
    rows = r_ref[...]
    part_lo = jnp.dot(rows, w1_ref[:half, :], preferred_element_type=F32)
    part_hi = jnp.dot(rows, w1_ref[half:, :], preferred_element_type=F32)
    @pl.when((pl.program_id(1) == 0) & (pl.program_id(2) == 0))
    def _():
        bias_ref[...] = b1_ref[...] + jnp.dot(
            pos_ref[...], w1_ref[...].astype(F32), precision=lax.Precision.HIGHEST,
            preferred_element_type=F32)

    hid = part_lo + pltpu.roll(part_hi, N_CMP_PAD - 1, axis=0) + bias_ref[...]
    hid = jax.nn.gelu(hid, approximate=True).astype(BF16)
    o_ref[...] = (jnp.dot(hid, w2_ref[...], preferred_element_type=F32) + b2_ref[...]).astype(o_ref.dtype)
    ot_ref[...] = (_nt_dot(w2t_ref[...], hid) + b2t_ref[...]).astype(ot_ref.dtype)


def _compress(raw, pos, w1, b1, w2, b2):
    two, b, g, n, width = raw.shape
    hidden = w1.shape[-1]
    sq = pl.Squeezed()
    per_branch = lambda *shape: pl.BlockSpec((sq,) + shape, lambda c, i, j: (c,) + (0,) * len(shape))
    return pl.pallas_call(
        _compress_kernel,
        grid=(two, b, g),
        in_specs=[
            pl.BlockSpec((sq, sq, sq, n, width), lambda c, i, j: (c, i, j, 0, 0)),
            per_branch(1, 2 * width), per_branch(2 * width, hidden), per_branch(1, hidden),
            per_branch(hidden, HEAD_DIM), per_branch(1, HEAD_DIM),
            per_branch(HEAD_DIM, hidden), per_branch(HEAD_DIM, 1),
        ],
        out_specs=[
            pl.BlockSpec((sq, sq, sq, n, HEAD_DIM), lambda c, i, j: (c, i, j, 0, 0)),
            pl.BlockSpec((sq, sq, sq, HEAD_DIM, n), lambda c, i, j: (c, i, j, 0, 0)),
        ],
        out_shape=[
            jax.ShapeDtypeStruct((two, b, g, n, HEAD_DIM), BF16),
            jax.ShapeDtypeStruct((two, b, g, HEAD_DIM, n), BF16),
        ],
        scratch_shapes=[pltpu.VMEM((1, hidden), F32)],
        compiler_params=_cparams("arbitrary", "arbitrary", "arbitrary"),
        name="nsa_compress",
    )(raw, pos, w1, b1.reshape(two, 1, hidden), w2, b2.reshape(two, 1, HEAD_DIM),
      w2.transpose(0, 2, 1), b2.reshape(two, HEAD_DIM, 1))


AUG = 256
SEL_ROW0 = HEAD_DIM
POS_ROW0 = 2 * HEAD_DIM
N_PIECES = 3
PAD_LANE = POS_ROW0 + 2 * N_PIECES
POS_ROWS = 16
V_ROWS = 80
NQ = HEADS_PER_GROUP * TQ
BIG = 1e30
LOG2E = 1.4426950408889634
assert TQ == TK_WIN and TK_SEL % TQ == 0 and WINDOW % TK_WIN == 0


def _flash_reset(m_ref, acc_ref):
    m_ref[...] = jnp.full_like(m_ref, NEG)
    acc_ref[...] = jnp.zeros_like(acc_ref)


def _scores(kaug, tile, tk, qat_ref):
    k0 = pl.multiple_of(tile * tk, tk)
    return jnp.dot(kaug[pl.ds(k0, tk), :], qat_ref[...], preferred_element_type=F32)


def _flash_update_steps(s, vt_tile, mask, m_ref, acc_ref):
    if mask is not None:
        s = jnp.where(mask, s, NEG)
    m_old = m_ref[...]
    m_new = jnp.maximum(m_old, jnp.max(s, axis=0, keepdims=True))
    yield
    p = jnp.exp2(s - m_new).astype(BF16)
    yield
    acc_ref[...] = jnp.exp2(m_old - m_new) * acc_ref[...] + jnp.dot(
        vt_tile, p, preferred_element_type=F32)
    m_ref[...] = m_new


def _flash_update(*args):
    for _ in _flash_update_steps(*args):
        pass


def _round_robin(generators):
    generators = list(generators)
    while generators:
        generators = [g for g in generators if next(g, True) is not True]


def _flash_result(acc_ref):
    return acc_ref[pl.ds(0, HEAD_DIM), :] * (1.0 / acc_ref[pl.ds(HEAD_DIM, 1), :])


GROUPS_PER_STEP = 2


def _nsa_kernel(sl_ref, q_ref, kc_all, vct_all, ks_all, vst_all, kw_all, vwt_all, gl_ref, bg_ref,
                constk_ref, constc_ref, o_ref, *scratch):
    qi = pl.program_id(2)
    gt_ref = scratch[-1]
    per_group = (kc_all, vct_all, ks_all, vst_all, kw_all, vwt_all) + scratch[:-1]
    groups = [tuple(ref.at[h] for ref in per_group) for h in range(GROUPS_PER_STEP)]

    @pl.when(qi == 0)
    def _():
        for refs in groups:
            _nsa_assemble(constk_ref, constc_ref, *refs)

    gt_ref[...] = jax.nn.sigmoid(gl_ref[...] + bg_ref[...]).T
    tiles = [_nsa_tile(pl.program_id(1) * GROUPS_PER_STEP + h, h * HEADS_PER_GROUP * HEAD_DIM, qi,
                       sl_ref, q_ref, o_ref, gt_ref, *refs) for h, refs in enumerate(groups)]
    heads = list(tiles)
    while heads:
        heads = [tile for tile in heads if not next(tile)]
    ends = [next(tile) for tile in tiles]
    for parities in itertools.product((0, 1), repeat=len(tiles)):
        @pl.when(functools.reduce(jnp.logical_and, [odd == want for (odd, _), want in zip(ends, parities)]))
        def _():
            _round_robin(last(want) for (_, last), want in zip(ends, parities))
    for tile in tiles:
        next(tile, None)


def _nsa_assemble(constk_ref, constc_ref, kc_ref, vct_ref, ks_ref, vst_ref, kw_ref, vwt_ref,
                  kaug_s, kaug_w, kaug_c, vt_s, vt_w, vt_c, *unused):
    s_len = ks_ref.shape[0]
    for kaug, row0, const, k in ((kaug_s, 0, constk_ref, ks_ref), (kaug_w, WINDOW, constk_ref, kw_ref),
                                 (kaug_c, 0, constc_ref, kc_ref)):
        kaug[pl.ds(row0, k.shape[0]), :] = const[...]
        kaug[pl.ds(row0, k.shape[0]), pl.ds(0, HEAD_DIM)] = k[...]
    pad_lane = lax.broadcasted_iota(jnp.int32, (WINDOW, AUG), 1) == PAD_LANE
    kaug_w[pl.ds(0, WINDOW), :] = jnp.where(pad_lane, 1.0, 0.0).astype(BF16)
    vt_w[:, pl.ds(0, WINDOW)] = jnp.zeros((V_ROWS, WINDOW), BF16)
    pad = V_ROWS - HEAD_DIM
    ones_rows = lambda n: jnp.where(
        lax.broadcasted_iota(jnp.int32, (pad, n), 0) == 0, 1.0, 0.0).astype(BF16)
    for c in range(vt_s.shape[0]):
        vt_s[c, pl.ds(0, HEAD_DIM), :] = vst_ref[:, pl.ds(c * TK_SEL, TK_SEL)]
        vt_s[c, pl.ds(HEAD_DIM, pad), :] = ones_rows(TK_SEL)
    vt_w[pl.ds(0, HEAD_DIM), pl.ds(WINDOW, s_len)] = vwt_ref[...]
    vt_w[pl.ds(HEAD_DIM, pad), pl.ds(WINDOW, s_len)] = ones_rows(s_len)
    vt_c[pl.ds(0, HEAD_DIM), :] = vct_ref[...]
    vt_c[pl.ds(HEAD_DIM, pad), :] = ones_rows(N_CMP_PAD)


def _nsa_tile(grp, lane0, qi, sl_ref, q_ref, o_ref, gt_ref, kc_ref, vct_ref, ks_ref, vst_ref, kw_ref,
              vwt_ref, kaug_s, kaug_w, kaug_c, vt_s, vt_w, vt_c, qat_ref, score_ref, m_ref, acc_ref,
              out_ref, sa_ref, sb_ref, qats_ref, tiles_ref):
    q0 = qi * TQ
    nrep = HEADS_PER_GROUP

    for half in range(nrep // 2):
        qt = q_ref[:, pl.ds(lane0 + half * LANES, LANES)].astype(F32).T * (HEAD_DIM ** -0.5 * LOG2E)
        for sub in range(2):
            q_rows = qt[sub * HEAD_DIM:(sub + 1) * HEAD_DIM].astype(BF16)
            for ref in (qat_ref, qats_ref):
                ref[pl.ds(0, HEAD_DIM), pl.ds((2 * half + sub) * TQ, TQ)] = q_rows
    qat_ref[pl.ds(SEL_ROW0, HEAD_DIM), :] = jnp.zeros((HEAD_DIM, NQ), BF16)
    piece = lax.broadcasted_iota(jnp.int32, (POS_ROWS, TQ), 0)
    tail0 = POS_ROW0 + POS_ROWS
    for r in range(nrep):
        tile = jnp.where(piece == PAD_LANE - POS_ROW0, -BIG, 0.0)
        for i in range(2 * N_PIECES):
            tile = jnp.where(piece == i, sl_ref[(grp * nrep + r) * N_PIECES + i % N_PIECES], tile)
        for ref in (qat_ref, qats_ref):
            ref[pl.ds(POS_ROW0, POS_ROWS), pl.ds(r * TQ, TQ)] = tile.astype(BF16)
    for ref in (qat_ref, qats_ref):
        ref[pl.ds(tail0, AUG - tail0), :] = jnp.zeros((AUG - tail0, NQ), BF16)

    yield False
    t_row = q0 + (lax.broadcasted_iota(jnp.int32, (1, NQ), 1) & (TQ - 1))
    gate_row = lambda br: jnp.concatenate(
        [gt_ref[pl.ds(3 * (grp * nrep + r) + br, 1), :] for r in range(nrep)], axis=1)

    s = jnp.dot(kaug_c[...], qat_ref[...], preferred_element_type=F32)
    yield False
    cmp_end = CMP_STRIDE * lax.broadcasted_iota(jnp.int32, (N_CMP_PAD, 1), 0) + (L_CMP - 1)
    gate_c = gate_row(0)
    p_sum = jnp.zeros((N_CMP_PAD, TQ), F32)
    for lo in range(0, NQ, NQ // 2):
        cols = slice(lo, lo + NQ // 2)
        t_half = t_row[:, cols]
        s_h = jnp.where(cmp_end <= t_half, s[:, cols], NEG)
        e = jnp.exp2(s_h - jnp.max(s_h, axis=0, keepdims=True))
        d = jnp.sum(e, axis=0, keepdims=True)
        p = e * jnp.where(t_half >= L_CMP - 1, 1.0 / d, 0.0)
        yield False
        p_sum = p_sum + sum(p[:, r * TQ:(r + 1) * TQ] for r in range(nrep // 2))
        o_c = jnp.dot(vt_c[...], p.astype(BF16), preferred_element_type=F32)
        out_ref[:, cols] = gate_c[:, cols] * o_c[:HEAD_DIM]
    yield False

    n_sb = score_ref.shape[0]
    jb = lax.broadcasted_iota(jnp.int32, (n_sb, TQ), 0)
    t_sel = q0 + lax.broadcasted_iota(jnp.int32, (n_sb, TQ), 1)
    valid = jb * L_SEL <= t_sel

    n_full = q0 // TK_SEL
    blocks_per_tile = TK_SEL // L_SEL

    def write_selection(selected):
        sel_bias = jnp.where(selected, 0.0, -BIG).astype(BF16)
        for r in range(nrep):
            qats_ref[pl.ds(SEL_ROW0, n_sb), pl.ds(r * TQ, TQ)] = sel_bias
        picked = jnp.where(selected, 1.0, 0.0)
        bits = sum(jnp.max(picked[k * blocks_per_tile:(k + 1) * blocks_per_tile], axis=(0, 1),
                           keepdims=True) * float(2 ** k) for k in range(n_sb // blocks_per_tile))
        bits = bits[0, 0].astype(jnp.int32)
        count = jnp.int32(0)
        for k in range(n_sb // blocks_per_tile):
            tiles_ref[count] = k
            count = count + jnp.where((k < n_full) & (((bits >> k) & 1) == 1), 1, 0)
        tiles_ref[count] = n_full
        return count

    def rank_blocks():
        cmp_start = CMP_STRIDE * lax.broadcasted_iota(jnp.int32, (n_sb, N_CMP_PAD), 1)
        sb_start = L_SEL * lax.broadcasted_iota(jnp.int32, (n_sb, N_CMP_PAD), 0)
        overlap_t = ((cmp_start < sb_start + L_SEL) & (cmp_start + L_CMP > sb_start)).astype(F32)
        imp_t = jnp.dot(overlap_t, p_sum, precision=lax.Precision.HIGHEST,
                        preferred_element_type=F32)
        cur = t_sel // L_SEL
        forced = (jb == 0) | (jb == cur) | (jb == cur - 1)
        score = jnp.where(valid, jnp.where(forced, jnp.inf, imp_t), -jnp.inf)
        score_ref[...] = score
        chunks = [score[c * SUBLANES:(c + 1) * SUBLANES] for c in range(n_sb // SUBLANES)]
        ranks = [jnp.zeros((SUBLANES, TQ), F32) for _ in chunks]
        sub_row = lax.broadcasted_iota(jnp.int32, (SUBLANES, TQ), 0)
        for i in range(n_sb):
            row = score_ref[pl.ds(i, 1), :]
            for c, chunk in enumerate(chunks):
                later = jnp.where(row >= chunk, 1.0, 0.0)
                earlier = jnp.where(row > chunk, 1.0, 0.0)
                if i < c * SUBLANES:
                    ranks[c] = ranks[c] + later
                elif i >= (c + 1) * SUBLANES:
                    ranks[c] = ranks[c] + earlier
                else:
                    ranks[c] = ranks[c] + jnp.where(sub_row > i - c * SUBLANES, later, earlier)
        rank = jnp.concatenate(ranks, axis=0)
        return write_selection(valid & (rank < float(N_SELECT)))

    n_win = WINDOW // TK_WIN
    k0 = pl.multiple_of(q0, TK_WIN)
    s_all = jnp.dot(kaug_w[pl.ds(k0, WINDOW + TQ), :], qat_ref[...], preferred_element_type=F32)
    yield False
    n_list = rank_blocks()
    yield False

    _flash_reset(m_ref, acc_ref)
    sel_scores = lambda i: _scores(kaug_s, tiles_ref[i], TK_SEL, qats_ref)
    sel_update = lambda s_ref, i, mask: _flash_update(s_ref[...], vt_s[tiles_ref[i]], mask, m_ref, acc_ref)
    sa_ref[...] = sel_scores(0)
    yield False

    not_after = q0 + lax.broadcasted_iota(jnp.int32, (TK_WIN, 1), 0) <= t_row
    gate_w = gate_row(2)
    for lo in range(0, NQ, NQ // 2):
        cols = slice(lo, lo + NQ // 2)
        s_w = [s_all[j * TK_WIN:(j + 1) * TK_WIN, cols] for j in range(n_win + 1)]
        s_w[0] = jnp.where(not_after[:, cols], NEG, s_w[0])
        s_w[-1] = jnp.where(not_after[:, cols], s_w[-1], NEG)
        m_w = functools.reduce(jnp.maximum, [jnp.max(x, axis=0, keepdims=True) for x in s_w])
        p_w = jnp.concatenate([jnp.exp2(x - m_w).astype(BF16) for x in s_w], axis=0)
        yield False
        acc_w = jnp.dot(vt_w[:, pl.ds(k0, WINDOW + TQ)], p_w, preferred_element_type=F32)
        out_ref[:, cols] += gate_w[:, cols] * (acc_w[:HEAD_DIM] * (1.0 / acc_w[HEAD_DIM:HEAD_DIM + 1]))
        if lo == 0:
            yield False
    yield True

    def sel_pair(j, carry):
        i = 2 * j
        sb_ref[...] = sel_scores(i + 1)
        sel_update(sa_ref, i, None)
        sa_ref[...] = sel_scores(i + 2)
        sel_update(sb_ref, i + 1, None)
        return carry

    lax.fori_loop(0, n_list // 2, sel_pair, 0)

    def last_tiles(odd):
        own = n_list
        own_rows = pl.ds(pl.multiple_of(q0 - n_full * TK_SEL, TQ), TQ)
        if odd:
            sb_ref[...] = sel_scores(own)
            yield
            yield from _flash_update_steps(sa_ref[...], vt_s[tiles_ref[own - 1]], None, m_ref, acc_ref)
            yield
        own_ref = sb_ref if odd else sa_ref
        own_ref[own_rows, :] = jnp.where(not_after, own_ref[own_rows, :], NEG)
        yield from _flash_update_steps(own_ref[...], vt_s[tiles_ref[own]], None, m_ref, acc_ref)

    yield n_list % 2, last_tiles

    out_ref[...] += gate_row(1) * _flash_result(acc_ref)

    for half in range(nrep // 2):
        slab = jnp.concatenate([out_ref[:, pl.ds(2 * half * TQ, TQ)],
                                out_ref[:, pl.ds((2 * half + 1) * TQ, TQ)]], axis=0)
        o_ref[:, pl.ds(lane0 + half * LANES, LANES)] = slab.T.astype(o_ref.dtype)


def _nsa_attention(slope_pieces, q, cmp, cmp_t, keys, vals_t, glogit, bgate, constk, constc):
    b, s, _ = q.shape
    g, nrep, dh = N_GROUPS, HEADS_PER_GROUP, HEAD_DIM
    n_sb = s // L_SEL
    sq = pl.Squeezed()
    gp = GROUPS_PER_STEP
    slot = lambda c, *shape: pl.BlockSpec((sq, sq, gp) + shape, lambda i, j, t: (c, i, j, 0, 0))
    full = lambda *shape: pl.BlockSpec(shape, lambda i, j, t: (0,) * len(shape))
    per_group = lambda shape, dtype: pltpu.VMEM((gp,) + shape, dtype)
    return pl.pallas_call(
        _nsa_kernel,
        grid=(b, g // gp, s // TQ),
        in_specs=[
            pl.BlockSpec(memory_space=pltpu.SMEM),
            pl.BlockSpec((sq, TQ, gp * nrep * dh), lambda i, j, t: (i, t, j)),
            slot(0, N_CMP_PAD, dh), slot(1, dh, N_CMP_PAD),
            slot(0, s, dh), slot(0, dh, s), slot(1, s, dh), slot(1, dh, s),
            pl.BlockSpec((sq, TQ, LANES), lambda i, j, t: (i, t, 0)),
            full(1, LANES), full(s, AUG), full(N_CMP_PAD, AUG),
        ],
        out_specs=pl.BlockSpec((sq, TQ, gp * nrep * dh), lambda i, j, t: (i, t, j)),
        out_shape=jax.ShapeDtypeStruct((b, s, NSA_Q), BF16),
        scratch_shapes=[
            per_group((s, AUG), BF16), per_group((WINDOW + s, AUG), BF16), per_group((N_CMP_PAD, AUG), BF16),
            per_group((s // TK_SEL, V_ROWS, TK_SEL), BF16), per_group((V_ROWS, WINDOW + s), BF16),
            per_group((V_ROWS, N_CMP_PAD), BF16),
            per_group((AUG, NQ), BF16), per_group((n_sb, TQ), F32), per_group((1, NQ), F32),
            per_group((V_ROWS, NQ), F32), per_group((dh, NQ), F32),
            per_group((TK_SEL, NQ), F32), per_group((TK_SEL, NQ), F32),
            per_group((AUG, NQ), BF16),
            pltpu.SMEM((gp, s // TK_SEL + 1), jnp.int32),
            pltpu.VMEM((LANES, TQ), F32),
        ],
        compiler_params=_cparams("arbitrary", "arbitrary", "arbitrary"),
        name="nsa_attention",
    )(slope_pieces, q, cmp, cmp_t, keys, vals_t, keys, vals_t, glogit, bgate, constk, constc)


def _position_pieces(pos):
    hi = (pos // L_SEL) * L_SEL
    return jnp.stack([hi] * N_PIECES + [pos - hi] * N_PIECES, axis=1).astype(F32)


def _nsa_constants(s):
    pos = jnp.arange(s)
    constk = jnp.zeros((s, AUG), F32)
    constk = constk.at[:, SEL_ROW0:SEL_ROW0 + s // L_SEL].set(jax.nn.one_hot(pos // L_SEL, s // L_SEL))
    constk = constk.at[:, POS_ROW0:POS_ROW0 + 2 * N_PIECES].set(_position_pieces(pos))
    cmp_end = CMP_STRIDE * jnp.arange(N_CMP_PAD) + (L_CMP - 1)
    constc = jnp.zeros((N_CMP_PAD, AUG), F32)
    constc = constc.at[:, POS_ROW0:POS_ROW0 + 2 * N_PIECES].set(_position_pieces(cmp_end))
    slopes = 2.0 ** (-8.0 * jnp.arange(1, N_HEADS + 1, dtype=F32) / N_HEADS)
    pieces, rest = [], slopes * LOG2E
    for _ in range(N_PIECES):
        piece = rest.astype(BF16).astype(F32)
        pieces.append(piece)
        rest = rest - piece
    return constk.astype(BF16), constc.astype(BF16), jnp.stack(pieces, axis=1).reshape(-1)


def _nsa_mixer(x, norm_g, w_in, b_gate, cmp_pos, cmp_w1, cmp_b1, cmp_w2, cmp_b2):
    b, s, d = x.shape
    g, nrep, dh = N_GROUPS, HEADS_PER_GROUP, HEAD_DIM
    w_in_p = jnp.pad(w_in, ((0, 0), (0, NSA_IN_PAD - NSA_IN))).astype(BF16)
    v_cols = lambda i: w_in_p[:, NSA_Q + i * NSA_KV:NSA_Q + (i + 1) * NSA_KV].T
    q, raw, keys, vals_t, glogit = _nsa_in_proj(x, norm_g, w_in_p, jnp.stack([v_cols(3), v_cols(5)]),
                                                tm=512)
    raw = raw.reshape(2, b, g, s // CMP_STRIDE, CMP_STRIDE * dh)
    cmp, cmp_t = _compress(raw, cmp_pos.reshape(2, 1, L_CMP * dh), cmp_w1.astype(BF16), cmp_b1,
                           cmp_w2.astype(BF16), cmp_b2)
    constk, constc, slope_pieces = _nsa_constants(s)
    bgate = jnp.pad(b_gate, (0, LANES - 3 * N_HEADS)).reshape(1, LANES)
    return _nsa_attention(slope_pieces, q, cmp, cmp_t, keys, vals_t, glogit, bgate, constk, constc)


TT = 256
SUB = SUBLANES


GATE_BAND = 256


def _gate_bands():
    block = D_RNN // LRU_BLOCKS
    bands = []
    for c0 in range(0, D_RNN, GATE_BAND):
        width = min(GATE_BAND, D_RNN - c0)
        k_lo = (c0 // block) * block // LANES * LANES
        k_hi = -(-(((c0 + width - 1) // block + 1) * block) // LANES) * LANES
        bands.append((k_lo, min(k_hi, D_RNN), c0, width))
    return tuple(bands)


GATE_BANDS = _gate_bands()


def _pack_gate_weights(w_a, w_x):
    dense = lambda w: jax.scipy.linalg.block_diag(*[w[i] for i in range(LRU_BLOCKS)])
    slabs = []
    for k_lo, k_hi, c0, width in GATE_BANDS:
        band = [jnp.pad(dense(w)[k_lo:k_hi, c0:c0 + width], ((0, 0), (0, GATE_BAND - width)))
                for w in (w_a, w_x)]
        slabs.append(jnp.concatenate(band, axis=1))
    return jnp.concatenate(slabs, axis=0).astype(BF16)


SEQS_PER_STEP = 2


def _lru_kernel(x_all, g_ref, win_ref, cw_ref, cb_ref, wax_ref, bax_ref, lam_ref, wout_ref, o_all,
                *scratch):
    seqs = [tuple(ref.at[h] for ref in (x_all, o_all) + scratch) for h in range(SEQS_PER_STEP)]

    @pl.when(pl.program_id(1) == 0)
    def _():
        for x_ref, o_ref, ext_ref, a_ref, u_ref, h_ref, gate_ref in seqs:
            ext_ref[pl.ds(TT, SUB), :] = jnp.zeros((SUB, D_RNN), F32)
            h_ref[...] = jnp.zeros_like(h_ref)

    tiles = [_lru_tile(g_ref, win_ref, cw_ref, cb_ref, wax_ref, bax_ref, lam_ref, wout_ref, *refs)
             for refs in seqs]
    while tiles:
        tiles = [tile for tile in tiles if next(tile, True) is not True]


def _lru_tile(g_ref, win_ref, cw_ref, cb_ref, wax_ref, bax_ref, lam_ref, wout_ref,
              x_ref, o_ref, ext_ref, a_ref, u_ref, h_ref, gate_ref):
    xn = _rms(x_ref[...], g_ref[...]).astype(BF16)
    gate_ref[...] = _gelu_tanh(jnp.dot(xn, win_ref[:, :D_RNN], preferred_element_type=F32))
    ext_ref[pl.ds(0, SUB), :] = ext_ref[pl.ds(TT, SUB), :]
    ext_ref[pl.ds(SUB, TT), :] = jnp.dot(xn, win_ref[:, D_RNN:], preferred_element_type=F32)
    yield
    cw = cw_ref[...]
    xr = cb_ref[...] + sum(
        cw[w:w + 1, :] * ext_ref[pl.ds(SUB - (CONV_W - 1) + w, TT), :] for w in range(CONV_W))

    lam = lam_ref[...]
    softplus_neg = jnp.maximum(-lam, 0.0) + jnp.log1p(jnp.exp(-jnp.abs(lam)))
    half_log_scale = (-0.5 * LRU_C * LOG2E) * softplus_neg
    xr_b = xr.astype(BF16)
    half_bias = bax_ref[...]
    row0 = 0
    for k_lo, k_hi, c0, width in GATE_BANDS:
        z = jnp.dot(xr_b[:, k_lo:k_hi], wax_ref[row0:row0 + k_hi - k_lo, :], preferred_element_type=F32)
        row0 += k_hi - k_lo
        tanh_r = jnp.tanh(z[:, :width] + half_bias[:, c0:c0 + width])
        tanh_i = jnp.tanh(z[:, GATE_BAND:GATE_BAND + width] + half_bias[:, D_RNN + c0:D_RNN + c0 + width])
        scale = half_log_scale[:, c0:c0 + width]
        a = jnp.exp2(scale * tanh_r + scale)
        a_ref[:, c0:c0 + width] = a
        half_xr = 0.5 * xr[:, c0:c0 + width]
        u_ref[:, c0:c0 + width] = jnp.sqrt(jnp.maximum(1.0 - a * a, 0.0)) * (half_xr * tanh_i + half_xr)
    yield

    row = lax.broadcasted_iota(jnp.int32, (SUB, D_RNN), 0)

    def scan_rows(c, h_prev):
        r0 = c * SUB
        a_c = a_ref[pl.ds(r0, SUB), :]
        u_c = u_ref[pl.ds(r0, SUB), :]
        shift = 1
        while shift < SUB:
            keep = row >= shift
            u_c = u_c + a_c * jnp.where(keep, pltpu.roll(u_c, shift, axis=0), 0.0)
            a_c = a_c * jnp.where(keep, pltpu.roll(a_c, shift, axis=0), 1.0)
            shift *= 2
        h_c = u_c + a_c * h_prev
        u_ref[pl.ds(r0, SUB), :] = h_c
        return jnp.broadcast_to(h_c[SUB - 1:SUB, :], (SUB, D_RNN))

    h_state = h_ref[...]
    for c in range(TT // SUB):
        h_state = scan_rows(c, h_state)
        if c % (TT // SUB // 4) == TT // SUB // 4 - 1:
            yield
    h_ref[...] = h_state
    gated = (u_ref[...] * gate_ref[...]).astype(BF16)
    o_ref[...] = x_ref[...] + jnp.dot(gated, wout_ref[...], preferred_element_type=F32)


def _lru_mixer(x, norm_g, w_in, conv_w, conv_b, w_a, b_a, w_x, b_x, lam, w_out):
    b, s, d = x.shape
    wax = _pack_gate_weights(0.5 * w_a, 0.5 * w_x)
    bax = 0.5 * jnp.concatenate([b_a, b_x])
    nseq = SEQS_PER_STEP
    resident = lambda *shape: pl.BlockSpec(shape, lambda i, t: (0,) * len(shape),
                                           pipeline_mode=pl.Buffered(1))
    per_seq = lambda rows: pltpu.VMEM((nseq, rows, D_RNN), F32)
    return pl.pallas_call(
        _lru_kernel,
        grid=(b // nseq, s // TT),
        in_specs=[
            pl.BlockSpec((nseq, TT, d), lambda i, t: (i, t, 0)),
            resident(1, d), resident(d, 2 * D_RNN), resident(CONV_W, D_RNN), resident(1, D_RNN),
            resident(*wax.shape), resident(1, 2 * D_RNN), resident(1, D_RNN), resident(D_RNN, d),
        ],
        out_specs=pl.BlockSpec((nseq, TT, d), lambda i, t: (i, t, 0)),
        out_shape=jax.ShapeDtypeStruct((b, s, d), F32),
        scratch_shapes=[per_seq(TT + SUB), per_seq(TT), per_seq(TT), per_seq(SUB), per_seq(TT)],
        compiler_params=_cparams("parallel", "arbitrary"),
        name="lru_block",
    )(x, norm_g.reshape(1, d), w_in.astype(BF16), conv_w, conv_b.reshape(1, -1), wax,
      bax.reshape(1, -1), lam.reshape(1, -1), w_out.astype(BF16))


def _layer_to_bf16_kernel(w_ref, o_ref):
    o_ref[...] = w_ref[...].astype(o_ref.dtype)


def _layer_to_bf16(w, layer, *, n_blocks=8):
    _, rows, cols = w.shape
    tr = rows // n_blocks
    return pl.pallas_call(
        _layer_to_bf16_kernel,
        grid=(n_blocks,),
        in_specs=[pl.BlockSpec((pl.Squeezed(), tr, cols), lambda i: (layer, i, 0))],
        out_specs=pl.BlockSpec((tr, cols), lambda i: (i, 0)),
        out_shape=jax.ShapeDtypeStruct((rows, cols), BF16),
        compiler_params=_cparams("parallel"),
        name="weights_to_bf16",
    )(w)


def kernel(x, norm_mix, norm_ffn, norm_final, nsa_w_in, nsa_b_gate, nsa_cmp_pos, nsa_cmp_w1,
           nsa_cmp_b1, nsa_cmp_w2, nsa_cmp_b2, nsa_w_out, lru_w_in, lru_conv_w, lru_conv_b,
           lru_w_a, lru_b_a, lru_w_x, lru_b_x, lru_lambda, lru_w_out, ffn_w_in, ffn_w_out):
    b, s, d = x.shape
    ffn = functools.partial(_ffn, tm=512, tf=256)
    attn = _nsa_mixer(x, norm_mix[0], nsa_w_in[0], nsa_b_gate[0], nsa_cmp_pos[0], nsa_cmp_w1[0],
                      nsa_cmp_b1[0], nsa_cmp_w2[0], nsa_cmp_b2[0])
    x = ffn(x.reshape(b * s, d), norm_ffn[0], _layer_to_bf16(ffn_w_in, 0), _layer_to_bf16(ffn_w_out, 0),
            norm_final, final_norm=False, name="ffn0", mix=attn.reshape(b * s, NSA_Q),
            w_mix=nsa_w_out[0].astype(BF16)).reshape(b, s, d)
    x = _lru_mixer(x, norm_mix[1], lru_w_in[0], lru_conv_w[0], lru_conv_b[0], lru_w_a[0], lru_b_a[0],
                   lru_w_x[0], lru_b_x[0], lru_lambda[0], lru_w_out[0])
    x = ffn(x.reshape(b * s, d), norm_ffn[1], _layer_to_bf16(ffn_w_in, 1), _layer_to_bf16(ffn_w_out, 1),
            norm_final, final_norm=True, name="ffn1").reshape(b, s, d)
    return x
```
